```python
import jax, jax.numpy as jnp
from jax import lax
import numpy as np

D_MODEL = 1024
BATCH = 16
SEQ = 2048
DEPTH = 1

MEM_LEN = 256
HEAD_DIM = 64
CHUNK = 128
A_GROUPS = 4
A_WIDTH = D_MODEL // 2
A_GROUP_W = A_WIDTH // A_GROUPS
SWA_HEADS = 4
SWA_KV_HEADS = 2
SWA_WIDTH = SWA_HEADS * HEAD_DIM
SWA_KV_WIDTH = SWA_KV_HEADS * HEAD_DIM
WINDOW = 128
MEM_HEADS = 4
MEM_WIDTH = MEM_HEADS * HEAD_DIM
MIX_WIDTH = A_WIDTH + SWA_WIDTH + MEM_WIDTH
IN_WIDTH = 2 * A_WIDTH + SWA_WIDTH + 2 * SWA_KV_WIDTH + MEM_WIDTH + MIX_WIDTH
N_BUCKETS = 32
MAX_DISTANCE = 128
EPS = 1e-6
NEG = -1e30

kernel_name = "hymba_gmlp_swa_sink_memxattn_layer"


def rms_norm(x, g):
    xf = x.astype(jnp.float32)
    y = xf * lax.rsqrt(jnp.mean(xf * xf, axis=-1, keepdims=True) + EPS)
    return (y * g.astype(jnp.float32)).astype(x.dtype)


def t5_causal_buckets(dist):
    n = np.maximum(dist, 0)
    max_exact = N_BUCKETS // 2
    large = max_exact + (np.log(np.maximum(n, 1) / max_exact) / np.log(MAX_DISTANCE / max_exact)
                         * (N_BUCKETS - max_exact)).astype(np.int32)
    large = np.minimum(large, N_BUCKETS - 1)
    return np.where(n < max_exact, n, large).astype(np.int32)


def chunked_spatial_gating(u, v, v_g, v_b, w_s, b_s):
    b, s, _ = u.shape
    nc = s // CHUNK
    vg = v.reshape(b, s, A_GROUPS, A_GROUP_W).astype(jnp.float32)
    mu = jnp.mean(vg, axis=-1, keepdims=True)
    var = jnp.mean(jnp.square(vg - mu), axis=-1, keepdims=True)
    vg = (vg - mu) * lax.rsqrt(var + EPS)
    vg = vg * v_g.reshape(A_GROUPS, A_GROUP_W).astype(jnp.float32) + v_b.reshape(A_GROUPS, A_GROUP_W).astype(jnp.float32)
    vc = vg.astype(v.dtype).reshape(b, nc, CHUNK, A_GROUPS, A_GROUP_W)
    causal = jnp.tril(jnp.ones((CHUNK, CHUNK), dtype=w_s.dtype))
    w = w_s * causal[None]
    sv = jnp.einsum('gts,bnsgc->bntgc', w, vc) + b_s.T[None, None, :, :, None]
    return u * sv.reshape(b, s, A_WIDTH)


def sliding_window_attention(q, k, v, sinks, rel_bias):
    b, s, hq, dh = q.shape
    nb = s // CHUNK
    g = hq // SWA_KV_HEADS
    qb = q.reshape(b, nb, CHUNK, SWA_KV_HEADS, g, dh)

    def band(t):
        tb = t.reshape(b, nb, CHUNK, SWA_KV_HEADS, dh)
        prev = jnp.pad(tb, ((0, 0), (1, 0), (0, 0), (0, 0), (0, 0)))[:, :-1]
        return jnp.concatenate([prev, tb], axis=2)

    kb, vb = band(k), band(v)
    logits = jnp.einsum('bnqhgd,bnjhd->bnhgqj', qb, kb).astype(jnp.float32) * (dh ** -0.5)

    qi = np.arange(CHUNK)[:, None]
    kj = np.arange(2 * CHUNK)[None, :]
    dist = qi + CHUNK - kj
    blk = np.arange(nb)[:, None, None]
    valid = (dist >= 0) & (dist < WINDOW) & (blk * CHUNK + kj - CHUNK >= 0)
    buckets = t5_causal_buckets(dist)
    bias = rel_bias.astype(jnp.float32)[buckets]
    bias = jnp.transpose(bias, (2, 0, 1)).reshape(SWA_KV_HEADS, g, CHUNK, 2 * CHUNK)

    logits = jnp.where(valid[None, :, None, None], logits + bias[None, None], NEG)
    sink = sinks.astype(jnp.float32).reshape(1, 1, SWA_KV_HEADS, g, 1, 1)
    m = jnp.maximum(jnp.max(logits, axis=-1, keepdims=True), sink)
    p = jnp.exp(logits - m)
    probs = p / (jnp.sum(p, axis=-1, keepdims=True) + jnp.exp(sink - m))
    out = jnp.einsum('bnhgqj,bnjhd->bnqhgd', probs.astype(v.dtype), vb)
    return out.reshape(b, s, hq * dh)


def memory_cross_attention(q, mem_k, mem_v):
    b, s, h, dh = q.shape
    logits = jnp.einsum('bshd,bmhd->bhsm', q, mem_k).astype(jnp.float32) * (dh ** -0.5)
    probs = jax.nn.softmax(logits, axis=-1)
    out = jnp.einsum('bhsm,bmhd->bshd', probs.astype(mem_v.dtype), mem_v)
    return out.reshape(b, s, h * dh)


def setup_inputs(seed: int = 0) -> dict:
    key = jax.random.key(seed)
    ks = jax.random.split(key, 16)
    f32 = jnp.float32
    x = jax.random.normal(ks[0], (BATCH, SEQ, D_MODEL), f32)
    mem = jax.random.normal(ks[1], (BATCH, MEM_LEN, D_MODEL), f32)
    pre_norm_g = 1.0 + 0.05 * jax.random.normal(ks[2], (DEPTH, D_MODEL), f32)
    post_norm_g = 1.0 + 0.05 * jax.random.normal(ks[3], (DEPTH, D_MODEL), f32)
    mem_norm_g = 1.0 + 0.05 * jax.random.normal(ks[4], (DEPTH, D_MODEL), f32)
    w_in = jax.random.normal(ks[5], (DEPTH, D_MODEL, IN_WIDTH), f32) * D_MODEL ** -0.5
    w_mem_kv = jax.random.normal(ks[6], (DEPTH, D_MODEL, 2 * MEM_WIDTH), f32) * D_MODEL ** -0.5
    v_norm_g = 1.0 + 0.05 * jax.random.normal(ks[7], (DEPTH, A_WIDTH), f32)
    v_norm_b = 0.02 * jax.random.normal(ks[8], (DEPTH, A_WIDTH), f32)
    w_spatial = jax.random.normal(ks[9], (DEPTH, A_GROUPS, CHUNK, CHUNK), f32) * CHUNK ** -0.5
    b_spatial = 1.0 + 0.1 * jax.random.normal(ks[10], (DEPTH, A_GROUPS, CHUNK), f32)
    attn_sinks = 0.5 * jax.random.normal(ks[11], (DEPTH, SWA_HEADS), f32)
    rel_bias = 0.5 * jax.random.normal(ks[12], (N_BUCKETS, SWA_HEADS), f32)
    w_out = jax.random.normal(ks[13], (DEPTH, MIX_WIDTH, D_MODEL), f32) * MIX_WIDTH ** -0.5
    return {"x": x, "mem": mem, "pre_norm_g": pre_norm_g, "post_norm_g": post_norm_g,
            "mem_norm_g": mem_norm_g, "w_in": w_in, "w_mem_kv": w_mem_kv,
            "v_norm_g": v_norm_g, "v_norm_b": v_norm_b, "w_spatial": w_spatial,
            "b_spatial": b_spatial, "attn_sinks": attn_sinks, "rel_bias": rel_bias,
            "w_out": w_out}


def reference(x, mem, pre_norm_g, post_norm_g, mem_norm_g, w_in, w_mem_kv, v_norm_g, v_norm_b,
              w_spatial, b_spatial, attn_sinks, rel_bias, w_out):
    b, s, _ = x.shape
    m_len = mem.shape[1]
    split_at = np.cumsum([A_WIDTH, A_WIDTH, SWA_WIDTH, SWA_KV_WIDTH, SWA_KV_WIDTH, MEM_WIDTH]).tolist()
    for layer in range(DEPTH):
        h = rms_norm(x, pre_norm_g[layer])
        proj = h @ w_in[layer]
        a_u, a_v, sq, sk, sv, mq, z = jnp.split(proj, split_at, axis=-1)

        y_a = chunked_spatial_gating(jax.nn.gelu(a_u), jax.nn.gelu(a_v), v_norm_g[layer],
                                     v_norm_b[layer], w_spatial[layer], b_spatial[layer])

        y_b = sliding_window_attention(sq.reshape(b, s, SWA_HEADS, HEAD_DIM),
                                       sk.reshape(b, s, SWA_KV_HEADS, HEAD_DIM),
                                       sv.reshape(b, s, SWA_KV_HEADS, HEAD_DIM),
                                       attn_sinks[layer], rel_bias)

        mkv = rms_norm(mem, mem_norm_g[layer]) @ w_mem_kv[layer]
        mk, mv = jnp.split(mkv, 2, axis=-1)
        y_c = memory_cross_attention(mq.reshape(b, s, MEM_HEADS, HEAD_DIM),
                                     mk.reshape(b, m_len, MEM_HEADS, HEAD_DIM),
                                     mv.reshape(b, m_len, MEM_HEADS, HEAD_DIM))

        y = jnp.concatenate([y_a, y_b, y_c], axis=-1) * jax.nn.silu(z)
        x = x + rms_norm(y @ w_out[layer], post_norm_g[layer])
    return x
```

```python
import functools

import numpy as np
import jax
import jax.numpy as jnp
from jax import lax
from jax.experimental import pallas as pl
from jax.experimental.pallas import tpu as pltpu

D_MODEL = 1024
MEM_LEN = 256
HEAD_DIM = 64
CHUNK = 128
A_GROUPS = 4
A_WIDTH = 512
SWA_HEADS = 4
SWA_WIDTH = 256
SWA_KV_WIDTH = 128
MEM_HEADS = 4
MEM_WIDTH = 256
MIX_WIDTH = 1024
IN_WIDTH = 2816
N_BUCKETS = 32
MAX_DISTANCE = 128
WINDOW = 128
EPS = 1e-6
NEG = -1e30

OFF_U = 0
OFF_V = A_WIDTH
OFF_SQ = 2 * A_WIDTH
OFF_SK = OFF_SQ + SWA_WIDTH
OFF_SV = OFF_SK + SWA_KV_WIDTH
OFF_MQ = OFF_SV + SWA_KV_WIDTH
OFF_Z = OFF_MQ + MEM_WIDTH

LANES = 128
SEQ_TILE = 512
VMEM_LIMIT_BYTES = 56 * 1024 * 1024

BF16 = jnp.bfloat16
F32 = jnp.float32


def _t5_causal_buckets(dist):
    n = np.maximum(dist, 0)
    max_exact = N_BUCKETS // 2
    large = max_exact + (np.log(np.maximum(n, 1) / max_exact) / np.log(MAX_DISTANCE / max_exact)
                         * (N_BUCKETS - max_exact)).astype(np.int32)
    large = np.minimum(large, N_BUCKETS - 1)
    return np.where(n < max_exact, n, large).astype(np.int32)


def _band_buckets():
    qi = np.arange(CHUNK)[:, None]
    kj = np.arange(2 * CHUNK)[None, :]
    dist = qi + CHUNK - kj
    valid = (dist >= 0) & (dist < WINDOW)
    return np.where(valid, _t5_causal_buckets(dist), -1).astype(np.int32)


def _gelu_tanh(x):
    c = np.float32(np.sqrt(2.0 / np.pi))
    return 0.5 * x * (1.0 + jnp.tanh(c * (x + np.float32(0.044715) * (x * x * x))))


def _dot(a, b):
    return jnp.dot(a, b, preferred_element_type=F32)


def _dot_nt(a, b):
    return lax.dot_general(a, b, (((1,), (1,)), ((), ())), preferred_element_type=F32)


def _lane_half_mask(parity):
    lane = lax.broadcasted_iota(jnp.int32, (1, LANES), 1)
    return (lane >= HEAD_DIM) if parity else (lane < HEAD_DIM)


def _mem_kv_kernel(mem_ref, g_ref, w_ref, mk_ref, mv_ref):
    x = mem_ref[0]
    ms = jnp.mean(x * x, axis=-1, keepdims=True)
    h = (x * lax.rsqrt(ms + EPS) * g_ref[...]).astype(BF16)
    kv = _dot(h, w_ref[...])
    for hd in range(MEM_HEADS):
        pair = hd // 2
        keep = _lane_half_mask(hd % 2)
        k_pair = kv[:, pair * LANES:(pair + 1) * LANES]
        v_pair = kv[:, MEM_WIDTH + pair * LANES:MEM_WIDTH + (pair + 1) * LANES]
        mk_ref[0, hd] = jnp.where(keep, k_pair, 0.0).astype(BF16)
        mv_ref[0, hd] = jnp.where(keep, v_pair, 0.0).astype(BF16)


def _mem_kv(mem, mem_norm_g, w_mem_kv_bf16):
    batch = mem.shape[0]
    out = jax.ShapeDtypeStruct((batch, MEM_HEADS, MEM_LEN, LANES), BF16)
    return pl.pallas_call(
        _mem_kv_kernel,
        grid=(batch,),
        in_specs=[
            pl.BlockSpec((1, MEM_LEN, D_MODEL), lambda b: (b, 0, 0)),
            pl.BlockSpec((1, D_MODEL), lambda b: (0, 0)),
            pl.BlockSpec((D_MODEL, 2 * MEM_WIDTH), lambda b: (0, 0)),
        ],
        out_specs=[
            pl.BlockSpec((1, MEM_HEADS, MEM_LEN, LANES), lambda b: (b, 0, 0, 0)),
            pl.BlockSpec((1, MEM_HEADS, MEM_LEN, LANES), lambda b: (b, 0, 0, 0)),
        ],
        out_shape=[out, out],
        compiler_params=pltpu.CompilerParams(dimension_semantics=("arbitrary",)),
        name="mem_kv",
    )(mem, mem_norm_g, w_mem_kv_bf16)


def _layer_kernel(sinks_ref, relb_ref,
                  x_ref, mk_ref, mv_ref, win_ref, wout_ref, preg_ref, postg_ref,
                  vng_ref, vnb_ref, ws_ref, bs_ref, buckets_ref,
                  out_ref,
                  h_scr, proj_scr, kvar_scr, vvar_scr, bias_scr, wsm_scr, y_scr, ybf_scr):
    ts = x_ref.shape[1]
    n_chunks = ts // CHUNK
    b = pl.program_id(0)
    t = pl.program_id(1)

    @pl.when((b == 0) & (t == 0))
    def _init_tables():
        row = lax.broadcasted_iota(jnp.int32, (CHUNK, CHUNK), 0)
        col = lax.broadcasted_iota(jnp.int32, (CHUNK, CHUNK), 1)
        for g in range(A_GROUPS):
            wsm_scr[g] = jnp.where(row >= col, ws_ref[g], 0.0).astype(BF16)
        buckets = buckets_ref[...]
        prev_key = lax.broadcasted_iota(jnp.int32, (CHUNK, 2 * CHUNK), 1) < CHUNK
        for hd in range(SWA_HEADS):
            tbl = jnp.full((CHUNK, 2 * CHUNK), NEG, F32)
            for bk in range(N_BUCKETS):
                tbl = jnp.where(buckets == bk, relb_ref[bk, hd], tbl)
            bias_scr[0, hd] = tbl
            bias_scr[1, hd] = jnp.where(prev_key, NEG, tbl)

    @pl.when(t == 0)
    def _zero_carry():
        kvar_scr[:, 0:CHUNK, :] = jnp.zeros((SWA_HEADS, CHUNK, LANES), BF16)
        vvar_scr[:, 0:CHUNK, :] = jnp.zeros((SWA_HEADS, CHUNK, LANES), BF16)

    @pl.when(t > 0)
    def _copy_carry():
        kvar_scr[:, 0:CHUNK, :] = kvar_scr[:, ts:ts + CHUNK, :]
        vvar_scr[:, 0:CHUNK, :] = vvar_scr[:, ts:ts + CHUNK, :]

    for c in range(n_chunks):
        rows = pl.ds(c * CHUNK, CHUNK)
        x = x_ref[0, rows, :]
        ms = jnp.mean(x * x, axis=-1, keepdims=True)
        h_scr[rows, :] = (x * lax.rsqrt(ms + EPS) * preg_ref[...]).astype(BF16)

    proj_scr[...] = _dot(h_scr[...], win_ref[...])

    sk = proj_scr[:, OFF_SK:OFF_SK + LANES]
    sv = proj_scr[:, OFF_SV:OFF_SV + LANES]
    sk_rot = pltpu.roll(sk, HEAD_DIM, 1)
    sv_rot = pltpu.roll(sv, HEAD_DIM, 1)
    lo = _lane_half_mask(0)
    hi = _lane_half_mask(1)
    new_rows = pl.ds(CHUNK, ts)
    for hd, (keep, ksrc, vsrc) in enumerate(
            [(lo, sk, sv), (hi, sk_rot, sv_rot), (lo, sk_rot, sv_rot), (hi, sk, sv)]):
        kvar_scr[hd, new_rows, :] = jnp.where(keep, ksrc, 0.0).astype(BF16)
        vvar_scr[hd, new_rows, :] = jnp.where(keep, vsrc, 0.0).astype(BF16)

    first_tile = jnp.where(t == 0, 1, 0)

    for c in range(n_chunks):
        r0 = c * CHUNK
        rows = pl.ds(r0, CHUNK)

        for g in range(A_GROUPS):
            cols = pl.ds(g * LANES, LANES)
            u = _gelu_tanh(proj_scr[rows, pl.ds(OFF_U + g * LANES, LANES)])
            v = _gelu_tanh(proj_scr[rows, pl.ds(OFF_V + g * LANES, LANES)])
            mu = jnp.mean(v, axis=-1, keepdims=True)
            d = v - mu
            var = jnp.mean(d * d, axis=-1, keepdims=True)
            vn = d * lax.rsqrt(var + EPS) * vng_ref[:, cols] + vnb_ref[:, cols]
            sv_g = _dot(wsm_scr[g], vn.astype(BF16)) + bs_ref[g]
            y_scr[rows, cols] = u * sv_g

        band = pl.ds(r0, 2 * CHUNK)
        for pair in range(SWA_HEADS // 2):
            qp = (proj_scr[rows, pl.ds(OFF_SQ + pair * LANES, LANES)] * (HEAD_DIM ** -0.5)).astype(BF16)
            o_pair = None
            for hd in (2 * pair, 2 * pair + 1):
                s = _dot_nt(qp, kvar_scr[hd, band, :])
                if c == 0:
                    s = s + bias_scr[first_tile, hd]
                else:
                    s = s + bias_scr[0, hd]
                sink = sinks_ref[hd]
                m = jnp.maximum(jnp.max(s, axis=-1, keepdims=True), sink)
                p = jnp.exp(s - m)
                denom = jnp.sum(p, axis=-1, keepdims=True) + jnp.exp(sink - m)
                o = _dot(p.astype(BF16), vvar_scr[hd, band, :]) * (1.0 / denom)
                o_pair = o if o_pair is None else o_pair + o
            y_scr[rows, pl.ds(A_WIDTH + pair * LANES, LANES)] = o_pair

        for pair in range(MEM_HEADS // 2):
            qp = (proj_scr[rows, pl.ds(OFF_MQ + pair * LANES, LANES)] * (HEAD_DIM ** -0.5)).astype(BF16)
            o_pair = None
            for hd in (2 * pair, 2 * pair + 1):
                s = _dot_nt(qp, mk_ref[0, hd])
                m = jnp.max(s, axis=-1, keepdims=True)
                p = jnp.exp(s - m)
                denom = jnp.sum(p, axis=-1, keepdims=True)
                o = _dot(p.astype(BF16), mv_ref[0, hd]) * (1.0 / denom)
                o_pair = o if o_pair is None else o_pair + o
            y_scr[rows, pl.ds(A_WIDTH + SWA_WIDTH + pair * LANES, LANES)] = o_pair

        z = proj_scr[rows, pl.ds(OFF_Z, MIX_WIDTH)]
        gate = z * (1.0 / (1.0 + jnp.exp(-z)))
        ybf_scr[rows, :] = (y_scr[rows, :] * gate).astype(BF16)

    y_scr[...] = _dot(ybf_scr[...], wout_ref[...])
    for c in range(n_chunks):
        rows = pl.ds(c * CHUNK, CHUNK)
        o = y_scr[rows, :]
        ms = jnp.mean(o * o, axis=-1, keepdims=True)
        out_ref[0, rows, :] = x_ref[0, rows, :] + o * lax.rsqrt(ms + EPS) * postg_ref[...]


def _layer(x, mk_var, mv_var, w_in_bf16, w_out_bf16, pre_g, post_g, vng, vnb, w_spatial, bs_tbl,
           buckets, sinks, rel_bias):
    batch, seq, _ = x.shape
    ts = SEQ_TILE
    n_tiles = seq // ts
    const2 = lambda b, t, *_: (0, 0)
    const3 = lambda b, t, *_: (0, 0, 0)
    grid_spec = pltpu.PrefetchScalarGridSpec(
        num_scalar_prefetch=2,
        grid=(batch, n_tiles),
        in_specs=[
            pl.BlockSpec((1, ts, D_MODEL), lambda b, t, *_: (b, t, 0)),
            pl.BlockSpec((1, MEM_HEADS, MEM_LEN, LANES), lambda b, t, *_: (b, 0, 0, 0)),
            pl.BlockSpec((1, MEM_HEADS, MEM_LEN, LANES), lambda b, t, *_: (b, 0, 0, 0)),
            pl.BlockSpec((D_MODEL, IN_WIDTH), const2),
            pl.BlockSpec((MIX_WIDTH, D_MODEL), const2),
            pl.BlockSpec((1, D_MODEL), const2),
            pl.BlockSpec((1, D_MODEL), const2),
            pl.BlockSpec((1, A_WIDTH), const2),
            pl.BlockSpec((1, A_WIDTH), const2),
            pl.BlockSpec((A_GROUPS, CHUNK, CHUNK), const3),
            pl.BlockSpec((A_GROUPS, CHUNK, LANES), const3),
            pl.BlockSpec((CHUNK, 2 * CHUNK), const2),
        ],
        out_specs=pl.BlockSpec((1, ts, D_MODEL), lambda b, t, *_: (b, t, 0)),
        scratch_shapes=[
            pltpu.VMEM((ts, D_MODEL), BF16),
            pltpu.VMEM((ts, IN_WIDTH), F32),
            pltpu.VMEM((SWA_HEADS, CHUNK + ts, LANES), BF16),
            pltpu.VMEM((SWA_HEADS, CHUNK + ts, LANES), BF16),
            pltpu.VMEM((2, SWA_HEADS, CHUNK, 2 * CHUNK), F32),
            pltpu.VMEM((A_GROUPS, CHUNK, CHUNK), BF16),
            pltpu.VMEM((ts, MIX_WIDTH), F32),
            pltpu.VMEM((ts, MIX_WIDTH), BF16),
        ],
    )
    return pl.pallas_call(
        _layer_kernel,
        grid_spec=grid_spec,
        out_shape=jax.ShapeDtypeStruct(x.shape, x.dtype),
        compiler_params=pltpu.CompilerParams(
            dimension_semantics=("arbitrary", "arbitrary"),
            vmem_limit_bytes=VMEM_LIMIT_BYTES),
        name="layer",
    )(sinks, rel_bias, x, mk_var, mv_var, w_in_bf16, w_out_bf16, pre_g, post_g, vng, vnb,
      w_spatial, bs_tbl, buckets)


def kernel(x, mem, pre_norm_g, post_norm_g, mem_norm_g, w_in, w_mem_kv, v_norm_g, v_norm_b,
           w_spatial, b_spatial, attn_sinks, rel_bias, w_out):
    depth = w_in.shape[0]
    buckets = jnp.asarray(_band_buckets())
    for layer in range(depth):
        mk_var, mv_var = _mem_kv(mem, mem_norm_g[layer][None, :], w_mem_kv[layer].astype(BF16))
        bs_tbl = jnp.broadcast_to(b_spatial[layer][:, :, None], (A_GROUPS, CHUNK, LANES))
        x = _layer(x, mk_var, mv_var, w_in[layer].astype(BF16), w_out[layer].astype(BF16),
                   pre_norm_g[layer][None, :], post_norm_g[layer][None, :],
                   v_norm_g[layer][None, :], v_norm_b[layer][None, :],
                   w_spatial[layer], bs_tbl, buckets, attn_sinks[layer], rel_bias)
    return x
```

```python
import numpy as np
import jax
import jax.numpy as jnp
from jax import lax
from jax.experimental import pallas as pl
from jax.experimental.pallas import tpu as pltpu

D_MODEL = 1024
MEM_LEN = 256
HEAD_DIM = 64
CHUNK = 128
A_GROUPS = 4
A_WIDTH = 512
SWA_HEADS = 4
SWA_WIDTH = 256
SWA_KV_WIDTH = 128
MEM_HEADS = 4
MEM_WIDTH = 256
MIX_WIDTH = 1024
IN_WIDTH = 2816
N_BUCKETS = 32
MAX_DISTANCE = 128
WINDOW = 128
EPS = 1e-6
NEG = -1e30
LOG2E = float(np.log2(np.e))
Q_SCALE = HEAD_DIM ** -0.5 * LOG2E

OFF_U = 0
OFF_V = A_WIDTH
OFF_SQ = 2 * A_WIDTH
OFF_SK = OFF_SQ + SWA_WIDTH
OFF_SV = OFF_SK + SWA_KV_WIDTH
OFF_MQ = OFF_SV + SWA_KV_WIDTH
OFF_Z = OFF_MQ + MEM_WIDTH

LANES = 128
SEQ_TILE = 512
SUB_TILE = 256
PROJ_COLS = 512
ITEM_LAG = 2
FILLER_PER_ITEM_STAGE = 0.25
VMEM_LIMIT_BYTES = 56 * 1024 * 1024

BF16 = jnp.bfloat16
F32 = jnp.float32


def _t5_causal_buckets(dist):
    n = np.maximum(dist, 0)
    max_exact = N_BUCKETS // 2
    large = max_exact + (np.log(np.maximum(n, 1) / max_exact) / np.log(MAX_DISTANCE / max_exact)
                         * (N_BUCKETS - max_exact)).astype(np.int32)
    large = np.minimum(large, N_BUCKETS - 1)
    return np.where(n < max_exact, n, large).astype(np.int32)


def _band_buckets():
    qi = np.arange(CHUNK)[:, None]
    kj = np.arange(2 * CHUNK)[None, :]
    dist = qi + CHUNK - kj
    valid = (dist >= 0) & (dist < WINDOW)
    return np.where(valid, _t5_causal_buckets(dist), -1).astype(np.int32)


def _gelu_tanh(x):
    c = np.float32(np.sqrt(2.0 / np.pi))
    ck = np.float32(np.sqrt(2.0 / np.pi) * 0.044715)
    hx = 0.5 * x
    return hx + hx * jnp.tanh(x * (c + ck * (x * x)))


def _silu(z):
    hz = 0.5 * z
    return hz + hz * jnp.tanh(hz)


def _dot(a, b):
    return jnp.dot(a, b, preferred_element_type=F32)


def _dot_nt(a, b):
    return lax.dot_general(a, b, (((1,), (1,)), ((), ())), preferred_element_type=F32)


_DONE = object()


def _skewed(items, lag):
    waiting = []
    for item in items:
        if next(item, _DONE) is not _DONE:
            waiting.append(item)
        yield
        if len(waiting) > lag:
            next(waiting.pop(0), _DONE)
            yield
    for item in waiting:
        next(item, _DONE)
        yield


def _chain(*gens):
    for g in gens:
        yield from g


def _interleave(a, b, b_per_a):
    credit = 0.0
    a_live = b_live = True
    while a_live or b_live:
        if a_live:
            a_live = next(a, _DONE) is not _DONE
        credit += b_per_a
        while b_live and (credit >= 1.0 or not a_live):
            b_live = next(b, _DONE) is not _DONE
            credit -= 1.0


def _lane_half_mask(parity):
    lane = lax.broadcasted_iota(jnp.int32, (1, LANES), 1)
    return (lane >= HEAD_DIM) if parity else (lane < HEAD_DIM)


def _mem_kv_kernel(mem_ref, g_ref, w_ref, mk_ref, mv_ref):
    x = mem_ref[0]
    ms = jnp.mean(x * x, axis=-1, keepdims=True)
    h = (x * lax.rsqrt(ms + EPS) * g_ref[...]).astype(BF16)
    kv = _dot(h, w_ref[...])
    for hd in range(MEM_HEADS):
        pair = hd // 2
        keep = _lane_half_mask(hd % 2)
        k_pair = kv[:, pair * LANES:(pair + 1) * LANES]
        v_pair = kv[:, MEM_WIDTH + pair * LANES:MEM_WIDTH + (pair + 1) * LANES]
        mk_ref[0, hd] = jnp.where(keep, k_pair, 0.0).astype(BF16)
        mv_ref[0, hd] = jnp.where(keep, v_pair, 0.0).astype(BF16)


def _mem_kv(mem, mem_norm_g, w_mem_kv_bf16):
    batch = mem.shape[0]
    out = jax.ShapeDtypeStruct((batch, MEM_HEADS, MEM_LEN, LANES), BF16)
    return pl.pallas_call(
        _mem_kv_kernel,
        grid=(batch,),
        in_specs=[
            pl.BlockSpec((1, MEM_LEN, D_MODEL), lambda b: (b, 0, 0)),
            pl.BlockSpec((1, D_MODEL), lambda b: (0, 0)),
            pl.BlockSpec((D_MODEL, 2 * MEM_WIDTH), lambda b: (0, 0)),
        ],
        out_specs=[
            pl.BlockSpec((1, MEM_HEADS, MEM_LEN, LANES), lambda b: (b, 0, 0, 0)),
            pl.BlockSpec((1, MEM_HEADS, MEM_LEN, LANES), lambda b: (b, 0, 0, 0)),
        ],
        out_shape=[out, out],
        compiler_params=pltpu.CompilerParams(dimension_semantics=("arbitrary",)),
        name="mem_kv",
    )(mem, mem_norm_g, w_mem_kv_bf16)


def _layer_kernel(sinks_ref, relb_ref,
                  x_ref, mk_ref, mv_ref, win_ref, wout_ref, preg_ref, postg_ref,
                  vng_ref, vnb_ref, ws_ref, bs_ref, buckets_ref,
                  out_ref,
                  h_scr, gu_scr, vn_scr, qs_scr, kvar_scr, vvar_scr, zg_scr,
                  bias_scr, wsm_scr, y_scr, ybf_scr, o_scr):
    ts = x_ref.shape[1]
    b = pl.program_id(0)
    t = pl.program_id(1)

    @pl.when((b == 0) & (t == 0))
    def _init():
        row = lax.broadcasted_iota(jnp.int32, (CHUNK, CHUNK), 0)
        col = lax.broadcasted_iota(jnp.int32, (CHUNK, CHUNK), 1)
        for g in range(A_GROUPS):
            wsm_scr[g] = jnp.where(row >= col, ws_ref[g], 0.0).astype(BF16)
        buckets = buckets_ref[...]
        prev_key = lax.broadcasted_iota(jnp.int32, (CHUNK, 2 * CHUNK), 1) < CHUNK
        for hd in range(SWA_HEADS):
            tbl = jnp.full((CHUNK, 2 * CHUNK), NEG, F32)
            for bk in range(N_BUCKETS):
                tbl = jnp.where(buckets == bk, relb_ref[bk, hd] * LOG2E, tbl)
            bias_scr[0, hd] = tbl
            bias_scr[1, hd] = jnp.where(prev_key, NEG, tbl)

    @pl.when(t == 0)
    def _zero_carry():
        kvar_scr[:, 0:CHUNK, :] = jnp.zeros((SWA_HEADS, CHUNK, LANES), BF16)
        vvar_scr[:, 0:CHUNK, :] = jnp.zeros((SWA_HEADS, CHUNK, LANES), BF16)

    @pl.when(t > 0)
    def _copy_carry():
        kvar_scr[:, 0:CHUNK, :] = kvar_scr[:, ts:ts + CHUNK, :]
        vvar_scr[:, 0:CHUNK, :] = vvar_scr[:, ts:ts + CHUNK, :]

    lo = _lane_half_mask(0)
    hi = _lane_half_mask(1)
    first_tile = jnp.where(t == 0, 1, 0)

    def pre_norm(r0, nrows):
        for rc in range(nrows // CHUNK):
            crows = pl.ds(r0 + rc * CHUNK, CHUNK)
            x = x_ref[0, crows, :]
            ms = jnp.mean(x * x, axis=-1, keepdims=True)
            h_scr[crows, :] = (x * lax.rsqrt(ms + EPS) * preg_ref[...]).astype(BF16)

    def proj(rows, c0, width):
        return _dot(h_scr[rows, :], win_ref[:, c0:c0 + width])

    def project(r0, nrows):
        rows = pl.ds(r0, nrows)

        gu_scr[rows, :] = _gelu_tanh(proj(rows, OFF_U, A_WIDTH))
        yield

        v_all = _gelu_tanh(proj(rows, OFF_V, A_WIDTH))
        for g in range(A_GROUPS):
            cols = pl.ds(g * LANES, LANES)
            v = v_all[:, g * LANES:(g + 1) * LANES]
            mu = jnp.mean(v, axis=-1, keepdims=True)
            d = v - mu
            var = jnp.mean(d * d, axis=-1, keepdims=True)
            vn = d * lax.rsqrt(var + EPS) * vng_ref[:, cols] + vnb_ref[:, cols]
            vn_scr[rows, cols] = vn.astype(BF16)
        yield

        qkv = proj(rows, OFF_SQ, SWA_WIDTH + 2 * SWA_KV_WIDTH)
        qs_scr[rows, 0:SWA_WIDTH] = (qkv[:, 0:SWA_WIDTH] * Q_SCALE).astype(BF16)
        sk = qkv[:, SWA_WIDTH:SWA_WIDTH + LANES]
        sv = qkv[:, SWA_WIDTH + LANES:]
        sk_rot = pltpu.roll(sk, HEAD_DIM, 1)
        sv_rot = pltpu.roll(sv, HEAD_DIM, 1)
        new_rows = pl.ds(CHUNK + r0, nrows)
        for hd, (keep, ksrc, vsrc) in enumerate(
                [(lo, sk, sv), (hi, sk_rot, sv_rot), (lo, sk_rot, sv_rot), (hi, sk, sv)]):
            kvar_scr[hd, new_rows, :] = jnp.where(keep, ksrc, 0.0).astype(BF16)
            vvar_scr[hd, new_rows, :] = jnp.where(keep, vsrc, 0.0).astype(BF16)
        yield

        mqz = proj(rows, OFF_MQ, PROJ_COLS)
        qs_scr[rows, SWA_WIDTH:] = (mqz[:, 0:MEM_WIDTH] * Q_SCALE).astype(BF16)
        zg_scr[rows, 0:PROJ_COLS - MEM_WIDTH] = _silu(mqz[:, MEM_WIDTH:])
        yield

        done = PROJ_COLS - MEM_WIDTH
        while done < MIX_WIDTH:
            width = min(PROJ_COLS, MIX_WIDTH - done)
            zg_scr[rows, done:done + width] = _silu(proj(rows, OFF_Z + done, width))
            done += width
            yield

    def gmlp_item(c, g):
        rows = pl.ds(c * CHUNK, CHUNK)
        cols = pl.ds(g * LANES, LANES)
        sv_g = _dot(wsm_scr[g], vn_scr[rows, cols]) + bs_ref[g]
        y_scr[rows, cols] = gu_scr[rows, cols] * sv_g
        return
        yield

    def attn_item(c, pair, sliding):
        r0 = c * CHUNK
        rows = pl.ds(r0, CHUNK)
        heads = (2 * pair, 2 * pair + 1)
        if sliding:
            band = pl.ds(r0, 2 * CHUNK)
            qp = qs_scr[rows, pl.ds(pair * LANES, LANES)]
            keys = [kvar_scr[hd, band, :] for hd in heads]
            bias_sel = first_tile if c == 0 else 0
            logits = [_dot_nt(qp, k) + bias_scr[bias_sel, hd] for hd, k in zip(heads, keys)]
        else:
            qp = qs_scr[rows, pl.ds(SWA_WIDTH + pair * LANES, LANES)]
            logits = [_dot_nt(qp, mk_ref[0, hd]) for hd in heads]
        yield
        o_pair = None
        for hd, s in zip(heads, logits):
            m = jnp.max(s, axis=-1, keepdims=True)
            if sliding:
                sink = sinks_ref[hd] * LOG2E
                m = jnp.maximum(m, sink)
            p = jnp.exp2(s - m)
            denom = jnp.sum(p, axis=-1, keepdims=True)
            if sliding:
                denom = denom + jnp.exp2(sink - m)
                values = vvar_scr[hd, band, :]
            else:
                values = mv_ref[0, hd]
            o = _dot(p.astype(BF16), values) * (1.0 / denom)
            o_pair = o if o_pair is None else o_pair + o
        off = A_WIDTH if sliding else A_WIDTH + SWA_WIDTH
        y_scr[rows, pl.ds(off + pair * LANES, LANES)] = o_pair

    def head_items(r0, nrows):
        for c in range(r0 // CHUNK, (r0 + nrows) // CHUNK):
            for pair in range(SWA_HEADS // 2):
                yield attn_item(c, pair, True)
            for pair in range(MEM_HEADS // 2):
                yield attn_item(c, pair, False)
            for g in range(A_GROUPS):
                yield gmlp_item(c, g)

    def project_out(r0, nrows):
        rows = pl.ds(r0, nrows)
        ybf_scr[rows, :] = (y_scr[rows, :] * zg_scr[rows, :]).astype(BF16)
        for c0 in range(0, D_MODEL, PROJ_COLS):
            o_scr[rows, c0:c0 + PROJ_COLS] = _dot(ybf_scr[rows, :], wout_ref[:, c0:c0 + PROJ_COLS])
            yield
        for rc in range(nrows // CHUNK):
            crows = pl.ds(r0 + rc * CHUNK, CHUNK)
            o = o_scr[crows, :]
            ms = jnp.mean(o * o, axis=-1, keepdims=True)
            out_ref[0, crows, :] = x_ref[0, crows, :] + o * lax.rsqrt(ms + EPS) * postg_ref[...]
            yield

    n_sub = ts // SUB_TILE
    pre_norm(0, SUB_TILE)
    filler = project(0, SUB_TILE)
    for s in range(n_sub):
        if s + 1 < n_sub:
            pre_norm((s + 1) * SUB_TILE, SUB_TILE)
        for _ in filler:
            pass
        heads = _skewed(head_items(s * SUB_TILE, SUB_TILE), ITEM_LAG)
        if s + 1 < n_sub:
            filler = project((s + 1) * SUB_TILE, SUB_TILE)
        else:
            filler = iter(())
        if s >= 1:
            filler = _chain(project_out((s - 1) * SUB_TILE, SUB_TILE), filler)
        _interleave(heads, filler, FILLER_PER_ITEM_STAGE)
        filler = iter(())
    for _ in project_out((n_sub - 1) * SUB_TILE, SUB_TILE):
        pass


def _layer(x, mk_var, mv_var, w_in_bf16, w_out_bf16, pre_g, post_g, vng, vnb, w_spatial, bs_tbl,
           buckets, sinks, rel_bias):
    batch, seq, _ = x.shape
    ts = SEQ_TILE
    n_tiles = seq // ts
    const2 = lambda b, t, *_: (0, 0)
    const3 = lambda b, t, *_: (0, 0, 0)
    grid_spec = pltpu.PrefetchScalarGridSpec(
        num_scalar_prefetch=2,
        grid=(batch, n_tiles),
        in_specs=[
            pl.BlockSpec((1, ts, D_MODEL), lambda b, t, *_: (b, t, 0)),
            pl.BlockSpec((1, MEM_HEADS, MEM_LEN, LANES), lambda b, t, *_: (b, 0, 0, 0)),
            pl.BlockSpec((1, MEM_HEADS, MEM_LEN, LANES), lambda b, t, *_: (b, 0, 0, 0)),
            pl.BlockSpec((D_MODEL, IN_WIDTH), const2),
            pl.BlockSpec((MIX_WIDTH, D_MODEL), const2),
            pl.BlockSpec((1, D_MODEL), const2),
            pl.BlockSpec((1, D_MODEL), const2),
            pl.BlockSpec((1, A_WIDTH), const2),
            pl.BlockSpec((1, A_WIDTH), const2),
            pl.BlockSpec((A_GROUPS, CHUNK, CHUNK), const3),
            pl.BlockSpec((A_GROUPS, CHUNK, LANES), const3),
            pl.BlockSpec((CHUNK, 2 * CHUNK), const2),
        ],
        out_specs=pl.BlockSpec((1, ts, D_MODEL), lambda b, t, *_: (b, t, 0)),
        scratch_shapes=[
            pltpu.VMEM((ts, D_MODEL), BF16),
            pltpu.VMEM((ts, A_WIDTH), F32),
            pltpu.VMEM((ts, A_WIDTH), BF16),
            pltpu.VMEM((ts, SWA_WIDTH + MEM_WIDTH), BF16),
            pltpu.VMEM((SWA_HEADS, CHUNK + ts, LANES), BF16),
            pltpu.VMEM((SWA_HEADS, CHUNK + ts, LANES), BF16),
            pltpu.VMEM((ts, MIX_WIDTH), F32),
            pltpu.VMEM((2, SWA_HEADS, CHUNK, 2 * CHUNK), F32),
            pltpu.VMEM((A_GROUPS, CHUNK, CHUNK), BF16),
            pltpu.VMEM((ts, MIX_WIDTH), F32),
            pltpu.VMEM((ts, MIX_WIDTH), BF16),
            pltpu.VMEM((ts, D_MODEL), F32),
        ],
    )
    return pl.pallas_call(
        _layer_kernel,
        grid_spec=grid_spec,
        out_shape=jax.ShapeDtypeStruct(x.shape, x.dtype),
        compiler_params=pltpu.CompilerParams(
            dimension_semantics=("arbitrary", "arbitrary"),
            vmem_limit_bytes=VMEM_LIMIT_BYTES),
        name="layer",
    )(sinks, rel_bias, x, mk_var, mv_var, w_in_bf16, w_out_bf16, pre_g, post_g, vng, vnb,
      w_spatial, bs_tbl, buckets)


def kernel(x, mem, pre_norm_g, post_norm_g, mem_norm_g, w_in, w_mem_kv, v_norm_g, v_norm_b,
           w_spatial, b_spatial, attn_sinks, rel_bias, w_out):
    depth = w_in.shape[0]
    buckets = jnp.asarray(_band_buckets())
    for layer in range(depth):
        mk_var, mv_var = _mem_kv(mem, mem_norm_g[layer][None, :], w_mem_kv[layer].astype(BF16))
        bs_tbl = jnp.broadcast_to(b_spatial[layer][:, :, None], (A_GROUPS, CHUNK, LANES))
        x = _layer(x, mk_var, mv_var, w_in[layer].astype(BF16), w_out[layer].astype(BF16),
                   pre_norm_g[layer][None, :], post_norm_g[layer][None, :],
                   v_norm_g[layer][None, :], v_norm_b[layer][None, :],
                   w_spatial[layer], bs_tbl, buckets, attn_sinks[layer], rel_bias)
    return x
```

```python
import numpy as np
import jax
import jax.numpy as jnp
from jax import lax
from jax.experimental import pallas as pl
from jax.experimental.pallas import tpu as pltpu

D_MODEL = 1024
MEM_LEN = 256
HEAD_DIM = 64
CHUNK = 128
A_GROUPS = 4
A_WIDTH = 512
SWA_HEADS = 4
SWA_WIDTH = 256
SWA_KV_WIDTH = 128
MEM_HEADS = 4
MEM_WIDTH = 256
MIX_WIDTH = 1024
IN_WIDTH = 2816
N_BUCKETS = 32
MAX_DISTANCE = 128
WINDOW = 128
EPS = 1e-6
NEG = -1e30
LOG2E = float(np.log2(np.e))
Q_SCALE = HEAD_DIM ** -0.5 * LOG2E

OFF_U = 0
OFF_V = A_WIDTH
OFF_SQ = 2 * A_WIDTH
OFF_SK = OFF_SQ + SWA_WIDTH
OFF_SV = OFF_SK + SWA_KV_WIDTH
OFF_MQ = OFF_SV + SWA_KV_WIDTH
OFF_Z = OFF_MQ + MEM_WIDTH

LANES = 128
SEQ_TILE = 1024
SUB_TILE = 256
PROJ_COLS = 512
ITEM_LAG = 2
PROJECT_PIECES = 6
PROJECT_OUT_PIECES = D_MODEL // PROJ_COLS + SUB_TILE // CHUNK
HEAD_STAGES_PER_CHUNK = 2 * (SWA_HEADS // 2 + MEM_HEADS // 2) + A_GROUPS
VMEM_LIMIT_BYTES = 56 * 1024 * 1024

BF16 = jnp.bfloat16
F32 = jnp.float32


def _t5_causal_buckets(dist):
    n = np.maximum(dist, 0)
    max_exact = N_BUCKETS // 2
    large = max_exact + (np.log(np.maximum(n, 1) / max_exact) / np.log(MAX_DISTANCE / max_exact)
                         * (N_BUCKETS - max_exact)).astype(np.int32)
    large = np.minimum(large, N_BUCKETS - 1)
    return np.where(n < max_exact, n, large).astype(np.int32)


def _band_buckets():
    qi = np.arange(CHUNK)[:, None]
    kj = np.arange(2 * CHUNK)[None, :]
    dist = qi + CHUNK - kj
    valid = (dist >= 0) & (dist < WINDOW)
    return np.where(valid, _t5_causal_buckets(dist), -1).astype(np.int32)


def _gelu_tanh(x):
    c = np.float32(np.sqrt(2.0 / np.pi))
    ck = np.float32(np.sqrt(2.0 / np.pi) * 0.044715)
    hx = 0.5 * x
    return hx + hx * jnp.tanh(x * (c + ck * (x * x)))


def _silu(z):
    hz = 0.5 * z
    return hz + hz * jnp.tanh(hz)


def _dot(a, b):
    return jnp.dot(a, b, preferred_element_type=F32)


def _dot_nt(a, b):
    return lax.dot_general(a, b, (((1,), (1,)), ((), ())), preferred_element_type=F32)


_DONE = object()


def _skewed(items, lag):
    waiting = []
    for item in items:
        if next(item, _DONE) is not _DONE:
            waiting.append(item)
        yield
        if len(waiting) > lag:
            next(waiting.pop(0), _DONE)
            yield
    for item in waiting:
        next(item, _DONE)
        yield


def _chain(*gens):
    for g in gens:
        yield from g


def _interleave(a, b, b_per_a):
    credit = 0.0
    a_live = b_live = True
    while a_live or b_live:
        if a_live:
            a_live = next(a, _DONE) is not _DONE
        credit += b_per_a
        while b_live and (credit >= 1.0 or not a_live):
            b_live = next(b, _DONE) is not _DONE
            credit -= 1.0


def _lane_half_mask(parity):
    lane = lax.broadcasted_iota(jnp.int32, (1, LANES), 1)
    return (lane >= HEAD_DIM) if parity else (lane < HEAD_DIM)


def _mem_kv_kernel(mem_ref, g_ref, w_ref, mk_ref, mv_ref):
    x = mem_ref[0]
    ms = jnp.mean(x * x, axis=-1, keepdims=True)
    h = (x * lax.rsqrt(ms + EPS) * g_ref[...]).astype(BF16)
    kv = _dot(h, w_ref[...])
    for hd in range(MEM_HEADS):
        pair = hd // 2
        keep = _lane_half_mask(hd % 2)
        k_pair = kv[:, pair * LANES:(pair + 1) * LANES]
        v_pair = kv[:, MEM_WIDTH + pair * LANES:MEM_WIDTH + (pair + 1) * LANES]
        mk_ref[0, hd] = jnp.where(keep, k_pair, 0.0).astype(BF16)
        mv_ref[0, hd] = jnp.where(keep, v_pair, 0.0).astype(BF16)


def _mem_kv(mem, mem_norm_g, w_mem_kv_bf16):
    batch = mem.shape[0]
    out = jax.ShapeDtypeStruct((batch, MEM_HEADS, MEM_LEN, LANES), BF16)
    return pl.pallas_call(
        _mem_kv_kernel,
        grid=(batch,),
        in_specs=[
            pl.BlockSpec((1, MEM_LEN, D_MODEL), lambda b: (b, 0, 0)),
            pl.BlockSpec((1, D_MODEL), lambda b: (0, 0)),
            pl.BlockSpec((D_MODEL, 2 * MEM_WIDTH), lambda b: (0, 0)),
        ],
        out_specs=[
            pl.BlockSpec((1, MEM_HEADS, MEM_LEN, LANES), lambda b: (b, 0, 0, 0)),
            pl.BlockSpec((1, MEM_HEADS, MEM_LEN, LANES), lambda b: (b, 0, 0, 0)),
        ],
        out_shape=[out, out],
        compiler_params=pltpu.CompilerParams(dimension_semantics=("arbitrary",)),
        name="mem_kv",
    )(mem, mem_norm_g, w_mem_kv_bf16)


def _layer_kernel(sinks_ref, relb_ref,
                  x_ref, mk_ref, mv_ref, win_ref, wout_ref, preg_ref, postg_ref,
                  vng_ref, vnb_ref, ws_ref, bs_ref, buckets_ref,
                  out_ref,
                  h_scr, gu_scr, vn_scr, qs_scr, kvar_scr, vvar_scr, zg_scr,
                  bias_scr, wsm_scr, y_scr, ybf_scr, o_scr):
    ts = x_ref.shape[1]
    b = pl.program_id(0)
    t = pl.program_id(1)

    @pl.when((b == 0) & (t == 0))
    def _init():
        row = lax.broadcasted_iota(jnp.int32, (CHUNK, CHUNK), 0)
        col = lax.broadcasted_iota(jnp.int32, (CHUNK, CHUNK), 1)
        for g in range(A_GROUPS):
            wsm_scr[g] = jnp.where(row >= col, ws_ref[g], 0.0).astype(BF16)
        buckets = buckets_ref[...]
        prev_key = lax.broadcasted_iota(jnp.int32, (CHUNK, 2 * CHUNK), 1) < CHUNK
        for hd in range(SWA_HEADS):
            tbl = jnp.full((CHUNK, 2 * CHUNK), NEG, F32)
            for bk in range(N_BUCKETS):
                tbl = jnp.where(buckets == bk, relb_ref[bk, hd] * LOG2E, tbl)
            bias_scr[0, hd] = tbl
            bias_scr[1, hd] = jnp.where(prev_key, NEG, tbl)

    @pl.when(t == 0)
    def _zero_carry():
        kvar_scr[:, 0:CHUNK, :] = jnp.zeros((SWA_HEADS, CHUNK, LANES), BF16)
        vvar_scr[:, 0:CHUNK, :] = jnp.zeros((SWA_HEADS, CHUNK, LANES), BF16)

    @pl.when(t > 0)
    def _copy_carry():
        kvar_scr[:, 0:CHUNK, :] = kvar_scr[:, ts:ts + CHUNK, :]
        vvar_scr[:, 0:CHUNK, :] = vvar_scr[:, ts:ts + CHUNK, :]

    lo = _lane_half_mask(0)
    hi = _lane_half_mask(1)
    first_tile = jnp.where(t == 0, 1, 0)

    def pre_norm(r0, nrows):
        for rc in range(nrows // CHUNK):
            crows = pl.ds(r0 + rc * CHUNK, CHUNK)
            x = x_ref[0, crows, :]
            ms = jnp.mean(x * x, axis=-1, keepdims=True)
            h_scr[crows, :] = (x * lax.rsqrt(ms + EPS) * preg_ref[...]).astype(BF16)

    def proj(rows, c0, width):
        return _dot(h_scr[rows, :], win_ref[:, c0:c0 + width])

    def project(r0, nrows):
        rows = pl.ds(r0, nrows)

        gu_scr[rows, :] = _gelu_tanh(proj(rows, OFF_U, A_WIDTH))
        yield

        v_all = _gelu_tanh(proj(rows, OFF_V, A_WIDTH))
        for g in range(A_GROUPS):
            cols = pl.ds(g * LANES, LANES)
            v = v_all[:, g * LANES:(g + 1) * LANES]
            mu = jnp.mean(v, axis=-1, keepdims=True)
            d = v - mu
            var = jnp.mean(d * d, axis=-1, keepdims=True)
            vn = d * lax.rsqrt(var + EPS) * vng_ref[:, cols] + vnb_ref[:, cols]
            vn_scr[rows, cols] = vn.astype(BF16)
        yield

        qkv = proj(rows, OFF_SQ, SWA_WIDTH + 2 * SWA_KV_WIDTH)
        qs_scr[rows, 0:SWA_WIDTH] = (qkv[:, 0:SWA_WIDTH] * Q_SCALE).astype(BF16)
        sk = qkv[:, SWA_WIDTH:SWA_WIDTH + LANES]
        sv = qkv[:, SWA_WIDTH + LANES:]
        sk_rot = pltpu.roll(sk, HEAD_DIM, 1)
        sv_rot = pltpu.roll(sv, HEAD_DIM, 1)
        new_rows = pl.ds(CHUNK + r0, nrows)
        for hd, (keep, ksrc, vsrc) in enumerate(
                [(lo, sk, sv), (hi, sk_rot, sv_rot), (lo, sk_rot, sv_rot), (hi, sk, sv)]):
            kvar_scr[hd, new_rows, :] = jnp.where(keep, ksrc, 0.0).astype(BF16)
            vvar_scr[hd, new_rows, :] = jnp.where(keep, vsrc, 0.0).astype(BF16)
        yield

        mqz = proj(rows, OFF_MQ, PROJ_COLS)
        qs_scr[rows, SWA_WIDTH:] = (mqz[:, 0:MEM_WIDTH] * Q_SCALE).astype(BF16)
        zg_scr[rows, 0:PROJ_COLS - MEM_WIDTH] = _silu(mqz[:, MEM_WIDTH:])
        yield

        done = PROJ_COLS - MEM_WIDTH
        while done < MIX_WIDTH:
            width = min(PROJ_COLS, MIX_WIDTH - done)
            zg_scr[rows, done:done + width] = _silu(proj(rows, OFF_Z + done, width))
            done += width
            yield

    def gmlp_item(c, g):
        rows = pl.ds(c * CHUNK, CHUNK)
        cols = pl.ds(g * LANES, LANES)
        sv_g = _dot(wsm_scr[g], vn_scr[rows, cols]) + bs_ref[g]
        y_scr[rows, cols] = gu_scr[rows, cols] * sv_g
        return
        yield

    def attn_item(c, pair, sliding):
        r0 = c * CHUNK
        rows = pl.ds(r0, CHUNK)
        heads = (2 * pair, 2 * pair + 1)
        if sliding:
            band = pl.ds(r0, 2 * CHUNK)
            qp = qs_scr[rows, pl.ds(pair * LANES, LANES)]
            keys = [kvar_scr[hd, band, :] for hd in heads]
            bias_sel = first_tile if c == 0 else 0
            logits = [_dot_nt(qp, k) + bias_scr[bias_sel, hd] for hd, k in zip(heads, keys)]
        else:
            qp = qs_scr[rows, pl.ds(SWA_WIDTH + pair * LANES, LANES)]
            logits = [_dot_nt(qp, mk_ref[0, hd]) for hd in heads]
        yield
        o_pair = None
        for hd, s in zip(heads, logits):
            m = jnp.max(s, axis=-1, keepdims=True)
            if sliding:
                sink = sinks_ref[hd] * LOG2E
                m = jnp.maximum(m, sink)
            p = jnp.exp2(s - m)
            denom = jnp.sum(p, axis=-1, keepdims=True)
            if sliding:
                denom = denom + jnp.exp2(sink - m)
                values = vvar_scr[hd, band, :]
            else:
                values = mv_ref[0, hd]
            o = _dot(p.astype(BF16), values) * (1.0 / denom)
            o_pair = o if o_pair is None else o_pair + o
        off = A_WIDTH if sliding else A_WIDTH + SWA_WIDTH
        y_scr[rows, pl.ds(off + pair * LANES, LANES)] = o_pair

    def head_items(r0, nrows):
        for c in range(r0 // CHUNK, (r0 + nrows) // CHUNK):
            for pair in range(SWA_HEADS // 2):
                yield attn_item(c, pair, True)
            for pair in range(MEM_HEADS // 2):
                yield attn_item(c, pair, False)
            for g in range(A_GROUPS):
                yield gmlp_item(c, g)

    def project_out(r0, nrows):
        rows = pl.ds(r0, nrows)
        ybf_scr[rows, :] = (y_scr[rows, :] * zg_scr[rows, :]).astype(BF16)
        for c0 in range(0, D_MODEL, PROJ_COLS):
            o_scr[rows, c0:c0 + PROJ_COLS] = _dot(ybf_scr[rows, :], wout_ref[:, c0:c0 + PROJ_COLS])
            yield
        for rc in range(nrows // CHUNK):
            crows = pl.ds(r0 + rc * CHUNK, CHUNK)
            o = o_scr[crows, :]
            ms = jnp.mean(o * o, axis=-1, keepdims=True)
            out_ref[0, crows, :] = x_ref[0, crows, :] + o * lax.rsqrt(ms + EPS) * postg_ref[...]
            yield

    n_sub = ts // SUB_TILE
    pre_norm(0, SUB_TILE)
    filler = project(0, SUB_TILE)
    for s in range(n_sub):
        if s + 1 < n_sub:
            pre_norm((s + 1) * SUB_TILE, SUB_TILE)
        for _ in filler:
            pass
        heads = _skewed(head_items(s * SUB_TILE, SUB_TILE), ITEM_LAG)
        n_filler = 0
        if s + 1 < n_sub:
            filler = project((s + 1) * SUB_TILE, SUB_TILE)
            n_filler += PROJECT_PIECES
        else:
            filler = iter(())
        if s >= 1:
            filler = _chain(project_out((s - 1) * SUB_TILE, SUB_TILE), filler)
            n_filler += PROJECT_OUT_PIECES
        _interleave(heads, filler, n_filler / ((SUB_TILE // CHUNK) * HEAD_STAGES_PER_CHUNK))
        filler = iter(())
    for _ in project_out((n_sub - 1) * SUB_TILE, SUB_TILE):
        pass


def _layer(x, mk_var, mv_var, w_in_bf16, w_out_bf16, pre_g, post_g, vng, vnb, w_spatial, bs_tbl,
           buckets, sinks, rel_bias):
    batch, seq, _ = x.shape
    ts = SEQ_TILE
    n_tiles = seq // ts
    const2 = lambda b, t, *_: (0, 0)
    const3 = lambda b, t, *_: (0, 0, 0)
    grid_spec = pltpu.PrefetchScalarGridSpec(
        num_scalar_prefetch=2,
        grid=(batch, n_tiles),
        in_specs=[
            pl.BlockSpec((1, ts, D_MODEL), lambda b, t, *_: (b, t, 0)),
            pl.BlockSpec((1, MEM_HEADS, MEM_LEN, LANES), lambda b, t, *_: (b, 0, 0, 0)),
            pl.BlockSpec((1, MEM_HEADS, MEM_LEN, LANES), lambda b, t, *_: (b, 0, 0, 0)),
            pl.BlockSpec((D_MODEL, IN_WIDTH), const2),
            pl.BlockSpec((MIX_WIDTH, D_MODEL), const2),
            pl.BlockSpec((1, D_MODEL), const2),
            pl.BlockSpec((1, D_MODEL), const2),
            pl.BlockSpec((1, A_WIDTH), const2),
            pl.BlockSpec((1, A_WIDTH), const2),
            pl.BlockSpec((A_GROUPS, CHUNK, CHUNK), const3),
            pl.BlockSpec((A_GROUPS, CHUNK, LANES), const3),
            pl.BlockSpec((CHUNK, 2 * CHUNK), const2),
        ],
        out_specs=pl.BlockSpec((1, ts, D_MODEL), lambda b, t, *_: (b, t, 0)),
        scratch_shapes=[
            pltpu.VMEM((ts, D_MODEL), BF16),
            pltpu.VMEM((ts, A_WIDTH), F32),
            pltpu.VMEM((ts, A_WIDTH), BF16),
            pltpu.VMEM((ts, SWA_WIDTH + MEM_WIDTH), BF16),
            pltpu.VMEM((SWA_HEADS, CHUNK + ts, LANES), BF16),
            pltpu.VMEM((SWA_HEADS, CHUNK + ts, LANES), BF16),
            pltpu.VMEM((ts, MIX_WIDTH), F32),
            pltpu.VMEM((2, SWA_HEADS, CHUNK, 2 * CHUNK), F32),
            pltpu.VMEM((A_GROUPS, CHUNK, CHUNK), BF16),
            pltpu.VMEM((ts, MIX_WIDTH), F32),
            pltpu.VMEM((ts, MIX_WIDTH), BF16),
            pltpu.VMEM((ts, D_MODEL), F32),
        ],
    )
    return pl.pallas_call(
        _layer_kernel,
        grid_spec=grid_spec,
        out_shape=jax.ShapeDtypeStruct(x.shape, x.dtype),
        compiler_params=pltpu.CompilerParams(
            dimension_semantics=("arbitrary", "arbitrary"),
            vmem_limit_bytes=VMEM_LIMIT_BYTES),
        name="layer",
    )(sinks, rel_bias, x, mk_var, mv_var, w_in_bf16, w_out_bf16, pre_g, post_g, vng, vnb,
      w_spatial, bs_tbl, buckets)


def kernel(x, mem, pre_norm_g, post_norm_g, mem_norm_g, w_in, w_mem_kv, v_norm_g, v_norm_b,
           w_spatial, b_spatial, attn_sinks, rel_bias, w_out):
    depth = w_in.shape[0]
    buckets = jnp.asarray(_band_buckets())
    for layer in range(depth):
        mk_var, mv_var = _mem_kv(mem, mem_norm_g[layer][None, :], w_mem_kv[layer].astype(BF16))
        bs_tbl = jnp.broadcast_to(b_spatial[layer][:, :, None], (A_GROUPS, CHUNK, LANES))
        x = _layer(x, mk_var, mv_var, w_in[layer].astype(BF16), w_out[layer].astype(BF16),
                   pre_norm_g[layer][None, :], post_norm_g[layer][None, :],
                   v_norm_g[layer][None, :], v_norm_b[layer][None, :],
                   w_spatial[layer], bs_tbl, buckets, attn_sinks[layer], rel_bias)
    return x
```

```python
import numpy as np
import jax
import jax.numpy as jnp
from jax import lax
from jax.experimental import pallas as pl
from jax.experimental.pallas import tpu as pltpu

D_MODEL = 1024
MEM_LEN = 256
HEAD_DIM = 64
CHUNK = 128
A_GROUPS = 4
A_WIDTH = 512
SWA_HEADS = 4
SWA_WIDTH = 256
SWA_KV_WIDTH = 128
MEM_HEADS = 4
MEM_WIDTH = 256
MIX_WIDTH = 1024
IN_WIDTH = 2816
N_BUCKETS = 32
MAX_DISTANCE = 128
WINDOW = 128
EPS = 1e-6
NEG = -1e30
LOG2E = float(np.log2(np.e))
Q_SCALE = HEAD_DIM ** -0.5 * LOG2E

OFF_U = 0
OFF_V = A_WIDTH
OFF_SQ = 2 * A_WIDTH
OFF_SK = OFF_SQ + SWA_WIDTH
OFF_SV = OFF_SK + SWA_KV_WIDTH
OFF_MQ = OFF_SV + SWA_KV_WIDTH
OFF_Z = OFF_MQ + MEM_WIDTH

LANES = 128
SEQ_TILE = 1024
SUB_TILES = (256, 256, 256, 256)
MEM_KV_BATCH = 4
PROJ_COLS = 512
ITEM_LAG = 2
PROJECT_PIECES = 6
PROJECT_OUT_DOTS = D_MODEL // PROJ_COLS
HEAD_STAGES_PER_CHUNK = 2 * (SWA_HEADS // 2 + MEM_HEADS // 2) + A_GROUPS
VMEM_LIMIT_BYTES = 56 * 1024 * 1024

BF16 = jnp.bfloat16
F32 = jnp.float32


def _t5_causal_buckets(dist):
    n = np.maximum(dist, 0)
    max_exact = N_BUCKETS // 2
    large = max_exact + (np.log(np.maximum(n, 1) / max_exact) / np.log(MAX_DISTANCE / max_exact)
                         * (N_BUCKETS - max_exact)).astype(np.int32)
    large = np.minimum(large, N_BUCKETS - 1)
    return np.where(n < max_exact, n, large).astype(np.int32)


def _band_buckets():
    qi = np.arange(CHUNK)[:, None]
    kj = np.arange(2 * CHUNK)[None, :]
    dist = qi + CHUNK - kj
    valid = (dist >= 0) & (dist < WINDOW)
    return np.where(valid, _t5_causal_buckets(dist), -1).astype(np.int32)


def _gelu_tanh(x):
    c2 = np.float32(-2.0 * np.sqrt(2.0 / np.pi) * LOG2E)
    ck2 = np.float32(-2.0 * np.sqrt(2.0 / np.pi) * 0.044715 * LOG2E)
    return x / (1.0 + jnp.exp2(x * (c2 + ck2 * (x * x))))


def _silu(z):
    hz = 0.5 * z
    return hz + hz * jnp.tanh(hz)


def _dot(a, b):
    return jnp.dot(a, b, preferred_element_type=F32)


def _dot_nt(a, b):
    return lax.dot_general(a, b, (((1,), (1,)), ((), ())), preferred_element_type=F32)


_DONE = object()


def _skewed(items, lag):
    waiting = []
    for item in items:
        if next(item, _DONE) is not _DONE:
            waiting.append(item)
        yield
        if len(waiting) > lag:
            next(waiting.pop(0), _DONE)
            yield
    for item in waiting:
        next(item, _DONE)
        yield


def _chain(*gens):
    for g in gens:
        yield from g


def _interleave(a, b, b_per_a):
    credit = 0.0
    a_live = b_live = True
    while a_live or b_live:
        if a_live:
            a_live = next(a, _DONE) is not _DONE
        credit += b_per_a
        while b_live and (credit >= 1.0 or not a_live):
            b_live = next(b, _DONE) is not _DONE
            credit -= 1.0


def _lane_half_mask(parity):
    lane = lax.broadcasted_iota(jnp.int32, (1, LANES), 1)
    return (lane >= HEAD_DIM) if parity else (lane < HEAD_DIM)


def _mem_kv_kernel(mem_ref, w_ref, mk_ref, mv_ref):
    for i in range(mem_ref.shape[0]):
        x = mem_ref[i]
        ms = jnp.mean(x * x, axis=-1, keepdims=True)
        h = (x * lax.rsqrt(ms + EPS)).astype(BF16)
        kv = _dot(h, w_ref[...])
        for hd in range(MEM_HEADS):
            pair = hd // 2
            keep = _lane_half_mask(hd % 2)
            k_pair = kv[:, pair * LANES:(pair + 1) * LANES]
            v_pair = kv[:, MEM_WIDTH + pair * LANES:MEM_WIDTH + (pair + 1) * LANES]
            mk_ref[i, hd] = jnp.where(keep, k_pair, 0.0).astype(BF16)
            mv_ref[i, hd] = jnp.where(keep, v_pair, 0.0).astype(BF16)


def _mem_kv(mem, w_mem_kv_bf16):
    batch = mem.shape[0]
    bb = MEM_KV_BATCH
    out = jax.ShapeDtypeStruct((batch, MEM_HEADS, MEM_LEN, LANES), BF16)
    return pl.pallas_call(
        _mem_kv_kernel,
        grid=(batch // bb,),
        in_specs=[
            pl.BlockSpec((bb, MEM_LEN, D_MODEL), lambda b: (b, 0, 0)),
            pl.BlockSpec((D_MODEL, 2 * MEM_WIDTH), lambda b: (0, 0)),
        ],
        out_specs=[
            pl.BlockSpec((bb, MEM_HEADS, MEM_LEN, LANES), lambda b: (b, 0, 0, 0)),
            pl.BlockSpec((bb, MEM_HEADS, MEM_LEN, LANES), lambda b: (b, 0, 0, 0)),
        ],
        out_shape=[out, out],
        compiler_params=pltpu.CompilerParams(dimension_semantics=("arbitrary",)),
        name="mem_kv",
    )(mem, w_mem_kv_bf16)


def _layer_kernel(sinks_ref, relb_ref,
                  x_ref, mk_ref, mv_ref, win_ref, wout_ref, postg_ref,
                  vng_ref, vnb_ref, ws_ref, bs_ref, buckets_ref,
                  out_ref,
                  h_scr, gu_scr, vn_scr, qs_scr, kvar_scr, vvar_scr, zg_scr,
                  bias_scr, wsm_scr, y_scr, ybf_scr, o_scr):
    ts = x_ref.shape[1]
    b = pl.program_id(0)
    t = pl.program_id(1)

    @pl.when((b == 0) & (t == 0))
    def _init():
        row = lax.broadcasted_iota(jnp.int32, (CHUNK, CHUNK), 0)
        col = lax.broadcasted_iota(jnp.int32, (CHUNK, CHUNK), 1)
        for g in range(A_GROUPS):
            wsm_scr[g] = jnp.where(row >= col, ws_ref[g], 0.0).astype(BF16)
        buckets = buckets_ref[...]
        prev_key = lax.broadcasted_iota(jnp.int32, (CHUNK, 2 * CHUNK), 1) < CHUNK
        for hd in range(SWA_HEADS):
            tbl = jnp.full((CHUNK, 2 * CHUNK), NEG, F32)
            for bk in range(N_BUCKETS):
                tbl = jnp.where(buckets == bk, relb_ref[bk, hd] * LOG2E, tbl)
            bias_scr[0, hd] = tbl
            bias_scr[1, hd] = jnp.where(prev_key, NEG, tbl)

    @pl.when(t == 0)
    def _zero_carry():
        kvar_scr[:, 0:CHUNK, :] = jnp.zeros((SWA_HEADS, CHUNK, LANES), BF16)
        vvar_scr[:, 0:CHUNK, :] = jnp.zeros((SWA_HEADS, CHUNK, LANES), BF16)

    @pl.when(t > 0)
    def _copy_carry():
        kvar_scr[:, 0:CHUNK, :] = kvar_scr[:, ts:ts + CHUNK, :]
        vvar_scr[:, 0:CHUNK, :] = vvar_scr[:, ts:ts + CHUNK, :]

    lo = _lane_half_mask(0)
    hi = _lane_half_mask(1)
    first_tile = jnp.where(t == 0, 1, 0)

    def pre_norm(r0, nrows):
        for rc in range(nrows // CHUNK):
            crows = pl.ds(r0 + rc * CHUNK, CHUNK)
            x = x_ref[0, crows, :]
            ms = jnp.mean(x * x, axis=-1, keepdims=True)
            h_scr[crows, :] = (x * lax.rsqrt(ms + EPS)).astype(BF16)

    def proj(rows, c0, width):
        return _dot(h_scr[rows, :], win_ref[:, c0:c0 + width])

    def project(r0, nrows):
        rows = pl.ds(r0, nrows)

        gu_scr[rows, :] = _gelu_tanh(proj(rows, OFF_U, A_WIDTH))
        yield

        v_all = _gelu_tanh(proj(rows, OFF_V, A_WIDTH))
        for g in range(A_GROUPS):
            cols = pl.ds(g * LANES, LANES)
            v = v_all[:, g * LANES:(g + 1) * LANES]
            mu = jnp.mean(v, axis=-1, keepdims=True)
            d = v - mu
            var = jnp.mean(d * d, axis=-1, keepdims=True)
            vn = d * lax.rsqrt(var + EPS) * vng_ref[:, cols] + vnb_ref[:, cols]
            vn_scr[rows, cols] = vn.astype(BF16)
        yield

        qkv = proj(rows, OFF_SQ, SWA_WIDTH + 2 * SWA_KV_WIDTH)
        qs_scr[rows, 0:SWA_WIDTH] = (qkv[:, 0:SWA_WIDTH] * Q_SCALE).astype(BF16)
        sk = qkv[:, SWA_WIDTH:SWA_WIDTH + LANES]
        sv = qkv[:, SWA_WIDTH + LANES:]
        sk_rot = pltpu.roll(sk, HEAD_DIM, 1)
        sv_rot = pltpu.roll(sv, HEAD_DIM, 1)
        new_rows = pl.ds(CHUNK + r0, nrows)
        for hd, (keep, ksrc, vsrc) in enumerate(
                [(lo, sk, sv), (hi, sk_rot, sv_rot), (lo, sk_rot, sv_rot), (hi, sk, sv)]):
            kvar_scr[hd, new_rows, :] = jnp.where(keep, ksrc, 0.0).astype(BF16)
            vvar_scr[hd, new_rows, :] = jnp.where(keep, vsrc, 0.0).astype(BF16)
        yield

        mqz = proj(rows, OFF_MQ, PROJ_COLS)
        qs_scr[rows, SWA_WIDTH:] = (mqz[:, 0:MEM_WIDTH] * Q_SCALE).astype(BF16)
        zg_scr[rows, 0:PROJ_COLS - MEM_WIDTH] = _silu(mqz[:, MEM_WIDTH:])
        yield

        done = PROJ_COLS - MEM_WIDTH
        while done < MIX_WIDTH:
            width = min(PROJ_COLS, MIX_WIDTH - done)
            zg_scr[rows, done:done + width] = _silu(proj(rows, OFF_Z + done, width))
            done += width
            yield

    def gmlp_item(c, g):
        rows = pl.ds(c * CHUNK, CHUNK)
        cols = pl.ds(g * LANES, LANES)
        sv_g = _dot(wsm_scr[g], vn_scr[rows, cols]) + bs_ref[g]
        y_scr[rows, cols] = gu_scr[rows, cols] * sv_g
        return
        yield

    def attn_item(c, pair, sliding):
        r0 = c * CHUNK
        rows = pl.ds(r0, CHUNK)
        heads = (2 * pair, 2 * pair + 1)
        if sliding:
            band = pl.ds(r0, 2 * CHUNK)
            qp = qs_scr[rows, pl.ds(pair * LANES, LANES)]
            keys = [kvar_scr[hd, band, :] for hd in heads]
            bias_sel = first_tile if c == 0 else 0
            logits = [_dot_nt(qp, k) + bias_scr[bias_sel, hd] for hd, k in zip(heads, keys)]
        else:
            qp = qs_scr[rows, pl.ds(SWA_WIDTH + pair * LANES, LANES)]
            logits = [_dot_nt(qp, mk_ref[0, hd]) for hd in heads]
        yield
        o_pair = None
        for hd, s in zip(heads, logits):
            m = jnp.max(s, axis=-1, keepdims=True)
            if sliding:
                sink = sinks_ref[hd] * LOG2E
                m = jnp.maximum(m, sink)
            p = jnp.exp2(s - m)
            denom = jnp.sum(p, axis=-1, keepdims=True)
            if sliding:
                denom = denom + jnp.exp2(sink - m)
                values = vvar_scr[hd, band, :]
            else:
                values = mv_ref[0, hd]
            o = _dot(p.astype(BF16), values) * (1.0 / denom)
            o_pair = o if o_pair is None else o_pair + o
        off = A_WIDTH if sliding else A_WIDTH + SWA_WIDTH
        y_scr[rows, pl.ds(off + pair * LANES, LANES)] = o_pair

    def head_items(r0, nrows):
        for c in range(r0 // CHUNK, (r0 + nrows) // CHUNK):
            for pair in range(SWA_HEADS // 2):
                yield attn_item(c, pair, True)
            for pair in range(MEM_HEADS // 2):
                yield attn_item(c, pair, False)
            for g in range(A_GROUPS):
                yield gmlp_item(c, g)

    def project_out(r0, nrows):
        rows = pl.ds(r0, nrows)
        ybf_scr[rows, :] = (y_scr[rows, :] * zg_scr[rows, :]).astype(BF16)
        for c0 in range(0, D_MODEL, PROJ_COLS):
            o_scr[rows, c0:c0 + PROJ_COLS] = _dot(ybf_scr[rows, :], wout_ref[:, c0:c0 + PROJ_COLS])
            yield
        for rc in range(nrows // CHUNK):
            crows = pl.ds(r0 + rc * CHUNK, CHUNK)
            o = o_scr[crows, :]
            ms = jnp.mean(o * o, axis=-1, keepdims=True)
            out_ref[0, crows, :] = x_ref[0, crows, :] + o * lax.rsqrt(ms + EPS) * postg_ref[...]
            yield

    assert sum(SUB_TILES) == ts
    starts = [sum(SUB_TILES[:s]) for s in range(len(SUB_TILES))]
    subs = list(zip(starts, SUB_TILES))
    n_sub = len(subs)
    pre_norm(*subs[0])
    filler = project(*subs[0])
    for s in range(n_sub):
        if s + 1 < n_sub:
            pre_norm(*subs[s + 1])
        for _ in filler:
            pass
        heads = _skewed(head_items(*subs[s]), ITEM_LAG)
        n_filler = 0
        if s + 1 < n_sub:
            filler = project(*subs[s + 1])
            n_filler += PROJECT_PIECES
        else:
            filler = iter(())
        if s >= 1:
            filler = _chain(project_out(*subs[s - 1]), filler)
            n_filler += PROJECT_OUT_DOTS + subs[s - 1][1] // CHUNK
        _interleave(heads, filler, n_filler / ((subs[s][1] // CHUNK) * HEAD_STAGES_PER_CHUNK))
        filler = iter(())
    for _ in project_out(*subs[-1]):
        pass


def _layer(x, mk_var, mv_var, w_in_bf16, w_out_bf16, post_g, vng, vnb, w_spatial, bs_tbl,
           buckets, sinks, rel_bias):
    batch, seq, _ = x.shape
    ts = SEQ_TILE
    n_tiles = seq // ts
    const2 = lambda b, t, *_: (0, 0)
    const3 = lambda b, t, *_: (0, 0, 0)
    grid_spec = pltpu.PrefetchScalarGridSpec(
        num_scalar_prefetch=2,
        grid=(batch, n_tiles),
        in_specs=[
            pl.BlockSpec((1, ts, D_MODEL), lambda b, t, *_: (b, t, 0)),
            pl.BlockSpec((1, MEM_HEADS, MEM_LEN, LANES), lambda b, t, *_: (b, 0, 0, 0)),
            pl.BlockSpec((1, MEM_HEADS, MEM_LEN, LANES), lambda b, t, *_: (b, 0, 0, 0)),
            pl.BlockSpec((D_MODEL, IN_WIDTH), const2),
            pl.BlockSpec((MIX_WIDTH, D_MODEL), const2),
            pl.BlockSpec((1, D_MODEL), const2),
            pl.BlockSpec((1, A_WIDTH), const2),
            pl.BlockSpec((1, A_WIDTH), const2),
            pl.BlockSpec((A_GROUPS, CHUNK, CHUNK), const3),
            pl.BlockSpec((A_GROUPS, CHUNK, LANES), const3),
            pl.BlockSpec((CHUNK, 2 * CHUNK), const2),
        ],
        out_specs=pl.BlockSpec((1, ts, D_MODEL), lambda b, t, *_: (b, t, 0)),
        scratch_shapes=[
            pltpu.VMEM((ts, D_MODEL), BF16),
            pltpu.VMEM((ts, A_WIDTH), F32),
            pltpu.VMEM((ts, A_WIDTH), BF16),
            pltpu.VMEM((ts, SWA_WIDTH + MEM_WIDTH), BF16),
            pltpu.VMEM((SWA_HEADS, CHUNK + ts, LANES), BF16),
            pltpu.VMEM((SWA_HEADS, CHUNK + ts, LANES), BF16),
            pltpu.VMEM((ts, MIX_WIDTH), F32),
            pltpu.VMEM((2, SWA_HEADS, CHUNK, 2 * CHUNK), F32),
            pltpu.VMEM((A_GROUPS, CHUNK, CHUNK), BF16),
            pltpu.VMEM((ts, MIX_WIDTH), F32),
            pltpu.VMEM((ts, MIX_WIDTH), BF16),
            pltpu.VMEM((ts, D_MODEL), F32),
        ],
    )
    return pl.pallas_call(
        _layer_kernel,
        grid_spec=grid_spec,
        out_shape=jax.ShapeDtypeStruct(x.shape, x.dtype),
        compiler_params=pltpu.CompilerParams(
            dimension_semantics=("arbitrary", "arbitrary"),
            vmem_limit_bytes=VMEM_LIMIT_BYTES),
        name="layer",
    )(sinks, rel_bias, x, mk_var, mv_var, w_in_bf16, w_out_bf16, post_g, vng, vnb,
      w_spatial, bs_tbl, buckets)


def kernel(x, mem, pre_norm_g, post_norm_g, mem_norm_g, w_in, w_mem_kv, v_norm_g, v_norm_b,
           w_spatial, b_spatial, attn_sinks, rel_bias, w_out):
    depth = w_in.shape[0]
    buckets = jnp.asarray(_band_buckets())
    for layer in range(depth):
        w_kv = (w_mem_kv[layer] * mem_norm_g[layer][:, None]).astype(BF16)
        w_in_g = (w_in[layer] * pre_norm_g[layer][:, None]).astype(BF16)
        mk_var, mv_var = _mem_kv(mem, w_kv)
        bs_tbl = jnp.broadcast_to(b_spatial[layer][:, :, None], (A_GROUPS, CHUNK, LANES))
        x = _layer(x, mk_var, mv_var, w_in_g, w_out[layer].astype(BF16),
                   post_norm_g[layer][None, :],
                   v_norm_g[layer][None, :], v_norm_b[layer][None, :],
                   w_spatial[layer], bs_tbl, buckets, attn_sinks[layer], rel_bias)
    return x
```

```python
import numpy as np
import jax
import jax.numpy as jnp
from jax import lax
from jax.experimental import pallas as pl
from jax.experimental.pallas import tpu as pltpu

D_MODEL = 1024
MEM_LEN = 256
HEAD_DIM = 64
CHUNK = 128
A_GROUPS = 4
A_WIDTH = 512
SWA_HEADS = 4
SWA_WIDTH = 256
SWA_KV_WIDTH = 128
MEM_HEADS = 4
MEM_WIDTH = 256
MIX_WIDTH = 1024
IN_WIDTH = 2816
N_BUCKETS = 32
MAX_DISTANCE = 128
WINDOW = 128
EPS = 1e-6
NEG = -1e30
LOG2E = float(np.log2(np.e))
Q_SCALE = HEAD_DIM ** -0.5 * LOG2E

OFF_U = 0
OFF_V = A_WIDTH
OFF_SQ = 2 * A_WIDTH
OFF_SK = OFF_SQ + SWA_WIDTH
OFF_SV = OFF_SK + SWA_KV_WIDTH
OFF_MQ = OFF_SV + SWA_KV_WIDTH
OFF_Z = OFF_MQ + MEM_WIDTH

LANES = 128
SEQ_TILE = 1024
SUB_TILES = (256, 256, 256, 256)
MEM_KV_BATCH = 4
PROJ_COLS = 512
ITEM_LAG = 2
PROJECT_PIECES = 6
PROJECT_OUT_DOTS = D_MODEL // PROJ_COLS
HEAD_STAGES_PER_CHUNK = 2 * (SWA_HEADS // 2 + MEM_HEADS // 2) + A_GROUPS
VMEM_LIMIT_BYTES = 56 * 1024 * 1024

BF16 = jnp.bfloat16
F32 = jnp.float32


def _t5_causal_buckets(dist):
    n = np.maximum(dist, 0)
    max_exact = N_BUCKETS // 2
    large = max_exact + (np.log(np.maximum(n, 1) / max_exact) / np.log(MAX_DISTANCE / max_exact)
                         * (N_BUCKETS - max_exact)).astype(np.int32)
    large = np.minimum(large, N_BUCKETS - 1)
    return np.where(n < max_exact, n, large).astype(np.int32)


def _band_buckets():
    qi = np.arange(CHUNK)[:, None]
    kj = np.arange(2 * CHUNK)[None, :]
    dist = qi + CHUNK - kj
    valid = (dist >= 0) & (dist < WINDOW)
    return np.where(valid, _t5_causal_buckets(dist), -1).astype(np.int32)


def _gelu_tanh(x):
    c = np.float32(np.sqrt(2.0 / np.pi))
    ck = np.float32(np.sqrt(2.0 / np.pi) * 0.044715)
    hx = 0.5 * x
    return hx + hx * jnp.tanh(x * (c + ck * (x * x)))


def _silu(z):
    hz = 0.5 * z
    return hz + hz * jnp.tanh(hz)


def _dot(a, b):
    return jnp.dot(a, b, preferred_element_type=F32)


_DONE = object()


def _skewed(items, lag):
    waiting = []
    for item in items:
        if next(item, _DONE) is not _DONE:
            waiting.append(item)
        yield
        if len(waiting) > lag:
            next(waiting.pop(0), _DONE)
            yield
    for item in waiting:
        next(item, _DONE)
        yield


def _chain(*gens):
    for g in gens:
        yield from g


def _interleave(a, b, b_per_a):
    credit = 0.0
    a_live = b_live = True
    while a_live or b_live:
        if a_live:
            a_live = next(a, _DONE) is not _DONE
        credit += b_per_a
        while b_live and (credit >= 1.0 or not a_live):
            b_live = next(b, _DONE) is not _DONE
            credit -= 1.0


def _lane_half_mask(parity):
    lane = lax.broadcasted_iota(jnp.int32, (1, LANES), 1)
    return (lane >= HEAD_DIM) if parity else (lane < HEAD_DIM)


def _place_head_rows(kt_head, parity):
    zeros = jnp.zeros_like(kt_head)
    return jnp.concatenate([zeros, kt_head] if parity else [kt_head, zeros], axis=0)


def _mem_kv_kernel(mem_ref, w_ref, mk_ref, mv_ref):
    for i in range(mem_ref.shape[0]):
        x = mem_ref[i]
        ms = jnp.mean(x * x, axis=-1, keepdims=True)
        h = (x * lax.rsqrt(ms + EPS)).astype(BF16)
        kv = _dot(h, w_ref[...])
        for pair in range(MEM_HEADS // 2):
            kt_pair = kv[:, pair * LANES:(pair + 1) * LANES].T
            v_pair = kv[:, MEM_WIDTH + pair * LANES:MEM_WIDTH + (pair + 1) * LANES]
            for parity in range(2):
                hd = 2 * pair + parity
                kt_head = kt_pair[parity * HEAD_DIM:(parity + 1) * HEAD_DIM]
                mk_ref[i, hd] = _place_head_rows(kt_head, parity).astype(BF16)
                mv_ref[i, hd] = jnp.where(_lane_half_mask(parity), v_pair, 0.0).astype(BF16)


def _mem_kv(mem, w_mem_kv_bf16):
    batch = mem.shape[0]
    bb = MEM_KV_BATCH
    out_k = jax.ShapeDtypeStruct((batch, MEM_HEADS, LANES, MEM_LEN), BF16)
    out = jax.ShapeDtypeStruct((batch, MEM_HEADS, MEM_LEN, LANES), BF16)
    return pl.pallas_call(
        _mem_kv_kernel,
        grid=(batch // bb,),
        in_specs=[
            pl.BlockSpec((bb, MEM_LEN, D_MODEL), lambda b: (b, 0, 0)),
            pl.BlockSpec((D_MODEL, 2 * MEM_WIDTH), lambda b: (0, 0)),
        ],
        out_specs=[
            pl.BlockSpec((bb, MEM_HEADS, LANES, MEM_LEN), lambda b: (b, 0, 0, 0)),
            pl.BlockSpec((bb, MEM_HEADS, MEM_LEN, LANES), lambda b: (b, 0, 0, 0)),
        ],
        out_shape=[out_k, out],
        compiler_params=pltpu.CompilerParams(dimension_semantics=("arbitrary",)),
        name="mem_kv",
    )(mem, w_mem_kv_bf16)


def _layer_kernel(sinks_ref, relb_ref,
                  x_ref, mk_ref, mv_ref, win_ref, wout_ref, postg_ref,
                  vng_ref, vnb_ref, ws_ref, bs_ref, buckets_ref,
                  out_ref,
                  h_scr, gu_scr, vn_scr, qs_scr, ktvar_scr, vvar_scr, zg_scr,
                  bias_scr, wsm_scr, y_scr, ybf_scr, o_scr):
    ts = x_ref.shape[1]
    b = pl.program_id(0)
    t = pl.program_id(1)

    @pl.when((b == 0) & (t == 0))
    def _init():
        row = lax.broadcasted_iota(jnp.int32, (CHUNK, CHUNK), 0)
        col = lax.broadcasted_iota(jnp.int32, (CHUNK, CHUNK), 1)
        for g in range(A_GROUPS):
            wsm_scr[g] = jnp.where(row >= col, ws_ref[g], 0.0).astype(BF16)
        buckets = buckets_ref[...]
        prev_key = lax.broadcasted_iota(jnp.int32, (CHUNK, 2 * CHUNK), 1) < CHUNK
        for hd in range(SWA_HEADS):
            tbl = jnp.full((CHUNK, 2 * CHUNK), NEG, F32)
            for bk in range(N_BUCKETS):
                tbl = jnp.where(buckets == bk, relb_ref[bk, hd] * LOG2E, tbl)
            bias_scr[0, hd] = tbl
            bias_scr[1, hd] = jnp.where(prev_key, NEG, tbl)

    @pl.when(t == 0)
    def _zero_carry():
        ktvar_scr[:, :, 0:CHUNK] = jnp.zeros((SWA_HEADS, LANES, CHUNK), BF16)
        vvar_scr[:, 0:CHUNK, :] = jnp.zeros((SWA_HEADS, CHUNK, LANES), BF16)

    @pl.when(t > 0)
    def _copy_carry():
        ktvar_scr[:, :, 0:CHUNK] = ktvar_scr[:, :, ts:ts + CHUNK]
        vvar_scr[:, 0:CHUNK, :] = vvar_scr[:, ts:ts + CHUNK, :]

    lo = _lane_half_mask(0)
    hi = _lane_half_mask(1)
    first_tile = jnp.where(t == 0, 1, 0)

    def pre_norm(r0, nrows):
        for rc in range(nrows // CHUNK):
            crows = pl.ds(r0 + rc * CHUNK, CHUNK)
            x = x_ref[0, crows, :]
            ms = jnp.mean(x * x, axis=-1, keepdims=True)
            h_scr[crows, :] = (x * lax.rsqrt(ms + EPS)).astype(BF16)

    def proj(rows, c0, width):
        return _dot(h_scr[rows, :], win_ref[:, c0:c0 + width])

    def project(r0, nrows):
        rows = pl.ds(r0, nrows)

        gu_scr[rows, :] = _gelu_tanh(proj(rows, OFF_U, A_WIDTH))
        yield

        v_all = _gelu_tanh(proj(rows, OFF_V, A_WIDTH))
        for g in range(A_GROUPS):
            cols = pl.ds(g * LANES, LANES)
            v = v_all[:, g * LANES:(g + 1) * LANES]
            mu = jnp.mean(v, axis=-1, keepdims=True)
            d = v - mu
            var = jnp.mean(d * d, axis=-1, keepdims=True)
            vn = d * lax.rsqrt(var + EPS) * vng_ref[:, cols] + vnb_ref[:, cols]
            vn_scr[rows, cols] = vn.astype(BF16)
        yield

        qkv = proj(rows, OFF_SQ, SWA_WIDTH + 2 * SWA_KV_WIDTH)
        qs_scr[rows, 0:SWA_WIDTH] = (qkv[:, 0:SWA_WIDTH] * Q_SCALE).astype(BF16)
        skt = qkv[:, SWA_WIDTH:SWA_WIDTH + LANES].T
        sv = qkv[:, SWA_WIDTH + LANES:]
        sv_rot = pltpu.roll(sv, HEAD_DIM, 1)
        new_rows = pl.ds(CHUNK + r0, nrows)
        for hd, (keep, vsrc) in enumerate([(lo, sv), (hi, sv_rot), (lo, sv_rot), (hi, sv)]):
            kv_head = hd // 2
            kt_head = skt[kv_head * HEAD_DIM:(kv_head + 1) * HEAD_DIM]
            ktvar_scr[hd, :, new_rows] = _place_head_rows(kt_head, hd % 2).astype(BF16)
            vvar_scr[hd, new_rows, :] = jnp.where(keep, vsrc, 0.0).astype(BF16)
        yield

        mqz = proj(rows, OFF_MQ, PROJ_COLS)
        qs_scr[rows, SWA_WIDTH:] = (mqz[:, 0:MEM_WIDTH] * Q_SCALE).astype(BF16)
        zg_scr[rows, 0:PROJ_COLS - MEM_WIDTH] = _silu(mqz[:, MEM_WIDTH:])
        yield

        done = PROJ_COLS - MEM_WIDTH
        while done < MIX_WIDTH:
            width = min(PROJ_COLS, MIX_WIDTH - done)
            zg_scr[rows, done:done + width] = _silu(proj(rows, OFF_Z + done, width))
            done += width
            yield

    def gmlp_item(c, g):
        rows = pl.ds(c * CHUNK, CHUNK)
        cols = pl.ds(g * LANES, LANES)
        sv_g = _dot(wsm_scr[g], vn_scr[rows, cols]) + bs_ref[g]
        y_scr[rows, cols] = gu_scr[rows, cols] * sv_g
        return
        yield

    def attn_item(c, pair, sliding):
        r0 = c * CHUNK
        rows = pl.ds(r0, CHUNK)
        heads = (2 * pair, 2 * pair + 1)
        if sliding:
            band = pl.ds(r0, 2 * CHUNK)
            qp = qs_scr[rows, pl.ds(pair * LANES, LANES)]
            keys = [ktvar_scr[hd, :, band] for hd in heads]
            bias_sel = first_tile if c == 0 else 0
            logits = [_dot(qp, kt) + bias_scr[bias_sel, hd] for hd, kt in zip(heads, keys)]
        else:
            qp = qs_scr[rows, pl.ds(SWA_WIDTH + pair * LANES, LANES)]
            logits = [_dot(qp, mk_ref[0, hd]) for hd in heads]
        yield
        o_pair = None
        for hd, s in zip(heads, logits):
            m = jnp.max(s, axis=-1, keepdims=True)
            if sliding:
                sink = sinks_ref[hd] * LOG2E
                m = jnp.maximum(m, sink)
            p = jnp.exp2(s - m)
            denom = jnp.sum(p, axis=-1, keepdims=True)
            if sliding:
                denom = denom + jnp.exp2(sink - m)
                values = vvar_scr[hd, band, :]
            else:
                values = mv_ref[0, hd]
            o = _dot(p.astype(BF16), values) * (1.0 / denom)
            o_pair = o if o_pair is None else o_pair + o
        off = A_WIDTH if sliding else A_WIDTH + SWA_WIDTH
        y_scr[rows, pl.ds(off + pair * LANES, LANES)] = o_pair

    def head_items(r0, nrows):
        for c in range(r0 // CHUNK, (r0 + nrows) // CHUNK):
            for pair in range(SWA_HEADS // 2):
                yield attn_item(c, pair, True)
            for pair in range(MEM_HEADS // 2):
                yield attn_item(c, pair, False)
            for g in range(A_GROUPS):
                yield gmlp_item(c, g)

    def project_out(r0, nrows):
        rows = pl.ds(r0, nrows)
        ybf_scr[rows, :] = (y_scr[rows, :] * zg_scr[rows, :]).astype(BF16)
        for c0 in range(0, D_MODEL, PROJ_COLS):
            o_scr[rows, c0:c0 + PROJ_COLS] = _dot(ybf_scr[rows, :], wout_ref[:, c0:c0 + PROJ_COLS])
            yield
        for rc in range(nrows // CHUNK):
            crows = pl.ds(r0 + rc * CHUNK, CHUNK)
            o = o_scr[crows, :]
            ms = jnp.mean(o * o, axis=-1, keepdims=True)
            out_ref[0, crows, :] = x_ref[0, crows, :] + o * lax.rsqrt(ms + EPS) * postg_ref[...]
            yield

    assert sum(SUB_TILES) == ts
    starts = [sum(SUB_TILES[:s]) for s in range(len(SUB_TILES))]
    subs = list(zip(starts, SUB_TILES))
    n_sub = len(subs)
    pre_norm(*subs[0])
    filler = project(*subs[0])
    for s in range(n_sub):
        if s + 1 < n_sub:
            pre_norm(*subs[s + 1])
        for _ in filler:
            pass
        heads = _skewed(head_items(*subs[s]), ITEM_LAG)
        n_filler = 0
        if s + 1 < n_sub:
            filler = project(*subs[s + 1])
            n_filler += PROJECT_PIECES
        else:
            filler = iter(())
        if s >= 1:
            filler = _chain(project_out(*subs[s - 1]), filler)
            n_filler += PROJECT_OUT_DOTS + subs[s - 1][1] // CHUNK
        _interleave(heads, filler, n_filler / ((subs[s][1] // CHUNK) * HEAD_STAGES_PER_CHUNK))
        filler = iter(())
    for _ in project_out(*subs[-1]):
        pass


def _layer(x, mk_var, mv_var, w_in_bf16, w_out_bf16, post_g, vng, vnb, w_spatial, bs_tbl,
           buckets, sinks, rel_bias):
    batch, seq, _ = x.shape
    ts = SEQ_TILE
    n_tiles = seq // ts
    const2 = lambda b, t, *_: (0, 0)
    const3 = lambda b, t, *_: (0, 0, 0)
    grid_spec = pltpu.PrefetchScalarGridSpec(
        num_scalar_prefetch=2,
        grid=(batch, n_tiles),
        in_specs=[
            pl.BlockSpec((1, ts, D_MODEL), lambda b, t, *_: (b, t, 0)),
            pl.BlockSpec((1, MEM_HEADS, LANES, MEM_LEN), lambda b, t, *_: (b, 0, 0, 0)),
            pl.BlockSpec((1, MEM_HEADS, MEM_LEN, LANES), lambda b, t, *_: (b, 0, 0, 0)),
            pl.BlockSpec((D_MODEL, IN_WIDTH), const2),
            pl.BlockSpec((MIX_WIDTH, D_MODEL), const2),
            pl.BlockSpec((1, D_MODEL), const2),
            pl.BlockSpec((1, A_WIDTH), const2),
            pl.BlockSpec((1, A_WIDTH), const2),
            pl.BlockSpec((A_GROUPS, CHUNK, CHUNK), const3),
            pl.BlockSpec((A_GROUPS, CHUNK, LANES), const3),
            pl.BlockSpec((CHUNK, 2 * CHUNK), const2),
        ],
        out_specs=pl.BlockSpec((1, ts, D_MODEL), lambda b, t, *_: (b, t, 0)),
        scratch_shapes=[
            pltpu.VMEM((ts, D_MODEL), BF16),
            pltpu.VMEM((ts, A_WIDTH), F32),
            pltpu.VMEM((ts, A_WIDTH), BF16),
            pltpu.VMEM((ts, SWA_WIDTH + MEM_WIDTH), BF16),
            pltpu.VMEM((SWA_HEADS, LANES, CHUNK + ts), BF16),
            pltpu.VMEM((SWA_HEADS, CHUNK + ts, LANES), BF16),
            pltpu.VMEM((ts, MIX_WIDTH), F32),
            pltpu.VMEM((2, SWA_HEADS, CHUNK, 2 * CHUNK), F32),
            pltpu.VMEM((A_GROUPS, CHUNK, CHUNK), BF16),
            pltpu.VMEM((ts, MIX_WIDTH), F32),
            pltpu.VMEM((ts, MIX_WIDTH), BF16),
            pltpu.VMEM((ts, D_MODEL), F32),
        ],
    )
    return pl.pallas_call(
        _layer_kernel,
        grid_spec=grid_spec,
        out_shape=jax.ShapeDtypeStruct(x.shape, x.dtype),
        compiler_params=pltpu.CompilerParams(
            dimension_semantics=("arbitrary", "arbitrary"),
            vmem_limit_bytes=VMEM_LIMIT_BYTES),
        name="layer",
    )(sinks, rel_bias, x, mk_var, mv_var, w_in_bf16, w_out_bf16, post_g, vng, vnb,
      w_spatial, bs_tbl, buckets)


def kernel(x, mem, pre_norm_g, post_norm_g, mem_norm_g, w_in, w_mem_kv, v_norm_g, v_norm_b,
           w_spatial, b_spatial, attn_sinks, rel_bias, w_out):
    depth = w_in.shape[0]
    buckets = jnp.asarray(_band_buckets())
    for layer in range(depth):
        w_kv = (w_mem_kv[layer] * mem_norm_g[layer][:, None]).astype(BF16)
        w_in_g = (w_in[layer] * pre_norm_g[layer][:, None]).astype(BF16)
        mk_var, mv_var = _mem_kv(mem, w_kv)
        bs_tbl = jnp.broadcast_to(b_spatial[layer][:, :, None], (A_GROUPS, CHUNK, LANES))
        x = _layer(x, mk_var, mv_var, w_in_g, w_out[layer].astype(BF16),
                   post_norm_g[layer][None, :],
                   v_norm_g[layer][None, :], v_norm_b[layer][None, :],
                   w_spatial[layer], bs_tbl, buckets, attn_sinks[layer], rel_bias)
    return x
```

```python
import numpy as np
import jax
import jax.numpy as jnp
from jax import lax
from jax.experimental import pallas as pl
from jax.experimental.pallas import tpu as pltpu

D_MODEL = 1024
MEM_LEN = 256
HEAD_DIM = 64
CHUNK = 128
A_GROUPS = 4
A_WIDTH = 512
SWA_HEADS = 4
SWA_WIDTH = 256
SWA_KV_WIDTH = 128
MEM_HEADS = 4
MEM_WIDTH = 256
MIX_WIDTH = 1024
IN_WIDTH = 2816
N_BUCKETS = 32
MAX_DISTANCE = 128
WINDOW = 128
EPS = 1e-6
NEG = -1e30
LOG2E = float(np.log2(np.e))
Q_SCALE = HEAD_DIM ** -0.5 * LOG2E

OFF_U = 0
OFF_V = A_WIDTH
OFF_SQ = 2 * A_WIDTH
OFF_SK = OFF_SQ + SWA_WIDTH
OFF_SV = OFF_SK + SWA_KV_WIDTH
OFF_MQ = OFF_SV + SWA_KV_WIDTH
OFF_Z = OFF_MQ + MEM_WIDTH

LANES = 128
SEQ_TILE = 1024
SUB_TILES = (256, 256, 256, 256)
PROJ_COLS = 512
ITEM_LAG = 2
PROJECT_PIECES = 6
PROJECT_OUT_DOTS = D_MODEL // PROJ_COLS
HEAD_STAGES_PER_CHUNK = 2 * (SWA_HEADS // 2 + MEM_HEADS // 2) + A_GROUPS
VMEM_LIMIT_BYTES = 56 * 1024 * 1024

BF16 = jnp.bfloat16
F32 = jnp.float32


def _t5_causal_buckets(dist):
    n = np.maximum(dist, 0)
    max_exact = N_BUCKETS // 2
    large = max_exact + (np.log(np.maximum(n, 1) / max_exact) / np.log(MAX_DISTANCE / max_exact)
                         * (N_BUCKETS - max_exact)).astype(np.int32)
    large = np.minimum(large, N_BUCKETS - 1)
    return np.where(n < max_exact, n, large).astype(np.int32)


def _band_buckets():
    qi = np.arange(CHUNK)[:, None]
    kj = np.arange(2 * CHUNK)[None, :]
    dist = qi + CHUNK - kj
    valid = (dist >= 0) & (dist < WINDOW)
    return np.where(valid, _t5_causal_buckets(dist), -1).astype(np.int32)


def _gelu_tanh(x):
    c = np.float32(np.sqrt(2.0 / np.pi))
    ck = np.float32(np.sqrt(2.0 / np.pi) * 0.044715)
    hx = 0.5 * x
    return hx + hx * jnp.tanh(x * (c + ck * (x * x)))


def _silu(z):
    hz = 0.5 * z
    return hz + hz * jnp.tanh(hz)


def _dot(a, b):
    return jnp.dot(a, b, preferred_element_type=F32)


_DONE = object()


def _skewed(items, lag):
    waiting = []
    for item in items:
        if next(item, _DONE) is not _DONE:
            waiting.append(item)
        yield
        if len(waiting) > lag:
            next(waiting.pop(0), _DONE)
            yield
    for item in waiting:
        next(item, _DONE)
        yield


def _chain(*gens):
    for g in gens:
        yield from g


def _interleave(a, b, b_per_a):
    credit = 0.0
    a_live = b_live = True
    while a_live or b_live:
        if a_live:
            a_live = next(a, _DONE) is not _DONE
        credit += b_per_a
        while b_live and (credit >= 1.0 or not a_live):
            b_live = next(b, _DONE) is not _DONE
            credit -= 1.0


def _lane_half_mask(parity):
    lane = lax.broadcasted_iota(jnp.int32, (1, LANES), 1)
    return (lane >= HEAD_DIM) if parity else (lane < HEAD_DIM)


def _place_head_rows(kt_head, parity):
    zeros = jnp.zeros_like(kt_head)
    return jnp.concatenate([zeros, kt_head] if parity else [kt_head, zeros], axis=0)


def _layer_kernel(sinks_ref, relb_ref,
                  x_ref, mem_ref, win_ref, wkv_ref, wout_ref, preg_ref, memg_ref, postg_ref,
                  vng_ref, vnb_ref, ws_ref, bs_ref, buckets_ref,
                  out_ref,
                  h_scr, gu_scr, vn_scr, qs_scr, ktvar_scr, vvar_scr, zg_scr,
                  mk_scr, mv_scr, bias_scr, wsm_scr, bst_scr, y_scr, ybf_scr, o_scr):
    ts = x_ref.shape[1]
    b = pl.program_id(0)
    t = pl.program_id(1)

    @pl.when((b == 0) & (t == 0))
    def _init():
        row = lax.broadcasted_iota(jnp.int32, (CHUNK, CHUNK), 0)
        col = lax.broadcasted_iota(jnp.int32, (CHUNK, CHUNK), 1)
        for g in range(A_GROUPS):
            wsm_scr[g] = jnp.where(row >= col, ws_ref[g], 0.0).astype(BF16)
            bs_col = jnp.sum(jnp.where(row == col, bs_ref[g:g + 1, :], 0.0), axis=1, keepdims=True)
            bst_scr[g] = jnp.broadcast_to(bs_col, (CHUNK, LANES))
        buckets = buckets_ref[...]
        prev_key = lax.broadcasted_iota(jnp.int32, (CHUNK, 2 * CHUNK), 1) < CHUNK
        for hd in range(SWA_HEADS):
            tbl = jnp.full((CHUNK, 2 * CHUNK), NEG, F32)
            for bk in range(N_BUCKETS):
                tbl = jnp.where(buckets == bk, relb_ref[bk, hd] * LOG2E, tbl)
            bias_scr[0, hd] = tbl
            bias_scr[1, hd] = jnp.where(prev_key, NEG, tbl)

    @pl.when(t == 0)
    def _sequence_start():
        ktvar_scr[:, :, 0:CHUNK] = jnp.zeros((SWA_HEADS, LANES, CHUNK), BF16)
        vvar_scr[:, 0:CHUNK, :] = jnp.zeros((SWA_HEADS, CHUNK, LANES), BF16)
        m = mem_ref[0]
        ms = jnp.mean(m * m, axis=-1, keepdims=True)
        hm = (m * lax.rsqrt(ms + EPS) * memg_ref[...]).astype(BF16)
        kv = _dot(hm, wkv_ref[...])
        for pair in range(MEM_HEADS // 2):
            kt_pair = kv[:, pair * LANES:(pair + 1) * LANES].T
            v_pair = kv[:, MEM_WIDTH + pair * LANES:MEM_WIDTH + (pair + 1) * LANES]
            for parity in range(2):
                hd = 2 * pair + parity
                kt_head = kt_pair[parity * HEAD_DIM:(parity + 1) * HEAD_DIM]
                mk_scr[hd] = _place_head_rows(kt_head, parity).astype(BF16)
                mv_scr[hd] = jnp.where(_lane_half_mask(parity), v_pair, 0.0).astype(BF16)


    @pl.when(t > 0)
    def _copy_carry():
        ktvar_scr[:, :, 0:CHUNK] = ktvar_scr[:, :, ts:ts + CHUNK]
        vvar_scr[:, 0:CHUNK, :] = vvar_scr[:, ts:ts + CHUNK, :]

    lo = _lane_half_mask(0)
    hi = _lane_half_mask(1)
    first_tile = jnp.where(t == 0, 1, 0)

    def pre_norm(r0, nrows):
        for rc in range(nrows // CHUNK):
            crows = pl.ds(r0 + rc * CHUNK, CHUNK)
            x = x_ref[0, crows, :]
            ms = jnp.mean(x * x, axis=-1, keepdims=True)
            h_scr[crows, :] = (x * lax.rsqrt(ms + EPS) * preg_ref[...]).astype(BF16)

    def proj(rows, c0, width):
        return _dot(h_scr[rows, :], win_ref[:, c0:c0 + width])

    def project(r0, nrows):
        rows = pl.ds(r0, nrows)

        gu_scr[rows, :] = _gelu_tanh(proj(rows, OFF_U, A_WIDTH))
        yield

        v_all = _gelu_tanh(proj(rows, OFF_V, A_WIDTH))
        for g in range(A_GROUPS):
            cols = pl.ds(g * LANES, LANES)
            v = v_all[:, g * LANES:(g + 1) * LANES]
            mu = jnp.mean(v, axis=-1, keepdims=True)
            d = v - mu
            var = jnp.mean(d * d, axis=-1, keepdims=True)
            vn = d * lax.rsqrt(var + EPS) * vng_ref[:, cols] + vnb_ref[:, cols]
            vn_scr[rows, cols] = vn.astype(BF16)
        yield

        qkv = proj(rows, OFF_SQ, SWA_WIDTH + 2 * SWA_KV_WIDTH)
        qs_scr[rows, 0:SWA_WIDTH] = (qkv[:, 0:SWA_WIDTH] * Q_SCALE).astype(BF16)
        skt = qkv[:, SWA_WIDTH:SWA_WIDTH + LANES].T
        sv = qkv[:, SWA_WIDTH + LANES:]
        sv_rot = pltpu.roll(sv, HEAD_DIM, 1)
        new_rows = pl.ds(CHUNK + r0, nrows)
        for hd, (keep, vsrc) in enumerate([(lo, sv), (hi, sv_rot), (lo, sv_rot), (hi, sv)]):
            kv_head = hd // 2
            kt_head = skt[kv_head * HEAD_DIM:(kv_head + 1) * HEAD_DIM]
            ktvar_scr[hd, :, new_rows] = _place_head_rows(kt_head, hd % 2).astype(BF16)
            vvar_scr[hd, new_rows, :] = jnp.where(keep, vsrc, 0.0).astype(BF16)
        yield

        mqz = proj(rows, OFF_MQ, PROJ_COLS)
        qs_scr[rows, SWA_WIDTH:] = (mqz[:, 0:MEM_WIDTH] * Q_SCALE).astype(BF16)
        zg_scr[rows, 0:PROJ_COLS - MEM_WIDTH] = _silu(mqz[:, MEM_WIDTH:])
        yield

        done = PROJ_COLS - MEM_WIDTH
        while done < MIX_WIDTH:
            width = min(PROJ_COLS, MIX_WIDTH - done)
            zg_scr[rows, done:done + width] = _silu(proj(rows, OFF_Z + done, width))
            done += width
            yield

    def gmlp_item(c, g):
        rows = pl.ds(c * CHUNK, CHUNK)
        cols = pl.ds(g * LANES, LANES)
        sv_g = _dot(wsm_scr[g], vn_scr[rows, cols]) + bst_scr[g]
        y_scr[rows, cols] = gu_scr[rows, cols] * sv_g
        return
        yield

    def attn_item(c, pair, sliding):
        r0 = c * CHUNK
        rows = pl.ds(r0, CHUNK)
        heads = (2 * pair, 2 * pair + 1)
        if sliding:
            band = pl.ds(r0, 2 * CHUNK)
            qp = qs_scr[rows, pl.ds(pair * LANES, LANES)]
            keys = [ktvar_scr[hd, :, band] for hd in heads]
            bias_sel = first_tile if c == 0 else 0
            logits = [_dot(qp, kt) + bias_scr[bias_sel, hd] for hd, kt in zip(heads, keys)]
        else:
            qp = qs_scr[rows, pl.ds(SWA_WIDTH + pair * LANES, LANES)]
            logits = [_dot(qp, mk_scr[hd]) for hd in heads]
        yield
        o_pair = None
        for hd, s in zip(heads, logits):
            m = jnp.max(s, axis=-1, keepdims=True)
            if sliding:
                sink = sinks_ref[hd] * LOG2E
                m = jnp.maximum(m, sink)
            p = jnp.exp2(s - m)
            denom = jnp.sum(p, axis=-1, keepdims=True)
            if sliding:
                denom = denom + jnp.exp2(sink - m)
                values = vvar_scr[hd, band, :]
            else:
                values = mv_scr[hd]
            o = _dot(p.astype(BF16), values) * (1.0 / denom)
            o_pair = o if o_pair is None else o_pair + o
        off = A_WIDTH if sliding else A_WIDTH + SWA_WIDTH
        y_scr[rows, pl.ds(off + pair * LANES, LANES)] = o_pair

    def head_items(r0, nrows):
        for c in range(r0 // CHUNK, (r0 + nrows) // CHUNK):
            for pair in range(SWA_HEADS // 2):
                yield attn_item(c, pair, True)
            for pair in range(MEM_HEADS // 2):
                yield attn_item(c, pair, False)
            for g in range(A_GROUPS):
                yield gmlp_item(c, g)

    def project_out(r0, nrows):
        rows = pl.ds(r0, nrows)
        ybf_scr[rows, :] = (y_scr[rows, :] * zg_scr[rows, :]).astype(BF16)
        for c0 in range(0, D_MODEL, PROJ_COLS):
            o_scr[rows, c0:c0 + PROJ_COLS] = _dot(ybf_scr[rows, :], wout_ref[:, c0:c0 + PROJ_COLS])
            yield
        for rc in range(nrows // CHUNK):
            crows = pl.ds(r0 + rc * CHUNK, CHUNK)
            o = o_scr[crows, :]
            ms = jnp.mean(o * o, axis=-1, keepdims=True)
            out_ref[0, crows, :] = x_ref[0, crows, :] + o * lax.rsqrt(ms + EPS) * postg_ref[...]
            yield

    assert sum(SUB_TILES) == ts
    starts = [sum(SUB_TILES[:s]) for s in range(len(SUB_TILES))]
    subs = list(zip(starts, SUB_TILES))
    n_sub = len(subs)
    pre_norm(*subs[0])
    filler = project(*subs[0])
    for s in range(n_sub):
        if s + 1 < n_sub:
            pre_norm(*subs[s + 1])
        for _ in filler:
            pass
        heads = _skewed(head_items(*subs[s]), ITEM_LAG)
        n_filler = 0
        if s + 1 < n_sub:
            filler = project(*subs[s + 1])
            n_filler += PROJECT_PIECES
        else:
            filler = iter(())
        if s >= 1:
            filler = _chain(project_out(*subs[s - 1]), filler)
            n_filler += PROJECT_OUT_DOTS + subs[s - 1][1] // CHUNK
        _interleave(heads, filler, n_filler / ((subs[s][1] // CHUNK) * HEAD_STAGES_PER_CHUNK))
        filler = iter(())
    for _ in project_out(*subs[-1]):
        pass


def _layer(x, mem, w_in_bf16, w_kv_bf16, w_out_bf16, pre_g, mem_g, post_g, vng, vnb, w_spatial,
           b_spatial, buckets, sinks, rel_bias):
    batch, seq, _ = x.shape
    ts = SEQ_TILE
    n_tiles = seq // ts
    const2 = lambda b, t, *_: (0, 0)
    const3 = lambda b, t, *_: (0, 0, 0)
    grid_spec = pltpu.PrefetchScalarGridSpec(
        num_scalar_prefetch=2,
        grid=(batch, n_tiles),
        in_specs=[
            pl.BlockSpec((1, ts, D_MODEL), lambda b, t, *_: (b, t, 0)),
            pl.BlockSpec((1, MEM_LEN, D_MODEL), lambda b, t, *_: (b, 0, 0)),
            pl.BlockSpec((D_MODEL, IN_WIDTH), const2),
            pl.BlockSpec((D_MODEL, 2 * MEM_WIDTH), const2),
            pl.BlockSpec((MIX_WIDTH, D_MODEL), const2),
            pl.BlockSpec((1, D_MODEL), const2),
            pl.BlockSpec((1, D_MODEL), const2),
            pl.BlockSpec((1, D_MODEL), const2),
            pl.BlockSpec((1, A_WIDTH), const2),
            pl.BlockSpec((1, A_WIDTH), const2),
            pl.BlockSpec((A_GROUPS, CHUNK, CHUNK), const3),
            pl.BlockSpec((A_GROUPS, CHUNK), const2),
            pl.BlockSpec((CHUNK, 2 * CHUNK), const2),
        ],
        out_specs=pl.BlockSpec((1, ts, D_MODEL), lambda b, t, *_: (b, t, 0)),
        scratch_shapes=[
            pltpu.VMEM((ts, D_MODEL), BF16),
            pltpu.VMEM((ts, A_WIDTH), F32),
            pltpu.VMEM((ts, A_WIDTH), BF16),
            pltpu.VMEM((ts, SWA_WIDTH + MEM_WIDTH), BF16),
            pltpu.VMEM((SWA_HEADS, LANES, CHUNK + ts), BF16),
            pltpu.VMEM((SWA_HEADS, CHUNK + ts, LANES), BF16),
            pltpu.VMEM((ts, MIX_WIDTH), F32),
            pltpu.VMEM((MEM_HEADS, LANES, MEM_LEN), BF16),
            pltpu.VMEM((MEM_HEADS, MEM_LEN, LANES), BF16),
            pltpu.VMEM((2, SWA_HEADS, CHUNK, 2 * CHUNK), F32),
            pltpu.VMEM((A_GROUPS, CHUNK, CHUNK), BF16),
            pltpu.VMEM((A_GROUPS, CHUNK, LANES), F32),
            pltpu.VMEM((ts, MIX_WIDTH), F32),
            pltpu.VMEM((ts, MIX_WIDTH), BF16),
            pltpu.VMEM((ts, D_MODEL), F32),
        ],
    )
    return pl.pallas_call(
        _layer_kernel,
        grid_spec=grid_spec,
        out_shape=jax.ShapeDtypeStruct(x.shape, x.dtype),
        compiler_params=pltpu.CompilerParams(
            dimension_semantics=("arbitrary", "arbitrary"),
            vmem_limit_bytes=VMEM_LIMIT_BYTES),
        name="layer",
    )(sinks, rel_bias, x, mem, w_in_bf16, w_kv_bf16, w_out_bf16, pre_g, mem_g, post_g, vng, vnb,
      w_spatial, b_spatial, buckets)


def kernel(x, mem, pre_norm_g, post_norm_g, mem_norm_g, w_in, w_mem_kv, v_norm_g, v_norm_b,
           w_spatial, b_spatial, attn_sinks, rel_bias, w_out):
    depth = w_in.shape[0]
    buckets = jnp.asarray(_band_buckets())
    for layer in range(depth):
        x = _layer(x, mem, w_in[layer].astype(BF16), w_mem_kv[layer].astype(BF16),
                   w_out[layer].astype(BF16),
                   pre_norm_g[layer][None, :], mem_norm_g[layer][None, :], post_norm_g[layer][None, :],
                   v_norm_g[layer][None, :], v_norm_b[layer][None, :],
                   w_spatial[layer], b_spatial[layer], buckets, attn_sinks[layer], rel_bias)
    return x
```

```python
import numpy as np
import jax
import jax.numpy as jnp
from jax import lax
from jax.experimental import pallas as pl
from jax.experimental.pallas import tpu as pltpu

D_MODEL = 1024
MEM_LEN = 256
HEAD_DIM = 64
CHUNK = 128
A_GROUPS = 4
A_WIDTH = 512
SWA_HEADS = 4
SWA_WIDTH = 256
SWA_KV_WIDTH = 128
MEM_HEADS = 4
MEM_WIDTH = 256
MIX_WIDTH = 1024
IN_WIDTH = 2816
N_BUCKETS = 32
MAX_DISTANCE = 128
WINDOW = 128
EPS = 1e-6
NEG = -1e30
LOG2E = float(np.log2(np.e))
Q_SCALE = HEAD_DIM ** -0.5 * LOG2E

OFF_U = 0
OFF_V = A_WIDTH
OFF_SQ = 2 * A_WIDTH
OFF_SK = OFF_SQ + SWA_WIDTH
OFF_SV = OFF_SK + SWA_KV_WIDTH
OFF_MQ = OFF_SV + SWA_KV_WIDTH
OFF_Z = OFF_MQ + MEM_WIDTH

LANES = 128
SEQ_TILE = 1024
SUB_TILES = (256, 256, 256, 256)
PROJ_COLS = 512
ITEM_LAG = 2
PROJECT_PIECES = 6
PROJECT_OUT_DOTS = D_MODEL // PROJ_COLS
HEAD_STAGES_PER_CHUNK = 2 * (SWA_HEADS // 2 + MEM_HEADS // 2) + A_GROUPS
VMEM_LIMIT_BYTES = 56 * 1024 * 1024

BF16 = jnp.bfloat16
F32 = jnp.float32


def _t5_causal_buckets(dist):
    n = np.maximum(dist, 0)
    max_exact = N_BUCKETS // 2
    large = max_exact + (np.log(np.maximum(n, 1) / max_exact) / np.log(MAX_DISTANCE / max_exact)
                         * (N_BUCKETS - max_exact)).astype(np.int32)
    large = np.minimum(large, N_BUCKETS - 1)
    return np.where(n < max_exact, n, large).astype(np.int32)


def _band_buckets():
    qi = np.arange(CHUNK)[:, None]
    kj = np.arange(2 * CHUNK)[None, :]
    dist = qi + CHUNK - kj
    valid = (dist >= 0) & (dist < WINDOW)
    return np.where(valid, _t5_causal_buckets(dist), -1).astype(np.int32)


def _gelu_tanh(x):
    c = np.float32(np.sqrt(2.0 / np.pi))
    ck = np.float32(np.sqrt(2.0 / np.pi) * 0.044715)
    hx = 0.5 * x
    return hx + hx * jnp.tanh(x * (c + ck * (x * x)))


def _silu(z):
    hz = 0.5 * z
    return hz + hz * jnp.tanh(hz)


def _dot(a, b):
    return jnp.dot(a, b, preferred_element_type=F32)


_DONE = object()


def _skewed(items, lag):
    waiting = []
    for item in items:
        if next(item, _DONE) is not _DONE:
            waiting.append(item)
        yield
        if len(waiting) > lag:
            next(waiting.pop(0), _DONE)
            yield
    for item in waiting:
        next(item, _DONE)
        yield


def _chain(*gens):
    for g in gens:
        yield from g


def _interleave(a, b, b_per_a):
    credit = 0.0
    a_live = b_live = True
    while a_live or b_live:
        if a_live:
            a_live = next(a, _DONE) is not _DONE
        credit += b_per_a
        while b_live and (credit >= 1.0 or not a_live):
            b_live = next(b, _DONE) is not _DONE
            credit -= 1.0


def _lane_half_mask(parity):
    lane = lax.broadcasted_iota(jnp.int32, (1, LANES), 1)
    return (lane >= HEAD_DIM) if parity else (lane < HEAD_DIM)


def _place_head_rows(kt_head, parity):
    zeros = jnp.zeros_like(kt_head)
    return jnp.concatenate([zeros, kt_head] if parity else [kt_head, zeros], axis=0)


def _layer_kernel(sinks_ref, relb_ref,
                  x_ref, mem_ref, win_ref, wkv_ref, wout_ref, preg_ref, memg_ref, postg_ref,
                  vng_ref, vnb_ref, ws_ref, bs_ref, buckets_ref,
                  out_ref,
                  h_scr, gu_scr, vn_scr, qs_scr, ktvar_scr, vvar_scr, zg_scr,
                  mk_scr, mv_scr, bias_scr, wsm_scr, bst_scr, y_scr, ybf_scr, o_scr):
    ts = x_ref.shape[1]
    b = pl.program_id(0)
    t = pl.program_id(1)

    @pl.when((b == 0) & (t == 0))
    def _init():
        row = lax.broadcasted_iota(jnp.int32, (CHUNK, CHUNK), 0)
        col = lax.broadcasted_iota(jnp.int32, (CHUNK, CHUNK), 1)
        for g in range(A_GROUPS):
            wsm_scr[g] = jnp.where(row >= col, ws_ref[g], 0.0).astype(BF16)
            bs_col = jnp.sum(jnp.where(row == col, bs_ref[g:g + 1, :], 0.0), axis=1, keepdims=True)
            bst_scr[g] = jnp.broadcast_to(bs_col, (CHUNK, LANES))
        buckets = buckets_ref[...]
        prev_key = lax.broadcasted_iota(jnp.int32, (CHUNK, 2 * CHUNK), 1) < CHUNK
        for hd in range(SWA_HEADS):
            tbl = jnp.full((CHUNK, 2 * CHUNK), NEG, F32)
            for bk in range(N_BUCKETS):
                tbl = jnp.where(buckets == bk, relb_ref[bk, hd] * LOG2E, tbl)
            bias_scr[0, hd] = tbl
            bias_scr[1, hd] = jnp.where(prev_key, NEG, tbl)

    @pl.when(t == 0)
    def _sequence_start():
        ktvar_scr[:, :, 0:CHUNK] = jnp.zeros((SWA_HEADS, LANES, CHUNK), BF16)
        vvar_scr[:, 0:CHUNK, :] = jnp.zeros((SWA_HEADS, CHUNK, LANES), BF16)
        m = mem_ref[0]
        ms = jnp.mean(m * m, axis=-1, keepdims=True)
        hm = (m * lax.rsqrt(ms + EPS) * memg_ref[...]).astype(BF16)
        kv = _dot(hm, wkv_ref[...])
        for pair in range(MEM_HEADS // 2):
            kt_pair = kv[:, pair * LANES:(pair + 1) * LANES].T
            v_pair = kv[:, MEM_WIDTH + pair * LANES:MEM_WIDTH + (pair + 1) * LANES]
            for parity in range(2):
                hd = 2 * pair + parity
                kt_head = kt_pair[parity * HEAD_DIM:(parity + 1) * HEAD_DIM]
                mk_scr[hd] = _place_head_rows(kt_head, parity).astype(BF16)
                mv_scr[hd] = jnp.where(_lane_half_mask(parity), v_pair, 0.0).astype(BF16)


    @pl.when(t > 0)
    def _copy_carry():
        ktvar_scr[:, :, 0:CHUNK] = ktvar_scr[:, :, ts:ts + CHUNK]
        vvar_scr[:, 0:CHUNK, :] = vvar_scr[:, ts:ts + CHUNK, :]

    lo = _lane_half_mask(0)
    hi = _lane_half_mask(1)
    first_tile = jnp.where(t == 0, 1, 0)

    def pre_norm(r0, nrows):
        for rc in range(nrows // CHUNK):
            crows = pl.ds(r0 + rc * CHUNK, CHUNK)
            x = x_ref[0, crows, :]
            ms = jnp.mean(x * x, axis=-1, keepdims=True)
            h_scr[crows, :] = (x * lax.rsqrt(ms + EPS) * preg_ref[...]).astype(BF16)

    def proj(rows, c0, width):
        return _dot(h_scr[rows, :], win_ref[:, c0:c0 + width])

    def project(r0, nrows):
        rows = pl.ds(r0, nrows)

        gu_scr[rows, :] = _gelu_tanh(proj(rows, OFF_U, A_WIDTH))
        yield

        v_all = _gelu_tanh(proj(rows, OFF_V, A_WIDTH))
        for g in range(A_GROUPS):
            cols = pl.ds(g * LANES, LANES)
            v = v_all[:, g * LANES:(g + 1) * LANES]
            mu = jnp.mean(v, axis=-1, keepdims=True)
            d = v - mu
            var = jnp.mean(d * d, axis=-1, keepdims=True)
            vn = d * lax.rsqrt(var + EPS) * vng_ref[:, cols] + vnb_ref[:, cols]
            vn_scr[rows, cols] = vn.astype(BF16)
        yield

        qkv = proj(rows, OFF_SQ, SWA_WIDTH + 2 * SWA_KV_WIDTH)
        qs_scr[rows, 0:SWA_WIDTH] = (qkv[:, 0:SWA_WIDTH] * Q_SCALE).astype(BF16)
        skt = qkv[:, SWA_WIDTH:SWA_WIDTH + LANES].T
        sv = qkv[:, SWA_WIDTH + LANES:]
        sv_rot = pltpu.roll(sv, HEAD_DIM, 1)
        new_rows = pl.ds(CHUNK + r0, nrows)
        for hd, (keep, vsrc) in enumerate([(lo, sv), (hi, sv_rot), (lo, sv_rot), (hi, sv)]):
            kv_head = hd // 2
            kt_head = skt[kv_head * HEAD_DIM:(kv_head + 1) * HEAD_DIM]
            ktvar_scr[hd, :, new_rows] = _place_head_rows(kt_head, hd % 2).astype(BF16)
            vvar_scr[hd, new_rows, :] = jnp.where(keep, vsrc, 0.0).astype(BF16)
        yield

        mqz = proj(rows, OFF_MQ, PROJ_COLS)
        qs_scr[rows, SWA_WIDTH:] = (mqz[:, 0:MEM_WIDTH] * Q_SCALE).astype(BF16)
        zg_scr[rows, 0:PROJ_COLS - MEM_WIDTH] = _silu(mqz[:, MEM_WIDTH:])
        yield

        done = PROJ_COLS - MEM_WIDTH
        while done < MIX_WIDTH:
            width = min(PROJ_COLS, MIX_WIDTH - done)
            zg_scr[rows, done:done + width] = _silu(proj(rows, OFF_Z + done, width))
            done += width
            yield

    def gmlp_item(c, g):
        rows = pl.ds(c * CHUNK, CHUNK)
        cols = pl.ds(g * LANES, LANES)
        sv_g = _dot(wsm_scr[g], vn_scr[rows, cols]) + bst_scr[g]
        y_scr[rows, cols] = gu_scr[rows, cols] * sv_g
        return
        yield

    def attn_item(c, pair, sliding):
        r0 = c * CHUNK
        rows = pl.ds(r0, CHUNK)
        heads = (2 * pair, 2 * pair + 1)
        if sliding:
            band = pl.ds(r0, 2 * CHUNK)
            qp = qs_scr[rows, pl.ds(pair * LANES, LANES)]
            keys = [ktvar_scr[hd, :, band] for hd in heads]
            bias_sel = first_tile if c == 0 else 0
            logits = [_dot(qp, kt) + bias_scr[bias_sel, hd] for hd, kt in zip(heads, keys)]
        else:
            qp = qs_scr[rows, pl.ds(SWA_WIDTH + pair * LANES, LANES)]
            logits = [_dot(qp, mk_scr[hd]) for hd in heads]
        yield
        o_pair = None
        for hd, s in zip(heads, logits):
            m = jnp.max(s, axis=-1, keepdims=True)
            if sliding:
                sink = sinks_ref[hd] * LOG2E
                m = jnp.maximum(m, sink)
            p = jnp.exp2(s - m)
            denom = jnp.sum(p, axis=-1, keepdims=True)
            if sliding:
                denom = denom + jnp.exp2(sink - m)
                values = vvar_scr[hd, band, :]
            else:
                values = mv_scr[hd]
            o = _dot(p.astype(BF16), values) * (1.0 / denom)
            o_pair = o if o_pair is None else o_pair + o
        off = A_WIDTH if sliding else A_WIDTH + SWA_WIDTH
        y_scr[rows, pl.ds(off + pair * LANES, LANES)] = o_pair

    def head_items(r0, nrows):
        for c in range(r0 // CHUNK, (r0 + nrows) // CHUNK):
            for pair in range(SWA_HEADS // 2):
                yield attn_item(c, pair, True)
            for pair in range(MEM_HEADS // 2):
                yield attn_item(c, pair, False)
            for g in range(A_GROUPS):
                yield gmlp_item(c, g)

    def project_out(r0, nrows):
        rows = pl.ds(r0, nrows)
        ybf_scr[rows, :] = (y_scr[rows, :] * zg_scr[rows, :]).astype(BF16)
        for c0 in range(0, D_MODEL, PROJ_COLS):
            o_scr[rows, c0:c0 + PROJ_COLS] = _dot(ybf_scr[rows, :], wout_ref[:, c0:c0 + PROJ_COLS])
            yield
        for rc in range(nrows // CHUNK):
            crows = pl.ds(r0 + rc * CHUNK, CHUNK)
            o = o_scr[crows, :]
            ms = jnp.mean(o * o, axis=-1, keepdims=True)
            out_ref[0, crows, :] = x_ref[0, crows, :] + o * lax.rsqrt(ms + EPS) * postg_ref[...]
            yield

    assert sum(SUB_TILES) == ts
    starts = [sum(SUB_TILES[:s]) for s in range(len(SUB_TILES))]
    subs = list(zip(starts, SUB_TILES))
    n_sub = len(subs)
    pre_norm(*subs[0])
    filler = project(*subs[0])
    for s in range(n_sub):
        if s + 1 < n_sub:
            pre_norm(*subs[s + 1])
        for _ in filler:
            pass
        heads = _skewed(head_items(*subs[s]), ITEM_LAG)
        n_filler = 0
        if s + 1 < n_sub:
            filler = project(*subs[s + 1])
            n_filler += PROJECT_PIECES
        else:
            filler = iter(())
        if s >= 1:
            filler = _chain(project_out(*subs[s - 1]), filler)
            n_filler += PROJECT_OUT_DOTS + subs[s - 1][1] // CHUNK
        _interleave(heads, filler, n_filler / ((subs[s][1] // CHUNK) * HEAD_STAGES_PER_CHUNK))
        filler = iter(())
    for _ in project_out(*subs[-1]):
        pass


def _layer(x, mem, w_in_bf16, w_kv_bf16, w_out_bf16, pre_g, mem_g, post_g, vng, vnb, w_spatial,
           b_spatial, buckets, sinks, rel_bias):
    batch, seq, _ = x.shape
    ts = SEQ_TILE
    n_tiles = seq // ts
    const2 = lambda b, t, *_: (0, 0)
    const3 = lambda b, t, *_: (0, 0, 0)
    grid_spec = pltpu.PrefetchScalarGridSpec(
        num_scalar_prefetch=2,
        grid=(batch, n_tiles),
        in_specs=[
            pl.BlockSpec((1, ts, D_MODEL), lambda b, t, *_: (b, t, 0)),
            pl.BlockSpec((1, MEM_LEN, D_MODEL), lambda b, t, *_: (b, 0, 0)),
            pl.BlockSpec((D_MODEL, IN_WIDTH), const2),
            pl.BlockSpec((D_MODEL, 2 * MEM_WIDTH), const2),
            pl.BlockSpec((MIX_WIDTH, D_MODEL), const2),
            pl.BlockSpec((1, D_MODEL), const2),
            pl.BlockSpec((1, D_MODEL), const2),
            pl.BlockSpec((1, D_MODEL), const2),
            pl.BlockSpec((1, A_WIDTH), const2),
            pl.BlockSpec((1, A_WIDTH), const2),
            pl.BlockSpec((A_GROUPS, CHUNK, CHUNK), const3),
            pl.BlockSpec((A_GROUPS, CHUNK), const2),
            pl.BlockSpec((CHUNK, 2 * CHUNK), const2),
        ],
        out_specs=pl.BlockSpec((1, ts, D_MODEL), lambda b, t, *_: (b, t, 0)),
        scratch_shapes=[
            pltpu.VMEM((ts, D_MODEL), BF16),
            pltpu.VMEM((ts, A_WIDTH), F32),
            pltpu.VMEM((ts, A_WIDTH), BF16),
            pltpu.VMEM((ts, SWA_WIDTH + MEM_WIDTH), BF16),
            pltpu.VMEM((SWA_HEADS, LANES, CHUNK + ts), BF16),
            pltpu.VMEM((SWA_HEADS, CHUNK + ts, LANES), BF16),
            pltpu.VMEM((ts, MIX_WIDTH), F32),
            pltpu.VMEM((MEM_HEADS, LANES, MEM_LEN), BF16),
            pltpu.VMEM((MEM_HEADS, MEM_LEN, LANES), BF16),
            pltpu.VMEM((2, SWA_HEADS, CHUNK, 2 * CHUNK), F32),
            pltpu.VMEM((A_GROUPS, CHUNK, CHUNK), BF16),
            pltpu.VMEM((A_GROUPS, CHUNK, LANES), F32),
            pltpu.VMEM((ts, MIX_WIDTH), F32),
            pltpu.VMEM((ts, MIX_WIDTH), BF16),
            pltpu.VMEM((ts, D_MODEL), F32),
        ],
    )
    return pl.pallas_call(
        _layer_kernel,
        grid_spec=grid_spec,
        out_shape=jax.ShapeDtypeStruct(x.shape, x.dtype),
        compiler_params=pltpu.CompilerParams(
            dimension_semantics=("arbitrary", "arbitrary"),
            allow_input_fusion=[False] * 4 + [True] * 3 + [False] * 8,
            vmem_limit_bytes=VMEM_LIMIT_BYTES),
        name="layer",
    )(sinks, rel_bias, x, mem, w_in_bf16, w_kv_bf16, w_out_bf16, pre_g, mem_g, post_g, vng, vnb,
      w_spatial, b_spatial, buckets)


def kernel(x, mem, pre_norm_g, post_norm_g, mem_norm_g, w_in, w_mem_kv, v_norm_g, v_norm_b,
           w_spatial, b_spatial, attn_sinks, rel_bias, w_out):
    depth = w_in.shape[0]
    buckets = jnp.asarray(_band_buckets())
    for layer in range(depth):
        x = _layer(x, mem, w_in[layer].astype(BF16), w_mem_kv[layer].astype(BF16),
                   w_out[layer].astype(BF16),
                   pre_norm_g[layer][None, :], mem_norm_g[layer][None, :], post_norm_g[layer][None, :],
                   v_norm_g[layer][None, :], v_norm_b[layer][None, :],
                   w_spatial[layer], b_spatial[layer], buckets, attn_sinks[layer], rel_bias)
    return x
```

```python
import numpy as np
import jax
import jax.numpy as jnp
from jax import lax
from jax.experimental import pallas as pl
from jax.experimental.pallas import tpu as pltpu

D_MODEL = 1024
MEM_LEN = 256
HEAD_DIM = 64
CHUNK = 128
A_GROUPS = 4
A_WIDTH = 512
SWA_HEADS = 4
SWA_WIDTH = 256
SWA_KV_WIDTH = 128
MEM_HEADS = 4
MEM_WIDTH = 256
MIX_WIDTH = 1024
IN_WIDTH = 2816
N_BUCKETS = 32
MAX_DISTANCE = 128
WINDOW = 128
EPS = 1e-6
NEG = -1e30
LOG2E = float(np.log2(np.e))
Q_SCALE = HEAD_DIM ** -0.5 * LOG2E

OFF_U = 0
OFF_V = A_WIDTH
OFF_SQ = 2 * A_WIDTH
OFF_SK = OFF_SQ + SWA_WIDTH
OFF_SV = OFF_SK + SWA_KV_WIDTH
OFF_MQ = OFF_SV + SWA_KV_WIDTH
OFF_Z = OFF_MQ + MEM_WIDTH

LANES = 128
SEQ_TILE = 1024
SUB_TILES = (256, 256, 256, 256)
PROJ_COLS = 512
ITEM_LAG = 2
PROJECT_COLUMN_PIECES = 6
PROJECT_OUT_DOTS = D_MODEL // PROJ_COLS
HEAD_STAGES_PER_CHUNK = 2 * (SWA_HEADS // 2 + MEM_HEADS // 2)
VMEM_LIMIT_BYTES = 56 * 1024 * 1024

BF16 = jnp.bfloat16
F32 = jnp.float32


def _t5_causal_buckets(dist):
    n = np.maximum(dist, 0)
    max_exact = N_BUCKETS // 2
    large = max_exact + (np.log(np.maximum(n, 1) / max_exact) / np.log(MAX_DISTANCE / max_exact)
                         * (N_BUCKETS - max_exact)).astype(np.int32)
    large = np.minimum(large, N_BUCKETS - 1)
    return np.where(n < max_exact, n, large).astype(np.int32)


def _band_buckets():
    qi = np.arange(CHUNK)[:, None]
    kj = np.arange(2 * CHUNK)[None, :]
    dist = qi + CHUNK - kj
    valid = (dist >= 0) & (dist < WINDOW)
    return np.where(valid, _t5_causal_buckets(dist), -1).astype(np.int32)


def _gelu_tanh(x):
    c = np.float32(np.sqrt(2.0 / np.pi))
    ck = np.float32(np.sqrt(2.0 / np.pi) * 0.044715)
    hx = 0.5 * x
    return hx + hx * jnp.tanh(x * (c + ck * (x * x)))


def _silu(z):
    hz = 0.5 * z
    return hz + hz * jnp.tanh(hz)


def _dot(a, b):
    return jnp.dot(a, b, preferred_element_type=F32)


_DONE = object()


def _skewed(items, lag):
    waiting = []
    for item in items:
        if next(item, _DONE) is not _DONE:
            waiting.append(item)
        yield
        if len(waiting) > lag:
            next(waiting.pop(0), _DONE)
            yield
    for item in waiting:
        next(item, _DONE)
        yield


def _chain(*gens):
    for g in gens:
        yield from g


def _interleave(a, b, b_per_a):
    credit = 0.0
    a_live = b_live = True
    while a_live or b_live:
        if a_live:
            a_live = next(a, _DONE) is not _DONE
        credit += b_per_a
        while b_live and (credit >= 1.0 or not a_live):
            b_live = next(b, _DONE) is not _DONE
            credit -= 1.0


def _lane_half_mask(parity):
    lane = lax.broadcasted_iota(jnp.int32, (1, LANES), 1)
    return (lane >= HEAD_DIM) if parity else (lane < HEAD_DIM)


def _place_head_rows(kt_head, parity):
    zeros = jnp.zeros_like(kt_head)
    return jnp.concatenate([zeros, kt_head] if parity else [kt_head, zeros], axis=0)


def _layer_kernel(sinks_ref, relb_ref,
                  x_ref, mem_ref, win_ref, wkv_ref, wout_ref, preg_ref, memg_ref, postg_ref,
                  vng_ref, vnb_ref, ws_ref, bs_ref, buckets_ref,
                  out_ref,
                  h_scr, gu_scr, vn_scr, qs_scr, ktvar_scr, vvar_scr, zg_scr,
                  mk_scr, mv_scr, bias_scr, wsm_scr, bst_scr, y_scr, ybf_scr, o_scr):
    ts = x_ref.shape[1]
    b = pl.program_id(0)
    t = pl.program_id(1)

    @pl.when((b == 0) & (t == 0))
    def _init():
        row = lax.broadcasted_iota(jnp.int32, (CHUNK, CHUNK), 0)
        col = lax.broadcasted_iota(jnp.int32, (CHUNK, CHUNK), 1)
        for g in range(A_GROUPS):
            wsm_scr[g] = jnp.where(row >= col, ws_ref[g], 0.0).astype(BF16)
            bs_col = jnp.sum(jnp.where(row == col, bs_ref[g:g + 1, :], 0.0), axis=1, keepdims=True)
            bst_scr[g] = jnp.broadcast_to(bs_col, (CHUNK, LANES))
        buckets = buckets_ref[...]
        prev_key = lax.broadcasted_iota(jnp.int32, (CHUNK, 2 * CHUNK), 1) < CHUNK
        for hd in range(SWA_HEADS):
            tbl = jnp.full((CHUNK, 2 * CHUNK), NEG, F32)
            for bk in range(N_BUCKETS):
                tbl = jnp.where(buckets == bk, relb_ref[bk, hd] * LOG2E, tbl)
            bias_scr[0, hd] = tbl
            bias_scr[1, hd] = jnp.where(prev_key, NEG, tbl)

    @pl.when(t == 0)
    def _sequence_start():
        ktvar_scr[:, :, 0:CHUNK] = jnp.zeros((SWA_HEADS, LANES, CHUNK), BF16)
        vvar_scr[:, 0:CHUNK, :] = jnp.zeros((SWA_HEADS, CHUNK, LANES), BF16)
        m = mem_ref[0]
        ms = jnp.mean(m * m, axis=-1, keepdims=True)
        hm = (m * lax.rsqrt(ms + EPS) * memg_ref[...]).astype(BF16)
        kv = _dot(hm, wkv_ref[...])
        for pair in range(MEM_HEADS // 2):
            kt_pair = kv[:, pair * LANES:(pair + 1) * LANES].T
            v_pair = kv[:, MEM_WIDTH + pair * LANES:MEM_WIDTH + (pair + 1) * LANES]
            for parity in range(2):
                hd = 2 * pair + parity
                kt_head = kt_pair[parity * HEAD_DIM:(parity + 1) * HEAD_DIM]
                mk_scr[hd] = _place_head_rows(kt_head, parity).astype(BF16)
                mv_scr[hd] = jnp.where(_lane_half_mask(parity), v_pair, 0.0).astype(BF16)


    @pl.when(t > 0)
    def _copy_carry():
        ktvar_scr[:, :, 0:CHUNK] = ktvar_scr[:, :, ts:ts + CHUNK]
        vvar_scr[:, 0:CHUNK, :] = vvar_scr[:, ts:ts + CHUNK, :]

    lo = _lane_half_mask(0)
    hi = _lane_half_mask(1)
    first_tile = jnp.where(t == 0, 1, 0)

    def pre_norm(r0, nrows):
        for rc in range(nrows // CHUNK):
            crows = pl.ds(r0 + rc * CHUNK, CHUNK)
            x = x_ref[0, crows, :]
            ms = jnp.mean(x * x, axis=-1, keepdims=True)
            h_scr[crows, :] = (x * lax.rsqrt(ms + EPS) * preg_ref[...]).astype(BF16)

    def proj(rows, c0, width):
        return _dot(h_scr[rows, :], win_ref[:, c0:c0 + width])

    def project(r0, nrows):
        rows = pl.ds(r0, nrows)

        v_all = _gelu_tanh(proj(rows, OFF_V, A_WIDTH))
        for g in range(A_GROUPS):
            cols = pl.ds(g * LANES, LANES)
            v = v_all[:, g * LANES:(g + 1) * LANES]
            mu = jnp.mean(v, axis=-1, keepdims=True)
            d = v - mu
            var = jnp.mean(d * d, axis=-1, keepdims=True)
            vn = d * lax.rsqrt(var + EPS) * vng_ref[:, cols] + vnb_ref[:, cols]
            vn_scr[rows, cols] = vn.astype(BF16)
        yield

        gu_scr[rows, :] = _gelu_tanh(proj(rows, OFF_U, A_WIDTH))
        yield

        qkv = proj(rows, OFF_SQ, SWA_WIDTH + 2 * SWA_KV_WIDTH)
        qs_scr[rows, 0:SWA_WIDTH] = (qkv[:, 0:SWA_WIDTH] * Q_SCALE).astype(BF16)
        skt = qkv[:, SWA_WIDTH:SWA_WIDTH + LANES].T
        sv = qkv[:, SWA_WIDTH + LANES:]
        sv_rot = pltpu.roll(sv, HEAD_DIM, 1)
        new_rows = pl.ds(CHUNK + r0, nrows)
        for hd, (keep, vsrc) in enumerate([(lo, sv), (hi, sv_rot), (lo, sv_rot), (hi, sv)]):
            kv_head = hd // 2
            kt_head = skt[kv_head * HEAD_DIM:(kv_head + 1) * HEAD_DIM]
            ktvar_scr[hd, :, new_rows] = _place_head_rows(kt_head, hd % 2).astype(BF16)
            vvar_scr[hd, new_rows, :] = jnp.where(keep, vsrc, 0.0).astype(BF16)
        yield

        mqz = proj(rows, OFF_MQ, PROJ_COLS)
        qs_scr[rows, SWA_WIDTH:] = (mqz[:, 0:MEM_WIDTH] * Q_SCALE).astype(BF16)
        zg_scr[rows, 0:PROJ_COLS - MEM_WIDTH] = _silu(mqz[:, MEM_WIDTH:])
        yield

        for c in range(r0 // CHUNK, (r0 + nrows) // CHUNK):
            for g in range(A_GROUPS):
                spatial_gating(c, g)
            yield

        done = PROJ_COLS - MEM_WIDTH
        while done < MIX_WIDTH:
            width = min(PROJ_COLS, MIX_WIDTH - done)
            zg_scr[rows, done:done + width] = _silu(proj(rows, OFF_Z + done, width))
            done += width
            yield

    def spatial_gating(c, g):
        rows = pl.ds(c * CHUNK, CHUNK)
        cols = pl.ds(g * LANES, LANES)
        sv_g = _dot(wsm_scr[g], vn_scr[rows, cols]) + bst_scr[g]
        y_scr[rows, cols] = gu_scr[rows, cols] * sv_g

    def attn_item(c, pair, sliding):
        r0 = c * CHUNK
        rows = pl.ds(r0, CHUNK)
        heads = (2 * pair, 2 * pair + 1)
        if sliding:
            band = pl.ds(r0, 2 * CHUNK)
            qp = qs_scr[rows, pl.ds(pair * LANES, LANES)]
            keys = jnp.concatenate([ktvar_scr[hd, :, band] for hd in heads], axis=1)
            bias_sel = first_tile if c == 0 else 0
            bias = jnp.concatenate([bias_scr[bias_sel, hd] for hd in heads], axis=1)
            logits = _dot(qp, keys) + bias
        else:
            qp = qs_scr[rows, pl.ds(SWA_WIDTH + pair * LANES, LANES)]
            keys = jnp.concatenate([mk_scr[hd] for hd in heads], axis=1)
            logits = _dot(qp, keys)
        n_keys = logits.shape[1] // 2
        yield
        probs, recips = [], []
        for i, hd in enumerate(heads):
            s = logits[:, i * n_keys:(i + 1) * n_keys]
            m = jnp.max(s, axis=-1, keepdims=True)
            if sliding:
                sink = sinks_ref[hd] * LOG2E
                m = jnp.maximum(m, sink)
            p = jnp.exp2(s - m)
            denom = jnp.sum(p, axis=-1, keepdims=True)
            if sliding:
                denom = denom + jnp.exp2(sink - m)
            probs.append(p.astype(BF16))
            recips.append(1.0 / denom)
        if sliding:
            values = jnp.concatenate([vvar_scr[hd, band, :] for hd in heads], axis=0)
        else:
            values = jnp.concatenate([mv_scr[hd] for hd in heads], axis=0)
        o_pair = _dot(jnp.concatenate(probs, axis=1), values)
        o_pair = o_pair * jnp.where(lo, recips[0], recips[1])
        off = A_WIDTH if sliding else A_WIDTH + SWA_WIDTH
        y_scr[rows, pl.ds(off + pair * LANES, LANES)] = o_pair

    def head_items(r0, nrows):
        for c in range(r0 // CHUNK, (r0 + nrows) // CHUNK):
            for pair in range(SWA_HEADS // 2):
                yield attn_item(c, pair, True)
            for pair in range(MEM_HEADS // 2):
                yield attn_item(c, pair, False)

    def project_out(r0, nrows):
        rows = pl.ds(r0, nrows)
        ybf_scr[rows, :] = (y_scr[rows, :] * zg_scr[rows, :]).astype(BF16)
        for c0 in range(0, D_MODEL, PROJ_COLS):
            o_scr[rows, c0:c0 + PROJ_COLS] = _dot(ybf_scr[rows, :], wout_ref[:, c0:c0 + PROJ_COLS])
            yield
        for rc in range(nrows // CHUNK):
            crows = pl.ds(r0 + rc * CHUNK, CHUNK)
            o = o_scr[crows, :]
            ms = jnp.mean(o * o, axis=-1, keepdims=True)
            out_ref[0, crows, :] = x_ref[0, crows, :] + o * lax.rsqrt(ms + EPS) * postg_ref[...]
            yield

    assert sum(SUB_TILES) == ts
    starts = [sum(SUB_TILES[:s]) for s in range(len(SUB_TILES))]
    subs = list(zip(starts, SUB_TILES))
    n_sub = len(subs)
    pre_norm(*subs[0])
    filler = project(*subs[0])
    for s in range(n_sub):
        if s + 1 < n_sub:
            pre_norm(*subs[s + 1])
        for _ in filler:
            pass
        heads = _skewed(head_items(*subs[s]), ITEM_LAG)
        n_filler = 0
        if s + 1 < n_sub:
            filler = project(*subs[s + 1])
            n_filler += PROJECT_COLUMN_PIECES + subs[s + 1][1] // CHUNK
        else:
            filler = iter(())
        if s >= 1:
            filler = _chain(project_out(*subs[s - 1]), filler)
            n_filler += PROJECT_OUT_DOTS + subs[s - 1][1] // CHUNK
        _interleave(heads, filler, n_filler / ((subs[s][1] // CHUNK) * HEAD_STAGES_PER_CHUNK))
        filler = iter(())
    for _ in project_out(*subs[-1]):
        pass


def _layer(x, mem, w_in_bf16, w_kv_bf16, w_out_bf16, pre_g, mem_g, post_g, vng, vnb, w_spatial,
           b_spatial, buckets, sinks, rel_bias):
    batch, seq, _ = x.shape
    ts = SEQ_TILE
    n_tiles = seq // ts
    const2 = lambda b, t, *_: (0, 0)
    const3 = lambda b, t, *_: (0, 0, 0)
    grid_spec = pltpu.PrefetchScalarGridSpec(
        num_scalar_prefetch=2,
        grid=(batch, n_tiles),
        in_specs=[
            pl.BlockSpec((1, ts, D_MODEL), lambda b, t, *_: (b, t, 0)),
            pl.BlockSpec((1, MEM_LEN, D_MODEL), lambda b, t, *_: (b, 0, 0)),
            pl.BlockSpec((D_MODEL, IN_WIDTH), const2),
            pl.BlockSpec((D_MODEL, 2 * MEM_WIDTH), const2),
            pl.BlockSpec((MIX_WIDTH, D_MODEL), const2),
            pl.BlockSpec((1, D_MODEL), const2),
            pl.BlockSpec((1, D_MODEL), const2),
            pl.BlockSpec((1, D_MODEL), const2),
            pl.BlockSpec((1, A_WIDTH), const2),
            pl.BlockSpec((1, A_WIDTH), const2),
            pl.BlockSpec((A_GROUPS, CHUNK, CHUNK), const3),
            pl.BlockSpec((A_GROUPS, CHUNK), const2),
            pl.BlockSpec((CHUNK, 2 * CHUNK), const2),
        ],
        out_specs=pl.BlockSpec((1, ts, D_MODEL), lambda b, t, *_: (b, t, 0)),
        scratch_shapes=[
            pltpu.VMEM((ts, D_MODEL), BF16),
            pltpu.VMEM((ts, A_WIDTH), F32),
            pltpu.VMEM((ts, A_WIDTH), BF16),
            pltpu.VMEM((ts, SWA_WIDTH + MEM_WIDTH), BF16),
            pltpu.VMEM((SWA_HEADS, LANES, CHUNK + ts), BF16),
            pltpu.VMEM((SWA_HEADS, CHUNK + ts, LANES), BF16),
            pltpu.VMEM((ts, MIX_WIDTH), F32),
            pltpu.VMEM((MEM_HEADS, LANES, MEM_LEN), BF16),
            pltpu.VMEM((MEM_HEADS, MEM_LEN, LANES), BF16),
            pltpu.VMEM((2, SWA_HEADS, CHUNK, 2 * CHUNK), F32),
            pltpu.VMEM((A_GROUPS, CHUNK, CHUNK), BF16),
            pltpu.VMEM((A_GROUPS, CHUNK, LANES), F32),
            pltpu.VMEM((ts, MIX_WIDTH), F32),
            pltpu.VMEM((ts, MIX_WIDTH), BF16),
            pltpu.VMEM((ts, D_MODEL), F32),
        ],
    )
    return pl.pallas_call(
        _layer_kernel,
        grid_spec=grid_spec,
        out_shape=jax.ShapeDtypeStruct(x.shape, x.dtype),
        compiler_params=pltpu.CompilerParams(
            dimension_semantics=("arbitrary", "arbitrary"),
            vmem_limit_bytes=VMEM_LIMIT_BYTES),
        name="layer",
    )(sinks, rel_bias, x, mem, w_in_bf16, w_kv_bf16, w_out_bf16, pre_g, mem_g, post_g, vng, vnb,
      w_spatial, b_spatial, buckets)


def kernel(x, mem, pre_norm_g, post_norm_g, mem_norm_g, w_in, w_mem_kv, v_norm_g, v_norm_b,
           w_spatial, b_spatial, attn_sinks, rel_bias, w_out):
    depth = w_in.shape[0]
    buckets = jnp.asarray(_band_buckets())
    for layer in range(depth):
        x = _layer(x, mem, w_in[layer].astype(BF16), w_mem_kv[layer].astype(BF16),
                   w_out[layer].astype(BF16),
                   pre_norm_g[layer][None, :], mem_norm_g[layer][None, :], post_norm_g[layer][None, :],
                   v_norm_g[layer][None, :], v_norm_b[layer][None, :],
                   w_spatial[layer], b_spatial[layer], buckets, attn_sinks[layer], rel_bias)
    return x
```

```python
import numpy as np
import jax
import jax.numpy as jnp
from jax import lax
from jax.experimental import pallas as pl
from jax.experimental.pallas import tpu as pltpu

D_MODEL = 1024
MEM_LEN = 256
HEAD_DIM = 64
CHUNK = 128
A_GROUPS = 4
A_WIDTH = 512
SWA_HEADS = 4
SWA_WIDTH = 256
SWA_KV_WIDTH = 128
MEM_HEADS = 4
MEM_WIDTH = 256
MIX_WIDTH = 1024
IN_WIDTH = 2816
N_BUCKETS = 32
MAX_DISTANCE = 128
WINDOW = 128
EPS = 1e-6
NEG = -1e30
LOG2E = float(np.log2(np.e))
Q_SCALE = HEAD_DIM ** -0.5 * LOG2E

OFF_U = 0
OFF_V = A_WIDTH
OFF_SQ = 2 * A_WIDTH
OFF_SK = OFF_SQ + SWA_WIDTH
OFF_SV = OFF_SK + SWA_KV_WIDTH
OFF_MQ = OFF_SV + SWA_KV_WIDTH
OFF_Z = OFF_MQ + MEM_WIDTH

LANES = 128
SEQ_TILE = 1024
SUB_TILES = (256, 256, 256, 256)
PROJ_COLS = 512
ITEM_LAG = 2
PROJECT_COLUMN_PIECES = 6
PROJECT_OUT_DOTS = D_MODEL // PROJ_COLS
HEAD_STAGES_PER_CHUNK = 2 * (SWA_HEADS // 2 + MEM_HEADS // 2)
VMEM_LIMIT_BYTES = 56 * 1024 * 1024

BF16 = jnp.bfloat16
F32 = jnp.float32


def _t5_causal_buckets(dist):
    n = np.maximum(dist, 0)
    max_exact = N_BUCKETS // 2
    large = max_exact + (np.log(np.maximum(n, 1) / max_exact) / np.log(MAX_DISTANCE / max_exact)
                         * (N_BUCKETS - max_exact)).astype(np.int32)
    large = np.minimum(large, N_BUCKETS - 1)
    return np.where(n < max_exact, n, large).astype(np.int32)


def _band_buckets():
    qi = np.arange(CHUNK)[:, None]
    kj = np.arange(2 * CHUNK)[None, :]
    dist = qi + CHUNK - kj
    valid = (dist >= 0) & (dist < WINDOW)
    return np.where(valid, _t5_causal_buckets(dist), -1).astype(np.int32)


def _gelu_tanh(x):
    c = np.float32(np.sqrt(2.0 / np.pi))
    ck = np.float32(np.sqrt(2.0 / np.pi) * 0.044715)
    hx = 0.5 * x
    return hx + hx * jnp.tanh(x * (c + ck * (x * x)))


def _silu(z):
    hz = 0.5 * z
    return hz + hz * jnp.tanh(hz)


def _dot(a, b):
    return jnp.dot(a, b, preferred_element_type=F32)


_DONE = object()


def _skewed(items, lag):
    waiting = []
    for item in items:
        if next(item, _DONE) is not _DONE:
            waiting.append(item)
        yield
        if len(waiting) > lag:
            next(waiting.pop(0), _DONE)
            yield
    for item in waiting:
        next(item, _DONE)
        yield


def _chain(*gens):
    for g in gens:
        yield from g


def _interleave(a, b, b_per_a):
    credit = 0.0
    a_live = b_live = True
    while a_live or b_live:
        if a_live:
            a_live = next(a, _DONE) is not _DONE
        credit += b_per_a
        while b_live and (credit >= 1.0 or not a_live):
            b_live = next(b, _DONE) is not _DONE
            credit -= 1.0


def _lane_half_mask(parity):
    lane = lax.broadcasted_iota(jnp.int32, (1, LANES), 1)
    return (lane >= HEAD_DIM) if parity else (lane < HEAD_DIM)


def _place_head_rows(kt_head, parity):
    zeros = jnp.zeros_like(kt_head)
    return jnp.concatenate([zeros, kt_head] if parity else [kt_head, zeros], axis=0)


def _layer_kernel(sinks_ref,
                  x_ref, mem_ref, win_ref, wkv_f32_ref, wout_f32_ref, preg_ref, memg_ref, postg_ref,
                  vng_ref, vnb_ref, ws_ref, bs_ref, relb_ref, buckets_ref,
                  out_ref,
                  h_scr, gu_scr, vn_scr, qs_scr, ktvar_scr, vvar_scr, zg_scr,
                  mk_scr, mv_scr, bias_scr, wsm_scr, bst_scr, wkv_ref, wout_ref,
                  y_scr, ybf_scr, o_scr):
    ts = x_ref.shape[1]
    b = pl.program_id(0)
    t = pl.program_id(1)

    @pl.when((b == 0) & (t == 0))
    def _init():
        row = lax.broadcasted_iota(jnp.int32, (CHUNK, CHUNK), 0)
        col = lax.broadcasted_iota(jnp.int32, (CHUNK, CHUNK), 1)
        for g in range(A_GROUPS):
            wsm_scr[g] = jnp.where(row >= col, ws_ref[g], 0.0).astype(BF16)
            bs_col = jnp.sum(jnp.where(row == col, bs_ref[g:g + 1, :], 0.0), axis=1, keepdims=True)
            bst_scr[g] = jnp.broadcast_to(bs_col, (CHUNK, LANES))
        for r0 in range(0, D_MODEL, 2 * CHUNK):
            wrows = pl.ds(r0, 2 * CHUNK)
            wkv_ref[wrows, :] = wkv_f32_ref[wrows, :].astype(BF16)
            wout_ref[wrows, :] = wout_f32_ref[wrows, :].astype(BF16)
        buckets = buckets_ref[...]
        prev_key = lax.broadcasted_iota(jnp.int32, (CHUNK, 2 * CHUNK), 1) < CHUNK
        for hd in range(SWA_HEADS):
            tbl = jnp.full((CHUNK, 2 * CHUNK), NEG, F32)
            for bk in range(N_BUCKETS):
                tbl = jnp.where(buckets == bk, relb_ref[bk:bk + 1, hd:hd + 1] * LOG2E, tbl)
            bias_scr[0, hd] = tbl
            bias_scr[1, hd] = jnp.where(prev_key, NEG, tbl)

    @pl.when(t == 0)
    def _sequence_start():
        ktvar_scr[:, :, 0:CHUNK] = jnp.zeros((SWA_HEADS, LANES, CHUNK), BF16)
        vvar_scr[:, 0:CHUNK, :] = jnp.zeros((SWA_HEADS, CHUNK, LANES), BF16)
        m = mem_ref[0]
        ms = jnp.mean(m * m, axis=-1, keepdims=True)
        hm = (m * lax.rsqrt(ms + EPS) * memg_ref[...]).astype(BF16)
        kv = _dot(hm, wkv_ref[...])
        for pair in range(MEM_HEADS // 2):
            kt_pair = kv[:, pair * LANES:(pair + 1) * LANES].T
            v_pair = kv[:, MEM_WIDTH + pair * LANES:MEM_WIDTH + (pair + 1) * LANES]
            for parity in range(2):
                hd = 2 * pair + parity
                kt_head = kt_pair[parity * HEAD_DIM:(parity + 1) * HEAD_DIM]
                mk_scr[hd] = _place_head_rows(kt_head, parity).astype(BF16)
                mv_scr[hd] = jnp.where(_lane_half_mask(parity), v_pair, 0.0).astype(BF16)


    @pl.when(t > 0)
    def _copy_carry():
        ktvar_scr[:, :, 0:CHUNK] = ktvar_scr[:, :, ts:ts + CHUNK]
        vvar_scr[:, 0:CHUNK, :] = vvar_scr[:, ts:ts + CHUNK, :]

    lo = _lane_half_mask(0)
    hi = _lane_half_mask(1)
    first_tile = jnp.where(t == 0, 1, 0)

    def pre_norm(r0, nrows):
        for rc in range(nrows // CHUNK):
            crows = pl.ds(r0 + rc * CHUNK, CHUNK)
            x = x_ref[0, crows, :]
            ms = jnp.mean(x * x, axis=-1, keepdims=True)
            h_scr[crows, :] = (x * lax.rsqrt(ms + EPS) * preg_ref[...]).astype(BF16)

    def proj(rows, c0, width):
        return _dot(h_scr[rows, :], win_ref[:, c0:c0 + width])

    def project(r0, nrows):
        rows = pl.ds(r0, nrows)

        v_all = _gelu_tanh(proj(rows, OFF_V, A_WIDTH))
        for g in range(A_GROUPS):
            cols = pl.ds(g * LANES, LANES)
            v = v_all[:, g * LANES:(g + 1) * LANES]
            mu = jnp.mean(v, axis=-1, keepdims=True)
            d = v - mu
            var = jnp.mean(d * d, axis=-1, keepdims=True)
            vn = d * lax.rsqrt(var + EPS) * vng_ref[:, cols] + vnb_ref[:, cols]
            vn_scr[rows, cols] = vn.astype(BF16)
        yield

        gu_scr[rows, :] = _gelu_tanh(proj(rows, OFF_U, A_WIDTH))
        yield

        qkv = proj(rows, OFF_SQ, SWA_WIDTH + 2 * SWA_KV_WIDTH)
        qs_scr[rows, 0:SWA_WIDTH] = (qkv[:, 0:SWA_WIDTH] * Q_SCALE).astype(BF16)
        skt = qkv[:, SWA_WIDTH:SWA_WIDTH + LANES].T
        sv = qkv[:, SWA_WIDTH + LANES:]
        sv_rot = pltpu.roll(sv, HEAD_DIM, 1)
        new_rows = pl.ds(CHUNK + r0, nrows)
        for hd, (keep, vsrc) in enumerate([(lo, sv), (hi, sv_rot), (lo, sv_rot), (hi, sv)]):
            kv_head = hd // 2
            kt_head = skt[kv_head * HEAD_DIM:(kv_head + 1) * HEAD_DIM]
            ktvar_scr[hd, :, new_rows] = _place_head_rows(kt_head, hd % 2).astype(BF16)
            vvar_scr[hd, new_rows, :] = jnp.where(keep, vsrc, 0.0).astype(BF16)
        yield

        mqz = proj(rows, OFF_MQ, PROJ_COLS)
        qs_scr[rows, SWA_WIDTH:] = (mqz[:, 0:MEM_WIDTH] * Q_SCALE).astype(BF16)
        zg_scr[rows, 0:PROJ_COLS - MEM_WIDTH] = _silu(mqz[:, MEM_WIDTH:])
        yield

        for c in range(r0 // CHUNK, (r0 + nrows) // CHUNK):
            for g in range(A_GROUPS):
                spatial_gating(c, g)
            yield

        done = PROJ_COLS - MEM_WIDTH
        while done < MIX_WIDTH:
            width = min(PROJ_COLS, MIX_WIDTH - done)
            zg_scr[rows, done:done + width] = _silu(proj(rows, OFF_Z + done, width))
            done += width
            yield

    def spatial_gating(c, g):
        rows = pl.ds(c * CHUNK, CHUNK)
        cols = pl.ds(g * LANES, LANES)
        sv_g = _dot(wsm_scr[g], vn_scr[rows, cols]) + bst_scr[g]
        y_scr[rows, cols] = gu_scr[rows, cols] * sv_g

    def attn_item(c, pair, sliding):
        r0 = c * CHUNK
        rows = pl.ds(r0, CHUNK)
        heads = (2 * pair, 2 * pair + 1)
        if sliding:
            band = pl.ds(r0, 2 * CHUNK)
            qp = qs_scr[rows, pl.ds(pair * LANES, LANES)]
            keys = jnp.concatenate([ktvar_scr[hd, :, band] for hd in heads], axis=1)
            bias_sel = first_tile if c == 0 else 0
            bias = jnp.concatenate([bias_scr[bias_sel, hd] for hd in heads], axis=1)
            logits = _dot(qp, keys) + bias
        else:
            qp = qs_scr[rows, pl.ds(SWA_WIDTH + pair * LANES, LANES)]
            keys = jnp.concatenate([mk_scr[hd] for hd in heads], axis=1)
            logits = _dot(qp, keys)
        n_keys = logits.shape[1] // 2
        yield
        probs, recips = [], []
        for i, hd in enumerate(heads):
            s = logits[:, i * n_keys:(i + 1) * n_keys]
            m = jnp.max(s, axis=-1, keepdims=True)
            if sliding:
                sink = sinks_ref[hd] * LOG2E
                m = jnp.maximum(m, sink)
            p = jnp.exp2(s - m)
            denom = jnp.sum(p, axis=-1, keepdims=True)
            if sliding:
                denom = denom + jnp.exp2(sink - m)
            probs.append(p.astype(BF16))
            recips.append(1.0 / denom)
        if sliding:
            values = jnp.concatenate([vvar_scr[hd, band, :] for hd in heads], axis=0)
        else:
            values = jnp.concatenate([mv_scr[hd] for hd in heads], axis=0)
        o_pair = _dot(jnp.concatenate(probs, axis=1), values)
        o_pair = o_pair * jnp.where(lo, recips[0], recips[1])
        off = A_WIDTH if sliding else A_WIDTH + SWA_WIDTH
        y_scr[rows, pl.ds(off + pair * LANES, LANES)] = o_pair

    def head_items(r0, nrows):
        for c in range(r0 // CHUNK, (r0 + nrows) // CHUNK):
            for pair in range(SWA_HEADS // 2):
                yield attn_item(c, pair, True)
            for pair in range(MEM_HEADS // 2):
                yield attn_item(c, pair, False)

    def project_out(r0, nrows):
        rows = pl.ds(r0, nrows)
        ybf_scr[rows, :] = (y_scr[rows, :] * zg_scr[rows, :]).astype(BF16)
        for c0 in range(0, D_MODEL, PROJ_COLS):
            o_scr[rows, c0:c0 + PROJ_COLS] = _dot(ybf_scr[rows, :], wout_ref[:, c0:c0 + PROJ_COLS])
            yield
        for rc in range(nrows // CHUNK):
            crows = pl.ds(r0 + rc * CHUNK, CHUNK)
            o = o_scr[crows, :]
            ms = jnp.mean(o * o, axis=-1, keepdims=True)
            out_ref[0, crows, :] = x_ref[0, crows, :] + o * lax.rsqrt(ms + EPS) * postg_ref[...]
            yield

    assert sum(SUB_TILES) == ts
    starts = [sum(SUB_TILES[:s]) for s in range(len(SUB_TILES))]
    subs = list(zip(starts, SUB_TILES))
    n_sub = len(subs)
    pre_norm(*subs[0])
    filler = project(*subs[0])
    for s in range(n_sub):
        if s + 1 < n_sub:
            pre_norm(*subs[s + 1])
        for _ in filler:
            pass
        heads = _skewed(head_items(*subs[s]), ITEM_LAG)
        n_filler = 0
        if s + 1 < n_sub:
            filler = project(*subs[s + 1])
            n_filler += PROJECT_COLUMN_PIECES + subs[s + 1][1] // CHUNK
        else:
            filler = iter(())
        if s >= 1:
            filler = _chain(project_out(*subs[s - 1]), filler)
            n_filler += PROJECT_OUT_DOTS + subs[s - 1][1] // CHUNK
        _interleave(heads, filler, n_filler / ((subs[s][1] // CHUNK) * HEAD_STAGES_PER_CHUNK))
        filler = iter(())
    for _ in project_out(*subs[-1]):
        pass


def _layer(x, mem, w_in_bf16, w_kv, w_out, pre_g, mem_g, post_g, vng, vnb, w_spatial,
           b_spatial, buckets, sinks, rel_bias):
    batch, seq, _ = x.shape
    ts = SEQ_TILE
    n_tiles = seq // ts
    const2 = lambda b, t, *_: (0, 0)
    const3 = lambda b, t, *_: (0, 0, 0)
    grid_spec = pltpu.PrefetchScalarGridSpec(
        num_scalar_prefetch=1,
        grid=(batch, n_tiles),
        in_specs=[
            pl.BlockSpec((1, ts, D_MODEL), lambda b, t, *_: (b, t, 0)),
            pl.BlockSpec((1, MEM_LEN, D_MODEL), lambda b, t, *_: (b, 0, 0)),
            pl.BlockSpec((D_MODEL, IN_WIDTH), const2),
            pl.BlockSpec((D_MODEL, 2 * MEM_WIDTH), const2),
            pl.BlockSpec((MIX_WIDTH, D_MODEL), const2),
            pl.BlockSpec((1, D_MODEL), const2),
            pl.BlockSpec((1, D_MODEL), const2),
            pl.BlockSpec((1, D_MODEL), const2),
            pl.BlockSpec((1, A_WIDTH), const2),
            pl.BlockSpec((1, A_WIDTH), const2),
            pl.BlockSpec((A_GROUPS, CHUNK, CHUNK), const3),
            pl.BlockSpec((A_GROUPS, CHUNK), const2),
            pl.BlockSpec((N_BUCKETS, SWA_HEADS), const2),
            pl.BlockSpec((CHUNK, 2 * CHUNK), const2),
        ],
        out_specs=pl.BlockSpec((1, ts, D_MODEL), lambda b, t, *_: (b, t, 0)),
        scratch_shapes=[
            pltpu.VMEM((ts, D_MODEL), BF16),
            pltpu.VMEM((ts, A_WIDTH), F32),
            pltpu.VMEM((ts, A_WIDTH), BF16),
            pltpu.VMEM((ts, SWA_WIDTH + MEM_WIDTH), BF16),
            pltpu.VMEM((SWA_HEADS, LANES, CHUNK + ts), BF16),
            pltpu.VMEM((SWA_HEADS, CHUNK + ts, LANES), BF16),
            pltpu.VMEM((ts, MIX_WIDTH), F32),
            pltpu.VMEM((MEM_HEADS, LANES, MEM_LEN), BF16),
            pltpu.VMEM((MEM_HEADS, MEM_LEN, LANES), BF16),
            pltpu.VMEM((2, SWA_HEADS, CHUNK, 2 * CHUNK), F32),
            pltpu.VMEM((A_GROUPS, CHUNK, CHUNK), BF16),
            pltpu.VMEM((A_GROUPS, CHUNK, LANES), F32),
            pltpu.VMEM((D_MODEL, 2 * MEM_WIDTH), BF16),
            pltpu.VMEM((MIX_WIDTH, D_MODEL), BF16),
            pltpu.VMEM((ts, MIX_WIDTH), F32),
            pltpu.VMEM((ts, MIX_WIDTH), BF16),
            pltpu.VMEM((ts, D_MODEL), F32),
        ],
    )
    return pl.pallas_call(
        _layer_kernel,
        grid_spec=grid_spec,
        out_shape=jax.ShapeDtypeStruct(x.shape, x.dtype),
        compiler_params=pltpu.CompilerParams(
            dimension_semantics=("arbitrary", "arbitrary"),
            vmem_limit_bytes=VMEM_LIMIT_BYTES),
        name="layer",
    )(sinks, x, mem, w_in_bf16, w_kv, w_out, pre_g, mem_g, post_g, vng, vnb,
      w_spatial, b_spatial, rel_bias, buckets)


def kernel(x, mem, pre_norm_g, post_norm_g, mem_norm_g, w_in, w_mem_kv, v_norm_g, v_norm_b,
           w_spatial, b_spatial, attn_sinks, rel_bias, w_out):
    depth = w_in.shape[0]
    buckets = jnp.asarray(_band_buckets())
    for layer in range(depth):
        x = _layer(x, mem, w_in[layer].astype(BF16), w_mem_kv[layer], w_out[layer],
                   pre_norm_g[layer][None, :], mem_norm_g[layer][None, :], post_norm_g[layer][None, :],
                   v_norm_g[layer][None, :], v_norm_b[layer][None, :],
                   w_spatial[layer], b_spatial[layer], buckets, attn_sinks[layer], rel_bias)
    return x
```

```python
import numpy as np
import jax
import jax.numpy as jnp
from jax import lax
from jax.experimental import pallas as pl
from jax.experimental.pallas import tpu as pltpu

D_MODEL = 1024
MEM_LEN = 256
HEAD_DIM = 64
CHUNK = 128
A_GROUPS = 4
A_WIDTH = 512
SWA_HEADS = 4
SWA_WIDTH = 256
SWA_KV_WIDTH = 128
MEM_HEADS = 4
MEM_WIDTH = 256
MIX_WIDTH = 1024
IN_WIDTH = 2816
N_BUCKETS = 32
MAX_DISTANCE = 128
WINDOW = 128
EPS = 1e-6
NEG = -1e30
LOG2E = float(np.log2(np.e))
Q_SCALE = HEAD_DIM ** -0.5 * LOG2E

OFF_U = 0
OFF_V = A_WIDTH
OFF_SQ = 2 * A_WIDTH
OFF_SK = OFF_SQ + SWA_WIDTH
OFF_SV = OFF_SK + SWA_KV_WIDTH
OFF_MQ = OFF_SV + SWA_KV_WIDTH
OFF_Z = OFF_MQ + MEM_WIDTH

LANES = 128
SEQ_TILE = 1024
SUB_TILES = (256, 256, 256, 256)
PROJ_COLS = 512
ITEM_LAG = 2
PROJECT_COLUMN_PIECES = 6
PROJECT_OUT_DOTS = D_MODEL // PROJ_COLS
HEAD_STAGES_PER_CHUNK = 2 * (SWA_HEADS // 2 + MEM_HEADS // 2)
VMEM_LIMIT_BYTES = 56 * 1024 * 1024

BF16 = jnp.bfloat16
F32 = jnp.float32


def _t5_causal_buckets(dist):
    n = np.maximum(dist, 0)
    max_exact = N_BUCKETS // 2
    large = max_exact + (np.log(np.maximum(n, 1) / max_exact) / np.log(MAX_DISTANCE / max_exact)
                         * (N_BUCKETS - max_exact)).astype(np.int32)
    large = np.minimum(large, N_BUCKETS - 1)
    return np.where(n < max_exact, n, large).astype(np.int32)


def _band_buckets():
    qi = np.arange(CHUNK)[:, None]
    kj = np.arange(2 * CHUNK)[None, :]
    dist = qi + CHUNK - kj
    valid = (dist >= 0) & (dist < WINDOW)
    return np.where(valid, _t5_causal_buckets(dist), -1).astype(np.int32)


def _gelu_tanh(x):
    c = np.float32(np.sqrt(2.0 / np.pi))
    ck = np.float32(np.sqrt(2.0 / np.pi) * 0.044715)
    hx = 0.5 * x
    return hx + hx * jnp.tanh(x * (c + ck * (x * x)))


def _silu(z):
    hz = 0.5 * z
    return hz + hz * jnp.tanh(hz)


def _dot(a, b):
    return jnp.dot(a, b, preferred_element_type=F32)


_DONE = object()


def _skewed(items, lag):
    waiting = []
    for item in items:
        if next(item, _DONE) is not _DONE:
            waiting.append(item)
        yield
        if len(waiting) > lag:
            next(waiting.pop(0), _DONE)
            yield
    for item in waiting:
        next(item, _DONE)
        yield


def _chain(*gens):
    for g in gens:
        yield from g


def _interleave(a, b, b_per_a):
    credit = 0.0
    a_live = b_live = True
    while a_live or b_live:
        if a_live:
            a_live = next(a, _DONE) is not _DONE
        credit += b_per_a
        while b_live and (credit >= 1.0 or not a_live):
            b_live = next(b, _DONE) is not _DONE
            credit -= 1.0


def _lane_half_mask(parity):
    lane = lax.broadcasted_iota(jnp.int32, (1, LANES), 1)
    return (lane >= HEAD_DIM) if parity else (lane < HEAD_DIM)


def _place_head_rows(kt_head, parity):
    zeros = jnp.zeros_like(kt_head)
    return jnp.concatenate([zeros, kt_head] if parity else [kt_head, zeros], axis=0)


def _layer_kernel(sinks_ref,
                  x_ref, mem_ref, win_ref, wkv_f32_ref, wout_f32_ref, preg_ref, memg_ref, postg_ref,
                  vng_ref, vnb_ref, ws_ref, bs_ref, relb_ref, buckets_ref,
                  out_ref,
                  h_scr, gu_scr, vn_scr, qs_scr, ktvar_scr, vvar_scr, zg_scr,
                  mk_scr, mv_scr, bias_scr, wsm_scr, bst_scr, wkv_ref, wout_ref,
                  y_scr, ybf_scr, o_scr):
    ts = x_ref.shape[1]
    b = pl.program_id(0)
    t = pl.program_id(1)

    @pl.when((b == 0) & (t == 0))
    def _init():
        row = lax.broadcasted_iota(jnp.int32, (CHUNK, CHUNK), 0)
        col = lax.broadcasted_iota(jnp.int32, (CHUNK, CHUNK), 1)
        for g in range(A_GROUPS):
            wsm_scr[g] = jnp.where(row >= col, ws_ref[g], 0.0).astype(BF16)
            bs_col = jnp.sum(jnp.where(row == col, bs_ref[g:g + 1, :], 0.0), axis=1, keepdims=True)
            bst_scr[g] = jnp.broadcast_to(bs_col, (CHUNK, LANES))
        for r0 in range(0, D_MODEL, 2 * CHUNK):
            wrows = pl.ds(r0, 2 * CHUNK)
            wkv_ref[wrows, :] = wkv_f32_ref[wrows, :].astype(BF16)
            wout_ref[wrows, :] = wout_f32_ref[wrows, :].astype(BF16)
        buckets = buckets_ref[...]
        prev_key = lax.broadcasted_iota(jnp.int32, (CHUNK, 2 * CHUNK), 1) < CHUNK
        for hd in range(SWA_HEADS):
            tbl = jnp.full((CHUNK, 2 * CHUNK), NEG, F32)
            for bk in range(N_BUCKETS):
                tbl = jnp.where(buckets == bk, relb_ref[bk:bk + 1, hd:hd + 1] * LOG2E, tbl)
            bias_scr[0, hd] = tbl
            bias_scr[1, hd] = jnp.where(prev_key, NEG, tbl)

    @pl.when(t == 0)
    def _sequence_start():
        ktvar_scr[:, :, 0:CHUNK] = jnp.zeros((SWA_HEADS, LANES, CHUNK), BF16)
        vvar_scr[:, 0:CHUNK, :] = jnp.zeros((SWA_HEADS, CHUNK, LANES), BF16)
        m = mem_ref[0]
        ms = jnp.mean(m * m, axis=-1, keepdims=True)
        hm = (m * lax.rsqrt(ms + EPS) * memg_ref[...]).astype(BF16)
        kv = _dot(hm, wkv_ref[...])
        for pair in range(MEM_HEADS // 2):
            kt_pair = kv[:, pair * LANES:(pair + 1) * LANES].T
            v_pair = kv[:, MEM_WIDTH + pair * LANES:MEM_WIDTH + (pair + 1) * LANES]
            for parity in range(2):
                hd = 2 * pair + parity
                kt_head = kt_pair[parity * HEAD_DIM:(parity + 1) * HEAD_DIM]
                mk_scr[hd] = _place_head_rows(kt_head, parity).astype(BF16)
                mv_scr[hd] = jnp.where(_lane_half_mask(parity), v_pair, 0.0).astype(BF16)


    @pl.when(t > 0)
    def _copy_carry():
        ktvar_scr[:, :, 0:CHUNK] = ktvar_scr[:, :, ts:ts + CHUNK]
        vvar_scr[:, 0:CHUNK, :] = vvar_scr[:, ts:ts + CHUNK, :]

    lo = _lane_half_mask(0)
    hi = _lane_half_mask(1)
    first_tile = jnp.where(t == 0, 1, 0)

    def pre_norm(r0, nrows):
        for rc in range(nrows // CHUNK):
            crows = pl.ds(r0 + rc * CHUNK, CHUNK)
            x = x_ref[0, crows, :]
            ms = jnp.mean(x * x, axis=-1, keepdims=True)
            h_scr[crows, :] = (x * lax.rsqrt(ms + EPS) * preg_ref[...]).astype(BF16)

    def proj(rows, c0, width):
        return _dot(h_scr[rows, :], win_ref[:, c0:c0 + width])

    def project(r0, nrows):
        rows = pl.ds(r0, nrows)

        v_all = _gelu_tanh(proj(rows, OFF_V, A_WIDTH))
        for g in range(A_GROUPS):
            cols = pl.ds(g * LANES, LANES)
            v = v_all[:, g * LANES:(g + 1) * LANES]
            mu = jnp.mean(v, axis=-1, keepdims=True)
            d = v - mu
            var = jnp.mean(d * d, axis=-1, keepdims=True)
            vn = d * lax.rsqrt(var + EPS) * vng_ref[:, cols] + vnb_ref[:, cols]
            vn_scr[rows, cols] = vn.astype(BF16)
        yield

        gu_scr[rows, :] = _gelu_tanh(proj(rows, OFF_U, A_WIDTH))
        yield

        qkv = proj(rows, OFF_SQ, SWA_WIDTH + 2 * SWA_KV_WIDTH)
        qs_scr[rows, 0:SWA_WIDTH] = (qkv[:, 0:SWA_WIDTH] * Q_SCALE).astype(BF16)
        skt = qkv[:, SWA_WIDTH:SWA_WIDTH + LANES].T
        sv = qkv[:, SWA_WIDTH + LANES:]
        sv_rot = pltpu.roll(sv, HEAD_DIM, 1)
        new_rows = pl.ds(CHUNK + r0, nrows)
        for hd, (keep, vsrc) in enumerate([(lo, sv), (hi, sv_rot), (lo, sv_rot), (hi, sv)]):
            kv_head = hd // 2
            kt_head = skt[kv_head * HEAD_DIM:(kv_head + 1) * HEAD_DIM]
            ktvar_scr[hd, :, new_rows] = _place_head_rows(kt_head, hd % 2).astype(BF16)
            vvar_scr[hd, new_rows, :] = jnp.where(keep, vsrc, 0.0).astype(BF16)
        yield

        mqz = proj(rows, OFF_MQ, PROJ_COLS)
        qs_scr[rows, SWA_WIDTH:] = (mqz[:, 0:MEM_WIDTH] * Q_SCALE).astype(BF16)
        zg_scr[rows, 0:PROJ_COLS - MEM_WIDTH] = _silu(mqz[:, MEM_WIDTH:])
        yield

        for c in range(r0 // CHUNK, (r0 + nrows) // CHUNK):
            for g in range(A_GROUPS):
                spatial_gating(c, g)
            yield

        done = PROJ_COLS - MEM_WIDTH
        while done < MIX_WIDTH:
            width = min(PROJ_COLS, MIX_WIDTH - done)
            zg_scr[rows, done:done + width] = _silu(proj(rows, OFF_Z + done, width))
            done += width
            yield

    def spatial_gating(c, g):
        rows = pl.ds(c * CHUNK, CHUNK)
        cols = pl.ds(g * LANES, LANES)
        sv_g = _dot(wsm_scr[g], vn_scr[rows, cols]) + bst_scr[g]
        y_scr[rows, cols] = gu_scr[rows, cols] * sv_g

    def attn_item(c, pair, sliding):
        r0 = c * CHUNK
        rows = pl.ds(r0, CHUNK)
        heads = (2 * pair, 2 * pair + 1)
        if sliding:
            band = pl.ds(r0, 2 * CHUNK)
            qp = qs_scr[rows, pl.ds(pair * LANES, LANES)]
            keys = [ktvar_scr[hd, :, band] for hd in heads]
            bias_sel = first_tile if c == 0 else 0
            logits = [_dot(qp, kt) + bias_scr[bias_sel, hd] for hd, kt in zip(heads, keys)]
        else:
            qp = qs_scr[rows, pl.ds(SWA_WIDTH + pair * LANES, LANES)]
            logits = [_dot(qp, mk_scr[hd]) for hd in heads]
        yield
        o_pair = None
        for hd, s in zip(heads, logits):
            m = jnp.max(s, axis=-1, keepdims=True)
            if sliding:
                sink = sinks_ref[hd] * LOG2E
                m = jnp.maximum(m, sink)
            p = jnp.exp2(s - m)
            denom = jnp.sum(p, axis=-1, keepdims=True)
            if sliding:
                denom = denom + jnp.exp2(sink - m)
                values = vvar_scr[hd, band, :]
            else:
                values = mv_scr[hd]
            o = _dot(p.astype(BF16), values) * (1.0 / denom)
            o_pair = o if o_pair is None else o_pair + o
        off = A_WIDTH if sliding else A_WIDTH + SWA_WIDTH
        y_scr[rows, pl.ds(off + pair * LANES, LANES)] = o_pair

    def head_items(r0, nrows):
        for c in range(r0 // CHUNK, (r0 + nrows) // CHUNK):
            for pair in range(SWA_HEADS // 2):
                yield attn_item(c, pair, True)
            for pair in range(MEM_HEADS // 2):
                yield attn_item(c, pair, False)

    def project_out(r0, nrows):
        rows = pl.ds(r0, nrows)
        ybf_scr[rows, :] = (y_scr[rows, :] * zg_scr[rows, :]).astype(BF16)
        for c0 in range(0, D_MODEL, PROJ_COLS):
            o_scr[rows, c0:c0 + PROJ_COLS] = _dot(ybf_scr[rows, :], wout_ref[:, c0:c0 + PROJ_COLS])
            yield
        for rc in range(nrows // CHUNK):
            crows = pl.ds(r0 + rc * CHUNK, CHUNK)
            o = o_scr[crows, :]
            ms = jnp.mean(o * o, axis=-1, keepdims=True)
            out_ref[0, crows, :] = x_ref[0, crows, :] + o * lax.rsqrt(ms + EPS) * postg_ref[...]
            yield

    assert sum(SUB_TILES) == ts
    starts = [sum(SUB_TILES[:s]) for s in range(len(SUB_TILES))]
    subs = list(zip(starts, SUB_TILES))
    n_sub = len(subs)
    pre_norm(*subs[0])
    filler = project(*subs[0])
    for s in range(n_sub):
        if s + 1 < n_sub:
            pre_norm(*subs[s + 1])
        for _ in filler:
            pass
        heads = _skewed(head_items(*subs[s]), ITEM_LAG)
        n_filler = 0
        if s + 1 < n_sub:
            filler = project(*subs[s + 1])
            n_filler += PROJECT_COLUMN_PIECES + subs[s + 1][1] // CHUNK
        else:
            filler = iter(())
        if s >= 1:
            filler = _chain(project_out(*subs[s - 1]), filler)
            n_filler += PROJECT_OUT_DOTS + subs[s - 1][1] // CHUNK
        _interleave(heads, filler, n_filler / ((subs[s][1] // CHUNK) * HEAD_STAGES_PER_CHUNK))
        filler = iter(())
    for _ in project_out(*subs[-1]):
        pass


def _layer(x, mem, w_in_bf16, w_kv, w_out, pre_g, mem_g, post_g, vng, vnb, w_spatial,
           b_spatial, buckets, sinks, rel_bias):
    batch, seq, _ = x.shape
    ts = SEQ_TILE
    n_tiles = seq // ts
    const2 = lambda b, t, *_: (0, 0)
    const3 = lambda b, t, *_: (0, 0, 0)
    grid_spec = pltpu.PrefetchScalarGridSpec(
        num_scalar_prefetch=1,
        grid=(batch, n_tiles),
        in_specs=[
            pl.BlockSpec((1, ts, D_MODEL), lambda b, t, *_: (b, t, 0)),
            pl.BlockSpec((1, MEM_LEN, D_MODEL), lambda b, t, *_: (b, 0, 0)),
            pl.BlockSpec((D_MODEL, IN_WIDTH), const2),
            pl.BlockSpec((D_MODEL, 2 * MEM_WIDTH), const2),
            pl.BlockSpec((MIX_WIDTH, D_MODEL), const2),
            pl.BlockSpec((1, D_MODEL), const2),
            pl.BlockSpec((1, D_MODEL), const2),
            pl.BlockSpec((1, D_MODEL), const2),
            pl.BlockSpec((1, A_WIDTH), const2),
            pl.BlockSpec((1, A_WIDTH), const2),
            pl.BlockSpec((A_GROUPS, CHUNK, CHUNK), const3),
            pl.BlockSpec((A_GROUPS, CHUNK), const2),
            pl.BlockSpec((N_BUCKETS, SWA_HEADS), const2),
            pl.BlockSpec((CHUNK, 2 * CHUNK), const2),
        ],
        out_specs=pl.BlockSpec((1, ts, D_MODEL), lambda b, t, *_: (b, t, 0)),
        scratch_shapes=[
            pltpu.VMEM((ts, D_MODEL), BF16),
            pltpu.VMEM((ts, A_WIDTH), F32),
            pltpu.VMEM((ts, A_WIDTH), BF16),
            pltpu.VMEM((ts, SWA_WIDTH + MEM_WIDTH), BF16),
            pltpu.VMEM((SWA_HEADS, LANES, CHUNK + ts), BF16),
            pltpu.VMEM((SWA_HEADS, CHUNK + ts, LANES), BF16),
            pltpu.VMEM((ts, MIX_WIDTH), F32),
            pltpu.VMEM((MEM_HEADS, LANES, MEM_LEN), BF16),
            pltpu.VMEM((MEM_HEADS, MEM_LEN, LANES), BF16),
            pltpu.VMEM((2, SWA_HEADS, CHUNK, 2 * CHUNK), F32),
            pltpu.VMEM((A_GROUPS, CHUNK, CHUNK), BF16),
            pltpu.VMEM((A_GROUPS, CHUNK, LANES), F32),
            pltpu.VMEM((D_MODEL, 2 * MEM_WIDTH), BF16),
            pltpu.VMEM((MIX_WIDTH, D_MODEL), BF16),
            pltpu.VMEM((ts, MIX_WIDTH), F32),
            pltpu.VMEM((ts, MIX_WIDTH), BF16),
            pltpu.VMEM((ts, D_MODEL), F32),
        ],
    )
    return pl.pallas_call(
        _layer_kernel,
        grid_spec=grid_spec,
        out_shape=jax.ShapeDtypeStruct(x.shape, x.dtype),
        compiler_params=pltpu.CompilerParams(
            dimension_semantics=("arbitrary", "arbitrary"),
            vmem_limit_bytes=VMEM_LIMIT_BYTES),
        name="layer",
    )(sinks, x, mem, w_in_bf16, w_kv, w_out, pre_g, mem_g, post_g, vng, vnb,
      w_spatial, b_spatial, rel_bias, buckets)


def kernel(x, mem, pre_norm_g, post_norm_g, mem_norm_g, w_in, w_mem_kv, v_norm_g, v_norm_b,
           w_spatial, b_spatial, attn_sinks, rel_bias, w_out):
    depth = w_in.shape[0]
    buckets = jnp.asarray(_band_buckets())
    for layer in range(depth):
        x = _layer(x, mem, w_in[layer].astype(BF16), w_mem_kv[layer], w_out[layer],
                   pre_norm_g[layer][None, :], mem_norm_g[layer][None, :], post_norm_g[layer][None, :],
                   v_norm_g[layer][None, :], v_norm_b[layer][None, :],
                   w_spatial[layer], b_spatial[layer], buckets, attn_sinks[layer], rel_bias)
    return x
```

```python
import numpy as np
import jax
import jax.numpy as jnp
from jax import lax
from jax.experimental import pallas as pl
from jax.experimental.pallas import tpu as pltpu

D_MODEL = 1024
MEM_LEN = 256
HEAD_DIM = 64
CHUNK = 128
A_GROUPS = 4
A_WIDTH = 512
SWA_HEADS = 4
SWA_WIDTH = 256
SWA_KV_WIDTH = 128
MEM_HEADS = 4
MEM_WIDTH = 256
MIX_WIDTH = 1024
IN_WIDTH = 2816
N_BUCKETS = 32
MAX_DISTANCE = 128
WINDOW = 128
EPS = 1e-6
NEG = -1e30
LOG2E = float(np.log2(np.e))
Q_SCALE = HEAD_DIM ** -0.5 * LOG2E

OFF_U = 0
OFF_V = A_WIDTH
OFF_SQ = 2 * A_WIDTH
OFF_SK = OFF_SQ + SWA_WIDTH
OFF_SV = OFF_SK + SWA_KV_WIDTH
OFF_MQ = OFF_SV + SWA_KV_WIDTH
OFF_Z = OFF_MQ + MEM_WIDTH

LANES = 128
SEQ_TILE = 1024
SUB_TILES = (256, 256, 256, 256)
PROJ_COLS = 512
ITEM_LAG = 2
PROJECT_COLUMN_PIECES = 6
PROJECT_OUT_DOTS = D_MODEL // PROJ_COLS
HEAD_STAGES_PER_CHUNK = 2 * (SWA_HEADS // 2 + MEM_HEADS // 2)
W_IN_SLAB = 128
VMEM_LIMIT_BYTES = 56 * 1024 * 1024

BF16 = jnp.bfloat16
F32 = jnp.float32


def _t5_causal_buckets(dist):
    n = np.maximum(dist, 0)
    max_exact = N_BUCKETS // 2
    large = max_exact + (np.log(np.maximum(n, 1) / max_exact) / np.log(MAX_DISTANCE / max_exact)
                         * (N_BUCKETS - max_exact)).astype(np.int32)
    large = np.minimum(large, N_BUCKETS - 1)
    return np.where(n < max_exact, n, large).astype(np.int32)


def _band_buckets():
    qi = np.arange(CHUNK)[:, None]
    kj = np.arange(2 * CHUNK)[None, :]
    dist = qi + CHUNK - kj
    valid = (dist >= 0) & (dist < WINDOW)
    return np.where(valid, _t5_causal_buckets(dist), -1).astype(np.int32)


def _gelu_tanh(x):
    c = np.float32(np.sqrt(2.0 / np.pi))
    ck = np.float32(np.sqrt(2.0 / np.pi) * 0.044715)
    hx = 0.5 * x
    return hx + hx * jnp.tanh(x * (c + ck * (x * x)))


def _silu(z):
    hz = 0.5 * z
    return hz + hz * jnp.tanh(hz)


def _dot(a, b):
    return jnp.dot(a, b, preferred_element_type=F32)


_DONE = object()


def _skewed(items, lag):
    waiting = []
    for item in items:
        if next(item, _DONE) is not _DONE:
            waiting.append(item)
        yield
        if len(waiting) > lag:
            next(waiting.pop(0), _DONE)
            yield
    for item in waiting:
        next(item, _DONE)
        yield


def _chain(*gens):
    for g in gens:
        yield from g


def _interleave(a, b, b_per_a):
    credit = 0.0
    a_live = b_live = True
    while a_live or b_live:
        if a_live:
            a_live = next(a, _DONE) is not _DONE
        credit += b_per_a
        while b_live and (credit >= 1.0 or not a_live):
            b_live = next(b, _DONE) is not _DONE
            credit -= 1.0


def _lane_half_mask(parity):
    lane = lax.broadcasted_iota(jnp.int32, (1, LANES), 1)
    return (lane >= HEAD_DIM) if parity else (lane < HEAD_DIM)


def _place_head_rows(kt_head, parity):
    zeros = jnp.zeros_like(kt_head)
    return jnp.concatenate([zeros, kt_head] if parity else [kt_head, zeros], axis=0)


def _layer_kernel(sinks_ref,
                  x_ref, mem_ref, win_hbm_ref, wkv_f32_ref, wout_f32_ref, preg_ref, memg_ref, postg_ref,
                  vng_ref, vnb_ref, ws_ref, bs_ref, relb_ref, buckets_ref,
                  out_ref,
                  h_scr, gu_scr, vn_scr, qs_scr, ktvar_scr, vvar_scr, zg_scr,
                  mk_scr, mv_scr, bias_scr, wsm_scr, bst_scr, win_ref, wkv_ref, wout_ref,
                  win_stage, win_sem, y_scr, ybf_scr, o_scr):
    ts = x_ref.shape[1]
    b = pl.program_id(0)
    t = pl.program_id(1)

    @pl.when((b == 0) & (t == 0))
    def _init():
        row = lax.broadcasted_iota(jnp.int32, (CHUNK, CHUNK), 0)
        col = lax.broadcasted_iota(jnp.int32, (CHUNK, CHUNK), 1)
        for g in range(A_GROUPS):
            wsm_scr[g] = jnp.where(row >= col, ws_ref[g], 0.0).astype(BF16)
            bs_col = jnp.sum(jnp.where(row == col, bs_ref[g:g + 1, :], 0.0), axis=1, keepdims=True)
            bst_scr[g] = jnp.broadcast_to(bs_col, (CHUNK, LANES))
        n_slabs = D_MODEL // W_IN_SLAB

        def slab_copy(k):
            return pltpu.make_async_copy(win_hbm_ref.at[pl.ds(k * W_IN_SLAB, W_IN_SLAB), :],
                                         win_stage.at[k % 2], win_sem.at[k % 2])

        slab_copy(0).start()
        for k in range(n_slabs):
            if k + 1 < n_slabs:
                slab_copy(k + 1).start()
            slab_copy(k).wait()
            win_ref[pl.ds(k * W_IN_SLAB, W_IN_SLAB), :] = win_stage[k % 2].astype(BF16)
        for r0 in range(0, D_MODEL, 2 * CHUNK):
            wrows = pl.ds(r0, 2 * CHUNK)
            wkv_ref[wrows, :] = wkv_f32_ref[wrows, :].astype(BF16)
            wout_ref[wrows, :] = wout_f32_ref[wrows, :].astype(BF16)
        buckets = buckets_ref[...]
        prev_key = lax.broadcasted_iota(jnp.int32, (CHUNK, 2 * CHUNK), 1) < CHUNK
        for hd in range(SWA_HEADS):
            tbl = jnp.full((CHUNK, 2 * CHUNK), NEG, F32)
            for bk in range(N_BUCKETS):
                tbl = jnp.where(buckets == bk, relb_ref[bk:bk + 1, hd:hd + 1] * LOG2E, tbl)
            bias_scr[0, hd] = tbl
            bias_scr[1, hd] = jnp.where(prev_key, NEG, tbl)

    @pl.when(t == 0)
    def _sequence_start():
        ktvar_scr[:, :, 0:CHUNK] = jnp.zeros((SWA_HEADS, LANES, CHUNK), BF16)
        vvar_scr[:, 0:CHUNK, :] = jnp.zeros((SWA_HEADS, CHUNK, LANES), BF16)
        m = mem_ref[0]
        ms = jnp.mean(m * m, axis=-1, keepdims=True)
        hm = (m * lax.rsqrt(ms + EPS) * memg_ref[...]).astype(BF16)
        kv = _dot(hm, wkv_ref[...])
        for pair in range(MEM_HEADS // 2):
            kt_pair = kv[:, pair * LANES:(pair + 1) * LANES].T
            v_pair = kv[:, MEM_WIDTH + pair * LANES:MEM_WIDTH + (pair + 1) * LANES]
            for parity in range(2):
                hd = 2 * pair + parity
                kt_head = kt_pair[parity * HEAD_DIM:(parity + 1) * HEAD_DIM]
                mk_scr[hd] = _place_head_rows(kt_head, parity).astype(BF16)
                mv_scr[hd] = jnp.where(_lane_half_mask(parity), v_pair, 0.0).astype(BF16)


    @pl.when(t > 0)
    def _copy_carry():
        ktvar_scr[:, :, 0:CHUNK] = ktvar_scr[:, :, ts:ts + CHUNK]
        vvar_scr[:, 0:CHUNK, :] = vvar_scr[:, ts:ts + CHUNK, :]

    lo = _lane_half_mask(0)
    hi = _lane_half_mask(1)
    first_tile = jnp.where(t == 0, 1, 0)

    def pre_norm(r0, nrows):
        for rc in range(nrows // CHUNK):
            crows = pl.ds(r0 + rc * CHUNK, CHUNK)
            x = x_ref[0, crows, :]
            ms = jnp.mean(x * x, axis=-1, keepdims=True)
            h_scr[crows, :] = (x * lax.rsqrt(ms + EPS) * preg_ref[...]).astype(BF16)

    def proj(rows, c0, width):
        return _dot(h_scr[rows, :], win_ref[:, c0:c0 + width])

    def project(r0, nrows):
        rows = pl.ds(r0, nrows)

        v_all = _gelu_tanh(proj(rows, OFF_V, A_WIDTH))
        for g in range(A_GROUPS):
            cols = pl.ds(g * LANES, LANES)
            v = v_all[:, g * LANES:(g + 1) * LANES]
            mu = jnp.mean(v, axis=-1, keepdims=True)
            d = v - mu
            var = jnp.mean(d * d, axis=-1, keepdims=True)
            vn = d * lax.rsqrt(var + EPS) * vng_ref[:, cols] + vnb_ref[:, cols]
            vn_scr[rows, cols] = vn.astype(BF16)
        yield

        gu_scr[rows, :] = _gelu_tanh(proj(rows, OFF_U, A_WIDTH))
        yield

        qkv = proj(rows, OFF_SQ, SWA_WIDTH + 2 * SWA_KV_WIDTH)
        qs_scr[rows, 0:SWA_WIDTH] = (qkv[:, 0:SWA_WIDTH] * Q_SCALE).astype(BF16)
        skt = qkv[:, SWA_WIDTH:SWA_WIDTH + LANES].T
        sv = qkv[:, SWA_WIDTH + LANES:]
        sv_rot = pltpu.roll(sv, HEAD_DIM, 1)
        new_rows = pl.ds(CHUNK + r0, nrows)
        for hd, (keep, vsrc) in enumerate([(lo, sv), (hi, sv_rot), (lo, sv_rot), (hi, sv)]):
            kv_head = hd // 2
            kt_head = skt[kv_head * HEAD_DIM:(kv_head + 1) * HEAD_DIM]
            ktvar_scr[hd, :, new_rows] = _place_head_rows(kt_head, hd % 2).astype(BF16)
            vvar_scr[hd, new_rows, :] = jnp.where(keep, vsrc, 0.0).astype(BF16)
        yield

        mqz = proj(rows, OFF_MQ, PROJ_COLS)
        qs_scr[rows, SWA_WIDTH:] = (mqz[:, 0:MEM_WIDTH] * Q_SCALE).astype(BF16)
        zg_scr[rows, 0:PROJ_COLS - MEM_WIDTH] = _silu(mqz[:, MEM_WIDTH:])
        yield

        for c in range(r0 // CHUNK, (r0 + nrows) // CHUNK):
            for g in range(A_GROUPS):
                spatial_gating(c, g)
            yield

        done = PROJ_COLS - MEM_WIDTH
        while done < MIX_WIDTH:
            width = min(PROJ_COLS, MIX_WIDTH - done)
            zg_scr[rows, done:done + width] = _silu(proj(rows, OFF_Z + done, width))
            done += width
            yield

    def spatial_gating(c, g):
        rows = pl.ds(c * CHUNK, CHUNK)
        cols = pl.ds(g * LANES, LANES)
        sv_g = _dot(wsm_scr[g], vn_scr[rows, cols]) + bst_scr[g]
        y_scr[rows, cols] = gu_scr[rows, cols] * sv_g

    def attn_item(c, pair, sliding):
        r0 = c * CHUNK
        rows = pl.ds(r0, CHUNK)
        heads = (2 * pair, 2 * pair + 1)
        if sliding:
            band = pl.ds(r0, 2 * CHUNK)
            qp = qs_scr[rows, pl.ds(pair * LANES, LANES)]
            keys = jnp.concatenate([ktvar_scr[hd, :, band] for hd in heads], axis=1)
            bias_sel = first_tile if c == 0 else 0
            bias = jnp.concatenate([bias_scr[bias_sel, hd] for hd in heads], axis=1)
            logits = _dot(qp, keys) + bias
        else:
            qp = qs_scr[rows, pl.ds(SWA_WIDTH + pair * LANES, LANES)]
            keys = jnp.concatenate([mk_scr[hd] for hd in heads], axis=1)
            logits = _dot(qp, keys)
        n_keys = logits.shape[1] // 2
        yield
        probs, recips = [], []
        for i, hd in enumerate(heads):
            s = logits[:, i * n_keys:(i + 1) * n_keys]
            m = jnp.max(s, axis=-1, keepdims=True)
            if sliding:
                sink = sinks_ref[hd] * LOG2E
                m = jnp.maximum(m, sink)
            p = jnp.exp2(s - m)
            denom = jnp.sum(p, axis=-1, keepdims=True)
            if sliding:
                denom = denom + jnp.exp2(sink - m)
            probs.append(p.astype(BF16))
            recips.append(1.0 / denom)
        if sliding:
            values = jnp.concatenate([vvar_scr[hd, band, :] for hd in heads], axis=0)
        else:
            values = jnp.concatenate([mv_scr[hd] for hd in heads], axis=0)
        o_pair = _dot(jnp.concatenate(probs, axis=1), values)
        o_pair = o_pair * jnp.where(lo, recips[0], recips[1])
        off = A_WIDTH if sliding else A_WIDTH + SWA_WIDTH
        y_scr[rows, pl.ds(off + pair * LANES, LANES)] = o_pair

    def head_items(r0, nrows):
        for c in range(r0 // CHUNK, (r0 + nrows) // CHUNK):
            for pair in range(SWA_HEADS // 2):
                yield attn_item(c, pair, True)
            for pair in range(MEM_HEADS // 2):
                yield attn_item(c, pair, False)

    def project_out(r0, nrows):
        rows = pl.ds(r0, nrows)
        ybf_scr[rows, :] = (y_scr[rows, :] * zg_scr[rows, :]).astype(BF16)
        for c0 in range(0, D_MODEL, PROJ_COLS):
            o_scr[rows, c0:c0 + PROJ_COLS] = _dot(ybf_scr[rows, :], wout_ref[:, c0:c0 + PROJ_COLS])
            yield
        for rc in range(nrows // CHUNK):
            crows = pl.ds(r0 + rc * CHUNK, CHUNK)
            o = o_scr[crows, :]
            ms = jnp.mean(o * o, axis=-1, keepdims=True)
            out_ref[0, crows, :] = x_ref[0, crows, :] + o * lax.rsqrt(ms + EPS) * postg_ref[...]
            yield

    assert sum(SUB_TILES) == ts
    starts = [sum(SUB_TILES[:s]) for s in range(len(SUB_TILES))]
    subs = list(zip(starts, SUB_TILES))
    n_sub = len(subs)
    pre_norm(*subs[0])
    filler = project(*subs[0])
    for s in range(n_sub):
        if s + 1 < n_sub:
            pre_norm(*subs[s + 1])
        for _ in filler:
            pass
        heads = _skewed(head_items(*subs[s]), ITEM_LAG)
        n_filler = 0
        if s + 1 < n_sub:
            filler = project(*subs[s + 1])
            n_filler += PROJECT_COLUMN_PIECES + subs[s + 1][1] // CHUNK
        else:
            filler = iter(())
        if s >= 1:
            filler = _chain(project_out(*subs[s - 1]), filler)
            n_filler += PROJECT_OUT_DOTS + subs[s - 1][1] // CHUNK
        _interleave(heads, filler, n_filler / ((subs[s][1] // CHUNK) * HEAD_STAGES_PER_CHUNK))
        filler = iter(())
    for _ in project_out(*subs[-1]):
        pass


def _layer(x, mem, w_in, w_kv, w_out, pre_g, mem_g, post_g, vng, vnb, w_spatial,
           b_spatial, buckets, sinks, rel_bias):
    batch, seq, _ = x.shape
    ts = SEQ_TILE
    n_tiles = seq // ts
    const2 = lambda b, t, *_: (0, 0)
    const3 = lambda b, t, *_: (0, 0, 0)
    grid_spec = pltpu.PrefetchScalarGridSpec(
        num_scalar_prefetch=1,
        grid=(batch, n_tiles),
        in_specs=[
            pl.BlockSpec((1, ts, D_MODEL), lambda b, t, *_: (b, t, 0)),
            pl.BlockSpec((1, MEM_LEN, D_MODEL), lambda b, t, *_: (b, 0, 0)),
            pl.BlockSpec(memory_space=pl.ANY),
            pl.BlockSpec((D_MODEL, 2 * MEM_WIDTH), const2),
            pl.BlockSpec((MIX_WIDTH, D_MODEL), const2),
            pl.BlockSpec((1, D_MODEL), const2),
            pl.BlockSpec((1, D_MODEL), const2),
            pl.BlockSpec((1, D_MODEL), const2),
            pl.BlockSpec((1, A_WIDTH), const2),
            pl.BlockSpec((1, A_WIDTH), const2),
            pl.BlockSpec((A_GROUPS, CHUNK, CHUNK), const3),
            pl.BlockSpec((A_GROUPS, CHUNK), const2),
            pl.BlockSpec((N_BUCKETS, SWA_HEADS), const2),
            pl.BlockSpec((CHUNK, 2 * CHUNK), const2),
        ],
        out_specs=pl.BlockSpec((1, ts, D_MODEL), lambda b, t, *_: (b, t, 0)),
        scratch_shapes=[
            pltpu.VMEM((ts, D_MODEL), BF16),
            pltpu.VMEM((ts, A_WIDTH), F32),
            pltpu.VMEM((ts, A_WIDTH), BF16),
            pltpu.VMEM((ts, SWA_WIDTH + MEM_WIDTH), BF16),
            pltpu.VMEM((SWA_HEADS, LANES, CHUNK + ts), BF16),
            pltpu.VMEM((SWA_HEADS, CHUNK + ts, LANES), BF16),
            pltpu.VMEM((ts, MIX_WIDTH), F32),
            pltpu.VMEM((MEM_HEADS, LANES, MEM_LEN), BF16),
            pltpu.VMEM((MEM_HEADS, MEM_LEN, LANES), BF16),
            pltpu.VMEM((2, SWA_HEADS, CHUNK, 2 * CHUNK), F32),
            pltpu.VMEM((A_GROUPS, CHUNK, CHUNK), BF16),
            pltpu.VMEM((A_GROUPS, CHUNK, LANES), F32),
            pltpu.VMEM((D_MODEL, IN_WIDTH), BF16),
            pltpu.VMEM((D_MODEL, 2 * MEM_WIDTH), BF16),
            pltpu.VMEM((MIX_WIDTH, D_MODEL), BF16),
            pltpu.VMEM((2, W_IN_SLAB, IN_WIDTH), F32),
            pltpu.SemaphoreType.DMA((2,)),
            pltpu.VMEM((ts, MIX_WIDTH), F32),
            pltpu.VMEM((ts, MIX_WIDTH), BF16),
            pltpu.VMEM((ts, D_MODEL), F32),
        ],
    )
    return pl.pallas_call(
        _layer_kernel,
        grid_spec=grid_spec,
        out_shape=jax.ShapeDtypeStruct(x.shape, x.dtype),
        compiler_params=pltpu.CompilerParams(
            dimension_semantics=("arbitrary", "arbitrary"),
            vmem_limit_bytes=VMEM_LIMIT_BYTES),
        name="layer",
    )(sinks, x, mem, w_in, w_kv, w_out, pre_g, mem_g, post_g, vng, vnb,
      w_spatial, b_spatial, rel_bias, buckets)


def kernel(x, mem, pre_norm_g, post_norm_g, mem_norm_g, w_in, w_mem_kv, v_norm_g, v_norm_b,
           w_spatial, b_spatial, attn_sinks, rel_bias, w_out):
    depth = w_in.shape[0]
    buckets = jnp.asarray(_band_buckets())
    for layer in range(depth):
        x = _layer(x, mem, w_in[layer], w_mem_kv[layer], w_out[layer],
                   pre_norm_g[layer][None, :], mem_norm_g[layer][None, :], post_norm_g[layer][None, :],
                   v_norm_g[layer][None, :], v_norm_b[layer][None, :],
                   w_spatial[layer], b_spatial[layer], buckets, attn_sinks[layer], rel_bias)
    return x
```

```python
import numpy as np
import jax
import jax.numpy as jnp
from jax import lax
from jax.experimental import pallas as pl
from jax.experimental.pallas import tpu as pltpu

D_MODEL = 1024
MEM_LEN = 256
HEAD_DIM = 64
CHUNK = 128
A_GROUPS = 4
A_WIDTH = 512
SWA_HEADS = 4
SWA_WIDTH = 256
SWA_KV_WIDTH = 128
MEM_HEADS = 4
MEM_WIDTH = 256
MIX_WIDTH = 1024
IN_WIDTH = 2816
N_BUCKETS = 32
MAX_DISTANCE = 128
WINDOW = 128
EPS = 1e-6
NEG = -1e30
LOG2E = float(np.log2(np.e))
Q_SCALE = HEAD_DIM ** -0.5 * LOG2E

OFF_U = 0
OFF_V = A_WIDTH
OFF_SQ = 2 * A_WIDTH
OFF_SK = OFF_SQ + SWA_WIDTH
OFF_SV = OFF_SK + SWA_KV_WIDTH
OFF_MQ = OFF_SV + SWA_KV_WIDTH
OFF_Z = OFF_MQ + MEM_WIDTH

LANES = 128
SEQ_TILE = 1024
SUB_TILES = (256, 256, 256, 256)
PROJ_COLS = 512
ITEM_LAG = 2
PROJECT_COLUMN_PIECES = 6
PROJECT_OUT_DOTS = D_MODEL // PROJ_COLS
HEAD_STAGES_PER_CHUNK = 2 * (SWA_HEADS // 2 + MEM_HEADS // 2)
VMEM_LIMIT_BYTES = 56 * 1024 * 1024

BF16 = jnp.bfloat16
F32 = jnp.float32


def _t5_causal_buckets(dist):
    n = np.maximum(dist, 0)
    max_exact = N_BUCKETS // 2
    large = max_exact + (np.log(np.maximum(n, 1) / max_exact) / np.log(MAX_DISTANCE / max_exact)
                         * (N_BUCKETS - max_exact)).astype(np.int32)
    large = np.minimum(large, N_BUCKETS - 1)
    return np.where(n < max_exact, n, large).astype(np.int32)


def _band_buckets():
    qi = np.arange(CHUNK)[:, None]
    kj = np.arange(2 * CHUNK)[None, :]
    dist = qi + CHUNK - kj
    valid = (dist >= 0) & (dist < WINDOW)
    return np.where(valid, _t5_causal_buckets(dist), -1).astype(np.int32)


def _gelu_tanh(x):
    c = np.float32(np.sqrt(2.0 / np.pi))
    ck = np.float32(np.sqrt(2.0 / np.pi) * 0.044715)
    hx = 0.5 * x
    return hx + hx * jnp.tanh(x * (c + ck * (x * x)))


def _silu(z):
    hz = 0.5 * z
    return hz + hz * jnp.tanh(hz)


def _dot(a, b):
    return jnp.dot(a, b, preferred_element_type=F32)


_DONE = object()


def _skewed(items, lag):
    waiting = []
    for item in items:
        if next(item, _DONE) is not _DONE:
            waiting.append(item)
        yield
        if len(waiting) > lag:
            next(waiting.pop(0), _DONE)
            yield
    for item in waiting:
        next(item, _DONE)
        yield


def _chain(*gens):
    for g in gens:
        yield from g


def _interleave(a, b, b_per_a):
    credit = 0.0
    a_live = b_live = True
    while a_live or b_live:
        if a_live:
            a_live = next(a, _DONE) is not _DONE
        credit += b_per_a
        while b_live and (credit >= 1.0 or not a_live):
            b_live = next(b, _DONE) is not _DONE
            credit -= 1.0


def _lane_half_mask(parity):
    lane = lax.broadcasted_iota(jnp.int32, (1, LANES), 1)
    return (lane >= HEAD_DIM) if parity else (lane < HEAD_DIM)


def _place_head_rows(kt_head, parity):
    zeros = jnp.zeros_like(kt_head)
    return jnp.concatenate([zeros, kt_head] if parity else [kt_head, zeros], axis=0)


def _layer_kernel(sinks_ref,
                  x_ref, mem_ref, win_ref, wkv_f32_ref, wout_f32_ref, preg_ref, memg_ref, postg_ref,
                  vng_ref, vnb_ref, ws_ref, bs_ref, relb_ref, buckets_ref,
                  out_ref,
                  h_scr, gu_scr, vn_scr, qs_scr, ktvar_scr, vvar_scr, zg_scr,
                  mk_scr, mv_scr, bias_scr, wsm_scr, bst_scr, wkv_ref, wout_ref,
                  ybf_scr, o_scr):
    ts = x_ref.shape[1]
    b = pl.program_id(0)
    t = pl.program_id(1)

    @pl.when((b == 0) & (t == 0))
    def _init():
        row = lax.broadcasted_iota(jnp.int32, (CHUNK, CHUNK), 0)
        col = lax.broadcasted_iota(jnp.int32, (CHUNK, CHUNK), 1)
        for g in range(A_GROUPS):
            wsm_scr[g] = jnp.where(row >= col, ws_ref[g], 0.0).astype(BF16)
            bs_col = jnp.sum(jnp.where(row == col, bs_ref[g:g + 1, :], 0.0), axis=1, keepdims=True)
            bst_scr[g] = jnp.broadcast_to(bs_col, (CHUNK, LANES))
        for r0 in range(0, D_MODEL, 2 * CHUNK):
            wrows = pl.ds(r0, 2 * CHUNK)
            wkv_ref[wrows, :] = wkv_f32_ref[wrows, :].astype(BF16)
            wout_ref[wrows, :] = wout_f32_ref[wrows, :].astype(BF16)
        buckets = buckets_ref[...]
        prev_key = lax.broadcasted_iota(jnp.int32, (CHUNK, 2 * CHUNK), 1) < CHUNK
        for hd in range(SWA_HEADS):
            tbl = jnp.full((CHUNK, 2 * CHUNK), NEG, F32)
            for bk in range(N_BUCKETS):
                tbl = jnp.where(buckets == bk, relb_ref[bk:bk + 1, hd:hd + 1] * LOG2E, tbl)
            bias_scr[0, hd] = tbl
            bias_scr[1, hd] = jnp.where(prev_key, NEG, tbl)

    @pl.when(t == 0)
    def _sequence_start():
        ktvar_scr[:, :, 0:CHUNK] = jnp.zeros((SWA_HEADS, LANES, CHUNK), BF16)
        vvar_scr[:, 0:CHUNK, :] = jnp.zeros((SWA_HEADS, CHUNK, LANES), BF16)
        m = mem_ref[0]
        ms = jnp.mean(m * m, axis=-1, keepdims=True)
        hm = (m * lax.rsqrt(ms + EPS) * memg_ref[...]).astype(BF16)
        kv = _dot(hm, wkv_ref[...])
        for pair in range(MEM_HEADS // 2):
            kt_pair = kv[:, pair * LANES:(pair + 1) * LANES].T
            v_pair = kv[:, MEM_WIDTH + pair * LANES:MEM_WIDTH + (pair + 1) * LANES]
            for parity in range(2):
                hd = 2 * pair + parity
                kt_head = kt_pair[parity * HEAD_DIM:(parity + 1) * HEAD_DIM]
                mk_scr[hd] = _place_head_rows(kt_head, parity).astype(BF16)
                mv_scr[hd] = jnp.where(_lane_half_mask(parity), v_pair, 0.0).astype(BF16)


    @pl.when(t > 0)
    def _copy_carry():
        ktvar_scr[:, :, 0:CHUNK] = ktvar_scr[:, :, ts:ts + CHUNK]
        vvar_scr[:, 0:CHUNK, :] = vvar_scr[:, ts:ts + CHUNK, :]

    lo = _lane_half_mask(0)
    hi = _lane_half_mask(1)
    first_tile = jnp.where(t == 0, 1, 0)

    def pre_norm(r0, nrows):
        for rc in range(nrows // CHUNK):
            crows = pl.ds(r0 + rc * CHUNK, CHUNK)
            x = x_ref[0, crows, :]
            ms = jnp.mean(x * x, axis=-1, keepdims=True)
            h_scr[crows, :] = (x * lax.rsqrt(ms + EPS) * preg_ref[...]).astype(BF16)

    def proj(rows, c0, width):
        return _dot(h_scr[rows, :], win_ref[:, c0:c0 + width])

    def project(r0, nrows):
        rows = pl.ds(r0, nrows)

        v_all = _gelu_tanh(proj(rows, OFF_V, A_WIDTH))
        for g in range(A_GROUPS):
            cols = pl.ds(g * LANES, LANES)
            v = v_all[:, g * LANES:(g + 1) * LANES]
            mu = jnp.mean(v, axis=-1, keepdims=True)
            d = v - mu
            var = jnp.mean(d * d, axis=-1, keepdims=True)
            vn = d * lax.rsqrt(var + EPS) * vng_ref[:, cols] + vnb_ref[:, cols]
            vn_scr[rows, cols] = vn.astype(BF16)
        yield

        gu_scr[rows, :] = _gelu_tanh(proj(rows, OFF_U, A_WIDTH))
        yield

        qkv = proj(rows, OFF_SQ, SWA_WIDTH + 2 * SWA_KV_WIDTH)
        qs_scr[rows, 0:SWA_WIDTH] = (qkv[:, 0:SWA_WIDTH] * Q_SCALE).astype(BF16)
        skt = qkv[:, SWA_WIDTH:SWA_WIDTH + LANES].T
        sv = qkv[:, SWA_WIDTH + LANES:]
        sv_rot = pltpu.roll(sv, HEAD_DIM, 1)
        new_rows = pl.ds(CHUNK + r0, nrows)
        for hd, (keep, vsrc) in enumerate([(lo, sv), (hi, sv_rot), (lo, sv_rot), (hi, sv)]):
            kv_head = hd // 2
            kt_head = skt[kv_head * HEAD_DIM:(kv_head + 1) * HEAD_DIM]
            ktvar_scr[hd, :, new_rows] = _place_head_rows(kt_head, hd % 2).astype(BF16)
            vvar_scr[hd, new_rows, :] = jnp.where(keep, vsrc, 0.0).astype(BF16)
        yield

        mqz = proj(rows, OFF_MQ, PROJ_COLS)
        qs_scr[rows, SWA_WIDTH:] = (mqz[:, 0:MEM_WIDTH] * Q_SCALE).astype(BF16)
        zg_scr[rows, 0:PROJ_COLS - MEM_WIDTH] = _silu(mqz[:, MEM_WIDTH:])
        yield

        done = PROJ_COLS - MEM_WIDTH
        while done < MIX_WIDTH:
            width = min(PROJ_COLS, MIX_WIDTH - done)
            zg_scr[rows, done:done + width] = _silu(proj(rows, OFF_Z + done, width))
            done += width
            yield

        for c in range(r0 // CHUNK, (r0 + nrows) // CHUNK):
            for g in range(A_GROUPS):
                spatial_gating(c, g)
            yield

    def spatial_gating(c, g):
        rows = pl.ds(c * CHUNK, CHUNK)
        cols = pl.ds(g * LANES, LANES)
        sv_g = _dot(wsm_scr[g], vn_scr[rows, cols]) + bst_scr[g]
        ybf_scr[rows, cols] = (gu_scr[rows, cols] * sv_g * zg_scr[rows, cols]).astype(BF16)

    def attn_item(c, pair, sliding):
        r0 = c * CHUNK
        rows = pl.ds(r0, CHUNK)
        heads = (2 * pair, 2 * pair + 1)
        if sliding:
            band = pl.ds(r0, 2 * CHUNK)
            qp = qs_scr[rows, pl.ds(pair * LANES, LANES)]
            keys = jnp.concatenate([ktvar_scr[hd, :, band] for hd in heads], axis=1)
            bias_sel = first_tile if c == 0 else 0
            bias = jnp.concatenate([bias_scr[bias_sel, hd] for hd in heads], axis=1)
            logits = _dot(qp, keys) + bias
        else:
            qp = qs_scr[rows, pl.ds(SWA_WIDTH + pair * LANES, LANES)]
            keys = jnp.concatenate([mk_scr[hd] for hd in heads], axis=1)
            logits = _dot(qp, keys)
        n_keys = logits.shape[1] // 2
        yield
        probs, recips = [], []
        for i, hd in enumerate(heads):
            s = logits[:, i * n_keys:(i + 1) * n_keys]
            m = jnp.max(s, axis=-1, keepdims=True)
            if sliding:
                sink = sinks_ref[hd] * LOG2E
                m = jnp.maximum(m, sink)
            p = jnp.exp2(s - m)
            denom = jnp.sum(p, axis=-1, keepdims=True)
            if sliding:
                denom = denom + jnp.exp2(sink - m)
            probs.append(p.astype(BF16))
            recips.append(1.0 / denom)
        if sliding:
            values = jnp.concatenate([vvar_scr[hd, band, :] for hd in heads], axis=0)
        else:
            values = jnp.concatenate([mv_scr[hd] for hd in heads], axis=0)
        o_pair = _dot(jnp.concatenate(probs, axis=1), values)
        o_pair = o_pair * jnp.where(lo, recips[0], recips[1])
        cols = pl.ds((A_WIDTH if sliding else A_WIDTH + SWA_WIDTH) + pair * LANES, LANES)
        ybf_scr[rows, cols] = (o_pair * zg_scr[rows, cols]).astype(BF16)

    def head_items(r0, nrows):
        for c in range(r0 // CHUNK, (r0 + nrows) // CHUNK):
            for pair in range(SWA_HEADS // 2):
                yield attn_item(c, pair, True)
            for pair in range(MEM_HEADS // 2):
                yield attn_item(c, pair, False)

    def project_out(r0, nrows):
        rows = pl.ds(r0, nrows)
        for c0 in range(0, D_MODEL, PROJ_COLS):
            o_scr[rows, c0:c0 + PROJ_COLS] = _dot(ybf_scr[rows, :], wout_ref[:, c0:c0 + PROJ_COLS])
            yield
        for rc in range(nrows // CHUNK):
            crows = pl.ds(r0 + rc * CHUNK, CHUNK)
            o = o_scr[crows, :]
            ms = jnp.mean(o * o, axis=-1, keepdims=True)
            out_ref[0, crows, :] = x_ref[0, crows, :] + o * lax.rsqrt(ms + EPS) * postg_ref[...]
            yield

    assert sum(SUB_TILES) == ts
    starts = [sum(SUB_TILES[:s]) for s in range(len(SUB_TILES))]
    subs = list(zip(starts, SUB_TILES))
    n_sub = len(subs)
    pre_norm(*subs[0])
    filler = project(*subs[0])
    for s in range(n_sub):
        if s + 1 < n_sub:
            pre_norm(*subs[s + 1])
        for _ in filler:
            pass
        heads = _skewed(head_items(*subs[s]), ITEM_LAG)
        n_filler = 0
        if s + 1 < n_sub:
            filler = project(*subs[s + 1])
            n_filler += PROJECT_COLUMN_PIECES + subs[s + 1][1] // CHUNK
        else:
            filler = iter(())
        if s >= 1:
            filler = _chain(project_out(*subs[s - 1]), filler)
            n_filler += PROJECT_OUT_DOTS + subs[s - 1][1] // CHUNK
        _interleave(heads, filler, n_filler / ((subs[s][1] // CHUNK) * HEAD_STAGES_PER_CHUNK))
        filler = iter(())
    for _ in project_out(*subs[-1]):
        pass


def _layer(x, mem, w_in_bf16, w_kv, w_out, pre_g, mem_g, post_g, vng, vnb, w_spatial,
           b_spatial, buckets, sinks, rel_bias):
    batch, seq, _ = x.shape
    ts = SEQ_TILE
    n_tiles = seq // ts
    const2 = lambda b, t, *_: (0, 0)
    const3 = lambda b, t, *_: (0, 0, 0)
    grid_spec = pltpu.PrefetchScalarGridSpec(
        num_scalar_prefetch=1,
        grid=(batch, n_tiles),
        in_specs=[
            pl.BlockSpec((1, ts, D_MODEL), lambda b, t, *_: (b, t, 0)),
            pl.BlockSpec((1, MEM_LEN, D_MODEL), lambda b, t, *_: (b, 0, 0)),
            pl.BlockSpec((D_MODEL, IN_WIDTH), const2),
            pl.BlockSpec((D_MODEL, 2 * MEM_WIDTH), const2),
            pl.BlockSpec((MIX_WIDTH, D_MODEL), const2),
            pl.BlockSpec((1, D_MODEL), const2),
            pl.BlockSpec((1, D_MODEL), const2),
            pl.BlockSpec((1, D_MODEL), const2),
            pl.BlockSpec((1, A_WIDTH), const2),
            pl.BlockSpec((1, A_WIDTH), const2),
            pl.BlockSpec((A_GROUPS, CHUNK, CHUNK), const3),
            pl.BlockSpec((A_GROUPS, CHUNK), const2),
            pl.BlockSpec((N_BUCKETS, SWA_HEADS), const2),
            pl.BlockSpec((CHUNK, 2 * CHUNK), const2),
        ],
        out_specs=pl.BlockSpec((1, ts, D_MODEL), lambda b, t, *_: (b, t, 0)),
        scratch_shapes=[
            pltpu.VMEM((ts, D_MODEL), BF16),
            pltpu.VMEM((ts, A_WIDTH), F32),
            pltpu.VMEM((ts, A_WIDTH), BF16),
            pltpu.VMEM((ts, SWA_WIDTH + MEM_WIDTH), BF16),
            pltpu.VMEM((SWA_HEADS, LANES, CHUNK + ts), BF16),
            pltpu.VMEM((SWA_HEADS, CHUNK + ts, LANES), BF16),
            pltpu.VMEM((ts, MIX_WIDTH), F32),
            pltpu.VMEM((MEM_HEADS, LANES, MEM_LEN), BF16),
            pltpu.VMEM((MEM_HEADS, MEM_LEN, LANES), BF16),
            pltpu.VMEM((2, SWA_HEADS, CHUNK, 2 * CHUNK), F32),
            pltpu.VMEM((A_GROUPS, CHUNK, CHUNK), BF16),
            pltpu.VMEM((A_GROUPS, CHUNK, LANES), F32),
            pltpu.VMEM((D_MODEL, 2 * MEM_WIDTH), BF16),
            pltpu.VMEM((MIX_WIDTH, D_MODEL), BF16),
            pltpu.VMEM((ts, MIX_WIDTH), BF16),
            pltpu.VMEM((ts, D_MODEL), F32),
        ],
    )
    return pl.pallas_call(
        _layer_kernel,
        grid_spec=grid_spec,
        out_shape=jax.ShapeDtypeStruct(x.shape, x.dtype),
        compiler_params=pltpu.CompilerParams(
            dimension_semantics=("arbitrary", "arbitrary"),
            vmem_limit_bytes=VMEM_LIMIT_BYTES),
        name="layer",
    )(sinks, x, mem, w_in_bf16, w_kv, w_out, pre_g, mem_g, post_g, vng, vnb,
      w_spatial, b_spatial, rel_bias, buckets)


def kernel(x, mem, pre_norm_g, post_norm_g, mem_norm_g, w_in, w_mem_kv, v_norm_g, v_norm_b,
           w_spatial, b_spatial, attn_sinks, rel_bias, w_out):
    depth = w_in.shape[0]
    buckets = jnp.asarray(_band_buckets())
    for layer in range(depth):
        x = _layer(x, mem, w_in[layer].astype(BF16), w_mem_kv[layer], w_out[layer],
                   pre_norm_g[layer][None, :], mem_norm_g[layer][None, :], post_norm_g[layer][None, :],
                   v_norm_g[layer][None, :], v_norm_b[layer][None, :],
                   w_spatial[layer], b_spatial[layer], buckets, attn_sinks[layer], rel_bias)
    return x
```

```python
import numpy as np
import jax
import jax.numpy as jnp
from jax import lax
from jax.experimental import pallas as pl
from jax.experimental.pallas import tpu as pltpu

D_MODEL = 1024
MEM_LEN = 256
HEAD_DIM = 64
CHUNK = 128
A_GROUPS = 4
A_WIDTH = 512
SWA_HEADS = 4
SWA_WIDTH = 256
SWA_KV_WIDTH = 128
MEM_HEADS = 4
MEM_WIDTH = 256
MIX_WIDTH = 1024
IN_WIDTH = 2816
N_BUCKETS = 32
MAX_DISTANCE = 128
WINDOW = 128
EPS = 1e-6
NEG = -1e30
LOG2E = float(np.log2(np.e))
Q_SCALE = HEAD_DIM ** -0.5 * LOG2E

OFF_U = 0
OFF_V = A_WIDTH
OFF_SQ = 2 * A_WIDTH
OFF_SK = OFF_SQ + SWA_WIDTH
OFF_SV = OFF_SK + SWA_KV_WIDTH
OFF_MQ = OFF_SV + SWA_KV_WIDTH
OFF_Z = OFF_MQ + MEM_WIDTH

LANES = 128
SEQ_TILE = 1024
SUB_TILES = (256, 256, 256, 256)
PROJ_COLS = 512
ITEM_LAG = 3
PROJECT_COLUMN_PIECES = 6
PROJECT_OUT_DOTS = D_MODEL // PROJ_COLS
HEAD_STAGES_PER_CHUNK = 2 * (SWA_HEADS // 2 + MEM_HEADS // 2)
VMEM_LIMIT_BYTES = 56 * 1024 * 1024

BF16 = jnp.bfloat16
F32 = jnp.float32


def _t5_causal_buckets(dist):
    n = np.maximum(dist, 0)
    max_exact = N_BUCKETS // 2
    large = max_exact + (np.log(np.maximum(n, 1) / max_exact) / np.log(MAX_DISTANCE / max_exact)
                         * (N_BUCKETS - max_exact)).astype(np.int32)
    large = np.minimum(large, N_BUCKETS - 1)
    return np.where(n < max_exact, n, large).astype(np.int32)


def _band_buckets():
    qi = np.arange(CHUNK)[:, None]
    kj = np.arange(2 * CHUNK)[None, :]
    dist = qi + CHUNK - kj
    valid = (dist >= 0) & (dist < WINDOW)
    return np.where(valid, _t5_causal_buckets(dist), -1).astype(np.int32)


def _gelu_tanh(x):
    c = np.float32(np.sqrt(2.0 / np.pi))
    ck = np.float32(np.sqrt(2.0 / np.pi) * 0.044715)
    hx = 0.5 * x
    return hx + hx * jnp.tanh(x * (c + ck * (x * x)))


def _silu(z):
    hz = 0.5 * z
    return hz + hz * jnp.tanh(hz)


def _dot(a, b):
    return jnp.dot(a, b, preferred_element_type=F32)


_DONE = object()


def _skewed(items, lag):
    waiting = []
    for item in items:
        if next(item, _DONE) is not _DONE:
            waiting.append(item)
        yield
        if len(waiting) > lag:
            next(waiting.pop(0), _DONE)
            yield
    for item in waiting:
        next(item, _DONE)
        yield


def _chain(*gens):
    for g in gens:
        yield from g


def _interleave(a, b, b_per_a):
    credit = 0.0
    a_live = b_live = True
    while a_live or b_live:
        if a_live:
            a_live = next(a, _DONE) is not _DONE
        credit += b_per_a
        while b_live and (credit >= 1.0 or not a_live):
            b_live = next(b, _DONE) is not _DONE
            credit -= 1.0


def _lane_half_mask(parity):
    lane = lax.broadcasted_iota(jnp.int32, (1, LANES), 1)
    return (lane >= HEAD_DIM) if parity else (lane < HEAD_DIM)


def _place_head_rows(kt_head, parity):
    zeros = jnp.zeros_like(kt_head)
    return jnp.concatenate([zeros, kt_head] if parity else [kt_head, zeros], axis=0)


def _layer_kernel(sinks_ref,
                  x_ref, mem_ref, win_ref, wkv_f32_ref, wout_f32_ref, preg_ref, memg_ref, postg_ref,
                  vng_ref, vnb_ref, ws_ref, bs_ref, relb_ref, buckets_ref,
                  out_ref,
                  h_scr, gu_scr, vn_scr, qs_scr, ktvar_scr, vvar_scr, zg_scr,
                  mk_scr, mv_scr, bias_scr, wsm_scr, bst_scr, wkv_ref, wout_ref,
                  y_scr, ybf_scr, o_scr):
    ts = x_ref.shape[1]
    b = pl.program_id(0)
    t = pl.program_id(1)

    @pl.when((b == 0) & (t == 0))
    def _init():
        row = lax.broadcasted_iota(jnp.int32, (CHUNK, CHUNK), 0)
        col = lax.broadcasted_iota(jnp.int32, (CHUNK, CHUNK), 1)
        for g in range(A_GROUPS):
            wsm_scr[g] = jnp.where(row >= col, ws_ref[g], 0.0).astype(BF16)
            bs_col = jnp.sum(jnp.where(row == col, bs_ref[g:g + 1, :], 0.0), axis=1, keepdims=True)
            bst_scr[g] = jnp.broadcast_to(bs_col, (CHUNK, LANES))
        for r0 in range(0, D_MODEL, 2 * CHUNK):
            wrows = pl.ds(r0, 2 * CHUNK)
            wkv_ref[wrows, :] = wkv_f32_ref[wrows, :].astype(BF16)
            wout_ref[wrows, :] = wout_f32_ref[wrows, :].astype(BF16)
        buckets = buckets_ref[...]
        prev_key = lax.broadcasted_iota(jnp.int32, (CHUNK, 2 * CHUNK), 1) < CHUNK
        for hd in range(SWA_HEADS):
            tbl = jnp.full((CHUNK, 2 * CHUNK), NEG, F32)
            for bk in range(N_BUCKETS):
                tbl = jnp.where(buckets == bk, relb_ref[bk:bk + 1, hd:hd + 1] * LOG2E, tbl)
            bias_scr[0, hd] = tbl
            bias_scr[1, hd] = jnp.where(prev_key, NEG, tbl)

    @pl.when(t == 0)
    def _sequence_start():
        ktvar_scr[:, :, 0:CHUNK] = jnp.zeros((SWA_HEADS, LANES, CHUNK), BF16)
        vvar_scr[:, 0:CHUNK, :] = jnp.zeros((SWA_HEADS, CHUNK, LANES), BF16)
        m = mem_ref[0]
        ms = jnp.mean(m * m, axis=-1, keepdims=True)
        hm = (m * lax.rsqrt(ms + EPS) * memg_ref[...]).astype(BF16)
        kv = _dot(hm, wkv_ref[...])
        for pair in range(MEM_HEADS // 2):
            kt_pair = kv[:, pair * LANES:(pair + 1) * LANES].T
            v_pair = kv[:, MEM_WIDTH + pair * LANES:MEM_WIDTH + (pair + 1) * LANES]
            for parity in range(2):
                hd = 2 * pair + parity
                kt_head = kt_pair[parity * HEAD_DIM:(parity + 1) * HEAD_DIM]
                mk_scr[hd] = _place_head_rows(kt_head, parity).astype(BF16)
                mv_scr[hd] = jnp.where(_lane_half_mask(parity), v_pair, 0.0).astype(BF16)


    @pl.when(t > 0)
    def _copy_carry():
        ktvar_scr[:, :, 0:CHUNK] = ktvar_scr[:, :, ts:ts + CHUNK]
        vvar_scr[:, 0:CHUNK, :] = vvar_scr[:, ts:ts + CHUNK, :]

    lo = _lane_half_mask(0)
    hi = _lane_half_mask(1)
    first_tile = jnp.where(t == 0, 1, 0)

    def pre_norm(r0, nrows):
        for rc in range(nrows // CHUNK):
            crows = pl.ds(r0 + rc * CHUNK, CHUNK)
            x = x_ref[0, crows, :]
            ms = jnp.mean(x * x, axis=-1, keepdims=True)
            h_scr[crows, :] = (x * lax.rsqrt(ms + EPS) * preg_ref[...]).astype(BF16)

    def proj(rows, c0, width):
        return _dot(h_scr[rows, :], win_ref[:, c0:c0 + width])

    def project(r0, nrows):
        rows = pl.ds(r0, nrows)

        v_all = _gelu_tanh(proj(rows, OFF_V, A_WIDTH))
        for g in range(A_GROUPS):
            cols = pl.ds(g * LANES, LANES)
            v = v_all[:, g * LANES:(g + 1) * LANES]
            mu = jnp.mean(v, axis=-1, keepdims=True)
            d = v - mu
            var = jnp.mean(d * d, axis=-1, keepdims=True)
            vn = d * lax.rsqrt(var + EPS) * vng_ref[:, cols] + vnb_ref[:, cols]
            vn_scr[rows, cols] = vn.astype(BF16)
        yield

        gu_scr[rows, :] = _gelu_tanh(proj(rows, OFF_U, A_WIDTH))
        yield

        qkv = proj(rows, OFF_SQ, SWA_WIDTH + 2 * SWA_KV_WIDTH)
        qs_scr[rows, 0:SWA_WIDTH] = (qkv[:, 0:SWA_WIDTH] * Q_SCALE).astype(BF16)
        skt = qkv[:, SWA_WIDTH:SWA_WIDTH + LANES].T
        sv = qkv[:, SWA_WIDTH + LANES:]
        sv_rot = pltpu.roll(sv, HEAD_DIM, 1)
        new_rows = pl.ds(CHUNK + r0, nrows)
        for hd, (keep, vsrc) in enumerate([(lo, sv), (hi, sv_rot), (lo, sv_rot), (hi, sv)]):
            kv_head = hd // 2
            kt_head = skt[kv_head * HEAD_DIM:(kv_head + 1) * HEAD_DIM]
            ktvar_scr[hd, :, new_rows] = _place_head_rows(kt_head, hd % 2).astype(BF16)
            vvar_scr[hd, new_rows, :] = jnp.where(keep, vsrc, 0.0).astype(BF16)
        yield

        mqz = proj(rows, OFF_MQ, PROJ_COLS)
        qs_scr[rows, SWA_WIDTH:] = (mqz[:, 0:MEM_WIDTH] * Q_SCALE).astype(BF16)
        zg_scr[rows, 0:PROJ_COLS - MEM_WIDTH] = _silu(mqz[:, MEM_WIDTH:])
        yield

        for c in range(r0 // CHUNK, (r0 + nrows) // CHUNK):
            for g in range(A_GROUPS):
                spatial_gating(c, g)
            yield

        done = PROJ_COLS - MEM_WIDTH
        while done < MIX_WIDTH:
            width = min(PROJ_COLS, MIX_WIDTH - done)
            zg_scr[rows, done:done + width] = _silu(proj(rows, OFF_Z + done, width))
            done += width
            yield

    def spatial_gating(c, g):
        rows = pl.ds(c * CHUNK, CHUNK)
        cols = pl.ds(g * LANES, LANES)
        sv_g = _dot(wsm_scr[g], vn_scr[rows, cols]) + bst_scr[g]
        y_scr[rows, cols] = gu_scr[rows, cols] * sv_g

    def attn_item(c, pair, sliding):
        r0 = c * CHUNK
        rows = pl.ds(r0, CHUNK)
        heads = (2 * pair, 2 * pair + 1)
        if sliding:
            band = pl.ds(r0, 2 * CHUNK)
            qp = qs_scr[rows, pl.ds(pair * LANES, LANES)]
            keys = jnp.concatenate([ktvar_scr[hd, :, band] for hd in heads], axis=1)
            bias_sel = first_tile if c == 0 else 0
            bias = jnp.concatenate([bias_scr[bias_sel, hd] for hd in heads], axis=1)
            logits = _dot(qp, keys) + bias
        else:
            qp = qs_scr[rows, pl.ds(SWA_WIDTH + pair * LANES, LANES)]
            keys = jnp.concatenate([mk_scr[hd] for hd in heads], axis=1)
            logits = _dot(qp, keys)
        n_keys = logits.shape[1] // 2
        yield
        probs, recips = [], []
        for i, hd in enumerate(heads):
            s = logits[:, i * n_keys:(i + 1) * n_keys]
            m = jnp.max(s, axis=-1, keepdims=True)
            if sliding:
                sink = sinks_ref[hd] * LOG2E
                m = jnp.maximum(m, sink)
            p = jnp.exp2(s - m)
            denom = jnp.sum(p, axis=-1, keepdims=True)
            if sliding:
                denom = denom + jnp.exp2(sink - m)
            probs.append(p.astype(BF16))
            recips.append(1.0 / denom)
        if sliding:
            values = jnp.concatenate([vvar_scr[hd, band, :] for hd in heads], axis=0)
        else:
            values = jnp.concatenate([mv_scr[hd] for hd in heads], axis=0)
        o_pair = _dot(jnp.concatenate(probs, axis=1), values)
        o_pair = o_pair * jnp.where(lo, recips[0], recips[1])
        off = A_WIDTH if sliding else A_WIDTH + SWA_WIDTH
        y_scr[rows, pl.ds(off + pair * LANES, LANES)] = o_pair

    def head_items(r0, nrows):
        for c in range(r0 // CHUNK, (r0 + nrows) // CHUNK):
            for pair in range(SWA_HEADS // 2):
                yield attn_item(c, pair, True)
            for pair in range(MEM_HEADS // 2):
                yield attn_item(c, pair, False)

    def project_out(r0, nrows):
        rows = pl.ds(r0, nrows)
        ybf_scr[rows, :] = (y_scr[rows, :] * zg_scr[rows, :]).astype(BF16)
        for c0 in range(0, D_MODEL, PROJ_COLS):
            o_scr[rows, c0:c0 + PROJ_COLS] = _dot(ybf_scr[rows, :], wout_ref[:, c0:c0 + PROJ_COLS])
            yield
        for rc in range(nrows // CHUNK):
            crows = pl.ds(r0 + rc * CHUNK, CHUNK)
            o = o_scr[crows, :]
            ms = jnp.mean(o * o, axis=-1, keepdims=True)
            out_ref[0, crows, :] = x_ref[0, crows, :] + o * lax.rsqrt(ms + EPS) * postg_ref[...]
            yield

    assert sum(SUB_TILES) == ts
    starts = [sum(SUB_TILES[:s]) for s in range(len(SUB_TILES))]
    subs = list(zip(starts, SUB_TILES))
    n_sub = len(subs)
    pre_norm(*subs[0])
    filler = project(*subs[0])
    for s in range(n_sub):
        if s + 1 < n_sub:
            pre_norm(*subs[s + 1])
        for _ in filler:
            pass
        heads = _skewed(head_items(*subs[s]), ITEM_LAG)
        n_filler = 0
        if s + 1 < n_sub:
            filler = project(*subs[s + 1])
            n_filler += PROJECT_COLUMN_PIECES + subs[s + 1][1] // CHUNK
        else:
            filler = iter(())
        if s >= 1:
            filler = _chain(project_out(*subs[s - 1]), filler)
            n_filler += PROJECT_OUT_DOTS + subs[s - 1][1] // CHUNK
        _interleave(heads, filler, n_filler / ((subs[s][1] // CHUNK) * HEAD_STAGES_PER_CHUNK))
        filler = iter(())
    for _ in project_out(*subs[-1]):
        pass


def _layer(x, mem, w_in_bf16, w_kv, w_out, pre_g, mem_g, post_g, vng, vnb, w_spatial,
           b_spatial, buckets, sinks, rel_bias):
    batch, seq, _ = x.shape
    ts = SEQ_TILE
    n_tiles = seq // ts
    const2 = lambda b, t, *_: (0, 0)
    const3 = lambda b, t, *_: (0, 0, 0)
    grid_spec = pltpu.PrefetchScalarGridSpec(
        num_scalar_prefetch=1,
        grid=(batch, n_tiles),
        in_specs=[
            pl.BlockSpec((1, ts, D_MODEL), lambda b, t, *_: (b, t, 0)),
            pl.BlockSpec((1, MEM_LEN, D_MODEL), lambda b, t, *_: (b, 0, 0)),
            pl.BlockSpec((D_MODEL, IN_WIDTH), const2),
            pl.BlockSpec((D_MODEL, 2 * MEM_WIDTH), const2),
            pl.BlockSpec((MIX_WIDTH, D_MODEL), const2),
            pl.BlockSpec((1, D_MODEL), const2),
            pl.BlockSpec((1, D_MODEL), const2),
            pl.BlockSpec((1, D_MODEL), const2),
            pl.BlockSpec((1, A_WIDTH), const2),
            pl.BlockSpec((1, A_WIDTH), const2),
            pl.BlockSpec((A_GROUPS, CHUNK, CHUNK), const3),
            pl.BlockSpec((A_GROUPS, CHUNK), const2),
            pl.BlockSpec((N_BUCKETS, SWA_HEADS), const2),
            pl.BlockSpec((CHUNK, 2 * CHUNK), const2),
        ],
        out_specs=pl.BlockSpec((1, ts, D_MODEL), lambda b, t, *_: (b, t, 0)),
        scratch_shapes=[
            pltpu.VMEM((ts, D_MODEL), BF16),
            pltpu.VMEM((ts, A_WIDTH), F32),
            pltpu.VMEM((ts, A_WIDTH), BF16),
            pltpu.VMEM((ts, SWA_WIDTH + MEM_WIDTH), BF16),
            pltpu.VMEM((SWA_HEADS, LANES, CHUNK + ts), BF16),
            pltpu.VMEM((SWA_HEADS, CHUNK + ts, LANES), BF16),
            pltpu.VMEM((ts, MIX_WIDTH), F32),
            pltpu.VMEM((MEM_HEADS, LANES, MEM_LEN), BF16),
            pltpu.VMEM((MEM_HEADS, MEM_LEN, LANES), BF16),
            pltpu.VMEM((2, SWA_HEADS, CHUNK, 2 * CHUNK), F32),
            pltpu.VMEM((A_GROUPS, CHUNK, CHUNK), BF16),
            pltpu.VMEM((A_GROUPS, CHUNK, LANES), F32),
            pltpu.VMEM((D_MODEL, 2 * MEM_WIDTH), BF16),
            pltpu.VMEM((MIX_WIDTH, D_MODEL), BF16),
            pltpu.VMEM((ts, MIX_WIDTH), F32),
            pltpu.VMEM((ts, MIX_WIDTH), BF16),
            pltpu.VMEM((ts, D_MODEL), F32),
        ],
    )
    return pl.pallas_call(
        _layer_kernel,
        grid_spec=grid_spec,
        out_shape=jax.ShapeDtypeStruct(x.shape, x.dtype),
        compiler_params=pltpu.CompilerParams(
            dimension_semantics=("arbitrary", "arbitrary"),
            vmem_limit_bytes=VMEM_LIMIT_BYTES),
        name="layer",
    )(sinks, x, mem, w_in_bf16, w_kv, w_out, pre_g, mem_g, post_g, vng, vnb,
      w_spatial, b_spatial, rel_bias, buckets)


def kernel(x, mem, pre_norm_g, post_norm_g, mem_norm_g, w_in, w_mem_kv, v_norm_g, v_norm_b,
           w_spatial, b_spatial, attn_sinks, rel_bias, w_out):
    depth = w_in.shape[0]
    buckets = jnp.asarray(_band_buckets())
    for layer in range(depth):
        x = _layer(x, mem, w_in[layer].astype(BF16), w_mem_kv[layer], w_out[layer],
                   pre_norm_g[layer][None, :], mem_norm_g[layer][None, :], post_norm_g[layer][None, :],
                   v_norm_g[layer][None, :], v_norm_b[layer][None, :],
                   w_spatial[layer], b_spatial[layer], buckets, attn_sinks[layer], rel_bias)
    return x
```

```python
import numpy as np
import jax
import jax.numpy as jnp
from jax import lax
from jax.experimental import pallas as pl
from jax.experimental.pallas import tpu as pltpu

D_MODEL = 1024
MEM_LEN = 256
HEAD_DIM = 64
CHUNK = 128
A_GROUPS = 4
A_WIDTH = 512
SWA_HEADS = 4
SWA_WIDTH = 256
SWA_KV_WIDTH = 128
MEM_HEADS = 4
MEM_WIDTH = 256
MIX_WIDTH = 1024
IN_WIDTH = 2816
N_BUCKETS = 32
MAX_DISTANCE = 128
WINDOW = 128
EPS = 1e-6
NEG = -1e30
LOG2E = float(np.log2(np.e))
Q_SCALE = HEAD_DIM ** -0.5 * LOG2E

OFF_U = 0
OFF_V = A_WIDTH
OFF_SQ = 2 * A_WIDTH
OFF_SK = OFF_SQ + SWA_WIDTH
OFF_SV = OFF_SK + SWA_KV_WIDTH
OFF_MQ = OFF_SV + SWA_KV_WIDTH
OFF_Z = OFF_MQ + MEM_WIDTH

LANES = 128
SEQ_TILE = 1024
SUB_TILES = (256, 256, 256, 256)
PROJ_COLS = 512
ITEM_LAG = 1
PROJECT_COLUMN_PIECES = 6
PROJECT_OUT_DOTS = D_MODEL // PROJ_COLS
HEAD_STAGES_PER_CHUNK = 2 * (SWA_HEADS // 2 + MEM_HEADS // 2)
VMEM_LIMIT_BYTES = 56 * 1024 * 1024

BF16 = jnp.bfloat16
F32 = jnp.float32


def _t5_causal_buckets(dist):
    n = np.maximum(dist, 0)
    max_exact = N_BUCKETS // 2
    large = max_exact + (np.log(np.maximum(n, 1) / max_exact) / np.log(MAX_DISTANCE / max_exact)
                         * (N_BUCKETS - max_exact)).astype(np.int32)
    large = np.minimum(large, N_BUCKETS - 1)
    return np.where(n < max_exact, n, large).astype(np.int32)


def _band_buckets():
    qi = np.arange(CHUNK)[:, None]
    kj = np.arange(2 * CHUNK)[None, :]
    dist = qi + CHUNK - kj
    valid = (dist >= 0) & (dist < WINDOW)
    return np.where(valid, _t5_causal_buckets(dist), -1).astype(np.int32)


def _gelu_tanh(x):
    c = np.float32(np.sqrt(2.0 / np.pi))
    ck = np.float32(np.sqrt(2.0 / np.pi) * 0.044715)
    hx = 0.5 * x
    return hx + hx * jnp.tanh(x * (c + ck * (x * x)))


def _silu(z):
    hz = 0.5 * z
    return hz + hz * jnp.tanh(hz)


def _dot(a, b):
    return jnp.dot(a, b, preferred_element_type=F32)


_DONE = object()


def _skewed(items, lag):
    waiting = []
    for item in items:
        if next(item, _DONE) is not _DONE:
            waiting.append(item)
        yield
        if len(waiting) > lag:
            next(waiting.pop(0), _DONE)
            yield
    for item in waiting:
        next(item, _DONE)
        yield


def _chain(*gens):
    for g in gens:
        yield from g


def _interleave(a, b, b_per_a):
    credit = 0.0
    a_live = b_live = True
    while a_live or b_live:
        if a_live:
            a_live = next(a, _DONE) is not _DONE
        credit += b_per_a
        while b_live and (credit >= 1.0 or not a_live):
            b_live = next(b, _DONE) is not _DONE
            credit -= 1.0


def _lane_half_mask(parity):
    lane = lax.broadcasted_iota(jnp.int32, (1, LANES), 1)
    return (lane >= HEAD_DIM) if parity else (lane < HEAD_DIM)


def _place_head_rows(kt_head, parity):
    zeros = jnp.zeros_like(kt_head)
    return jnp.concatenate([zeros, kt_head] if parity else [kt_head, zeros], axis=0)


def _layer_kernel(sinks_ref,
                  x_ref, mem_ref, win_ref, wkv_f32_ref, wout_f32_ref, preg_ref, memg_ref, postg_ref,
                  vng_ref, vnb_ref, ws_ref, bs_ref, relb_ref, buckets_ref,
                  out_ref,
                  h_scr, gu_scr, vn_scr, qs_scr, ktvar_scr, vvar_scr, zg_scr,
                  mk_scr, mv_scr, bias_scr, wsm_scr, bst_scr, wkv_ref, wout_ref,
                  y_scr, ybf_scr, o_scr):
    ts = x_ref.shape[1]
    b = pl.program_id(0)
    t = pl.program_id(1)

    @pl.when((b == 0) & (t == 0))
    def _init():
        row = lax.broadcasted_iota(jnp.int32, (CHUNK, CHUNK), 0)
        col = lax.broadcasted_iota(jnp.int32, (CHUNK, CHUNK), 1)
        for g in range(A_GROUPS):
            wsm_scr[g] = jnp.where(row >= col, ws_ref[g], 0.0).astype(BF16)
            bs_col = jnp.sum(jnp.where(row == col, bs_ref[g:g + 1, :], 0.0), axis=1, keepdims=True)
            bst_scr[g] = jnp.broadcast_to(bs_col, (CHUNK, LANES))
        for r0 in range(0, D_MODEL, 2 * CHUNK):
            wrows = pl.ds(r0, 2 * CHUNK)
            wkv_ref[wrows, :] = wkv_f32_ref[wrows, :].astype(BF16)
            wout_ref[wrows, :] = wout_f32_ref[wrows, :].astype(BF16)
        buckets = buckets_ref[...]
        prev_key = lax.broadcasted_iota(jnp.int32, (CHUNK, 2 * CHUNK), 1) < CHUNK
        for hd in range(SWA_HEADS):
            tbl = jnp.full((CHUNK, 2 * CHUNK), NEG, F32)
            for bk in range(N_BUCKETS):
                tbl = jnp.where(buckets == bk, relb_ref[bk:bk + 1, hd:hd + 1] * LOG2E, tbl)
            bias_scr[0, hd] = tbl
            bias_scr[1, hd] = jnp.where(prev_key, NEG, tbl)

    @pl.when(t == 0)
    def _sequence_start():
        ktvar_scr[:, :, 0:CHUNK] = jnp.zeros((SWA_HEADS, LANES, CHUNK), BF16)
        vvar_scr[:, 0:CHUNK, :] = jnp.zeros((SWA_HEADS, CHUNK, LANES), BF16)
        m = mem_ref[0]
        ms = jnp.mean(m * m, axis=-1, keepdims=True)
        hm = (m * lax.rsqrt(ms + EPS) * memg_ref[...]).astype(BF16)
        kv = _dot(hm, wkv_ref[...])
        for pair in range(MEM_HEADS // 2):
            kt_pair = kv[:, pair * LANES:(pair + 1) * LANES].T
            v_pair = kv[:, MEM_WIDTH + pair * LANES:MEM_WIDTH + (pair + 1) * LANES]
            for parity in range(2):
                hd = 2 * pair + parity
                kt_head = kt_pair[parity * HEAD_DIM:(parity + 1) * HEAD_DIM]
                mk_scr[hd] = _place_head_rows(kt_head, parity).astype(BF16)
                mv_scr[hd] = jnp.where(_lane_half_mask(parity), v_pair, 0.0).astype(BF16)


    @pl.when(t > 0)
    def _copy_carry():
        ktvar_scr[:, :, 0:CHUNK] = ktvar_scr[:, :, ts:ts + CHUNK]
        vvar_scr[:, 0:CHUNK, :] = vvar_scr[:, ts:ts + CHUNK, :]

    lo = _lane_half_mask(0)
    hi = _lane_half_mask(1)
    first_tile = jnp.where(t == 0, 1, 0)

    def pre_norm(r0, nrows):
        for rc in range(nrows // CHUNK):
            crows = pl.ds(r0 + rc * CHUNK, CHUNK)
            x = x_ref[0, crows, :]
            ms = jnp.mean(x * x, axis=-1, keepdims=True)
            h_scr[crows, :] = (x * lax.rsqrt(ms + EPS) * preg_ref[...]).astype(BF16)

    def proj(rows, c0, width):
        return _dot(h_scr[rows, :], win_ref[:, c0:c0 + width])

    def project(r0, nrows):
        rows = pl.ds(r0, nrows)

        v_all = _gelu_tanh(proj(rows, OFF_V, A_WIDTH))
        for g in range(A_GROUPS):
            cols = pl.ds(g * LANES, LANES)
            v = v_all[:, g * LANES:(g + 1) * LANES]
            mu = jnp.mean(v, axis=-1, keepdims=True)
            d = v - mu
            var = jnp.mean(d * d, axis=-1, keepdims=True)
            vn = d * lax.rsqrt(var + EPS) * vng_ref[:, cols] + vnb_ref[:, cols]
            vn_scr[rows, cols] = vn.astype(BF16)
        yield

        gu_scr[rows, :] = _gelu_tanh(proj(rows, OFF_U, A_WIDTH))
        yield

        qkv = proj(rows, OFF_SQ, SWA_WIDTH + 2 * SWA_KV_WIDTH)
        qs_scr[rows, 0:SWA_WIDTH] = (qkv[:, 0:SWA_WIDTH] * Q_SCALE).astype(BF16)
        skt = qkv[:, SWA_WIDTH:SWA_WIDTH + LANES].T
        sv = qkv[:, SWA_WIDTH + LANES:]
        sv_rot = pltpu.roll(sv, HEAD_DIM, 1)
        new_rows = pl.ds(CHUNK + r0, nrows)
        for hd, (keep, vsrc) in enumerate([(lo, sv), (hi, sv_rot), (lo, sv_rot), (hi, sv)]):
            kv_head = hd // 2
            kt_head = skt[kv_head * HEAD_DIM:(kv_head + 1) * HEAD_DIM]
            ktvar_scr[hd, :, new_rows] = _place_head_rows(kt_head, hd % 2).astype(BF16)
            vvar_scr[hd, new_rows, :] = jnp.where(keep, vsrc, 0.0).astype(BF16)
        yield

        mqz = proj(rows, OFF_MQ, PROJ_COLS)
        qs_scr[rows, SWA_WIDTH:] = (mqz[:, 0:MEM_WIDTH] * Q_SCALE).astype(BF16)
        zg_scr[rows, 0:PROJ_COLS - MEM_WIDTH] = _silu(mqz[:, MEM_WIDTH:])
        yield

        for c in range(r0 // CHUNK, (r0 + nrows) // CHUNK):
            for g in range(A_GROUPS):
                spatial_gating(c, g)
            yield

        done = PROJ_COLS - MEM_WIDTH
        while done < MIX_WIDTH:
            width = min(PROJ_COLS, MIX_WIDTH - done)
            zg_scr[rows, done:done + width] = _silu(proj(rows, OFF_Z + done, width))
            done += width
            yield

    def spatial_gating(c, g):
        rows = pl.ds(c * CHUNK, CHUNK)
        cols = pl.ds(g * LANES, LANES)
        sv_g = _dot(wsm_scr[g], vn_scr[rows, cols]) + bst_scr[g]
        y_scr[rows, cols] = gu_scr[rows, cols] * sv_g

    def attn_item(c, pair, sliding):
        r0 = c * CHUNK
        rows = pl.ds(r0, CHUNK)
        heads = (2 * pair, 2 * pair + 1)
        if sliding:
            band = pl.ds(r0, 2 * CHUNK)
            qp = qs_scr[rows, pl.ds(pair * LANES, LANES)]
            keys = jnp.concatenate([ktvar_scr[hd, :, band] for hd in heads], axis=1)
            bias_sel = first_tile if c == 0 else 0
            bias = jnp.concatenate([bias_scr[bias_sel, hd] for hd in heads], axis=1)
            logits = _dot(qp, keys) + bias
        else:
            qp = qs_scr[rows, pl.ds(SWA_WIDTH + pair * LANES, LANES)]
            keys = jnp.concatenate([mk_scr[hd] for hd in heads], axis=1)
            logits = _dot(qp, keys)
        n_keys = logits.shape[1] // 2
        yield
        probs, recips = [], []
        for i, hd in enumerate(heads):
            s = logits[:, i * n_keys:(i + 1) * n_keys]
            m = jnp.max(s, axis=-1, keepdims=True)
            if sliding:
                sink = sinks_ref[hd] * LOG2E
                m = jnp.maximum(m, sink)
            p = jnp.exp2(s - m)
            denom = jnp.sum(p, axis=-1, keepdims=True)
            if sliding:
                denom = denom + jnp.exp2(sink - m)
            probs.append(p.astype(BF16))
            recips.append(1.0 / denom)
        if sliding:
            values = jnp.concatenate([vvar_scr[hd, band, :] for hd in heads], axis=0)
        else:
            values = jnp.concatenate([mv_scr[hd] for hd in heads], axis=0)
        o_pair = _dot(jnp.concatenate(probs, axis=1), values)
        o_pair = o_pair * jnp.where(lo, recips[0], recips[1])
        off = A_WIDTH if sliding else A_WIDTH + SWA_WIDTH
        y_scr[rows, pl.ds(off + pair * LANES, LANES)] = o_pair

    def head_items(r0, nrows):
        for c in range(r0 // CHUNK, (r0 + nrows) // CHUNK):
            for pair in range(SWA_HEADS // 2):
                yield attn_item(c, pair, True)
            for pair in range(MEM_HEADS // 2):
                yield attn_item(c, pair, False)

    def project_out(r0, nrows):
        rows = pl.ds(r0, nrows)
        ybf_scr[rows, :] = (y_scr[rows, :] * zg_scr[rows, :]).astype(BF16)
        for c0 in range(0, D_MODEL, PROJ_COLS):
            o_scr[rows, c0:c0 + PROJ_COLS] = _dot(ybf_scr[rows, :], wout_ref[:, c0:c0 + PROJ_COLS])
            yield
        for rc in range(nrows // CHUNK):
            crows = pl.ds(r0 + rc * CHUNK, CHUNK)
            o = o_scr[crows, :]
            ms = jnp.mean(o * o, axis=-1, keepdims=True)
            out_ref[0, crows, :] = x_ref[0, crows, :] + o * lax.rsqrt(ms + EPS) * postg_ref[...]
            yield

    assert sum(SUB_TILES) == ts
    starts = [sum(SUB_TILES[:s]) for s in range(len(SUB_TILES))]
    subs = list(zip(starts, SUB_TILES))
    n_sub = len(subs)
    pre_norm(*subs[0])
    filler = project(*subs[0])
    for s in range(n_sub):
        if s + 1 < n_sub:
            pre_norm(*subs[s + 1])
        for _ in filler:
            pass
        heads = _skewed(head_items(*subs[s]), ITEM_LAG)
        n_filler = 0
        if s + 1 < n_sub:
            filler = project(*subs[s + 1])
            n_filler += PROJECT_COLUMN_PIECES + subs[s + 1][1] // CHUNK
        else:
            filler = iter(())
        if s >= 1:
            filler = _chain(project_out(*subs[s - 1]), filler)
            n_filler += PROJECT_OUT_DOTS + subs[s - 1][1] // CHUNK
        _interleave(heads, filler, n_filler / ((subs[s][1] // CHUNK) * HEAD_STAGES_PER_CHUNK))
        filler = iter(())
    for _ in project_out(*subs[-1]):
        pass


def _layer(x, mem, w_in_bf16, w_kv, w_out, pre_g, mem_g, post_g, vng, vnb, w_spatial,
           b_spatial, buckets, sinks, rel_bias):
    batch, seq, _ = x.shape
    ts = SEQ_TILE
    n_tiles = seq // ts
    const2 = lambda b, t, *_: (0, 0)
    const3 = lambda b, t, *_: (0, 0, 0)
    grid_spec = pltpu.PrefetchScalarGridSpec(
        num_scalar_prefetch=1,
        grid=(batch, n_tiles),
        in_specs=[
            pl.BlockSpec((1, ts, D_MODEL), lambda b, t, *_: (b, t, 0)),
            pl.BlockSpec((1, MEM_LEN, D_MODEL), lambda b, t, *_: (b, 0, 0)),
            pl.BlockSpec((D_MODEL, IN_WIDTH), const2),
            pl.BlockSpec((D_MODEL, 2 * MEM_WIDTH), const2),
            pl.BlockSpec((MIX_WIDTH, D_MODEL), const2),
            pl.BlockSpec((1, D_MODEL), const2),
            pl.BlockSpec((1, D_MODEL), const2),
            pl.BlockSpec((1, D_MODEL), const2),
            pl.BlockSpec((1, A_WIDTH), const2),
            pl.BlockSpec((1, A_WIDTH), const2),
            pl.BlockSpec((A_GROUPS, CHUNK, CHUNK), const3),
            pl.BlockSpec((A_GROUPS, CHUNK), const2),
            pl.BlockSpec((N_BUCKETS, SWA_HEADS), const2),
            pl.BlockSpec((CHUNK, 2 * CHUNK), const2),
        ],
        out_specs=pl.BlockSpec((1, ts, D_MODEL), lambda b, t, *_: (b, t, 0)),
        scratch_shapes=[
            pltpu.VMEM((ts, D_MODEL), BF16),
            pltpu.VMEM((ts, A_WIDTH), F32),
            pltpu.VMEM((ts, A_WIDTH), BF16),
            pltpu.VMEM((ts, SWA_WIDTH + MEM_WIDTH), BF16),
            pltpu.VMEM((SWA_HEADS, LANES, CHUNK + ts), BF16),
            pltpu.VMEM((SWA_HEADS, CHUNK + ts, LANES), BF16),
            pltpu.VMEM((ts, MIX_WIDTH), F32),
            pltpu.VMEM((MEM_HEADS, LANES, MEM_LEN), BF16),
            pltpu.VMEM((MEM_HEADS, MEM_LEN, LANES), BF16),
            pltpu.VMEM((2, SWA_HEADS, CHUNK, 2 * CHUNK), F32),
            pltpu.VMEM((A_GROUPS, CHUNK, CHUNK), BF16),
            pltpu.VMEM((A_GROUPS, CHUNK, LANES), F32),
            pltpu.VMEM((D_MODEL, 2 * MEM_WIDTH), BF16),
            pltpu.VMEM((MIX_WIDTH, D_MODEL), BF16),
            pltpu.VMEM((ts, MIX_WIDTH), F32),
            pltpu.VMEM((ts, MIX_WIDTH), BF16),
            pltpu.VMEM((ts, D_MODEL), F32),
        ],
    )
    return pl.pallas_call(
        _layer_kernel,
        grid_spec=grid_spec,
        out_shape=jax.ShapeDtypeStruct(x.shape, x.dtype),
        compiler_params=pltpu.CompilerParams(
            dimension_semantics=("arbitrary", "arbitrary"),
            vmem_limit_bytes=VMEM_LIMIT_BYTES),
        name="layer",
    )(sinks, x, mem, w_in_bf16, w_kv, w_out, pre_g, mem_g, post_g, vng, vnb,
      w_spatial, b_spatial, rel_bias, buckets)


def kernel(x, mem, pre_norm_g, post_norm_g, mem_norm_g, w_in, w_mem_kv, v_norm_g, v_norm_b,
           w_spatial, b_spatial, attn_sinks, rel_bias, w_out):
    depth = w_in.shape[0]
    buckets = jnp.asarray(_band_buckets())
    for layer in range(depth):
        x = _layer(x, mem, w_in[layer].astype(BF16), w_mem_kv[layer], w_out[layer],
                   pre_norm_g[layer][None, :], mem_norm_g[layer][None, :], post_norm_g[layer][None, :],
                   v_norm_g[layer][None, :], v_norm_b[layer][None, :],
                   w_spatial[layer], b_spatial[layer], buckets, attn_sinks[layer], rel_bias)
    return x
```

```python
import numpy as np
import jax
import jax.numpy as jnp
from jax import lax
from jax.experimental import pallas as pl
from jax.experimental.pallas import tpu as pltpu

D_MODEL = 1024
MEM_LEN = 256
HEAD_DIM = 64
CHUNK = 128
A_GROUPS = 4
A_WIDTH = 512
SWA_HEADS = 4
SWA_WIDTH = 256
SWA_KV_WIDTH = 128
MEM_HEADS = 4
MEM_WIDTH = 256
MIX_WIDTH = 1024
IN_WIDTH = 2816
N_BUCKETS = 32
MAX_DISTANCE = 128
WINDOW = 128
EPS = 1e-6
NEG = -1e30
LOG2E = float(np.log2(np.e))
Q_SCALE = HEAD_DIM ** -0.5 * LOG2E

OFF_U = 0
OFF_V = A_WIDTH
OFF_SQ = 2 * A_WIDTH
OFF_SK = OFF_SQ + SWA_WIDTH
OFF_SV = OFF_SK + SWA_KV_WIDTH
OFF_MQ = OFF_SV + SWA_KV_WIDTH
OFF_Z = OFF_MQ + MEM_WIDTH

LANES = 128
SEQ_TILE = 1024
SUB_TILES = (256, 256, 256, 256)
PROJ_COLS = 512
ITEM_LAG = 2
PROJECT_COLUMN_PIECES = 6
PROJECT_OUT_DOTS = D_MODEL // PROJ_COLS
MEM_ITEM_CHUNKS = 2
HEAD_STAGES_PER_CHUNK = 2 * (SWA_HEADS // 2) + 2 * (MEM_HEADS // 2) / MEM_ITEM_CHUNKS
VMEM_LIMIT_BYTES = 56 * 1024 * 1024

BF16 = jnp.bfloat16
F32 = jnp.float32


def _t5_causal_buckets(dist):
    n = np.maximum(dist, 0)
    max_exact = N_BUCKETS // 2
    large = max_exact + (np.log(np.maximum(n, 1) / max_exact) / np.log(MAX_DISTANCE / max_exact)
                         * (N_BUCKETS - max_exact)).astype(np.int32)
    large = np.minimum(large, N_BUCKETS - 1)
    return np.where(n < max_exact, n, large).astype(np.int32)


def _band_buckets():
    qi = np.arange(CHUNK)[:, None]
    kj = np.arange(2 * CHUNK)[None, :]
    dist = qi + CHUNK - kj
    valid = (dist >= 0) & (dist < WINDOW)
    return np.where(valid, _t5_causal_buckets(dist), -1).astype(np.int32)


def _gelu_tanh(x):
    c = np.float32(np.sqrt(2.0 / np.pi))
    ck = np.float32(np.sqrt(2.0 / np.pi) * 0.044715)
    hx = 0.5 * x
    return hx + hx * jnp.tanh(x * (c + ck * (x * x)))


def _silu(z):
    hz = 0.5 * z
    return hz + hz * jnp.tanh(hz)


def _dot(a, b):
    return jnp.dot(a, b, preferred_element_type=F32)


_DONE = object()


def _skewed(items, lag):
    waiting = []
    for item in items:
        if next(item, _DONE) is not _DONE:
            waiting.append(item)
        yield
        if len(waiting) > lag:
            next(waiting.pop(0), _DONE)
            yield
    for item in waiting:
        next(item, _DONE)
        yield


def _chain(*gens):
    for g in gens:
        yield from g


def _interleave(a, b, b_per_a):
    credit = 0.0
    a_live = b_live = True
    while a_live or b_live:
        if a_live:
            a_live = next(a, _DONE) is not _DONE
        credit += b_per_a
        while b_live and (credit >= 1.0 or not a_live):
            b_live = next(b, _DONE) is not _DONE
            credit -= 1.0


def _lane_half_mask(parity):
    lane = lax.broadcasted_iota(jnp.int32, (1, LANES), 1)
    return (lane >= HEAD_DIM) if parity else (lane < HEAD_DIM)


def _place_head_rows(kt_head, parity):
    zeros = jnp.zeros_like(kt_head)
    return jnp.concatenate([zeros, kt_head] if parity else [kt_head, zeros], axis=0)


def _layer_kernel(sinks_ref,
                  x_ref, mem_ref, win_ref, wkv_f32_ref, wout_f32_ref, preg_ref, memg_ref, postg_ref,
                  vng_ref, vnb_ref, ws_ref, bs_ref, relb_ref, buckets_ref,
                  out_ref,
                  h_scr, gu_scr, vn_scr, qs_scr, ktvar_scr, vvar_scr, zg_scr,
                  mk_scr, mv_scr, bias_scr, wsm_scr, bst_scr, wkv_ref, wout_ref,
                  y_scr, ybf_scr, o_scr):
    ts = x_ref.shape[1]
    b = pl.program_id(0)
    t = pl.program_id(1)

    @pl.when((b == 0) & (t == 0))
    def _init():
        row = lax.broadcasted_iota(jnp.int32, (CHUNK, CHUNK), 0)
        col = lax.broadcasted_iota(jnp.int32, (CHUNK, CHUNK), 1)
        for g in range(A_GROUPS):
            wsm_scr[g] = jnp.where(row >= col, ws_ref[g], 0.0).astype(BF16)
            bs_col = jnp.sum(jnp.where(row == col, bs_ref[g:g + 1, :], 0.0), axis=1, keepdims=True)
            bst_scr[g] = jnp.broadcast_to(bs_col, (CHUNK, LANES))
        for r0 in range(0, D_MODEL, 2 * CHUNK):
            wrows = pl.ds(r0, 2 * CHUNK)
            wkv_ref[wrows, :] = wkv_f32_ref[wrows, :].astype(BF16)
            wout_ref[wrows, :] = wout_f32_ref[wrows, :].astype(BF16)
        buckets = buckets_ref[...]
        prev_key = lax.broadcasted_iota(jnp.int32, (CHUNK, 2 * CHUNK), 1) < CHUNK
        for hd in range(SWA_HEADS):
            tbl = jnp.full((CHUNK, 2 * CHUNK), NEG, F32)
            for bk in range(N_BUCKETS):
                tbl = jnp.where(buckets == bk, relb_ref[bk:bk + 1, hd:hd + 1] * LOG2E, tbl)
            bias_scr[0, hd] = tbl
            bias_scr[1, hd] = jnp.where(prev_key, NEG, tbl)

    @pl.when(t == 0)
    def _sequence_start():
        ktvar_scr[:, :, 0:CHUNK] = jnp.zeros((SWA_HEADS, LANES, CHUNK), BF16)
        vvar_scr[:, 0:CHUNK, :] = jnp.zeros((SWA_HEADS, CHUNK, LANES), BF16)
        m = mem_ref[0]
        ms = jnp.mean(m * m, axis=-1, keepdims=True)
        hm = (m * lax.rsqrt(ms + EPS) * memg_ref[...]).astype(BF16)
        kv = _dot(hm, wkv_ref[...])
        for pair in range(MEM_HEADS // 2):
            kt_pair = kv[:, pair * LANES:(pair + 1) * LANES].T
            v_pair = kv[:, MEM_WIDTH + pair * LANES:MEM_WIDTH + (pair + 1) * LANES]
            for parity in range(2):
                hd = 2 * pair + parity
                kt_head = kt_pair[parity * HEAD_DIM:(parity + 1) * HEAD_DIM]
                mk_scr[hd] = _place_head_rows(kt_head, parity).astype(BF16)
                mv_scr[hd] = jnp.where(_lane_half_mask(parity), v_pair, 0.0).astype(BF16)


    @pl.when(t > 0)
    def _copy_carry():
        ktvar_scr[:, :, 0:CHUNK] = ktvar_scr[:, :, ts:ts + CHUNK]
        vvar_scr[:, 0:CHUNK, :] = vvar_scr[:, ts:ts + CHUNK, :]

    lo = _lane_half_mask(0)
    hi = _lane_half_mask(1)
    first_tile = jnp.where(t == 0, 1, 0)

    def pre_norm(r0, nrows):
        for rc in range(nrows // CHUNK):
            crows = pl.ds(r0 + rc * CHUNK, CHUNK)
            x = x_ref[0, crows, :]
            ms = jnp.mean(x * x, axis=-1, keepdims=True)
            h_scr[crows, :] = (x * lax.rsqrt(ms + EPS) * preg_ref[...]).astype(BF16)

    def proj(rows, c0, width):
        return _dot(h_scr[rows, :], win_ref[:, c0:c0 + width])

    def project(r0, nrows):
        rows = pl.ds(r0, nrows)

        v_all = _gelu_tanh(proj(rows, OFF_V, A_WIDTH))
        for g in range(A_GROUPS):
            cols = pl.ds(g * LANES, LANES)
            v = v_all[:, g * LANES:(g + 1) * LANES]
            mu = jnp.mean(v, axis=-1, keepdims=True)
            d = v - mu
            var = jnp.mean(d * d, axis=-1, keepdims=True)
            vn = d * lax.rsqrt(var + EPS) * vng_ref[:, cols] + vnb_ref[:, cols]
            vn_scr[rows, cols] = vn.astype(BF16)
        yield

        gu_scr[rows, :] = _gelu_tanh(proj(rows, OFF_U, A_WIDTH))
        yield

        qkv = proj(rows, OFF_SQ, SWA_WIDTH + 2 * SWA_KV_WIDTH)
        qs_scr[rows, 0:SWA_WIDTH] = (qkv[:, 0:SWA_WIDTH] * Q_SCALE).astype(BF16)
        skt = qkv[:, SWA_WIDTH:SWA_WIDTH + LANES].T
        sv = qkv[:, SWA_WIDTH + LANES:]
        sv_rot = pltpu.roll(sv, HEAD_DIM, 1)
        new_rows = pl.ds(CHUNK + r0, nrows)
        for hd, (keep, vsrc) in enumerate([(lo, sv), (hi, sv_rot), (lo, sv_rot), (hi, sv)]):
            kv_head = hd // 2
            kt_head = skt[kv_head * HEAD_DIM:(kv_head + 1) * HEAD_DIM]
            ktvar_scr[hd, :, new_rows] = _place_head_rows(kt_head, hd % 2).astype(BF16)
            vvar_scr[hd, new_rows, :] = jnp.where(keep, vsrc, 0.0).astype(BF16)
        yield

        mqz = proj(rows, OFF_MQ, PROJ_COLS)
        qs_scr[rows, SWA_WIDTH:] = (mqz[:, 0:MEM_WIDTH] * Q_SCALE).astype(BF16)
        zg_scr[rows, 0:PROJ_COLS - MEM_WIDTH] = _silu(mqz[:, MEM_WIDTH:])
        yield

        for c in range(r0 // CHUNK, (r0 + nrows) // CHUNK):
            for g in range(A_GROUPS):
                spatial_gating(c, g)
            yield

        done = PROJ_COLS - MEM_WIDTH
        while done < MIX_WIDTH:
            width = min(PROJ_COLS, MIX_WIDTH - done)
            zg_scr[rows, done:done + width] = _silu(proj(rows, OFF_Z + done, width))
            done += width
            yield

    def spatial_gating(c, g):
        rows = pl.ds(c * CHUNK, CHUNK)
        cols = pl.ds(g * LANES, LANES)
        sv_g = _dot(wsm_scr[g], vn_scr[rows, cols]) + bst_scr[g]
        y_scr[rows, cols] = gu_scr[rows, cols] * sv_g

    def attn_item(c, pair, sliding):
        r0 = c * CHUNK
        rows = pl.ds(r0, CHUNK if sliding else MEM_ITEM_CHUNKS * CHUNK)
        heads = (2 * pair, 2 * pair + 1)
        if sliding:
            band = pl.ds(r0, 2 * CHUNK)
            qp = qs_scr[rows, pl.ds(pair * LANES, LANES)]
            keys = jnp.concatenate([ktvar_scr[hd, :, band] for hd in heads], axis=1)
            bias_sel = first_tile if c == 0 else 0
            bias = jnp.concatenate([bias_scr[bias_sel, hd] for hd in heads], axis=1)
            logits = _dot(qp, keys) + bias
        else:
            qp = qs_scr[rows, pl.ds(SWA_WIDTH + pair * LANES, LANES)]
            keys = jnp.concatenate([mk_scr[hd] for hd in heads], axis=1)
            logits = _dot(qp, keys)
        n_keys = logits.shape[1] // 2
        yield
        probs, recips = [], []
        for i, hd in enumerate(heads):
            s = logits[:, i * n_keys:(i + 1) * n_keys]
            m = jnp.max(s, axis=-1, keepdims=True)
            if sliding:
                sink = sinks_ref[hd] * LOG2E
                m = jnp.maximum(m, sink)
            p = jnp.exp2(s - m)
            denom = jnp.sum(p, axis=-1, keepdims=True)
            if sliding:
                denom = denom + jnp.exp2(sink - m)
            probs.append(p.astype(BF16))
            recips.append(1.0 / denom)
        if sliding:
            values = jnp.concatenate([vvar_scr[hd, band, :] for hd in heads], axis=0)
        else:
            values = jnp.concatenate([mv_scr[hd] for hd in heads], axis=0)
        o_pair = _dot(jnp.concatenate(probs, axis=1), values)
        o_pair = o_pair * jnp.where(lo, recips[0], recips[1])
        off = A_WIDTH if sliding else A_WIDTH + SWA_WIDTH
        y_scr[rows, pl.ds(off + pair * LANES, LANES)] = o_pair

    def head_items(r0, nrows):
        for c in range(r0 // CHUNK, (r0 + nrows) // CHUNK):
            for pair in range(SWA_HEADS // 2):
                yield attn_item(c, pair, True)
            if (c - r0 // CHUNK) % MEM_ITEM_CHUNKS == 0:
                for pair in range(MEM_HEADS // 2):
                    yield attn_item(c, pair, False)

    def project_out(r0, nrows):
        rows = pl.ds(r0, nrows)
        ybf_scr[rows, :] = (y_scr[rows, :] * zg_scr[rows, :]).astype(BF16)
        for c0 in range(0, D_MODEL, PROJ_COLS):
            o_scr[rows, c0:c0 + PROJ_COLS] = _dot(ybf_scr[rows, :], wout_ref[:, c0:c0 + PROJ_COLS])
            yield
        for rc in range(nrows // CHUNK):
            crows = pl.ds(r0 + rc * CHUNK, CHUNK)
            o = o_scr[crows, :]
            ms = jnp.mean(o * o, axis=-1, keepdims=True)
            out_ref[0, crows, :] = x_ref[0, crows, :] + o * lax.rsqrt(ms + EPS) * postg_ref[...]
            yield

    assert sum(SUB_TILES) == ts
    starts = [sum(SUB_TILES[:s]) for s in range(len(SUB_TILES))]
    subs = list(zip(starts, SUB_TILES))
    n_sub = len(subs)
    pre_norm(*subs[0])
    filler = project(*subs[0])
    for s in range(n_sub):
        if s + 1 < n_sub:
            pre_norm(*subs[s + 1])
        for _ in filler:
            pass
        heads = _skewed(head_items(*subs[s]), ITEM_LAG)
        n_filler = 0
        if s + 1 < n_sub:
            filler = project(*subs[s + 1])
            n_filler += PROJECT_COLUMN_PIECES + subs[s + 1][1] // CHUNK
        else:
            filler = iter(())
        if s >= 1:
            filler = _chain(project_out(*subs[s - 1]), filler)
            n_filler += PROJECT_OUT_DOTS + subs[s - 1][1] // CHUNK
        _interleave(heads, filler, n_filler / ((subs[s][1] // CHUNK) * HEAD_STAGES_PER_CHUNK))
        filler = iter(())
    for _ in project_out(*subs[-1]):
        pass


def _layer(x, mem, w_in_bf16, w_kv, w_out, pre_g, mem_g, post_g, vng, vnb, w_spatial,
           b_spatial, buckets, sinks, rel_bias):
    batch, seq, _ = x.shape
    ts = SEQ_TILE
    n_tiles = seq // ts
    const2 = lambda b, t, *_: (0, 0)
    const3 = lambda b, t, *_: (0, 0, 0)
    grid_spec = pltpu.PrefetchScalarGridSpec(
        num_scalar_prefetch=1,
        grid=(batch, n_tiles),
        in_specs=[
            pl.BlockSpec((1, ts, D_MODEL), lambda b, t, *_: (b, t, 0)),
            pl.BlockSpec((1, MEM_LEN, D_MODEL), lambda b, t, *_: (b, 0, 0)),
            pl.BlockSpec((D_MODEL, IN_WIDTH), const2),
            pl.BlockSpec((D_MODEL, 2 * MEM_WIDTH), const2),
            pl.BlockSpec((MIX_WIDTH, D_MODEL), const2),
            pl.BlockSpec((1, D_MODEL), const2),
            pl.BlockSpec((1, D_MODEL), const2),
            pl.BlockSpec((1, D_MODEL), const2),
            pl.BlockSpec((1, A_WIDTH), const2),
            pl.BlockSpec((1, A_WIDTH), const2),
            pl.BlockSpec((A_GROUPS, CHUNK, CHUNK), const3),
            pl.BlockSpec((A_GROUPS, CHUNK), const2),
            pl.BlockSpec((N_BUCKETS, SWA_HEADS), const2),
            pl.BlockSpec((CHUNK, 2 * CHUNK), const2),
        ],
        out_specs=pl.BlockSpec((1, ts, D_MODEL), lambda b, t, *_: (b, t, 0)),
        scratch_shapes=[
            pltpu.VMEM((ts, D_MODEL), BF16),
            pltpu.VMEM((ts, A_WIDTH), F32),
            pltpu.VMEM((ts, A_WIDTH), BF16),
            pltpu.VMEM((ts, SWA_WIDTH + MEM_WIDTH), BF16),
            pltpu.VMEM((SWA_HEADS, LANES, CHUNK + ts), BF16),
            pltpu.VMEM((SWA_HEADS, CHUNK + ts, LANES), BF16),
            pltpu.VMEM((ts, MIX_WIDTH), F32),
            pltpu.VMEM((MEM_HEADS, LANES, MEM_LEN), BF16),
            pltpu.VMEM((MEM_HEADS, MEM_LEN, LANES), BF16),
            pltpu.VMEM((2, SWA_HEADS, CHUNK, 2 * CHUNK), F32),
            pltpu.VMEM((A_GROUPS, CHUNK, CHUNK), BF16),
            pltpu.VMEM((A_GROUPS, CHUNK, LANES), F32),
            pltpu.VMEM((D_MODEL, 2 * MEM_WIDTH), BF16),
            pltpu.VMEM((MIX_WIDTH, D_MODEL), BF16),
            pltpu.VMEM((ts, MIX_WIDTH), F32),
            pltpu.VMEM((ts, MIX_WIDTH), BF16),
            pltpu.VMEM((ts, D_MODEL), F32),
        ],
    )
    return pl.pallas_call(
        _layer_kernel,
        grid_spec=grid_spec,
        out_shape=jax.ShapeDtypeStruct(x.shape, x.dtype),
        compiler_params=pltpu.CompilerParams(
            dimension_semantics=("arbitrary", "arbitrary"),
            vmem_limit_bytes=VMEM_LIMIT_BYTES),
        name="layer",
    )(sinks, x, mem, w_in_bf16, w_kv, w_out, pre_g, mem_g, post_g, vng, vnb,
      w_spatial, b_spatial, rel_bias, buckets)


def kernel(x, mem, pre_norm_g, post_norm_g, mem_norm_g, w_in, w_mem_kv, v_norm_g, v_norm_b,
           w_spatial, b_spatial, attn_sinks, rel_bias, w_out):
    depth = w_in.shape[0]
    buckets = jnp.asarray(_band_buckets())
    for layer in range(depth):
        x = _layer(x, mem, w_in[layer].astype(BF16), w_mem_kv[layer], w_out[layer],
                   pre_norm_g[layer][None, :], mem_norm_g[layer][None, :], post_norm_g[layer][None, :],
                   v_norm_g[layer][None, :], v_norm_b[layer][None, :],
                   w_spatial[layer], b_spatial[layer], buckets, attn_sinks[layer], rel_bias)
    return x
```

```python
import numpy as np
import jax
import jax.numpy as jnp
from jax import lax
from jax.experimental import pallas as pl
from jax.experimental.pallas import tpu as pltpu

D_MODEL = 1024
MEM_LEN = 256
HEAD_DIM = 64
CHUNK = 128
A_GROUPS = 4
A_WIDTH = 512
SWA_HEADS = 4
SWA_WIDTH = 256
SWA_KV_WIDTH = 128
MEM_HEADS = 4
MEM_WIDTH = 256
MIX_WIDTH = 1024
IN_WIDTH = 2816
N_BUCKETS = 32
MAX_DISTANCE = 128
WINDOW = 128
EPS = 1e-6
NEG = -1e30
LOG2E = float(np.log2(np.e))
Q_SCALE = HEAD_DIM ** -0.5 * LOG2E

OFF_U = 0
OFF_V = A_WIDTH
OFF_SQ = 2 * A_WIDTH
OFF_SK = OFF_SQ + SWA_WIDTH
OFF_SV = OFF_SK + SWA_KV_WIDTH
OFF_MQ = OFF_SV + SWA_KV_WIDTH
OFF_Z = OFF_MQ + MEM_WIDTH

LANES = 128
SEQ_TILE = 1024
SUB_TILES = (256, 256, 256, 128, 128)
PROJ_COLS = 512
ITEM_LAG = 2
PROJECT_COLUMN_PIECES = 6
PROJECT_OUT_DOTS = D_MODEL // PROJ_COLS
HEAD_STAGES_PER_CHUNK = 2 * (SWA_HEADS // 2 + MEM_HEADS // 2)
VMEM_LIMIT_BYTES = 56 * 1024 * 1024

BF16 = jnp.bfloat16
F32 = jnp.float32


def _t5_causal_buckets(dist):
    n = np.maximum(dist, 0)
    max_exact = N_BUCKETS // 2
    large = max_exact + (np.log(np.maximum(n, 1) / max_exact) / np.log(MAX_DISTANCE / max_exact)
                         * (N_BUCKETS - max_exact)).astype(np.int32)
    large = np.minimum(large, N_BUCKETS - 1)
    return np.where(n < max_exact, n, large).astype(np.int32)


def _band_buckets():
    qi = np.arange(CHUNK)[:, None]
    kj = np.arange(2 * CHUNK)[None, :]
    dist = qi + CHUNK - kj
    valid = (dist >= 0) & (dist < WINDOW)
    return np.where(valid, _t5_causal_buckets(dist), -1).astype(np.int32)


def _gelu_tanh(x):
    c = np.float32(np.sqrt(2.0 / np.pi))
    ck = np.float32(np.sqrt(2.0 / np.pi) * 0.044715)
    hx = 0.5 * x
    return hx + hx * jnp.tanh(x * (c + ck * (x * x)))


def _silu(z):
    hz = 0.5 * z
    return hz + hz * jnp.tanh(hz)


def _dot(a, b):
    return jnp.dot(a, b, preferred_element_type=F32)


_DONE = object()


def _skewed(items, lag):
    waiting = []
    for item in items:
        if next(item, _DONE) is not _DONE:
            waiting.append(item)
        yield
        if len(waiting) > lag:
            next(waiting.pop(0), _DONE)
            yield
    for item in waiting:
        next(item, _DONE)
        yield


def _chain(*gens):
    for g in gens:
        yield from g


def _interleave(a, b, b_per_a):
    credit = 0.0
    a_live = b_live = True
    while a_live or b_live:
        if a_live:
            a_live = next(a, _DONE) is not _DONE
        credit += b_per_a
        while b_live and (credit >= 1.0 or not a_live):
            b_live = next(b, _DONE) is not _DONE
            credit -= 1.0


def _lane_half_mask(parity):
    lane = lax.broadcasted_iota(jnp.int32, (1, LANES), 1)
    return (lane >= HEAD_DIM) if parity else (lane < HEAD_DIM)


def _place_head_rows(kt_head, parity):
    zeros = jnp.zeros_like(kt_head)
    return jnp.concatenate([zeros, kt_head] if parity else [kt_head, zeros], axis=0)


def _layer_kernel(sinks_ref,
                  x_ref, mem_ref, win_ref, wkv_f32_ref, wout_f32_ref, preg_ref, memg_ref, postg_ref,
                  vng_ref, vnb_ref, ws_ref, bs_ref, relb_ref, buckets_ref,
                  out_ref,
                  h_scr, gu_scr, vn_scr, qs_scr, ktvar_scr, vvar_scr, zg_scr,
                  mk_scr, mv_scr, bias_scr, wsm_scr, bst_scr, wkv_ref, wout_ref,
                  y_scr, ybf_scr, o_scr):
    ts = x_ref.shape[1]
    b = pl.program_id(0)
    t = pl.program_id(1)

    @pl.when((b == 0) & (t == 0))
    def _init():
        row = lax.broadcasted_iota(jnp.int32, (CHUNK, CHUNK), 0)
        col = lax.broadcasted_iota(jnp.int32, (CHUNK, CHUNK), 1)
        for g in range(A_GROUPS):
            wsm_scr[g] = jnp.where(row >= col, ws_ref[g], 0.0).astype(BF16)
            bs_col = jnp.sum(jnp.where(row == col, bs_ref[g:g + 1, :], 0.0), axis=1, keepdims=True)
            bst_scr[g] = jnp.broadcast_to(bs_col, (CHUNK, LANES))
        for r0 in range(0, D_MODEL, 2 * CHUNK):
            wrows = pl.ds(r0, 2 * CHUNK)
            wkv_ref[wrows, :] = wkv_f32_ref[wrows, :].astype(BF16)
            wout_ref[wrows, :] = wout_f32_ref[wrows, :].astype(BF16)
        buckets = buckets_ref[...]
        prev_key = lax.broadcasted_iota(jnp.int32, (CHUNK, 2 * CHUNK), 1) < CHUNK
        for hd in range(SWA_HEADS):
            tbl = jnp.full((CHUNK, 2 * CHUNK), NEG, F32)
            for bk in range(N_BUCKETS):
                tbl = jnp.where(buckets == bk, relb_ref[bk:bk + 1, hd:hd + 1] * LOG2E, tbl)
            bias_scr[0, hd] = tbl
            bias_scr[1, hd] = jnp.where(prev_key, NEG, tbl)

    @pl.when(t == 0)
    def _sequence_start():
        ktvar_scr[:, :, 0:CHUNK] = jnp.zeros((SWA_HEADS, LANES, CHUNK), BF16)
        vvar_scr[:, 0:CHUNK, :] = jnp.zeros((SWA_HEADS, CHUNK, LANES), BF16)
        m = mem_ref[0]
        ms = jnp.mean(m * m, axis=-1, keepdims=True)
        hm = (m * lax.rsqrt(ms + EPS) * memg_ref[...]).astype(BF16)
        kv = _dot(hm, wkv_ref[...])
        for pair in range(MEM_HEADS // 2):
            kt_pair = kv[:, pair * LANES:(pair + 1) * LANES].T
            v_pair = kv[:, MEM_WIDTH + pair * LANES:MEM_WIDTH + (pair + 1) * LANES]
            for parity in range(2):
                hd = 2 * pair + parity
                kt_head = kt_pair[parity * HEAD_DIM:(parity + 1) * HEAD_DIM]
                mk_scr[hd] = _place_head_rows(kt_head, parity).astype(BF16)
                mv_scr[hd] = jnp.where(_lane_half_mask(parity), v_pair, 0.0).astype(BF16)


    @pl.when(t > 0)
    def _copy_carry():
        ktvar_scr[:, :, 0:CHUNK] = ktvar_scr[:, :, ts:ts + CHUNK]
        vvar_scr[:, 0:CHUNK, :] = vvar_scr[:, ts:ts + CHUNK, :]

    lo = _lane_half_mask(0)
    hi = _lane_half_mask(1)
    first_tile = jnp.where(t == 0, 1, 0)

    def pre_norm(r0, nrows):
        for rc in range(nrows // CHUNK):
            crows = pl.ds(r0 + rc * CHUNK, CHUNK)
            x = x_ref[0, crows, :]
            ms = jnp.mean(x * x, axis=-1, keepdims=True)
            h_scr[crows, :] = (x * lax.rsqrt(ms + EPS) * preg_ref[...]).astype(BF16)

    def proj(rows, c0, width):
        return _dot(h_scr[rows, :], win_ref[:, c0:c0 + width])

    def project(r0, nrows):
        rows = pl.ds(r0, nrows)

        v_all = _gelu_tanh(proj(rows, OFF_V, A_WIDTH))
        for g in range(A_GROUPS):
            cols = pl.ds(g * LANES, LANES)
            v = v_all[:, g * LANES:(g + 1) * LANES]
            mu = jnp.mean(v, axis=-1, keepdims=True)
            d = v - mu
            var = jnp.mean(d * d, axis=-1, keepdims=True)
            vn = d * lax.rsqrt(var + EPS) * vng_ref[:, cols] + vnb_ref[:, cols]
            vn_scr[rows, cols] = vn.astype(BF16)
        yield

        gu_scr[rows, :] = _gelu_tanh(proj(rows, OFF_U, A_WIDTH))
        yield

        qkv = proj(rows, OFF_SQ, SWA_WIDTH + 2 * SWA_KV_WIDTH)
        qs_scr[rows, 0:SWA_WIDTH] = (qkv[:, 0:SWA_WIDTH] * Q_SCALE).astype(BF16)
        skt = qkv[:, SWA_WIDTH:SWA_WIDTH + LANES].T
        sv = qkv[:, SWA_WIDTH + LANES:]
        sv_rot = pltpu.roll(sv, HEAD_DIM, 1)
        new_rows = pl.ds(CHUNK + r0, nrows)
        for hd, (keep, vsrc) in enumerate([(lo, sv), (hi, sv_rot), (lo, sv_rot), (hi, sv)]):
            kv_head = hd // 2
            kt_head = skt[kv_head * HEAD_DIM:(kv_head + 1) * HEAD_DIM]
            ktvar_scr[hd, :, new_rows] = _place_head_rows(kt_head, hd % 2).astype(BF16)
            vvar_scr[hd, new_rows, :] = jnp.where(keep, vsrc, 0.0).astype(BF16)
        yield

        mqz = proj(rows, OFF_MQ, PROJ_COLS)
        qs_scr[rows, SWA_WIDTH:] = (mqz[:, 0:MEM_WIDTH] * Q_SCALE).astype(BF16)
        zg_scr[rows, 0:PROJ_COLS - MEM_WIDTH] = _silu(mqz[:, MEM_WIDTH:])
        yield

        for c in range(r0 // CHUNK, (r0 + nrows) // CHUNK):
            for g in range(A_GROUPS):
                spatial_gating(c, g)
            yield

        done = PROJ_COLS - MEM_WIDTH
        while done < MIX_WIDTH:
            width = min(PROJ_COLS, MIX_WIDTH - done)
            zg_scr[rows, done:done + width] = _silu(proj(rows, OFF_Z + done, width))
            done += width
            yield

    def spatial_gating(c, g):
        rows = pl.ds(c * CHUNK, CHUNK)
        cols = pl.ds(g * LANES, LANES)
        sv_g = _dot(wsm_scr[g], vn_scr[rows, cols]) + bst_scr[g]
        y_scr[rows, cols] = gu_scr[rows, cols] * sv_g

    def attn_item(c, pair, sliding):
        r0 = c * CHUNK
        rows = pl.ds(r0, CHUNK)
        heads = (2 * pair, 2 * pair + 1)
        if sliding:
            band = pl.ds(r0, 2 * CHUNK)
            qp = qs_scr[rows, pl.ds(pair * LANES, LANES)]
            keys = jnp.concatenate([ktvar_scr[hd, :, band] for hd in heads], axis=1)
            bias_sel = first_tile if c == 0 else 0
            bias = jnp.concatenate([bias_scr[bias_sel, hd] for hd in heads], axis=1)
            logits = _dot(qp, keys) + bias
        else:
            qp = qs_scr[rows, pl.ds(SWA_WIDTH + pair * LANES, LANES)]
            keys = jnp.concatenate([mk_scr[hd] for hd in heads], axis=1)
            logits = _dot(qp, keys)
        n_keys = logits.shape[1] // 2
        yield
        probs, recips = [], []
        for i, hd in enumerate(heads):
            s = logits[:, i * n_keys:(i + 1) * n_keys]
            m = jnp.max(s, axis=-1, keepdims=True)
            if sliding:
                sink = sinks_ref[hd] * LOG2E
                m = jnp.maximum(m, sink)
            p = jnp.exp2(s - m)
            denom = jnp.sum(p, axis=-1, keepdims=True)
            if sliding:
                denom = denom + jnp.exp2(sink - m)
            probs.append(p.astype(BF16))
            recips.append(1.0 / denom)
        if sliding:
            values = jnp.concatenate([vvar_scr[hd, band, :] for hd in heads], axis=0)
        else:
            values = jnp.concatenate([mv_scr[hd] for hd in heads], axis=0)
        o_pair = _dot(jnp.concatenate(probs, axis=1), values)
        o_pair = o_pair * jnp.where(lo, recips[0], recips[1])
        off = A_WIDTH if sliding else A_WIDTH + SWA_WIDTH
        y_scr[rows, pl.ds(off + pair * LANES, LANES)] = o_pair

    def head_items(r0, nrows):
        for c in range(r0 // CHUNK, (r0 + nrows) // CHUNK):
            for pair in range(SWA_HEADS // 2):
                yield attn_item(c, pair, True)
            for pair in range(MEM_HEADS // 2):
                yield attn_item(c, pair, False)

    def project_out(r0, nrows):
        rows = pl.ds(r0, nrows)
        ybf_scr[rows, :] = (y_scr[rows, :] * zg_scr[rows, :]).astype(BF16)
        for c0 in range(0, D_MODEL, PROJ_COLS):
            o_scr[rows, c0:c0 + PROJ_COLS] = _dot(ybf_scr[rows, :], wout_ref[:, c0:c0 + PROJ_COLS])
            yield
        for rc in range(nrows // CHUNK):
            crows = pl.ds(r0 + rc * CHUNK, CHUNK)
            o = o_scr[crows, :]
            ms = jnp.mean(o * o, axis=-1, keepdims=True)
            out_ref[0, crows, :] = x_ref[0, crows, :] + o * lax.rsqrt(ms + EPS) * postg_ref[...]
            yield

    assert sum(SUB_TILES) == ts
    starts = [sum(SUB_TILES[:s]) for s in range(len(SUB_TILES))]
    subs = list(zip(starts, SUB_TILES))
    n_sub = len(subs)
    pre_norm(*subs[0])
    filler = project(*subs[0])
    for s in range(n_sub):
        if s + 1 < n_sub:
            pre_norm(*subs[s + 1])
        for _ in filler:
            pass
        heads = _skewed(head_items(*subs[s]), ITEM_LAG)
        n_filler = 0
        if s + 1 < n_sub:
            filler = project(*subs[s + 1])
            n_filler += PROJECT_COLUMN_PIECES + subs[s + 1][1] // CHUNK
        else:
            filler = iter(())
        if s >= 1:
            filler = _chain(project_out(*subs[s - 1]), filler)
            n_filler += PROJECT_OUT_DOTS + subs[s - 1][1] // CHUNK
        _interleave(heads, filler, n_filler / ((subs[s][1] // CHUNK) * HEAD_STAGES_PER_CHUNK))
        filler = iter(())
    for _ in project_out(*subs[-1]):
        pass


def _layer(x, mem, w_in_bf16, w_kv, w_out, pre_g, mem_g, post_g, vng, vnb, w_spatial,
           b_spatial, buckets, sinks, rel_bias):
    batch, seq, _ = x.shape
    ts = SEQ_TILE
    n_tiles = seq // ts
    const2 = lambda b, t, *_: (0, 0)
    const3 = lambda b, t, *_: (0, 0, 0)
    grid_spec = pltpu.PrefetchScalarGridSpec(
        num_scalar_prefetch=1,
        grid=(batch, n_tiles),
        in_specs=[
            pl.BlockSpec((1, ts, D_MODEL), lambda b, t, *_: (b, t, 0)),
            pl.BlockSpec((1, MEM_LEN, D_MODEL), lambda b, t, *_: (b, 0, 0)),
            pl.BlockSpec((D_MODEL, IN_WIDTH), const2),
            pl.BlockSpec((D_MODEL, 2 * MEM_WIDTH), const2),
            pl.BlockSpec((MIX_WIDTH, D_MODEL), const2),
            pl.BlockSpec((1, D_MODEL), const2),
            pl.BlockSpec((1, D_MODEL), const2),
            pl.BlockSpec((1, D_MODEL), const2),
            pl.BlockSpec((1, A_WIDTH), const2),
            pl.BlockSpec((1, A_WIDTH), const2),
            pl.BlockSpec((A_GROUPS, CHUNK, CHUNK), const3),
            pl.BlockSpec((A_GROUPS, CHUNK), const2),
            pl.BlockSpec((N_BUCKETS, SWA_HEADS), const2),
            pl.BlockSpec((CHUNK, 2 * CHUNK), const2),
        ],
        out_specs=pl.BlockSpec((1, ts, D_MODEL), lambda b, t, *_: (b, t, 0)),
        scratch_shapes=[
            pltpu.VMEM((ts, D_MODEL), BF16),
            pltpu.VMEM((ts, A_WIDTH), F32),
            pltpu.VMEM((ts, A_WIDTH), BF16),
            pltpu.VMEM((ts, SWA_WIDTH + MEM_WIDTH), BF16),
            pltpu.VMEM((SWA_HEADS, LANES, CHUNK + ts), BF16),
            pltpu.VMEM((SWA_HEADS, CHUNK + ts, LANES), BF16),
            pltpu.VMEM((ts, MIX_WIDTH), F32),
            pltpu.VMEM((MEM_HEADS, LANES, MEM_LEN), BF16),
            pltpu.VMEM((MEM_HEADS, MEM_LEN, LANES), BF16),
            pltpu.VMEM((2, SWA_HEADS, CHUNK, 2 * CHUNK), F32),
            pltpu.VMEM((A_GROUPS, CHUNK, CHUNK), BF16),
            pltpu.VMEM((A_GROUPS, CHUNK, LANES), F32),
            pltpu.VMEM((D_MODEL, 2 * MEM_WIDTH), BF16),
            pltpu.VMEM((MIX_WIDTH, D_MODEL), BF16),
            pltpu.VMEM((ts, MIX_WIDTH), F32),
            pltpu.VMEM((ts, MIX_WIDTH), BF16),
            pltpu.VMEM((ts, D_MODEL), F32),
        ],
    )
    return pl.pallas_call(
        _layer_kernel,
        grid_spec=grid_spec,
        out_shape=jax.ShapeDtypeStruct(x.shape, x.dtype),
        compiler_params=pltpu.CompilerParams(
            dimension_semantics=("arbitrary", "arbitrary"),
            vmem_limit_bytes=VMEM_LIMIT_BYTES),
        name="layer",
    )(sinks, x, mem, w_in_bf16, w_kv, w_out, pre_g, mem_g, post_g, vng, vnb,
      w_spatial, b_spatial, rel_bias, buckets)


def kernel(x, mem, pre_norm_g, post_norm_g, mem_norm_g, w_in, w_mem_kv, v_norm_g, v_norm_b,
           w_spatial, b_spatial, attn_sinks, rel_bias, w_out):
    depth = w_in.shape[0]
    buckets = jnp.asarray(_band_buckets())
    for layer in range(depth):
        x = _layer(x, mem, w_in[layer].astype(BF16), w_mem_kv[layer], w_out[layer],
                   pre_norm_g[layer][None, :], mem_norm_g[layer][None, :], post_norm_g[layer][None, :],
                   v_norm_g[layer][None, :], v_norm_b[layer][None, :],
                   w_spatial[layer], b_spatial[layer], buckets, attn_sinks[layer], rel_bias)
    return x
```

```python
import numpy as np
import jax
import jax.numpy as jnp
from jax import lax
from jax.experimental import pallas as pl
from jax.experimental.pallas import tpu as pltpu

D_MODEL = 1024
MEM_LEN = 256
HEAD_DIM = 64
CHUNK = 128
A_GROUPS = 4
A_WIDTH = 512
SWA_HEADS = 4
SWA_WIDTH = 256
SWA_KV_WIDTH = 128
MEM_HEADS = 4
MEM_WIDTH = 256
MIX_WIDTH = 1024
IN_WIDTH = 2816
N_BUCKETS = 32
MAX_DISTANCE = 128
WINDOW = 128
EPS = 1e-6
NEG = -1e30
LOG2E = float(np.log2(np.e))
Q_SCALE = HEAD_DIM ** -0.5 * LOG2E

OFF_U = 0
OFF_V = A_WIDTH
OFF_SQ = 2 * A_WIDTH
OFF_SK = OFF_SQ + SWA_WIDTH
OFF_SV = OFF_SK + SWA_KV_WIDTH
OFF_MQ = OFF_SV + SWA_KV_WIDTH
OFF_Z = OFF_MQ + MEM_WIDTH

LANES = 128
SEQ_TILE = 1024
SUB_TILES = (256, 256, 256, 256)
PROJ_COLS = 512
ITEM_LAG = 2
PROJECT_COLUMN_PIECES = 6
PROJECT_OUT_DOTS = D_MODEL // PROJ_COLS
HEAD_STAGES_PER_CHUNK = 2 * (SWA_HEADS // 2 + MEM_HEADS // 2)
VMEM_LIMIT_BYTES = 56 * 1024 * 1024

BF16 = jnp.bfloat16
F32 = jnp.float32


def _t5_causal_buckets(dist):
    n = np.maximum(dist, 0)
    max_exact = N_BUCKETS // 2
    large = max_exact + (np.log(np.maximum(n, 1) / max_exact) / np.log(MAX_DISTANCE / max_exact)
                         * (N_BUCKETS - max_exact)).astype(np.int32)
    large = np.minimum(large, N_BUCKETS - 1)
    return np.where(n < max_exact, n, large).astype(np.int32)


def _band_buckets():
    qi = np.arange(CHUNK)[:, None]
    kj = np.arange(2 * CHUNK)[None, :]
    dist = qi + CHUNK - kj
    valid = (dist >= 0) & (dist < WINDOW)
    return np.where(valid, _t5_causal_buckets(dist), -1).astype(np.int32)


def _gelu_tanh(x):
    c = np.float32(np.sqrt(2.0 / np.pi))
    ck = np.float32(np.sqrt(2.0 / np.pi) * 0.044715)
    hx = 0.5 * x
    return hx + hx * jnp.tanh(x * (c + ck * (x * x)))


def _silu(z):
    hz = 0.5 * z
    return hz + hz * jnp.tanh(hz)


def _dot(a, b):
    return jnp.dot(a, b, preferred_element_type=F32)


_DONE = object()


def _skewed(items, lag):
    waiting = []
    for item in items:
        if next(item, _DONE) is not _DONE:
            waiting.append(item)
        yield
        if len(waiting) > lag:
            next(waiting.pop(0), _DONE)
            yield
    for item in waiting:
        next(item, _DONE)
        yield


def _chain(*gens):
    for g in gens:
        yield from g


def _interleave(a, b, b_per_a):
    credit = 0.0
    a_live = b_live = True
    while a_live or b_live:
        if a_live:
            a_live = next(a, _DONE) is not _DONE
        credit += b_per_a
        while b_live and (credit >= 1.0 or not a_live):
            b_live = next(b, _DONE) is not _DONE
            credit -= 1.0


def _lane_half_mask(parity):
    lane = lax.broadcasted_iota(jnp.int32, (1, LANES), 1)
    return (lane >= HEAD_DIM) if parity else (lane < HEAD_DIM)


def _place_head_rows(kt_head, parity):
    zeros = jnp.zeros_like(kt_head)
    return jnp.concatenate([zeros, kt_head] if parity else [kt_head, zeros], axis=0)


def _layer_kernel(sinks_ref,
                  x_ref, mem_ref, win_ref, wkv_f32_ref, wout_f32_ref, preg_ref, memg_ref, postg_ref,
                  vng_ref, vnb_ref, ws_ref, bs_ref, relb_ref, buckets_ref,
                  out_ref,
                  h_scr, gu_scr, vn_scr, qs_scr, ktvar_scr, vvar_scr, zg_scr,
                  mk_scr, mv_scr, bias_scr, wsm_scr, bst_scr, wkv_ref, wout_ref,
                  y_scr, ybf_scr, o_scr):
    ts = x_ref.shape[1]
    b = pl.program_id(0)
    t = pl.program_id(1)

    @pl.when((b == 0) & (t == 0))
    def _init():
        row = lax.broadcasted_iota(jnp.int32, (CHUNK, CHUNK), 0)
        col = lax.broadcasted_iota(jnp.int32, (CHUNK, CHUNK), 1)
        for g in range(A_GROUPS):
            wsm_scr[g] = jnp.where(row >= col, ws_ref[g], 0.0).astype(BF16)
            bs_col = jnp.sum(jnp.where(row == col, bs_ref[g:g + 1, :], 0.0), axis=1, keepdims=True)
            bst_scr[g] = jnp.broadcast_to(bs_col, (CHUNK, LANES))
        for r0 in range(0, D_MODEL, 2 * CHUNK):
            wrows = pl.ds(r0, 2 * CHUNK)
            wkv_ref[wrows, :] = wkv_f32_ref[wrows, :].astype(BF16)
            wout_ref[wrows, :] = wout_f32_ref[wrows, :].astype(BF16)
        buckets = buckets_ref[...]
        prev_key = lax.broadcasted_iota(jnp.int32, (CHUNK, 2 * CHUNK), 1) < CHUNK
        for hd in range(SWA_HEADS):
            tbl = jnp.full((CHUNK, 2 * CHUNK), NEG, F32)
            for bk in range(N_BUCKETS):
                tbl = jnp.where(buckets == bk, relb_ref[bk:bk + 1, hd:hd + 1] * LOG2E, tbl)
            bias_scr[0, hd] = tbl
            bias_scr[1, hd] = jnp.where(prev_key, NEG, tbl)

    @pl.when(t == 0)
    def _sequence_start():
        ktvar_scr[:, :, 0:CHUNK] = jnp.zeros((SWA_HEADS, LANES, CHUNK), BF16)
        vvar_scr[:, 0:CHUNK, :] = jnp.zeros((SWA_HEADS, CHUNK, LANES), BF16)
        m = mem_ref[0]
        ms = jnp.mean(m * m, axis=-1, keepdims=True)
        hm = (m * lax.rsqrt(ms + EPS) * memg_ref[...]).astype(BF16)
        kv = _dot(hm, wkv_ref[...])
        for pair in range(MEM_HEADS // 2):
            kt_pair = kv[:, pair * LANES:(pair + 1) * LANES].T
            v_pair = kv[:, MEM_WIDTH + pair * LANES:MEM_WIDTH + (pair + 1) * LANES]
            for parity in range(2):
                hd = 2 * pair + parity
                kt_head = kt_pair[parity * HEAD_DIM:(parity + 1) * HEAD_DIM]
                mk_scr[hd] = _place_head_rows(kt_head, parity).astype(BF16)
                mv_scr[hd] = jnp.where(_lane_half_mask(parity), v_pair, 0.0).astype(BF16)


    @pl.when(t > 0)
    def _copy_carry():
        ktvar_scr[:, :, 0:CHUNK] = ktvar_scr[:, :, ts:ts + CHUNK]
        vvar_scr[:, 0:CHUNK, :] = vvar_scr[:, ts:ts + CHUNK, :]

    lo = _lane_half_mask(0)
    hi = _lane_half_mask(1)
    first_tile = jnp.where(t == 0, 1, 0)

    def pre_norm(r0, nrows):
        for rc in range(nrows // CHUNK):
            crows = pl.ds(r0 + rc * CHUNK, CHUNK)
            x = x_ref[0, crows, :]
            ms = jnp.mean(x * x, axis=-1, keepdims=True)
            h_scr[crows, :] = (x * lax.rsqrt(ms + EPS) * preg_ref[...]).astype(BF16)

    def proj(rows, c0, width):
        return _dot(h_scr[rows, :], win_ref[:, c0:c0 + width])

    def project(r0, nrows):
        rows = pl.ds(r0, nrows)

        v_all = _gelu_tanh(proj(rows, OFF_V, A_WIDTH))
        for g in range(A_GROUPS):
            cols = pl.ds(g * LANES, LANES)
            v = v_all[:, g * LANES:(g + 1) * LANES]
            mu = jnp.mean(v, axis=-1, keepdims=True)
            d = v - mu
            var = jnp.mean(d * d, axis=-1, keepdims=True)
            vn = d * lax.rsqrt(var + EPS) * vng_ref[:, cols] + vnb_ref[:, cols]
            vn_scr[rows, cols] = vn.astype(BF16)
        yield

        gu_scr[rows, :] = _gelu_tanh(proj(rows, OFF_U, A_WIDTH))
        yield

        qkv = proj(rows, OFF_SQ, SWA_WIDTH + 2 * SWA_KV_WIDTH)
        qs_scr[rows, 0:SWA_WIDTH] = (qkv[:, 0:SWA_WIDTH] * Q_SCALE).astype(BF16)
        skt = qkv[:, SWA_WIDTH:SWA_WIDTH + LANES].T
        sv = qkv[:, SWA_WIDTH + LANES:]
        sv_rot = pltpu.roll(sv, HEAD_DIM, 1)
        new_rows = pl.ds(CHUNK + r0, nrows)
        for hd, (keep, vsrc) in enumerate([(lo, sv), (hi, sv_rot), (lo, sv_rot), (hi, sv)]):
            kv_head = hd // 2
            kt_head = skt[kv_head * HEAD_DIM:(kv_head + 1) * HEAD_DIM]
            ktvar_scr[hd, :, new_rows] = _place_head_rows(kt_head, hd % 2).astype(BF16)
            vvar_scr[hd, new_rows, :] = jnp.where(keep, vsrc, 0.0).astype(BF16)
        yield

        mqz = proj(rows, OFF_MQ, PROJ_COLS)
        qs_scr[rows, SWA_WIDTH:] = (mqz[:, 0:MEM_WIDTH] * Q_SCALE).astype(BF16)
        zg_scr[rows, 0:PROJ_COLS - MEM_WIDTH] = _silu(mqz[:, MEM_WIDTH:])
        yield

        for c in range(r0 // CHUNK, (r0 + nrows) // CHUNK):
            for g in range(A_GROUPS):
                spatial_gating(c, g)
            yield

        done = PROJ_COLS - MEM_WIDTH
        while done < MIX_WIDTH:
            width = min(PROJ_COLS, MIX_WIDTH - done)
            zg_scr[rows, done:done + width] = _silu(proj(rows, OFF_Z + done, width))
            done += width
            yield

    def spatial_gating(c, g):
        rows = pl.ds(c * CHUNK, CHUNK)
        cols = pl.ds(g * LANES, LANES)
        sv_g = _dot(wsm_scr[g], vn_scr[rows, cols]) + bst_scr[g]
        y_scr[rows, cols] = gu_scr[rows, cols] * sv_g

    def attn_item(c, pair, sliding):
        r0 = c * CHUNK
        rows = pl.ds(r0, CHUNK)
        heads = (2 * pair, 2 * pair + 1)
        if sliding:
            band = pl.ds(r0, 2 * CHUNK)
            qp = qs_scr[rows, pl.ds(pair * LANES, LANES)]
            keys = jnp.concatenate([ktvar_scr[hd, :, band] for hd in heads], axis=1)
            bias_sel = first_tile if c == 0 else 0
            bias = jnp.concatenate([bias_scr[bias_sel, hd] for hd in heads], axis=1)
            logits = _dot(qp, keys) + bias
        else:
            qp = qs_scr[rows, pl.ds(SWA_WIDTH + pair * LANES, LANES)]
            keys = jnp.concatenate([mk_scr[hd] for hd in heads], axis=1)
            logits = _dot(qp, keys)
        n_keys = logits.shape[1] // 2
        yield
        probs, recips = [], []
        for i, hd in enumerate(heads):
            s = logits[:, i * n_keys:(i + 1) * n_keys]
            m = jnp.max(s, axis=-1, keepdims=True)
            p = jnp.exp2(s - m)
            denom = jnp.sum(p, axis=-1, keepdims=True)
            if sliding:
                denom = denom + jnp.exp2(sinks_ref[hd] * LOG2E - m)
            probs.append(p.astype(BF16))
            recips.append(1.0 / denom)
        if sliding:
            values = jnp.concatenate([vvar_scr[hd, band, :] for hd in heads], axis=0)
        else:
            values = jnp.concatenate([mv_scr[hd] for hd in heads], axis=0)
        o_pair = _dot(jnp.concatenate(probs, axis=1), values)
        o_pair = o_pair * jnp.where(lo, recips[0], recips[1])
        off = A_WIDTH if sliding else A_WIDTH + SWA_WIDTH
        y_scr[rows, pl.ds(off + pair * LANES, LANES)] = o_pair

    def head_items(r0, nrows):
        for c in range(r0 // CHUNK, (r0 + nrows) // CHUNK):
            for pair in range(SWA_HEADS // 2):
                yield attn_item(c, pair, True)
            for pair in range(MEM_HEADS // 2):
                yield attn_item(c, pair, False)

    def project_out(r0, nrows):
        rows = pl.ds(r0, nrows)
        ybf_scr[rows, :] = (y_scr[rows, :] * zg_scr[rows, :]).astype(BF16)
        for c0 in range(0, D_MODEL, PROJ_COLS):
            o_scr[rows, c0:c0 + PROJ_COLS] = _dot(ybf_scr[rows, :], wout_ref[:, c0:c0 + PROJ_COLS])
            yield
        for rc in range(nrows // CHUNK):
            crows = pl.ds(r0 + rc * CHUNK, CHUNK)
            o = o_scr[crows, :]
            ms = jnp.mean(o * o, axis=-1, keepdims=True)
            out_ref[0, crows, :] = x_ref[0, crows, :] + o * lax.rsqrt(ms + EPS) * postg_ref[...]
            yield

    assert sum(SUB_TILES) == ts
    starts = [sum(SUB_TILES[:s]) for s in range(len(SUB_TILES))]
    subs = list(zip(starts, SUB_TILES))
    n_sub = len(subs)
    pre_norm(*subs[0])
    filler = project(*subs[0])
    for s in range(n_sub):
        if s + 1 < n_sub:
            pre_norm(*subs[s + 1])
        for _ in filler:
            pass
        heads = _skewed(head_items(*subs[s]), ITEM_LAG)
        n_filler = 0
        if s + 1 < n_sub:
            filler = project(*subs[s + 1])
            n_filler += PROJECT_COLUMN_PIECES + subs[s + 1][1] // CHUNK
        else:
            filler = iter(())
        if s >= 1:
            filler = _chain(project_out(*subs[s - 1]), filler)
            n_filler += PROJECT_OUT_DOTS + subs[s - 1][1] // CHUNK
        _interleave(heads, filler, n_filler / ((subs[s][1] // CHUNK) * HEAD_STAGES_PER_CHUNK))
        filler = iter(())
    for _ in project_out(*subs[-1]):
        pass


def _layer(x, mem, w_in_bf16, w_kv, w_out, pre_g, mem_g, post_g, vng, vnb, w_spatial,
           b_spatial, buckets, sinks, rel_bias):
    batch, seq, _ = x.shape
    ts = SEQ_TILE
    n_tiles = seq // ts
    const2 = lambda b, t, *_: (0, 0)
    const3 = lambda b, t, *_: (0, 0, 0)
    grid_spec = pltpu.PrefetchScalarGridSpec(
        num_scalar_prefetch=1,
        grid=(batch, n_tiles),
        in_specs=[
            pl.BlockSpec((1, ts, D_MODEL), lambda b, t, *_: (b, t, 0)),
            pl.BlockSpec((1, MEM_LEN, D_MODEL), lambda b, t, *_: (b, 0, 0)),
            pl.BlockSpec((D_MODEL, IN_WIDTH), const2),
            pl.BlockSpec((D_MODEL, 2 * MEM_WIDTH), const2),
            pl.BlockSpec((MIX_WIDTH, D_MODEL), const2),
            pl.BlockSpec((1, D_MODEL), const2),
            pl.BlockSpec((1, D_MODEL), const2),
            pl.BlockSpec((1, D_MODEL), const2),
            pl.BlockSpec((1, A_WIDTH), const2),
            pl.BlockSpec((1, A_WIDTH), const2),
            pl.BlockSpec((A_GROUPS, CHUNK, CHUNK), const3),
            pl.BlockSpec((A_GROUPS, CHUNK), const2),
            pl.BlockSpec((N_BUCKETS, SWA_HEADS), const2),
            pl.BlockSpec((CHUNK, 2 * CHUNK), const2),
        ],
        out_specs=pl.BlockSpec((1, ts, D_MODEL), lambda b, t, *_: (b, t, 0)),
        scratch_shapes=[
            pltpu.VMEM((ts, D_MODEL), BF16),
            pltpu.VMEM((ts, A_WIDTH), F32),
            pltpu.VMEM((ts, A_WIDTH), BF16),
            pltpu.VMEM((ts, SWA_WIDTH + MEM_WIDTH), BF16),
            pltpu.VMEM((SWA_HEADS, LANES, CHUNK + ts), BF16),
            pltpu.VMEM((SWA_HEADS, CHUNK + ts, LANES), BF16),
            pltpu.VMEM((ts, MIX_WIDTH), F32),
            pltpu.VMEM((MEM_HEADS, LANES, MEM_LEN), BF16),
            pltpu.VMEM((MEM_HEADS, MEM_LEN, LANES), BF16),
            pltpu.VMEM((2, SWA_HEADS, CHUNK, 2 * CHUNK), F32),
            pltpu.VMEM((A_GROUPS, CHUNK, CHUNK), BF16),
            pltpu.VMEM((A_GROUPS, CHUNK, LANES), F32),
            pltpu.VMEM((D_MODEL, 2 * MEM_WIDTH), BF16),
            pltpu.VMEM((MIX_WIDTH, D_MODEL), BF16),
            pltpu.VMEM((ts, MIX_WIDTH), F32),
            pltpu.VMEM((ts, MIX_WIDTH), BF16),
            pltpu.VMEM((ts, D_MODEL), F32),
        ],
    )
    return pl.pallas_call(
        _layer_kernel,
        grid_spec=grid_spec,
        out_shape=jax.ShapeDtypeStruct(x.shape, x.dtype),
        compiler_params=pltpu.CompilerParams(
            dimension_semantics=("arbitrary", "arbitrary"),
            vmem_limit_bytes=VMEM_LIMIT_BYTES),
        name="layer",
    )(sinks, x, mem, w_in_bf16, w_kv, w_out, pre_g, mem_g, post_g, vng, vnb,
      w_spatial, b_spatial, rel_bias, buckets)


def kernel(x, mem, pre_norm_g, post_norm_g, mem_norm_g, w_in, w_mem_kv, v_norm_g, v_norm_b,
           w_spatial, b_spatial, attn_sinks, rel_bias, w_out):
    depth = w_in.shape[0]
    buckets = jnp.asarray(_band_buckets())
    for layer in range(depth):
        x = _layer(x, mem, w_in[layer].astype(BF16), w_mem_kv[layer], w_out[layer],
                   pre_norm_g[layer][None, :], mem_norm_g[layer][None, :], post_norm_g[layer][None, :],
                   v_norm_g[layer][None, :], v_norm_b[layer][None, :],
                   w_spatial[layer], b_spatial[layer], buckets, attn_sinks[layer], rel_bias)
    return x
```

```python
import numpy as np
import jax
import jax.numpy as jnp
from jax import lax
from jax.experimental import pallas as pl
from jax.experimental.pallas import tpu as pltpu

D_MODEL = 1024
MEM_LEN = 256
HEAD_DIM = 64
CHUNK = 128
A_GROUPS = 4
A_WIDTH = 512
SWA_HEADS = 4
SWA_WIDTH = 256
SWA_KV_WIDTH = 128
MEM_HEADS = 4
MEM_WIDTH = 256
MIX_WIDTH = 1024
IN_WIDTH = 2816
N_BUCKETS = 32
MAX_DISTANCE = 128
WINDOW = 128
EPS = 1e-6
NEG = -1e30
LOG2E = float(np.log2(np.e))
Q_SCALE = HEAD_DIM ** -0.5 * LOG2E

OFF_U = 0
OFF_V = A_WIDTH
OFF_SQ = 2 * A_WIDTH
OFF_SK = OFF_SQ + SWA_WIDTH
OFF_SV = OFF_SK + SWA_KV_WIDTH
OFF_MQ = OFF_SV + SWA_KV_WIDTH
OFF_Z = OFF_MQ + MEM_WIDTH

LANES = 128
SEQ_TILE = 1024
SUB_TILES = (256, 256, 256, 256)
PROJ_COLS = 512
ITEM_LAG = 2
PROJECT_COLUMN_PIECES = 6
PROJECT_OUT_DOTS = D_MODEL // PROJ_COLS
HEAD_STAGES_PER_CHUNK = 2 * (SWA_HEADS // 2 + MEM_HEADS // 2) + A_GROUPS
VMEM_LIMIT_BYTES = 56 * 1024 * 1024

BF16 = jnp.bfloat16
F32 = jnp.float32


def _t5_causal_buckets(dist):
    n = np.maximum(dist, 0)
    max_exact = N_BUCKETS // 2
    large = max_exact + (np.log(np.maximum(n, 1) / max_exact) / np.log(MAX_DISTANCE / max_exact)
                         * (N_BUCKETS - max_exact)).astype(np.int32)
    large = np.minimum(large, N_BUCKETS - 1)
    return np.where(n < max_exact, n, large).astype(np.int32)


def _band_buckets():
    qi = np.arange(CHUNK)[:, None]
    kj = np.arange(2 * CHUNK)[None, :]
    dist = qi + CHUNK - kj
    valid = (dist >= 0) & (dist < WINDOW)
    return np.where(valid, _t5_causal_buckets(dist), -1).astype(np.int32)


def _gelu_tanh(x):
    c = np.float32(np.sqrt(2.0 / np.pi))
    ck = np.float32(np.sqrt(2.0 / np.pi) * 0.044715)
    hx = 0.5 * x
    return hx + hx * jnp.tanh(x * (c + ck * (x * x)))


def _silu(z):
    hz = 0.5 * z
    return hz + hz * jnp.tanh(hz)


def _dot(a, b):
    return jnp.dot(a, b, preferred_element_type=F32)


_DONE = object()


def _skewed(items, lag):
    waiting = []
    for item in items:
        if next(item, _DONE) is not _DONE:
            waiting.append(item)
        yield
        if len(waiting) > lag:
            next(waiting.pop(0), _DONE)
            yield
    for item in waiting:
        next(item, _DONE)
        yield


def _chain(*gens):
    for g in gens:
        yield from g


def _interleave(a, b, b_per_a):
    credit = 0.0
    a_live = b_live = True
    while a_live or b_live:
        if a_live:
            a_live = next(a, _DONE) is not _DONE
        credit += b_per_a
        while b_live and (credit >= 1.0 or not a_live):
            b_live = next(b, _DONE) is not _DONE
            credit -= 1.0


def _lane_half_mask(parity):
    lane = lax.broadcasted_iota(jnp.int32, (1, LANES), 1)
    return (lane >= HEAD_DIM) if parity else (lane < HEAD_DIM)


def _place_head_rows(kt_head, parity):
    zeros = jnp.zeros_like(kt_head)
    return jnp.concatenate([zeros, kt_head] if parity else [kt_head, zeros], axis=0)


def _layer_kernel(sinks_ref,
                  x_ref, mem_ref, win_ref, wkv_f32_ref, wout_f32_ref, preg_ref, memg_ref, postg_ref,
                  vng_ref, vnb_ref, ws_ref, bs_ref, relb_ref, buckets_ref,
                  out_ref,
                  h_scr, gu_scr, vn_scr, qs_scr, ktvar_scr, vvar_scr, zg_scr,
                  mk_scr, mv_scr, bias_scr, wsm_scr, bst_scr, wkv_ref, wout_ref,
                  y_scr, ybf_scr, o_scr):
    ts = x_ref.shape[1]
    b = pl.program_id(0)
    t = pl.program_id(1)

    @pl.when((b == 0) & (t == 0))
    def _init():
        row = lax.broadcasted_iota(jnp.int32, (CHUNK, CHUNK), 0)
        col = lax.broadcasted_iota(jnp.int32, (CHUNK, CHUNK), 1)
        for g in range(A_GROUPS):
            wsm_scr[g] = jnp.where(row >= col, ws_ref[g], 0.0).astype(BF16)
            bs_col = jnp.sum(jnp.where(row == col, bs_ref[g:g + 1, :], 0.0), axis=1, keepdims=True)
            bst_scr[g] = jnp.broadcast_to(bs_col, (CHUNK, LANES))
        for r0 in range(0, D_MODEL, 2 * CHUNK):
            wrows = pl.ds(r0, 2 * CHUNK)
            wkv_ref[wrows, :] = wkv_f32_ref[wrows, :].astype(BF16)
            wout_ref[wrows, :] = wout_f32_ref[wrows, :].astype(BF16)
        buckets = buckets_ref[...]
        prev_key = lax.broadcasted_iota(jnp.int32, (CHUNK, 2 * CHUNK), 1) < CHUNK
        for hd in range(SWA_HEADS):
            tbl = jnp.full((CHUNK, 2 * CHUNK), NEG, F32)
            for bk in range(N_BUCKETS):
                tbl = jnp.where(buckets == bk, relb_ref[bk:bk + 1, hd:hd + 1] * LOG2E, tbl)
            bias_scr[0, hd] = tbl
            bias_scr[1, hd] = jnp.where(prev_key, NEG, tbl)

    @pl.when(t == 0)
    def _sequence_start():
        ktvar_scr[:, :, 0:CHUNK] = jnp.zeros((SWA_HEADS, LANES, CHUNK), BF16)
        vvar_scr[:, 0:CHUNK, :] = jnp.zeros((SWA_HEADS, CHUNK, LANES), BF16)
        m = mem_ref[0]
        ms = jnp.mean(m * m, axis=-1, keepdims=True)
        hm = (m * lax.rsqrt(ms + EPS) * memg_ref[...]).astype(BF16)
        kv = _dot(hm, wkv_ref[...])
        for pair in range(MEM_HEADS // 2):
            kt_pair = kv[:, pair * LANES:(pair + 1) * LANES].T
            v_pair = kv[:, MEM_WIDTH + pair * LANES:MEM_WIDTH + (pair + 1) * LANES]
            for parity in range(2):
                hd = 2 * pair + parity
                kt_head = kt_pair[parity * HEAD_DIM:(parity + 1) * HEAD_DIM]
                mk_scr[hd] = _place_head_rows(kt_head, parity).astype(BF16)
                mv_scr[hd] = jnp.where(_lane_half_mask(parity), v_pair, 0.0).astype(BF16)


    @pl.when(t > 0)
    def _copy_carry():
        ktvar_scr[:, :, 0:CHUNK] = ktvar_scr[:, :, ts:ts + CHUNK]
        vvar_scr[:, 0:CHUNK, :] = vvar_scr[:, ts:ts + CHUNK, :]

    lo = _lane_half_mask(0)
    hi = _lane_half_mask(1)
    first_tile = jnp.where(t == 0, 1, 0)

    def pre_norm(r0, nrows):
        for rc in range(nrows // CHUNK):
            crows = pl.ds(r0 + rc * CHUNK, CHUNK)
            x = x_ref[0, crows, :]
            ms = jnp.mean(x * x, axis=-1, keepdims=True)
            h_scr[crows, :] = (x * lax.rsqrt(ms + EPS) * preg_ref[...]).astype(BF16)

    def proj(rows, c0, width):
        return _dot(h_scr[rows, :], win_ref[:, c0:c0 + width])

    def project(r0, nrows):
        rows = pl.ds(r0, nrows)

        v_all = _gelu_tanh(proj(rows, OFF_V, A_WIDTH))
        for g in range(A_GROUPS):
            cols = pl.ds(g * LANES, LANES)
            v = v_all[:, g * LANES:(g + 1) * LANES]
            mu = jnp.mean(v, axis=-1, keepdims=True)
            d = v - mu
            var = jnp.mean(d * d, axis=-1, keepdims=True)
            vn = d * lax.rsqrt(var + EPS) * vng_ref[:, cols] + vnb_ref[:, cols]
            vn_scr[rows, cols] = vn.astype(BF16)
        yield

        gu_scr[rows, :] = _gelu_tanh(proj(rows, OFF_U, A_WIDTH))
        yield

        qkv = proj(rows, OFF_SQ, SWA_WIDTH + 2 * SWA_KV_WIDTH)
        qs_scr[rows, 0:SWA_WIDTH] = (qkv[:, 0:SWA_WIDTH] * Q_SCALE).astype(BF16)
        skt = qkv[:, SWA_WIDTH:SWA_WIDTH + LANES].T
        sv = qkv[:, SWA_WIDTH + LANES:]
        sv_rot = pltpu.roll(sv, HEAD_DIM, 1)
        new_rows = pl.ds(CHUNK + r0, nrows)
        for hd, (keep, vsrc) in enumerate([(lo, sv), (hi, sv_rot), (lo, sv_rot), (hi, sv)]):
            kv_head = hd // 2
            kt_head = skt[kv_head * HEAD_DIM:(kv_head + 1) * HEAD_DIM]
            ktvar_scr[hd, :, new_rows] = _place_head_rows(kt_head, hd % 2).astype(BF16)
            vvar_scr[hd, new_rows, :] = jnp.where(keep, vsrc, 0.0).astype(BF16)
        yield

        mqz = proj(rows, OFF_MQ, PROJ_COLS)
        qs_scr[rows, SWA_WIDTH:] = (mqz[:, 0:MEM_WIDTH] * Q_SCALE).astype(BF16)
        zg_scr[rows, 0:PROJ_COLS - MEM_WIDTH] = _silu(mqz[:, MEM_WIDTH:])
        yield

        done = PROJ_COLS - MEM_WIDTH
        while done < MIX_WIDTH:
            width = min(PROJ_COLS, MIX_WIDTH - done)
            zg_scr[rows, done:done + width] = _silu(proj(rows, OFF_Z + done, width))
            done += width
            yield

    def spatial_gating(c, g):
        rows = pl.ds(c * CHUNK, CHUNK)
        cols = pl.ds(g * LANES, LANES)
        sv_g = _dot(wsm_scr[g], vn_scr[rows, cols]) + bst_scr[g]
        y_scr[rows, cols] = gu_scr[rows, cols] * sv_g

    def gating_item(c, g):
        spatial_gating(c, g)
        return
        yield

    def attn_item(c, pair, sliding):
        r0 = c * CHUNK
        rows = pl.ds(r0, CHUNK)
        heads = (2 * pair, 2 * pair + 1)
        if sliding:
            band = pl.ds(r0, 2 * CHUNK)
            qp = qs_scr[rows, pl.ds(pair * LANES, LANES)]
            keys = jnp.concatenate([ktvar_scr[hd, :, band] for hd in heads], axis=1)
            bias_sel = first_tile if c == 0 else 0
            bias = jnp.concatenate([bias_scr[bias_sel, hd] for hd in heads], axis=1)
            logits = _dot(qp, keys) + bias
        else:
            qp = qs_scr[rows, pl.ds(SWA_WIDTH + pair * LANES, LANES)]
            keys = jnp.concatenate([mk_scr[hd] for hd in heads], axis=1)
            logits = _dot(qp, keys)
        n_keys = logits.shape[1] // 2
        yield
        probs, recips = [], []
        for i, hd in enumerate(heads):
            s = logits[:, i * n_keys:(i + 1) * n_keys]
            m = jnp.max(s, axis=-1, keepdims=True)
            if sliding:
                sink = sinks_ref[hd] * LOG2E
                m = jnp.maximum(m, sink)
            p = jnp.exp2(s - m)
            denom = jnp.sum(p, axis=-1, keepdims=True)
            if sliding:
                denom = denom + jnp.exp2(sink - m)
            probs.append(p.astype(BF16))
            recips.append(1.0 / denom)
        if sliding:
            values = jnp.concatenate([vvar_scr[hd, band, :] for hd in heads], axis=0)
        else:
            values = jnp.concatenate([mv_scr[hd] for hd in heads], axis=0)
        o_pair = _dot(jnp.concatenate(probs, axis=1), values)
        o_pair = o_pair * jnp.where(lo, recips[0], recips[1])
        off = A_WIDTH if sliding else A_WIDTH + SWA_WIDTH
        y_scr[rows, pl.ds(off + pair * LANES, LANES)] = o_pair

    def head_items(r0, nrows):
        for c in range(r0 // CHUNK, (r0 + nrows) // CHUNK):
            for pair in range(SWA_HEADS // 2):
                yield attn_item(c, pair, True)
            for pair in range(MEM_HEADS // 2):
                yield attn_item(c, pair, False)
            for g in range(A_GROUPS):
                yield gating_item(c, g)

    def project_out(r0, nrows):
        rows = pl.ds(r0, nrows)
        ybf_scr[rows, :] = (y_scr[rows, :] * zg_scr[rows, :]).astype(BF16)
        for c0 in range(0, D_MODEL, PROJ_COLS):
            o_scr[rows, c0:c0 + PROJ_COLS] = _dot(ybf_scr[rows, :], wout_ref[:, c0:c0 + PROJ_COLS])
            yield
        for rc in range(nrows // CHUNK):
            crows = pl.ds(r0 + rc * CHUNK, CHUNK)
            o = o_scr[crows, :]
            ms = jnp.mean(o * o, axis=-1, keepdims=True)
            out_ref[0, crows, :] = x_ref[0, crows, :] + o * lax.rsqrt(ms + EPS) * postg_ref[...]
            yield

    assert sum(SUB_TILES) == ts
    starts = [sum(SUB_TILES[:s]) for s in range(len(SUB_TILES))]
    subs = list(zip(starts, SUB_TILES))
    n_sub = len(subs)
    pre_norm(*subs[0])
    filler = project(*subs[0])
    for s in range(n_sub):
        if s + 1 < n_sub:
            pre_norm(*subs[s + 1])
        for _ in filler:
            pass
        heads = _skewed(head_items(*subs[s]), ITEM_LAG)
        n_filler = 0
        if s + 1 < n_sub:
            filler = project(*subs[s + 1])
            n_filler += PROJECT_COLUMN_PIECES
        else:
            filler = iter(())
        if s >= 1:
            filler = _chain(project_out(*subs[s - 1]), filler)
            n_filler += PROJECT_OUT_DOTS + subs[s - 1][1] // CHUNK
        _interleave(heads, filler, n_filler / ((subs[s][1] // CHUNK) * HEAD_STAGES_PER_CHUNK))
        filler = iter(())
    for _ in project_out(*subs[-1]):
        pass


def _layer(x, mem, w_in_bf16, w_kv, w_out, pre_g, mem_g, post_g, vng, vnb, w_spatial,
           b_spatial, buckets, sinks, rel_bias):
    batch, seq, _ = x.shape
    ts = SEQ_TILE
    n_tiles = seq // ts
    const2 = lambda b, t, *_: (0, 0)
    const3 = lambda b, t, *_: (0, 0, 0)
    grid_spec = pltpu.PrefetchScalarGridSpec(
        num_scalar_prefetch=1,
        grid=(batch, n_tiles),
        in_specs=[
            pl.BlockSpec((1, ts, D_MODEL), lambda b, t, *_: (b, t, 0)),
            pl.BlockSpec((1, MEM_LEN, D_MODEL), lambda b, t, *_: (b, 0, 0)),
            pl.BlockSpec((D_MODEL, IN_WIDTH), const2),
            pl.BlockSpec((D_MODEL, 2 * MEM_WIDTH), const2),
            pl.BlockSpec((MIX_WIDTH, D_MODEL), const2),
            pl.BlockSpec((1, D_MODEL), const2),
            pl.BlockSpec((1, D_MODEL), const2),
            pl.BlockSpec((1, D_MODEL), const2),
            pl.BlockSpec((1, A_WIDTH), const2),
            pl.BlockSpec((1, A_WIDTH), const2),
            pl.BlockSpec((A_GROUPS, CHUNK, CHUNK), const3),
            pl.BlockSpec((A_GROUPS, CHUNK), const2),
            pl.BlockSpec((N_BUCKETS, SWA_HEADS), const2),
            pl.BlockSpec((CHUNK, 2 * CHUNK), const2),
        ],
        out_specs=pl.BlockSpec((1, ts, D_MODEL), lambda b, t, *_: (b, t, 0)),
        scratch_shapes=[
            pltpu.VMEM((ts, D_MODEL), BF16),
            pltpu.VMEM((ts, A_WIDTH), F32),
            pltpu.VMEM((ts, A_WIDTH), BF16),
            pltpu.VMEM((ts, SWA_WIDTH + MEM_WIDTH), BF16),
            pltpu.VMEM((SWA_HEADS, LANES, CHUNK + ts), BF16),
            pltpu.VMEM((SWA_HEADS, CHUNK + ts, LANES), BF16),
            pltpu.VMEM((ts, MIX_WIDTH), F32),
            pltpu.VMEM((MEM_HEADS, LANES, MEM_LEN), BF16),
            pltpu.VMEM((MEM_HEADS, MEM_LEN, LANES), BF16),
            pltpu.VMEM((2, SWA_HEADS, CHUNK, 2 * CHUNK), F32),
            pltpu.VMEM((A_GROUPS, CHUNK, CHUNK), BF16),
            pltpu.VMEM((A_GROUPS, CHUNK, LANES), F32),
            pltpu.VMEM((D_MODEL, 2 * MEM_WIDTH), BF16),
            pltpu.VMEM((MIX_WIDTH, D_MODEL), BF16),
            pltpu.VMEM((ts, MIX_WIDTH), F32),
            pltpu.VMEM((ts, MIX_WIDTH), BF16),
            pltpu.VMEM((ts, D_MODEL), F32),
        ],
    )
    return pl.pallas_call(
        _layer_kernel,
        grid_spec=grid_spec,
        out_shape=jax.ShapeDtypeStruct(x.shape, x.dtype),
        compiler_params=pltpu.CompilerParams(
            dimension_semantics=("arbitrary", "arbitrary"),
            vmem_limit_bytes=VMEM_LIMIT_BYTES),
        name="layer",
    )(sinks, x, mem, w_in_bf16, w_kv, w_out, pre_g, mem_g, post_g, vng, vnb,
      w_spatial, b_spatial, rel_bias, buckets)


def kernel(x, mem, pre_norm_g, post_norm_g, mem_norm_g, w_in, w_mem_kv, v_norm_g, v_norm_b,
           w_spatial, b_spatial, attn_sinks, rel_bias, w_out):
    depth = w_in.shape[0]
    buckets = jnp.asarray(_band_buckets())
    for layer in range(depth):
        x = _layer(x, mem, w_in[layer].astype(BF16), w_mem_kv[layer], w_out[layer],
                   pre_norm_g[layer][None, :], mem_norm_g[layer][None, :], post_norm_g[layer][None, :],
                   v_norm_g[layer][None, :], v_norm_b[layer][None, :],
                   w_spatial[layer], b_spatial[layer], buckets, attn_sinks[layer], rel_bias)
    return x
```

```python
import numpy as np
import jax
import jax.numpy as jnp
from jax import lax
from jax.experimental import pallas as pl
from jax.experimental.pallas import tpu as pltpu

D_MODEL = 1024
MEM_LEN = 256
HEAD_DIM = 64
CHUNK = 128
A_GROUPS = 4
A_WIDTH = 512
SWA_HEADS = 4
SWA_WIDTH = 256
SWA_KV_WIDTH = 128
MEM_HEADS = 4
MEM_WIDTH = 256
MIX_WIDTH = 1024
IN_WIDTH = 2816
N_BUCKETS = 32
MAX_DISTANCE = 128
WINDOW = 128
EPS = 1e-6
NEG = -1e30
LOG2E = float(np.log2(np.e))
Q_SCALE = HEAD_DIM ** -0.5 * LOG2E

OFF_U = 0
OFF_V = A_WIDTH
OFF_SQ = 2 * A_WIDTH
OFF_SK = OFF_SQ + SWA_WIDTH
OFF_SV = OFF_SK + SWA_KV_WIDTH
OFF_MQ = OFF_SV + SWA_KV_WIDTH
OFF_Z = OFF_MQ + MEM_WIDTH

LANES = 128
SEQ_TILE = 1024
SUB_TILES = (256, 256, 256, 256)
PROJ_COLS = 512
W_TILE = 256
ITEM_LAG = 2
PROJECT_COLUMN_PIECES = 6
PROJECT_OUT_DOTS = D_MODEL // PROJ_COLS
HEAD_STAGES_PER_CHUNK = 2 * (SWA_HEADS // 2 + MEM_HEADS // 2)
VMEM_LIMIT_BYTES = 56 * 1024 * 1024

BF16 = jnp.bfloat16
F32 = jnp.float32


def _t5_causal_buckets(dist):
    n = np.maximum(dist, 0)
    max_exact = N_BUCKETS // 2
    large = max_exact + (np.log(np.maximum(n, 1) / max_exact) / np.log(MAX_DISTANCE / max_exact)
                         * (N_BUCKETS - max_exact)).astype(np.int32)
    large = np.minimum(large, N_BUCKETS - 1)
    return np.where(n < max_exact, n, large).astype(np.int32)


def _band_buckets():
    qi = np.arange(CHUNK)[:, None]
    kj = np.arange(2 * CHUNK)[None, :]
    dist = qi + CHUNK - kj
    valid = (dist >= 0) & (dist < WINDOW)
    return np.where(valid, _t5_causal_buckets(dist), -1).astype(np.int32)


def _gelu_tanh(x):
    c = np.float32(np.sqrt(2.0 / np.pi))
    ck = np.float32(np.sqrt(2.0 / np.pi) * 0.044715)
    hx = 0.5 * x
    return hx + hx * jnp.tanh(x * (c + ck * (x * x)))


def _silu(z):
    hz = 0.5 * z
    return hz + hz * jnp.tanh(hz)


def _dot(a, b):
    return jnp.dot(a, b, preferred_element_type=F32)


_DONE = object()


def _skewed(items, lag):
    waiting = []
    for item in items:
        if next(item, _DONE) is not _DONE:
            waiting.append(item)
        yield
        if len(waiting) > lag:
            next(waiting.pop(0), _DONE)
            yield
    for item in waiting:
        next(item, _DONE)
        yield


def _chain(*gens):
    for g in gens:
        yield from g


def _interleave(a, b, b_per_a):
    credit = 0.0
    a_live = b_live = True
    while a_live or b_live:
        if a_live:
            a_live = next(a, _DONE) is not _DONE
        credit += b_per_a
        while b_live and (credit >= 1.0 or not a_live):
            b_live = next(b, _DONE) is not _DONE
            credit -= 1.0


def _weight_columns(w_ref, c0, width):
    tiles = [w_ref[j] for j in range(c0 // W_TILE, (c0 + width) // W_TILE)]
    return tiles[0] if len(tiles) == 1 else jnp.concatenate(tiles, axis=1)


def _lane_half_mask(parity):
    lane = lax.broadcasted_iota(jnp.int32, (1, LANES), 1)
    return (lane >= HEAD_DIM) if parity else (lane < HEAD_DIM)


def _place_head_rows(kt_head, parity):
    zeros = jnp.zeros_like(kt_head)
    return jnp.concatenate([zeros, kt_head] if parity else [kt_head, zeros], axis=0)


def _layer_kernel(sinks_ref,
                  x_ref, mem_ref, win_ref, wkv_f32_ref, wout_f32_ref, preg_ref, memg_ref, postg_ref,
                  vng_ref, vnb_ref, ws_ref, bs_ref, relb_ref, buckets_ref,
                  out_ref,
                  h_scr, gu_scr, vn_scr, qs_scr, ktvar_scr, vvar_scr, zg_scr,
                  mk_scr, mv_scr, bias_scr, wsm_scr, bst_scr, wkv_ref, wout_ref,
                  y_scr, ybf_scr, o_scr):
    ts = x_ref.shape[1]
    b = pl.program_id(0)
    t = pl.program_id(1)

    @pl.when((b == 0) & (t == 0))
    def _init():
        row = lax.broadcasted_iota(jnp.int32, (CHUNK, CHUNK), 0)
        col = lax.broadcasted_iota(jnp.int32, (CHUNK, CHUNK), 1)
        for g in range(A_GROUPS):
            wsm_scr[g] = jnp.where(row >= col, ws_ref[g], 0.0).astype(BF16)
            bs_col = jnp.sum(jnp.where(row == col, bs_ref[g:g + 1, :], 0.0), axis=1, keepdims=True)
            bst_scr[g] = jnp.broadcast_to(bs_col, (CHUNK, LANES))
        for r0 in range(0, D_MODEL, 2 * CHUNK):
            wrows = pl.ds(r0, 2 * CHUNK)
            wkv_ref[wrows, :] = wkv_f32_ref[wrows, :].astype(BF16)
            for j in range(D_MODEL // W_TILE):
                wout_ref[j, wrows, :] = wout_f32_ref[wrows, j * W_TILE:(j + 1) * W_TILE].astype(BF16)
        buckets = buckets_ref[...]
        prev_key = lax.broadcasted_iota(jnp.int32, (CHUNK, 2 * CHUNK), 1) < CHUNK
        for hd in range(SWA_HEADS):
            tbl = jnp.full((CHUNK, 2 * CHUNK), NEG, F32)
            for bk in range(N_BUCKETS):
                tbl = jnp.where(buckets == bk, relb_ref[bk:bk + 1, hd:hd + 1] * LOG2E, tbl)
            bias_scr[0, hd] = tbl
            bias_scr[1, hd] = jnp.where(prev_key, NEG, tbl)

    @pl.when(t == 0)
    def _sequence_start():
        ktvar_scr[:, :, 0:CHUNK] = jnp.zeros((SWA_HEADS, LANES, CHUNK), BF16)
        vvar_scr[:, 0:CHUNK, :] = jnp.zeros((SWA_HEADS, CHUNK, LANES), BF16)
        m = mem_ref[0]
        ms = jnp.mean(m * m, axis=-1, keepdims=True)
        hm = (m * lax.rsqrt(ms + EPS) * memg_ref[...]).astype(BF16)
        kv = _dot(hm, wkv_ref[...])
        for pair in range(MEM_HEADS // 2):
            kt_pair = kv[:, pair * LANES:(pair + 1) * LANES].T
            v_pair = kv[:, MEM_WIDTH + pair * LANES:MEM_WIDTH + (pair + 1) * LANES]
            for parity in range(2):
                hd = 2 * pair + parity
                kt_head = kt_pair[parity * HEAD_DIM:(parity + 1) * HEAD_DIM]
                mk_scr[hd] = _place_head_rows(kt_head, parity).astype(BF16)
                mv_scr[hd] = jnp.where(_lane_half_mask(parity), v_pair, 0.0).astype(BF16)


    @pl.when(t > 0)
    def _copy_carry():
        ktvar_scr[:, :, 0:CHUNK] = ktvar_scr[:, :, ts:ts + CHUNK]
        vvar_scr[:, 0:CHUNK, :] = vvar_scr[:, ts:ts + CHUNK, :]

    lo = _lane_half_mask(0)
    hi = _lane_half_mask(1)
    first_tile = jnp.where(t == 0, 1, 0)

    def pre_norm(r0, nrows):
        for rc in range(nrows // CHUNK):
            crows = pl.ds(r0 + rc * CHUNK, CHUNK)
            x = x_ref[0, crows, :]
            ms = jnp.mean(x * x, axis=-1, keepdims=True)
            h_scr[crows, :] = (x * lax.rsqrt(ms + EPS) * preg_ref[...]).astype(BF16)

    def proj(rows, c0, width):
        return _dot(h_scr[rows, :], _weight_columns(win_ref, c0, width))

    def project(r0, nrows):
        rows = pl.ds(r0, nrows)

        v_all = _gelu_tanh(proj(rows, OFF_V, A_WIDTH))
        for g in range(A_GROUPS):
            cols = pl.ds(g * LANES, LANES)
            v = v_all[:, g * LANES:(g + 1) * LANES]
            mu = jnp.mean(v, axis=-1, keepdims=True)
            d = v - mu
            var = jnp.mean(d * d, axis=-1, keepdims=True)
            vn = d * lax.rsqrt(var + EPS) * vng_ref[:, cols] + vnb_ref[:, cols]
            vn_scr[rows, cols] = vn.astype(BF16)
        yield

        gu_scr[rows, :] = _gelu_tanh(proj(rows, OFF_U, A_WIDTH))
        yield

        qkv = proj(rows, OFF_SQ, SWA_WIDTH + 2 * SWA_KV_WIDTH)
        qs_scr[rows, 0:SWA_WIDTH] = (qkv[:, 0:SWA_WIDTH] * Q_SCALE).astype(BF16)
        skt = qkv[:, SWA_WIDTH:SWA_WIDTH + LANES].T
        sv = qkv[:, SWA_WIDTH + LANES:]
        sv_rot = pltpu.roll(sv, HEAD_DIM, 1)
        new_rows = pl.ds(CHUNK + r0, nrows)
        for hd, (keep, vsrc) in enumerate([(lo, sv), (hi, sv_rot), (lo, sv_rot), (hi, sv)]):
            kv_head = hd // 2
            kt_head = skt[kv_head * HEAD_DIM:(kv_head + 1) * HEAD_DIM]
            ktvar_scr[hd, :, new_rows] = _place_head_rows(kt_head, hd % 2).astype(BF16)
            vvar_scr[hd, new_rows, :] = jnp.where(keep, vsrc, 0.0).astype(BF16)
        yield

        mqz = proj(rows, OFF_MQ, PROJ_COLS)
        qs_scr[rows, SWA_WIDTH:] = (mqz[:, 0:MEM_WIDTH] * Q_SCALE).astype(BF16)
        zg_scr[rows, 0:PROJ_COLS - MEM_WIDTH] = _silu(mqz[:, MEM_WIDTH:])
        yield

        for c in range(r0 // CHUNK, (r0 + nrows) // CHUNK):
            for g in range(A_GROUPS):
                spatial_gating(c, g)
            yield

        done = PROJ_COLS - MEM_WIDTH
        while done < MIX_WIDTH:
            width = min(PROJ_COLS, MIX_WIDTH - done)
            zg_scr[rows, done:done + width] = _silu(proj(rows, OFF_Z + done, width))
            done += width
            yield

    def spatial_gating(c, g):
        rows = pl.ds(c * CHUNK, CHUNK)
        cols = pl.ds(g * LANES, LANES)
        sv_g = _dot(wsm_scr[g], vn_scr[rows, cols]) + bst_scr[g]
        y_scr[rows, cols] = gu_scr[rows, cols] * sv_g

    def attn_item(c, pair, sliding):
        r0 = c * CHUNK
        rows = pl.ds(r0, CHUNK)
        heads = (2 * pair, 2 * pair + 1)
        if sliding:
            band = pl.ds(r0, 2 * CHUNK)
            qp = qs_scr[rows, pl.ds(pair * LANES, LANES)]
            keys = jnp.concatenate([ktvar_scr[hd, :, band] for hd in heads], axis=1)
            bias_sel = first_tile if c == 0 else 0
            bias = jnp.concatenate([bias_scr[bias_sel, hd] for hd in heads], axis=1)
            logits = _dot(qp, keys) + bias
        else:
            qp = qs_scr[rows, pl.ds(SWA_WIDTH + pair * LANES, LANES)]
            keys = jnp.concatenate([mk_scr[hd] for hd in heads], axis=1)
            logits = _dot(qp, keys)
        n_keys = logits.shape[1] // 2
        yield
        probs, recips = [], []
        for i, hd in enumerate(heads):
            s = logits[:, i * n_keys:(i + 1) * n_keys]
            m = jnp.max(s, axis=-1, keepdims=True)
            if sliding:
                sink = sinks_ref[hd] * LOG2E
                m = jnp.maximum(m, sink)
            p = jnp.exp2(s - m)
            denom = jnp.sum(p, axis=-1, keepdims=True)
            if sliding:
                denom = denom + jnp.exp2(sink - m)
            probs.append(p.astype(BF16))
            recips.append(1.0 / denom)
        if sliding:
            values = jnp.concatenate([vvar_scr[hd, band, :] for hd in heads], axis=0)
        else:
            values = jnp.concatenate([mv_scr[hd] for hd in heads], axis=0)
        o_pair = _dot(jnp.concatenate(probs, axis=1), values)
        o_pair = o_pair * jnp.where(lo, recips[0], recips[1])
        off = A_WIDTH if sliding else A_WIDTH + SWA_WIDTH
        y_scr[rows, pl.ds(off + pair * LANES, LANES)] = o_pair

    def head_items(r0, nrows):
        for c in range(r0 // CHUNK, (r0 + nrows) // CHUNK):
            for pair in range(SWA_HEADS // 2):
                yield attn_item(c, pair, True)
            for pair in range(MEM_HEADS // 2):
                yield attn_item(c, pair, False)

    def project_out(r0, nrows):
        rows = pl.ds(r0, nrows)
        ybf_scr[rows, :] = (y_scr[rows, :] * zg_scr[rows, :]).astype(BF16)
        for c0 in range(0, D_MODEL, PROJ_COLS):
            o_scr[rows, c0:c0 + PROJ_COLS] = _dot(ybf_scr[rows, :], _weight_columns(wout_ref, c0, PROJ_COLS))
            yield
        for rc in range(nrows // CHUNK):
            crows = pl.ds(r0 + rc * CHUNK, CHUNK)
            o = o_scr[crows, :]
            ms = jnp.mean(o * o, axis=-1, keepdims=True)
            out_ref[0, crows, :] = x_ref[0, crows, :] + o * lax.rsqrt(ms + EPS) * postg_ref[...]
            yield

    assert sum(SUB_TILES) == ts
    starts = [sum(SUB_TILES[:s]) for s in range(len(SUB_TILES))]
    subs = list(zip(starts, SUB_TILES))
    n_sub = len(subs)
    pre_norm(*subs[0])
    filler = project(*subs[0])
    for s in range(n_sub):
        if s + 1 < n_sub:
            pre_norm(*subs[s + 1])
        for _ in filler:
            pass
        heads = _skewed(head_items(*subs[s]), ITEM_LAG)
        n_filler = 0
        if s + 1 < n_sub:
            filler = project(*subs[s + 1])
            n_filler += PROJECT_COLUMN_PIECES + subs[s + 1][1] // CHUNK
        else:
            filler = iter(())
        if s >= 1:
            filler = _chain(project_out(*subs[s - 1]), filler)
            n_filler += PROJECT_OUT_DOTS + subs[s - 1][1] // CHUNK
        _interleave(heads, filler, n_filler / ((subs[s][1] // CHUNK) * HEAD_STAGES_PER_CHUNK))
        filler = iter(())
    for _ in project_out(*subs[-1]):
        pass


def _layer(x, mem, w_in_bf16, w_kv, w_out, pre_g, mem_g, post_g, vng, vnb, w_spatial,
           b_spatial, buckets, sinks, rel_bias):
    batch, seq, _ = x.shape
    ts = SEQ_TILE
    n_tiles = seq // ts
    const2 = lambda b, t, *_: (0, 0)
    const3 = lambda b, t, *_: (0, 0, 0)
    grid_spec = pltpu.PrefetchScalarGridSpec(
        num_scalar_prefetch=1,
        grid=(batch, n_tiles),
        in_specs=[
            pl.BlockSpec((1, ts, D_MODEL), lambda b, t, *_: (b, t, 0)),
            pl.BlockSpec((1, MEM_LEN, D_MODEL), lambda b, t, *_: (b, 0, 0)),
            pl.BlockSpec((IN_WIDTH // W_TILE, D_MODEL, W_TILE), const3),
            pl.BlockSpec((D_MODEL, 2 * MEM_WIDTH), const2),
            pl.BlockSpec((MIX_WIDTH, D_MODEL), const2),
            pl.BlockSpec((1, D_MODEL), const2),
            pl.BlockSpec((1, D_MODEL), const2),
            pl.BlockSpec((1, D_MODEL), const2),
            pl.BlockSpec((1, A_WIDTH), const2),
            pl.BlockSpec((1, A_WIDTH), const2),
            pl.BlockSpec((A_GROUPS, CHUNK, CHUNK), const3),
            pl.BlockSpec((A_GROUPS, CHUNK), const2),
            pl.BlockSpec((N_BUCKETS, SWA_HEADS), const2),
            pl.BlockSpec((CHUNK, 2 * CHUNK), const2),
        ],
        out_specs=pl.BlockSpec((1, ts, D_MODEL), lambda b, t, *_: (b, t, 0)),
        scratch_shapes=[
            pltpu.VMEM((ts, D_MODEL), BF16),
            pltpu.VMEM((ts, A_WIDTH), F32),
            pltpu.VMEM((ts, A_WIDTH), BF16),
            pltpu.VMEM((ts, SWA_WIDTH + MEM_WIDTH), BF16),
            pltpu.VMEM((SWA_HEADS, LANES, CHUNK + ts), BF16),
            pltpu.VMEM((SWA_HEADS, CHUNK + ts, LANES), BF16),
            pltpu.VMEM((ts, MIX_WIDTH), F32),
            pltpu.VMEM((MEM_HEADS, LANES, MEM_LEN), BF16),
            pltpu.VMEM((MEM_HEADS, MEM_LEN, LANES), BF16),
            pltpu.VMEM((2, SWA_HEADS, CHUNK, 2 * CHUNK), F32),
            pltpu.VMEM((A_GROUPS, CHUNK, CHUNK), BF16),
            pltpu.VMEM((A_GROUPS, CHUNK, LANES), F32),
            pltpu.VMEM((D_MODEL, 2 * MEM_WIDTH), BF16),
            pltpu.VMEM((D_MODEL // W_TILE, MIX_WIDTH, W_TILE), BF16),
            pltpu.VMEM((ts, MIX_WIDTH), F32),
            pltpu.VMEM((ts, MIX_WIDTH), BF16),
            pltpu.VMEM((ts, D_MODEL), F32),
        ],
    )
    return pl.pallas_call(
        _layer_kernel,
        grid_spec=grid_spec,
        out_shape=jax.ShapeDtypeStruct(x.shape, x.dtype),
        compiler_params=pltpu.CompilerParams(
            dimension_semantics=("arbitrary", "arbitrary"),
            vmem_limit_bytes=VMEM_LIMIT_BYTES),
        name="layer",
    )(sinks, x, mem, w_in_bf16, w_kv, w_out, pre_g, mem_g, post_g, vng, vnb,
      w_spatial, b_spatial, rel_bias, buckets)


def kernel(x, mem, pre_norm_g, post_norm_g, mem_norm_g, w_in, w_mem_kv, v_norm_g, v_norm_b,
           w_spatial, b_spatial, attn_sinks, rel_bias, w_out):
    depth = w_in.shape[0]
    buckets = jnp.asarray(_band_buckets())
    for layer in range(depth):
        w_in_tiles = jnp.transpose(
            w_in[layer].astype(BF16).reshape(D_MODEL, IN_WIDTH // W_TILE, W_TILE), (1, 0, 2))
        x = _layer(x, mem, w_in_tiles, w_mem_kv[layer], w_out[layer],
                   pre_norm_g[layer][None, :], mem_norm_g[layer][None, :], post_norm_g[layer][None, :],
                   v_norm_g[layer][None, :], v_norm_b[layer][None, :],
                   w_spatial[layer], b_spatial[layer], buckets, attn_sinks[layer], rel_bias)
    return x
```

```python
import numpy as np
import jax
import jax.numpy as jnp
from jax import lax
from jax.experimental import pallas as pl
from jax.experimental.pallas import tpu as pltpu

D_MODEL = 1024
MEM_LEN = 256
HEAD_DIM = 64
CHUNK = 128
A_GROUPS = 4
A_WIDTH = 512
SWA_HEADS = 4
SWA_WIDTH = 256
SWA_KV_WIDTH = 128
MEM_HEADS = 4
MEM_WIDTH = 256
MIX_WIDTH = 1024
IN_WIDTH = 2816
N_BUCKETS = 32
MAX_DISTANCE = 128
WINDOW = 128
EPS = 1e-6
NEG = -1e30
LOG2E = float(np.log2(np.e))
Q_SCALE = HEAD_DIM ** -0.5 * LOG2E

OFF_U = 0
OFF_V = A_WIDTH
OFF_SQ = 2 * A_WIDTH
OFF_SK = OFF_SQ + SWA_WIDTH
OFF_SV = OFF_SK + SWA_KV_WIDTH
OFF_MQ = OFF_SV + SWA_KV_WIDTH
OFF_Z = OFF_MQ + MEM_WIDTH

LANES = 128
SEQ_TILE = 1024
SUB_TILES = (256, 256, 256, 256)
PROJ_COLS = 512
ITEM_LAG = 2
PROJECT_COLUMN_PIECES = 6
PROJECT_OUT_DOTS = D_MODEL // PROJ_COLS
HEAD_STAGES_PER_CHUNK = 2 * (SWA_HEADS // 2 + MEM_HEADS // 2)
VMEM_LIMIT_BYTES = 56 * 1024 * 1024

BF16 = jnp.bfloat16
F32 = jnp.float32


def _t5_causal_buckets(dist):
    n = np.maximum(dist, 0)
    max_exact = N_BUCKETS // 2
    large = max_exact + (np.log(np.maximum(n, 1) / max_exact) / np.log(MAX_DISTANCE / max_exact)
                         * (N_BUCKETS - max_exact)).astype(np.int32)
    large = np.minimum(large, N_BUCKETS - 1)
    return np.where(n < max_exact, n, large).astype(np.int32)


def _band_buckets():
    qi = np.arange(CHUNK)[:, None]
    kj = np.arange(2 * CHUNK)[None, :]
    dist = qi + CHUNK - kj
    valid = (dist >= 0) & (dist < WINDOW)
    return np.where(valid, _t5_causal_buckets(dist), -1).astype(np.int32)


def _gelu_tanh(x):
    c = np.float32(np.sqrt(2.0 / np.pi))
    ck = np.float32(np.sqrt(2.0 / np.pi) * 0.044715)
    hx = 0.5 * x
    return hx + hx * jnp.tanh(x * (c + ck * (x * x)))


def _silu(z):
    hz = 0.5 * z
    return hz + hz * jnp.tanh(hz)


def _dot(a, b):
    return jnp.dot(a, b, preferred_element_type=F32)


_DONE = object()


def _skewed(items, lag):
    waiting = []
    for item in items:
        if next(item, _DONE) is not _DONE:
            waiting.append(item)
        yield
        if len(waiting) > lag:
            next(waiting.pop(0), _DONE)
            yield
    for item in waiting:
        next(item, _DONE)
        yield


def _chain(*gens):
    for g in gens:
        yield from g


def _interleave(a, b, b_per_a):
    credit = 0.0
    a_live = b_live = True
    while a_live or b_live:
        if a_live:
            a_live = next(a, _DONE) is not _DONE
        credit += b_per_a
        while b_live and (credit >= 1.0 or not a_live):
            b_live = next(b, _DONE) is not _DONE
            credit -= 1.0


def _lane_half_mask(parity):
    lane = lax.broadcasted_iota(jnp.int32, (1, LANES), 1)
    return (lane >= HEAD_DIM) if parity else (lane < HEAD_DIM)


def _place_head_rows(kt_head, parity):
    zeros = jnp.zeros_like(kt_head)
    return jnp.concatenate([zeros, kt_head] if parity else [kt_head, zeros], axis=0)


def _layer_kernel(sinks_ref,
                  x_ref, mem_ref, win_ref, wkv_f32_ref, wout_f32_ref, preg_ref, memg_ref, postg_ref,
                  vng_ref, vnb_ref, ws_ref, bs_ref, relb_ref, buckets_ref,
                  out_ref,
                  h_scr, gu_scr, vn_scr, qs_scr, ktvar_scr, vvar_scr, zg_scr,
                  mk_scr, mv_scr, bias_scr, wsm_scr, bst_scr, wkv_ref, wout_ref,
                  y_scr, ybf_scr, o_scr):
    ts = x_ref.shape[1]
    b = pl.program_id(0)
    t = pl.program_id(1)

    @pl.when((b == 0) & (t == 0))
    def _init():
        row = lax.broadcasted_iota(jnp.int32, (CHUNK, CHUNK), 0)
        col = lax.broadcasted_iota(jnp.int32, (CHUNK, CHUNK), 1)
        for g in range(A_GROUPS):
            wsm_scr[g] = jnp.where(row >= col, ws_ref[g], 0.0).astype(BF16)
            bs_col = jnp.sum(jnp.where(row == col, bs_ref[g:g + 1, :], 0.0), axis=1, keepdims=True)
            bst_scr[g] = jnp.broadcast_to(bs_col, (CHUNK, LANES))
        for r0 in range(0, D_MODEL, 2 * CHUNK):
            wrows = pl.ds(r0, 2 * CHUNK)
            wkv_ref[wrows, :] = wkv_f32_ref[wrows, :].astype(BF16)
            wout_ref[wrows, :] = wout_f32_ref[wrows, :].astype(BF16)
        buckets = buckets_ref[...]
        prev_key = lax.broadcasted_iota(jnp.int32, (CHUNK, 2 * CHUNK), 1) < CHUNK
        for hd in range(SWA_HEADS):
            tbl = jnp.full((CHUNK, 2 * CHUNK), NEG, F32)
            for bk in range(N_BUCKETS):
                tbl = jnp.where(buckets == bk, relb_ref[bk:bk + 1, hd:hd + 1] * LOG2E, tbl)
            bias_scr[0, hd] = tbl
            bias_scr[1, hd] = jnp.where(prev_key, NEG, tbl)

    @pl.when(t == 0)
    def _sequence_start():
        ktvar_scr[:, :, 0:CHUNK] = jnp.zeros((SWA_HEADS, LANES, CHUNK), BF16)
        vvar_scr[:, 0:CHUNK, :] = jnp.zeros((SWA_HEADS, CHUNK, LANES), BF16)
        m = mem_ref[0]
        ms = jnp.mean(m * m, axis=-1, keepdims=True)
        hm = (m * lax.rsqrt(ms + EPS) * memg_ref[...]).astype(BF16)
        kv = _dot(hm, wkv_ref[...])
        for pair in range(MEM_HEADS // 2):
            kt_pair = kv[:, pair * LANES:(pair + 1) * LANES].T
            v_pair = kv[:, MEM_WIDTH + pair * LANES:MEM_WIDTH + (pair + 1) * LANES]
            for parity in range(2):
                hd = 2 * pair + parity
                kt_head = kt_pair[parity * HEAD_DIM:(parity + 1) * HEAD_DIM]
                mk_scr[hd] = _place_head_rows(kt_head, parity).astype(BF16)
                mv_scr[hd] = jnp.where(_lane_half_mask(parity), v_pair, 0.0).astype(BF16)

    @pl.when(t > 0)
    def _copy_carry():
        ktvar_scr[:, :, 0:CHUNK] = ktvar_scr[:, :, ts:ts + CHUNK]
        vvar_scr[:, 0:CHUNK, :] = vvar_scr[:, ts:ts + CHUNK, :]

    lo = _lane_half_mask(0)
    hi = _lane_half_mask(1)
    first_tile = jnp.where(t == 0, 1, 0)

    def pre_norm(r0, nrows):
        for rc in range(nrows // CHUNK):
            crows = pl.ds(r0 + rc * CHUNK, CHUNK)
            x = x_ref[0, crows, :]
            ms = jnp.mean(x * x, axis=-1, keepdims=True)
            h_scr[crows, :] = (x * lax.rsqrt(ms + EPS) * preg_ref[...]).astype(BF16)

    def proj(rows, c0, width):
        return _dot(h_scr[rows, :], win_ref[:, c0:c0 + width])

    def project(r0, nrows):
        rows = pl.ds(r0, nrows)

        v_all = _gelu_tanh(proj(rows, OFF_V, A_WIDTH))
        for g in range(A_GROUPS):
            cols = pl.ds(g * LANES, LANES)
            v = v_all[:, g * LANES:(g + 1) * LANES]
            mu = jnp.mean(v, axis=-1, keepdims=True)
            d = v - mu
            var = jnp.mean(d * d, axis=-1, keepdims=True)
            vn = d * lax.rsqrt(var + EPS) * vng_ref[:, cols] + vnb_ref[:, cols]
            vn_scr[rows, cols] = vn.astype(BF16)
        yield

        gu_scr[rows, :] = _gelu_tanh(proj(rows, OFF_U, A_WIDTH))
        yield

        qkv = proj(rows, OFF_SQ, SWA_WIDTH + 2 * SWA_KV_WIDTH)
        qs_scr[rows, 0:SWA_WIDTH] = (qkv[:, 0:SWA_WIDTH] * Q_SCALE).astype(BF16)
        skt = qkv[:, SWA_WIDTH:SWA_WIDTH + LANES].T
        sv = qkv[:, SWA_WIDTH + LANES:]
        sv_rot = pltpu.roll(sv, HEAD_DIM, 1)
        new_rows = pl.ds(CHUNK + r0, nrows)
        for hd, (keep, vsrc) in enumerate([(lo, sv), (hi, sv_rot), (lo, sv_rot), (hi, sv)]):
            kv_head = hd // 2
            kt_head = skt[kv_head * HEAD_DIM:(kv_head + 1) * HEAD_DIM]
            ktvar_scr[hd, :, new_rows] = _place_head_rows(kt_head, hd % 2).astype(BF16)
            vvar_scr[hd, new_rows, :] = jnp.where(keep, vsrc, 0.0).astype(BF16)
        yield

        mqz = proj(rows, OFF_MQ, PROJ_COLS)
        qs_scr[rows, SWA_WIDTH:] = (mqz[:, 0:MEM_WIDTH] * Q_SCALE).astype(BF16)
        zg_scr[rows, 0:PROJ_COLS - MEM_WIDTH] = _silu(mqz[:, MEM_WIDTH:])
        yield

        for c in range(r0 // CHUNK, (r0 + nrows) // CHUNK):
            for g in range(A_GROUPS):
                spatial_gating(c, g)
            yield

        done = PROJ_COLS - MEM_WIDTH
        while done < MIX_WIDTH:
            width = min(PROJ_COLS, MIX_WIDTH - done)
            zg_scr[rows, done:done + width] = _silu(proj(rows, OFF_Z + done, width))
            done += width
            yield

    def spatial_gating(c, g):
        rows = pl.ds(c * CHUNK, CHUNK)
        cols = pl.ds(g * LANES, LANES)
        sv_g = _dot(wsm_scr[g], vn_scr[rows, cols]) + bst_scr[g]
        y_scr[rows, cols] = gu_scr[rows, cols] * sv_g

    def attn_item(c, pair, sliding):
        r0 = c * CHUNK
        rows = pl.ds(r0, CHUNK)
        heads = (2 * pair, 2 * pair + 1)
        if sliding:
            band = pl.ds(r0, 2 * CHUNK)
            qp = qs_scr[rows, pl.ds(pair * LANES, LANES)]
            keys = jnp.concatenate([ktvar_scr[hd, :, band] for hd in heads], axis=1)
            bias_sel = first_tile if c == 0 else 0
            bias = jnp.concatenate([bias_scr[bias_sel, hd] for hd in heads], axis=1)
            logits = _dot(qp, keys) + bias
        else:
            qp = qs_scr[rows, pl.ds(SWA_WIDTH + pair * LANES, LANES)]
            keys = jnp.concatenate([mk_scr[hd] for hd in heads], axis=1)
            logits = _dot(qp, keys)
        n_keys = logits.shape[1] // 2
        yield
        probs, recips = [], []
        for i, hd in enumerate(heads):
            s = logits[:, i * n_keys:(i + 1) * n_keys]
            m = jnp.max(s, axis=-1, keepdims=True)
            if sliding:
                sink = sinks_ref[hd] * LOG2E
                m = jnp.maximum(m, sink)
            p = jnp.exp2(s - m)
            denom = jnp.sum(p, axis=-1, keepdims=True)
            if sliding:
                denom = denom + jnp.exp2(sink - m)
            probs.append(p.astype(BF16))
            recips.append(1.0 / denom)
        if sliding:
            values = jnp.concatenate([vvar_scr[hd, band, :] for hd in heads], axis=0)
        else:
            values = jnp.concatenate([mv_scr[hd] for hd in heads], axis=0)
        o_pair = _dot(jnp.concatenate(probs, axis=1), values)
        o_pair = o_pair * jnp.where(lo, recips[0], recips[1])
        off = A_WIDTH if sliding else A_WIDTH + SWA_WIDTH
        y_scr[rows, pl.ds(off + pair * LANES, LANES)] = o_pair

    def head_items(r0, nrows):
        for c in range(r0 // CHUNK, (r0 + nrows) // CHUNK):
            for pair in range(SWA_HEADS // 2):
                yield attn_item(c, pair, True)
            for pair in range(MEM_HEADS // 2):
                yield attn_item(c, pair, False)

    def project_out(r0, nrows):
        rows = pl.ds(r0, nrows)
        ybf_scr[rows, :] = (y_scr[rows, :] * zg_scr[rows, :]).astype(BF16)
        for c0 in range(0, D_MODEL, PROJ_COLS):
            o_scr[rows, c0:c0 + PROJ_COLS] = _dot(ybf_scr[rows, :], wout_ref[:, c0:c0 + PROJ_COLS])
            yield
        for rc in range(nrows // CHUNK):
            crows = pl.ds(r0 + rc * CHUNK, CHUNK)
            o = o_scr[crows, :]
            ms = jnp.mean(o * o, axis=-1, keepdims=True)
            out_ref[0, crows, :] = x_ref[0, crows, :] + o * lax.rsqrt(ms + EPS) * postg_ref[...]
            yield

    assert sum(SUB_TILES) == ts
    starts = [sum(SUB_TILES[:s]) for s in range(len(SUB_TILES))]
    subs = list(zip(starts, SUB_TILES))
    n_sub = len(subs)
    pre_norm(*subs[0])
    filler = project(*subs[0])
    for s in range(n_sub):
        if s + 1 < n_sub:
            pre_norm(*subs[s + 1])
        for _ in filler:
            pass
        heads = _skewed(head_items(*subs[s]), ITEM_LAG)
        n_filler = 0
        if s + 1 < n_sub:
            filler = project(*subs[s + 1])
            n_filler += PROJECT_COLUMN_PIECES + subs[s + 1][1] // CHUNK
        else:
            filler = iter(())
        if s >= 1:
            filler = _chain(project_out(*subs[s - 1]), filler)
            n_filler += PROJECT_OUT_DOTS + subs[s - 1][1] // CHUNK
        _interleave(heads, filler, n_filler / ((subs[s][1] // CHUNK) * HEAD_STAGES_PER_CHUNK))
        filler = iter(())
    for _ in project_out(*subs[-1]):
        pass


def _layer(x, mem, w_in_bf16, w_kv, w_out, pre_g, mem_g, post_g, vng, vnb, w_spatial,
           b_spatial, buckets, sinks, rel_bias):
    batch, seq, _ = x.shape
    ts = SEQ_TILE
    n_tiles = seq // ts
    const2 = lambda b, t, *_: (0, 0)
    const3 = lambda b, t, *_: (0, 0, 0)
    grid_spec = pltpu.PrefetchScalarGridSpec(
        num_scalar_prefetch=1,
        grid=(batch, n_tiles),
        in_specs=[
            pl.BlockSpec((1, ts, D_MODEL), lambda b, t, *_: (b, t, 0)),
            pl.BlockSpec((1, MEM_LEN, D_MODEL), lambda b, t, *_: (b, 0, 0)),
            pl.BlockSpec((D_MODEL, IN_WIDTH), const2),
            pl.BlockSpec((D_MODEL, 2 * MEM_WIDTH), const2),
            pl.BlockSpec((MIX_WIDTH, D_MODEL), const2),
            pl.BlockSpec((1, D_MODEL), const2),
            pl.BlockSpec((1, D_MODEL), const2),
            pl.BlockSpec((1, D_MODEL), const2),
            pl.BlockSpec((1, A_WIDTH), const2),
            pl.BlockSpec((1, A_WIDTH), const2),
            pl.BlockSpec((A_GROUPS, CHUNK, CHUNK), const3),
            pl.BlockSpec((A_GROUPS, CHUNK), const2),
            pl.BlockSpec((N_BUCKETS, SWA_HEADS), const2),
            pl.BlockSpec((CHUNK, 2 * CHUNK), const2),
        ],
        out_specs=pl.BlockSpec((1, ts, D_MODEL), lambda b, t, *_: (b, t, 0)),
        scratch_shapes=[
            pltpu.VMEM((ts, D_MODEL), BF16),
            pltpu.VMEM((ts, A_WIDTH), F32),
            pltpu.VMEM((ts, A_WIDTH), BF16),
            pltpu.VMEM((ts, SWA_WIDTH + MEM_WIDTH), BF16),
            pltpu.VMEM((SWA_HEADS, LANES, CHUNK + ts), BF16),
            pltpu.VMEM((SWA_HEADS, CHUNK + ts, LANES), BF16),
            pltpu.VMEM((ts, MIX_WIDTH), F32),
            pltpu.VMEM((MEM_HEADS, LANES, MEM_LEN), BF16),
            pltpu.VMEM((MEM_HEADS, MEM_LEN, LANES), BF16),
            pltpu.VMEM((2, SWA_HEADS, CHUNK, 2 * CHUNK), F32),
            pltpu.VMEM((A_GROUPS, CHUNK, CHUNK), BF16),
            pltpu.VMEM((A_GROUPS, CHUNK, LANES), F32),
            pltpu.VMEM((D_MODEL, 2 * MEM_WIDTH), BF16),
            pltpu.VMEM((MIX_WIDTH, D_MODEL), BF16),
            pltpu.VMEM((ts, MIX_WIDTH), F32),
            pltpu.VMEM((ts, MIX_WIDTH), BF16),
            pltpu.VMEM((ts, D_MODEL), F32),
        ],
    )
    return pl.pallas_call(
        _layer_kernel,
        grid_spec=grid_spec,
        out_shape=jax.ShapeDtypeStruct(x.shape, x.dtype),
        compiler_params=pltpu.CompilerParams(
            dimension_semantics=("arbitrary", "arbitrary"),
            vmem_limit_bytes=VMEM_LIMIT_BYTES),
        name="layer",
    )(sinks, x, mem, w_in_bf16, w_kv, w_out, pre_g, mem_g, post_g, vng, vnb,
      w_spatial, b_spatial, rel_bias, buckets)


def kernel(x, mem, pre_norm_g, post_norm_g, mem_norm_g, w_in, w_mem_kv, v_norm_g, v_norm_b,
           w_spatial, b_spatial, attn_sinks, rel_bias, w_out):
    depth = w_in.shape[0]
    buckets = jnp.asarray(_band_buckets())
    for layer in range(depth):
        x = _layer(x, mem, w_in[layer].astype(BF16), w_mem_kv[layer], w_out[layer],
                   pre_norm_g[layer][None, :], mem_norm_g[layer][None, :], post_norm_g[layer][None, :],
                   v_norm_g[layer][None, :], v_norm_b[layer][None, :],
                   w_spatial[layer], b_spatial[layer], buckets, attn_sinks[layer], rel_bias)
    return x
```

```python
import numpy as np
import jax
import jax.numpy as jnp
from jax import lax
from jax.experimental import pallas as pl
from jax.experimental.pallas import tpu as pltpu

D_MODEL = 1024
MEM_LEN = 256
HEAD_DIM = 64
CHUNK = 128
A_GROUPS = 4
A_WIDTH = 512
SWA_HEADS = 4
SWA_WIDTH = 256
SWA_KV_WIDTH = 128
MEM_HEADS = 4
MEM_WIDTH = 256
MIX_WIDTH = 1024
IN_WIDTH = 2816
N_BUCKETS = 32
MAX_DISTANCE = 128
WINDOW = 128
EPS = 1e-6
NEG = -1e30
LOG2E = float(np.log2(np.e))
Q_SCALE = HEAD_DIM ** -0.5 * LOG2E

OFF_U = 0
OFF_V = A_WIDTH
OFF_SQ = 2 * A_WIDTH
OFF_SK = OFF_SQ + SWA_WIDTH
OFF_SV = OFF_SK + SWA_KV_WIDTH
OFF_MQ = OFF_SV + SWA_KV_WIDTH
OFF_Z = OFF_MQ + MEM_WIDTH

LANES = 128
SEQ_TILE = 1024
SUB_TILES = (256, 256, 256, 256)
PROJ_COLS = 512
ITEM_LAG = 2
PROJECT_COLUMN_PIECES = 6
PROJECT_OUT_DOTS = D_MODEL // PROJ_COLS
HEAD_STAGES_PER_CHUNK = 2 * (SWA_HEADS // 2 + MEM_HEADS // 2)
VMEM_LIMIT_BYTES = 56 * 1024 * 1024

BF16 = jnp.bfloat16
F32 = jnp.float32


def _t5_causal_buckets(dist):
    n = np.maximum(dist, 0)
    max_exact = N_BUCKETS // 2
    large = max_exact + (np.log(np.maximum(n, 1) / max_exact) / np.log(MAX_DISTANCE / max_exact)
                         * (N_BUCKETS - max_exact)).astype(np.int32)
    large = np.minimum(large, N_BUCKETS - 1)
    return np.where(n < max_exact, n, large).astype(np.int32)


def _band_buckets():
    qi = np.arange(CHUNK)[:, None]
    kj = np.arange(2 * CHUNK)[None, :]
    dist = qi + CHUNK - kj
    valid = (dist >= 0) & (dist < WINDOW)
    return np.where(valid, _t5_causal_buckets(dist), -1).astype(np.int32)


def _gelu_tanh(x):
    c = np.float32(np.sqrt(2.0 / np.pi))
    ck = np.float32(np.sqrt(2.0 / np.pi) * 0.044715)
    hx = 0.5 * x
    return hx + hx * jnp.tanh(x * (c + ck * (x * x)))


def _silu(z):
    hz = 0.5 * z
    return hz + hz * jnp.tanh(hz)


def _dot(a, b):
    return jnp.dot(a, b, preferred_element_type=F32)


_DONE = object()


def _skewed(items, lag):
    waiting = []
    for item in items:
        if next(item, _DONE) is not _DONE:
            waiting.append(item)
        yield
        if len(waiting) > lag:
            next(waiting.pop(0), _DONE)
            yield
    for item in waiting:
        next(item, _DONE)
        yield


def _chain(*gens):
    for g in gens:
        yield from g


def _interleave(a, b, b_per_a):
    credit = 1.0 - b_per_a
    a_live = b_live = True
    while a_live or b_live:
        if a_live:
            a_live = next(a, _DONE) is not _DONE
        credit += b_per_a
        while b_live and (credit >= 1.0 or not a_live):
            b_live = next(b, _DONE) is not _DONE
            credit -= 1.0


def _lane_half_mask(parity):
    lane = lax.broadcasted_iota(jnp.int32, (1, LANES), 1)
    return (lane >= HEAD_DIM) if parity else (lane < HEAD_DIM)


def _place_head_rows(kt_head, parity):
    zeros = jnp.zeros_like(kt_head)
    return jnp.concatenate([zeros, kt_head] if parity else [kt_head, zeros], axis=0)


def _layer_kernel(sinks_ref,
                  x_ref, mem_ref, win_ref, wkv_f32_ref, wout_f32_ref, preg_ref, memg_ref, postg_ref,
                  vng_ref, vnb_ref, ws_ref, bs_ref, relb_ref, buckets_ref,
                  out_ref,
                  h_scr, gu_scr, vn_scr, qs_scr, ktvar_scr, vvar_scr, zg_scr,
                  mk_scr, mv_scr, bias_scr, wsm_scr, bst_scr, wkv_ref, wout_ref,
                  y_scr, ybf_scr, o_scr):
    ts = x_ref.shape[1]
    b = pl.program_id(0)
    t = pl.program_id(1)

    @pl.when((b == 0) & (t == 0))
    def _init():
        row = lax.broadcasted_iota(jnp.int32, (CHUNK, CHUNK), 0)
        col = lax.broadcasted_iota(jnp.int32, (CHUNK, CHUNK), 1)
        for g in range(A_GROUPS):
            wsm_scr[g] = jnp.where(row >= col, ws_ref[g], 0.0).astype(BF16)
            bs_col = jnp.sum(jnp.where(row == col, bs_ref[g:g + 1, :], 0.0), axis=1, keepdims=True)
            bst_scr[g] = jnp.broadcast_to(bs_col, (CHUNK, LANES))
        for r0 in range(0, D_MODEL, 2 * CHUNK):
            wrows = pl.ds(r0, 2 * CHUNK)
            wkv_ref[wrows, :] = wkv_f32_ref[wrows, :].astype(BF16)
            wout_ref[wrows, :] = wout_f32_ref[wrows, :].astype(BF16)
        buckets = buckets_ref[...]
        prev_key = lax.broadcasted_iota(jnp.int32, (CHUNK, 2 * CHUNK), 1) < CHUNK
        for hd in range(SWA_HEADS):
            tbl = jnp.full((CHUNK, 2 * CHUNK), NEG, F32)
            for bk in range(N_BUCKETS):
                tbl = jnp.where(buckets == bk, relb_ref[bk:bk + 1, hd:hd + 1] * LOG2E, tbl)
            bias_scr[0, hd] = tbl
            bias_scr[1, hd] = jnp.where(prev_key, NEG, tbl)

    @pl.when(t == 0)
    def _sequence_start():
        ktvar_scr[:, :, 0:CHUNK] = jnp.zeros((SWA_HEADS, LANES, CHUNK), BF16)
        vvar_scr[:, 0:CHUNK, :] = jnp.zeros((SWA_HEADS, CHUNK, LANES), BF16)
        m = mem_ref[0]
        ms = jnp.mean(m * m, axis=-1, keepdims=True)
        hm = (m * lax.rsqrt(ms + EPS) * memg_ref[...]).astype(BF16)
        kv = _dot(hm, wkv_ref[...])
        for pair in range(MEM_HEADS // 2):
            kt_pair = kv[:, pair * LANES:(pair + 1) * LANES].T
            v_pair = kv[:, MEM_WIDTH + pair * LANES:MEM_WIDTH + (pair + 1) * LANES]
            for parity in range(2):
                hd = 2 * pair + parity
                kt_head = kt_pair[parity * HEAD_DIM:(parity + 1) * HEAD_DIM]
                mk_scr[hd] = _place_head_rows(kt_head, parity).astype(BF16)
                mv_scr[hd] = jnp.where(_lane_half_mask(parity), v_pair, 0.0).astype(BF16)

    @pl.when(t > 0)
    def _copy_carry():
        ktvar_scr[:, :, 0:CHUNK] = ktvar_scr[:, :, ts:ts + CHUNK]
        vvar_scr[:, 0:CHUNK, :] = vvar_scr[:, ts:ts + CHUNK, :]

    lo = _lane_half_mask(0)
    hi = _lane_half_mask(1)
    first_tile = jnp.where(t == 0, 1, 0)

    def pre_norm(r0, nrows):
        for rc in range(nrows // CHUNK):
            crows = pl.ds(r0 + rc * CHUNK, CHUNK)
            x = x_ref[0, crows, :]
            ms = jnp.mean(x * x, axis=-1, keepdims=True)
            h_scr[crows, :] = (x * lax.rsqrt(ms + EPS) * preg_ref[...]).astype(BF16)

    def proj(rows, c0, width):
        return _dot(h_scr[rows, :], win_ref[:, c0:c0 + width])

    def project(r0, nrows):
        rows = pl.ds(r0, nrows)

        v_all = _gelu_tanh(proj(rows, OFF_V, A_WIDTH))
        for g in range(A_GROUPS):
            cols = pl.ds(g * LANES, LANES)
            v = v_all[:, g * LANES:(g + 1) * LANES]
            mu = jnp.mean(v, axis=-1, keepdims=True)
            d = v - mu
            var = jnp.mean(d * d, axis=-1, keepdims=True)
            vn = d * lax.rsqrt(var + EPS) * vng_ref[:, cols] + vnb_ref[:, cols]
            vn_scr[rows, cols] = vn.astype(BF16)
        yield

        gu_scr[rows, :] = _gelu_tanh(proj(rows, OFF_U, A_WIDTH))
        yield

        qkv = proj(rows, OFF_SQ, SWA_WIDTH + 2 * SWA_KV_WIDTH)
        qs_scr[rows, 0:SWA_WIDTH] = (qkv[:, 0:SWA_WIDTH] * Q_SCALE).astype(BF16)
        skt = qkv[:, SWA_WIDTH:SWA_WIDTH + LANES].T
        sv = qkv[:, SWA_WIDTH + LANES:]
        sv_rot = pltpu.roll(sv, HEAD_DIM, 1)
        new_rows = pl.ds(CHUNK + r0, nrows)
        for hd, (keep, vsrc) in enumerate([(lo, sv), (hi, sv_rot), (lo, sv_rot), (hi, sv)]):
            kv_head = hd // 2
            kt_head = skt[kv_head * HEAD_DIM:(kv_head + 1) * HEAD_DIM]
            ktvar_scr[hd, :, new_rows] = _place_head_rows(kt_head, hd % 2).astype(BF16)
            vvar_scr[hd, new_rows, :] = jnp.where(keep, vsrc, 0.0).astype(BF16)
        yield

        mqz = proj(rows, OFF_MQ, PROJ_COLS)
        qs_scr[rows, SWA_WIDTH:] = (mqz[:, 0:MEM_WIDTH] * Q_SCALE).astype(BF16)
        zg_scr[rows, 0:PROJ_COLS - MEM_WIDTH] = _silu(mqz[:, MEM_WIDTH:])
        yield

        for c in range(r0 // CHUNK, (r0 + nrows) // CHUNK):
            for g in range(A_GROUPS):
                spatial_gating(c, g)
            yield

        done = PROJ_COLS - MEM_WIDTH
        while done < MIX_WIDTH:
            width = min(PROJ_COLS, MIX_WIDTH - done)
            zg_scr[rows, done:done + width] = _silu(proj(rows, OFF_Z + done, width))
            done += width
            yield

    def spatial_gating(c, g):
        rows = pl.ds(c * CHUNK, CHUNK)
        cols = pl.ds(g * LANES, LANES)
        sv_g = _dot(wsm_scr[g], vn_scr[rows, cols]) + bst_scr[g]
        y_scr[rows, cols] = gu_scr[rows, cols] * sv_g

    def attn_item(c, pair, sliding):
        r0 = c * CHUNK
        rows = pl.ds(r0, CHUNK)
        heads = (2 * pair, 2 * pair + 1)
        if sliding:
            band = pl.ds(r0, 2 * CHUNK)
            qp = qs_scr[rows, pl.ds(pair * LANES, LANES)]
            keys = jnp.concatenate([ktvar_scr[hd, :, band] for hd in heads], axis=1)
            bias_sel = first_tile if c == 0 else 0
            bias = jnp.concatenate([bias_scr[bias_sel, hd] for hd in heads], axis=1)
            logits = _dot(qp, keys) + bias
        else:
            qp = qs_scr[rows, pl.ds(SWA_WIDTH + pair * LANES, LANES)]
            keys = jnp.concatenate([mk_scr[hd] for hd in heads], axis=1)
            logits = _dot(qp, keys)
        n_keys = logits.shape[1] // 2
        yield
        probs, recips = [], []
        for i, hd in enumerate(heads):
            s = logits[:, i * n_keys:(i + 1) * n_keys]
            m = jnp.max(s, axis=-1, keepdims=True)
            if sliding:
                sink = sinks_ref[hd] * LOG2E
                m = jnp.maximum(m, sink)
            p = jnp.exp2(s - m)
            denom = jnp.sum(p, axis=-1, keepdims=True)
            if sliding:
                denom = denom + jnp.exp2(sink - m)
            probs.append(p.astype(BF16))
            recips.append(1.0 / denom)
        if sliding:
            values = jnp.concatenate([vvar_scr[hd, band, :] for hd in heads], axis=0)
        else:
            values = jnp.concatenate([mv_scr[hd] for hd in heads], axis=0)
        o_pair = _dot(jnp.concatenate(probs, axis=1), values)
        o_pair = o_pair * jnp.where(lo, recips[0], recips[1])
        off = A_WIDTH if sliding else A_WIDTH + SWA_WIDTH
        y_scr[rows, pl.ds(off + pair * LANES, LANES)] = o_pair

    def head_items(r0, nrows):
        for c in range(r0 // CHUNK, (r0 + nrows) // CHUNK):
            for pair in range(SWA_HEADS // 2):
                yield attn_item(c, pair, True)
            for pair in range(MEM_HEADS // 2):
                yield attn_item(c, pair, False)

    def project_out(r0, nrows):
        rows = pl.ds(r0, nrows)
        ybf_scr[rows, :] = (y_scr[rows, :] * zg_scr[rows, :]).astype(BF16)
        for c0 in range(0, D_MODEL, PROJ_COLS):
            o_scr[rows, c0:c0 + PROJ_COLS] = _dot(ybf_scr[rows, :], wout_ref[:, c0:c0 + PROJ_COLS])
            yield
        for rc in range(nrows // CHUNK):
            crows = pl.ds(r0 + rc * CHUNK, CHUNK)
            o = o_scr[crows, :]
            ms = jnp.mean(o * o, axis=-1, keepdims=True)
            out_ref[0, crows, :] = x_ref[0, crows, :] + o * lax.rsqrt(ms + EPS) * postg_ref[...]
            yield

    assert sum(SUB_TILES) == ts
    starts = [sum(SUB_TILES[:s]) for s in range(len(SUB_TILES))]
    subs = list(zip(starts, SUB_TILES))
    n_sub = len(subs)
    pre_norm(*subs[0])
    filler = project(*subs[0])
    for s in range(n_sub):
        if s + 1 < n_sub:
            pre_norm(*subs[s + 1])
        for _ in filler:
            pass
        heads = _skewed(head_items(*subs[s]), ITEM_LAG)
        n_filler = 0
        if s + 1 < n_sub:
            filler = project(*subs[s + 1])
            n_filler += PROJECT_COLUMN_PIECES + subs[s + 1][1] // CHUNK
        else:
            filler = iter(())
        if s >= 1:
            filler = _chain(project_out(*subs[s - 1]), filler)
            n_filler += PROJECT_OUT_DOTS + subs[s - 1][1] // CHUNK
        _interleave(heads, filler, n_filler / ((subs[s][1] // CHUNK) * HEAD_STAGES_PER_CHUNK))
        filler = iter(())
    for _ in project_out(*subs[-1]):
        pass


def _layer(x, mem, w_in_bf16, w_kv, w_out, pre_g, mem_g, post_g, vng, vnb, w_spatial,
           b_spatial, buckets, sinks, rel_bias):
    batch, seq, _ = x.shape
    ts = SEQ_TILE
    n_tiles = seq // ts
    const2 = lambda b, t, *_: (0, 0)
    const3 = lambda b, t, *_: (0, 0, 0)
    grid_spec = pltpu.PrefetchScalarGridSpec(
        num_scalar_prefetch=1,
        grid=(batch, n_tiles),
        in_specs=[
            pl.BlockSpec((1, ts, D_MODEL), lambda b, t, *_: (b, t, 0)),
            pl.BlockSpec((1, MEM_LEN, D_MODEL), lambda b, t, *_: (b, 0, 0)),
            pl.BlockSpec((D_MODEL, IN_WIDTH), const2),
            pl.BlockSpec((D_MODEL, 2 * MEM_WIDTH), const2),
            pl.BlockSpec((MIX_WIDTH, D_MODEL), const2),
            pl.BlockSpec((1, D_MODEL), const2),
            pl.BlockSpec((1, D_MODEL), const2),
            pl.BlockSpec((1, D_MODEL), const2),
            pl.BlockSpec((1, A_WIDTH), const2),
            pl.BlockSpec((1, A_WIDTH), const2),
            pl.BlockSpec((A_GROUPS, CHUNK, CHUNK), const3),
            pl.BlockSpec((A_GROUPS, CHUNK), const2),
            pl.BlockSpec((N_BUCKETS, SWA_HEADS), const2),
            pl.BlockSpec((CHUNK, 2 * CHUNK), const2),
        ],
        out_specs=pl.BlockSpec((1, ts, D_MODEL), lambda b, t, *_: (b, t, 0)),
        scratch_shapes=[
            pltpu.VMEM((ts, D_MODEL), BF16),
            pltpu.VMEM((ts, A_WIDTH), F32),
            pltpu.VMEM((ts, A_WIDTH), BF16),
            pltpu.VMEM((ts, SWA_WIDTH + MEM_WIDTH), BF16),
            pltpu.VMEM((SWA_HEADS, LANES, CHUNK + ts), BF16),
            pltpu.VMEM((SWA_HEADS, CHUNK + ts, LANES), BF16),
            pltpu.VMEM((ts, MIX_WIDTH), F32),
            pltpu.VMEM((MEM_HEADS, LANES, MEM_LEN), BF16),
            pltpu.VMEM((MEM_HEADS, MEM_LEN, LANES), BF16),
            pltpu.VMEM((2, SWA_HEADS, CHUNK, 2 * CHUNK), F32),
            pltpu.VMEM((A_GROUPS, CHUNK, CHUNK), BF16),
            pltpu.VMEM((A_GROUPS, CHUNK, LANES), F32),
            pltpu.VMEM((D_MODEL, 2 * MEM_WIDTH), BF16),
            pltpu.VMEM((MIX_WIDTH, D_MODEL), BF16),
            pltpu.VMEM((ts, MIX_WIDTH), F32),
            pltpu.VMEM((ts, MIX_WIDTH), BF16),
            pltpu.VMEM((ts, D_MODEL), F32),
        ],
    )
    return pl.pallas_call(
        _layer_kernel,
        grid_spec=grid_spec,
        out_shape=jax.ShapeDtypeStruct(x.shape, x.dtype),
        compiler_params=pltpu.CompilerParams(
            dimension_semantics=("arbitrary", "arbitrary"),
            vmem_limit_bytes=VMEM_LIMIT_BYTES),
        name="layer",
    )(sinks, x, mem, w_in_bf16, w_kv, w_out, pre_g, mem_g, post_g, vng, vnb,
      w_spatial, b_spatial, rel_bias, buckets)


def kernel(x, mem, pre_norm_g, post_norm_g, mem_norm_g, w_in, w_mem_kv, v_norm_g, v_norm_b,
           w_spatial, b_spatial, attn_sinks, rel_bias, w_out):
    depth = w_in.shape[0]
    buckets = jnp.asarray(_band_buckets())
    for layer in range(depth):
        x = _layer(x, mem, w_in[layer].astype(BF16), w_mem_kv[layer], w_out[layer],
                   pre_norm_g[layer][None, :], mem_norm_g[layer][None, :], post_norm_g[layer][None, :],
                   v_norm_g[layer][None, :], v_norm_b[layer][None, :],
                   w_spatial[layer], b_spatial[layer], buckets, attn_sinks[layer], rel_bias)
    return x
```

```python
import numpy as np
import jax
import jax.numpy as jnp
from jax import lax
from jax.experimental import pallas as pl
from jax.experimental.pallas import tpu as pltpu

D_MODEL = 1024
MEM_LEN = 256
HEAD_DIM = 64
CHUNK = 128
A_GROUPS = 4
A_WIDTH = 512
SWA_HEADS = 4
SWA_WIDTH = 256
SWA_KV_WIDTH = 128
MEM_HEADS = 4
MEM_WIDTH = 256
MIX_WIDTH = 1024
IN_WIDTH = 2816
N_BUCKETS = 32
MAX_DISTANCE = 128
WINDOW = 128
EPS = 1e-6
NEG = -1e30
LOG2E = float(np.log2(np.e))
Q_SCALE = HEAD_DIM ** -0.5 * LOG2E

OFF_U = 0
OFF_V = A_WIDTH
OFF_SQ = 2 * A_WIDTH
OFF_SK = OFF_SQ + SWA_WIDTH
OFF_SV = OFF_SK + SWA_KV_WIDTH
OFF_MQ = OFF_SV + SWA_KV_WIDTH
OFF_Z = OFF_MQ + MEM_WIDTH

LANES = 128
SEQ_TILE = 1024
SUB_TILES = (256, 256, 256, 256)
PROJ_COLS = 512
ITEM_LAG = 2
PROJECT_COLUMN_PIECES = 6
PROJECT_OUT_DOTS = D_MODEL // PROJ_COLS
HEAD_STAGES_PER_CHUNK = 2 * (SWA_HEADS // 2 + MEM_HEADS // 2)
VMEM_LIMIT_BYTES = 56 * 1024 * 1024

BF16 = jnp.bfloat16
F32 = jnp.float32


def _t5_causal_buckets(dist):
    n = np.maximum(dist, 0)
    max_exact = N_BUCKETS // 2
    large = max_exact + (np.log(np.maximum(n, 1) / max_exact) / np.log(MAX_DISTANCE / max_exact)
                         * (N_BUCKETS - max_exact)).astype(np.int32)
    large = np.minimum(large, N_BUCKETS - 1)
    return np.where(n < max_exact, n, large).astype(np.int32)


def _band_buckets():
    qi = np.arange(CHUNK)[:, None]
    kj = np.arange(2 * CHUNK)[None, :]
    dist = qi + CHUNK - kj
    valid = (dist >= 0) & (dist < WINDOW)
    return np.where(valid, _t5_causal_buckets(dist), -1).astype(np.int32)


def _gelu_tanh(x):
    c = np.float32(np.sqrt(2.0 / np.pi))
    ck = np.float32(np.sqrt(2.0 / np.pi) * 0.044715)
    hx = 0.5 * x
    return hx + hx * jnp.tanh(x * (c + ck * (x * x)))


def _silu(z):
    hz = 0.5 * z
    return hz + hz * jnp.tanh(hz)


def _dot(a, b):
    return jnp.dot(a, b, preferred_element_type=F32)


_DONE = object()


def _skewed(items, lag):
    waiting = []
    for item in items:
        if next(item, _DONE) is not _DONE:
            waiting.append(item)
        yield
        if len(waiting) > lag:
            next(waiting.pop(0), _DONE)
            yield
    for item in waiting:
        next(item, _DONE)
        yield


def _chain(*gens):
    for g in gens:
        yield from g


def _interleave(a, b, b_per_a):
    credit = 0.0
    a_live = b_live = True
    while a_live or b_live:
        if a_live:
            a_live = next(a, _DONE) is not _DONE
        credit += b_per_a
        while b_live and (credit >= 1.0 or not a_live):
            b_live = next(b, _DONE) is not _DONE
            credit -= 1.0


def _lane_half_mask(parity):
    lane = lax.broadcasted_iota(jnp.int32, (1, LANES), 1)
    return (lane >= HEAD_DIM) if parity else (lane < HEAD_DIM)


def _place_head_rows(kt_head, parity):
    zeros = jnp.zeros_like(kt_head)
    return jnp.concatenate([zeros, kt_head] if parity else [kt_head, zeros], axis=0)


def _layer_kernel(sinks_ref,
                  x_ref, mem_ref, win_ref, wkv_f32_ref, wout_f32_ref, preg_ref, memg_ref, postg_ref,
                  vng_ref, vnb_ref, ws_ref, bs_ref, relb_ref, buckets_ref,
                  out_ref,
                  h_scr, gu_scr, vn_scr, qs_scr, ktvar_scr, vvar_scr, zg_scr,
                  mk_scr, mv_scr, bias_scr, wsm_scr, bst_scr, wkv_ref, wout_ref,
                  y_scr, ybf_scr, o_scr):
    ts = x_ref.shape[1]
    b = pl.program_id(0)
    t = pl.program_id(1)

    @pl.when((b == 0) & (t == 0))
    def _init():
        row = lax.broadcasted_iota(jnp.int32, (CHUNK, CHUNK), 0)
        col = lax.broadcasted_iota(jnp.int32, (CHUNK, CHUNK), 1)
        for g in range(A_GROUPS):
            wsm_scr[g] = jnp.where(row >= col, ws_ref[g], 0.0).astype(BF16)
            bs_col = jnp.sum(jnp.where(row == col, bs_ref[g:g + 1, :], 0.0), axis=1, keepdims=True)
            bst_scr[g] = jnp.broadcast_to(bs_col, (CHUNK, LANES))
        for r0 in range(0, D_MODEL, 2 * CHUNK):
            wrows = pl.ds(r0, 2 * CHUNK)
            wkv_ref[wrows, :] = wkv_f32_ref[wrows, :].astype(BF16)
            wout_ref[wrows, :] = wout_f32_ref[wrows, :].astype(BF16)
        buckets = buckets_ref[...]
        prev_key = lax.broadcasted_iota(jnp.int32, (CHUNK, 2 * CHUNK), 1) < CHUNK
        for hd in range(SWA_HEADS):
            tbl = jnp.full((CHUNK, 2 * CHUNK), NEG, F32)
            for bk in range(N_BUCKETS):
                tbl = jnp.where(buckets == bk, relb_ref[bk:bk + 1, hd:hd + 1] * LOG2E, tbl)
            bias_scr[0, hd] = tbl
            bias_scr[1, hd] = jnp.where(prev_key, NEG, tbl)

    @pl.when(t == 0)
    def _sequence_start():
        ktvar_scr[:, :, 0:CHUNK] = jnp.zeros((SWA_HEADS, LANES, CHUNK), BF16)
        vvar_scr[:, 0:CHUNK, :] = jnp.zeros((SWA_HEADS, CHUNK, LANES), BF16)
        m = mem_ref[0]
        ms = jnp.mean(m * m, axis=-1, keepdims=True)
        hm = (m * lax.rsqrt(ms + EPS) * memg_ref[...]).astype(BF16)
        kv = _dot(hm, wkv_ref[...])
        for pair in range(MEM_HEADS // 2):
            kt_pair = kv[:, pair * LANES:(pair + 1) * LANES].T
            v_pair = kv[:, MEM_WIDTH + pair * LANES:MEM_WIDTH + (pair + 1) * LANES]
            for parity in range(2):
                hd = 2 * pair + parity
                kt_head = kt_pair[parity * HEAD_DIM:(parity + 1) * HEAD_DIM]
                mk_scr[hd] = _place_head_rows(kt_head, parity).astype(BF16)
                mv_scr[hd] = jnp.where(_lane_half_mask(parity), v_pair, 0.0).astype(BF16)

    @pl.when(t > 0)
    def _copy_carry():
        ktvar_scr[:, :, 0:CHUNK] = ktvar_scr[:, :, ts:ts + CHUNK]
        vvar_scr[:, 0:CHUNK, :] = vvar_scr[:, ts:ts + CHUNK, :]

    lo = _lane_half_mask(0)
    hi = _lane_half_mask(1)
    first_tile = jnp.where(t == 0, 1, 0)

    def pre_norm(r0, nrows):
        for rc in range(nrows // CHUNK):
            crows = pl.ds(r0 + rc * CHUNK, CHUNK)
            x = x_ref[0, crows, :]
            ms = jnp.mean(x * x, axis=-1, keepdims=True)
            h_scr[crows, :] = (x * lax.rsqrt(ms + EPS) * preg_ref[...]).astype(BF16)

    def proj(rows, c0, width):
        return _dot(h_scr[rows, :], win_ref[:, c0:c0 + width])

    def project(r0, nrows):
        rows = pl.ds(r0, nrows)

        v_all = _gelu_tanh(proj(rows, OFF_V, A_WIDTH))
        for g in range(A_GROUPS):
            cols = pl.ds(g * LANES, LANES)
            v = v_all[:, g * LANES:(g + 1) * LANES]
            mu = jnp.mean(v, axis=-1, keepdims=True)
            d = v - mu
            var = jnp.mean(d * d, axis=-1, keepdims=True)
            vn = d * lax.rsqrt(var + EPS) * vng_ref[:, cols] + vnb_ref[:, cols]
            vn_scr[rows, cols] = vn.astype(BF16)
        yield

        gu_scr[rows, :] = _gelu_tanh(proj(rows, OFF_U, A_WIDTH))
        yield

        qkv = proj(rows, OFF_SQ, SWA_WIDTH + 2 * SWA_KV_WIDTH)
        qs_scr[rows, 0:SWA_WIDTH] = (qkv[:, 0:SWA_WIDTH] * Q_SCALE).astype(BF16)
        skt = qkv[:, SWA_WIDTH:SWA_WIDTH + LANES].T
        sv = qkv[:, SWA_WIDTH + LANES:]
        sv_rot = pltpu.roll(sv, HEAD_DIM, 1)
        new_rows = pl.ds(CHUNK + r0, nrows)
        for hd, (keep, vsrc) in enumerate([(lo, sv), (hi, sv_rot), (lo, sv_rot), (hi, sv)]):
            kv_head = hd // 2
            kt_head = skt[kv_head * HEAD_DIM:(kv_head + 1) * HEAD_DIM]
            ktvar_scr[hd, :, new_rows] = _place_head_rows(kt_head, hd % 2).astype(BF16)
            vvar_scr[hd, new_rows, :] = jnp.where(keep, vsrc, 0.0).astype(BF16)
        yield

        mqz = proj(rows, OFF_MQ, PROJ_COLS)
        qs_scr[rows, SWA_WIDTH:] = (mqz[:, 0:MEM_WIDTH] * Q_SCALE).astype(BF16)
        zg_scr[rows, 0:PROJ_COLS - MEM_WIDTH] = _silu(mqz[:, MEM_WIDTH:])
        yield

        for c in range(r0 // CHUNK, (r0 + nrows) // CHUNK):
            for g in range(A_GROUPS):
                spatial_gating(c, g)
            yield

        done = PROJ_COLS - MEM_WIDTH
        while done < MIX_WIDTH:
            width = min(PROJ_COLS, MIX_WIDTH - done)
            zg_scr[rows, done:done + width] = _silu(proj(rows, OFF_Z + done, width))
            done += width
            yield

    def spatial_gating(c, g):
        rows = pl.ds(c * CHUNK, CHUNK)
        cols = pl.ds(g * LANES, LANES)
        sv_g = _dot(wsm_scr[g], vn_scr[rows, cols]) + bst_scr[g]
        y_scr[rows, cols] = gu_scr[rows, cols] * sv_g

    def attn_item(c, pair, sliding):
        r0 = c * CHUNK
        rows = pl.ds(r0, CHUNK)
        heads = (2 * pair, 2 * pair + 1)
        if sliding:
            band = pl.ds(r0, 2 * CHUNK)
            qp = qs_scr[rows, pl.ds(pair * LANES, LANES)]
            keys = jnp.concatenate([ktvar_scr[hd, :, band] for hd in heads], axis=1)
            bias_sel = first_tile if c == 0 else 0
            bias = jnp.concatenate([bias_scr[bias_sel, hd] for hd in heads], axis=1)
            logits = _dot(qp, keys) + bias
        else:
            qp = qs_scr[rows, pl.ds(SWA_WIDTH + pair * LANES, LANES)]
            keys = jnp.concatenate([mk_scr[hd] for hd in heads], axis=1)
            logits = _dot(qp, keys)
        n_keys = logits.shape[1] // 2
        yield
        probs, recips = [], []
        for i, hd in enumerate(heads):
            s = logits[:, i * n_keys:(i + 1) * n_keys]
            m = jnp.max(s, axis=-1, keepdims=True)
            if sliding:
                sink = sinks_ref[hd] * LOG2E
                m = jnp.maximum(m, sink)
            p = jnp.exp2(s - m)
            denom = jnp.sum(p, axis=-1, keepdims=True)
            if sliding:
                denom = denom + jnp.exp2(sink - m)
            probs.append(p.astype(BF16))
            recips.append(1.0 / denom)
        if sliding:
            values = jnp.concatenate([vvar_scr[hd, band, :] for hd in heads], axis=0)
        else:
            values = jnp.concatenate([mv_scr[hd] for hd in heads], axis=0)
        o_pair = _dot(jnp.concatenate(probs, axis=1), values)
        o_pair = o_pair * jnp.where(lo, recips[0], recips[1])
        off = A_WIDTH if sliding else A_WIDTH + SWA_WIDTH
        y_scr[rows, pl.ds(off + pair * LANES, LANES)] = o_pair

    def head_items(r0, nrows):
        for c in range(r0 // CHUNK, (r0 + nrows) // CHUNK):
            for pair in range(MEM_HEADS // 2):
                yield attn_item(c, pair, False)
            for pair in range(SWA_HEADS // 2):
                yield attn_item(c, pair, True)

    def project_out(r0, nrows):
        rows = pl.ds(r0, nrows)
        ybf_scr[rows, :] = (y_scr[rows, :] * zg_scr[rows, :]).astype(BF16)
        for c0 in range(0, D_MODEL, PROJ_COLS):
            o_scr[rows, c0:c0 + PROJ_COLS] = _dot(ybf_scr[rows, :], wout_ref[:, c0:c0 + PROJ_COLS])
            yield
        for rc in range(nrows // CHUNK):
            crows = pl.ds(r0 + rc * CHUNK, CHUNK)
            o = o_scr[crows, :]
            ms = jnp.mean(o * o, axis=-1, keepdims=True)
            out_ref[0, crows, :] = x_ref[0, crows, :] + o * lax.rsqrt(ms + EPS) * postg_ref[...]
            yield

    assert sum(SUB_TILES) == ts
    starts = [sum(SUB_TILES[:s]) for s in range(len(SUB_TILES))]
    subs = list(zip(starts, SUB_TILES))
    n_sub = len(subs)
    pre_norm(*subs[0])
    filler = project(*subs[0])
    for s in range(n_sub):
        if s + 1 < n_sub:
            pre_norm(*subs[s + 1])
        for _ in filler:
            pass
        heads = _skewed(head_items(*subs[s]), ITEM_LAG)
        n_filler = 0
        if s + 1 < n_sub:
            filler = project(*subs[s + 1])
            n_filler += PROJECT_COLUMN_PIECES + subs[s + 1][1] // CHUNK
        else:
            filler = iter(())
        if s >= 1:
            filler = _chain(project_out(*subs[s - 1]), filler)
            n_filler += PROJECT_OUT_DOTS + subs[s - 1][1] // CHUNK
        _interleave(heads, filler, n_filler / ((subs[s][1] // CHUNK) * HEAD_STAGES_PER_CHUNK))
        filler = iter(())
    for _ in project_out(*subs[-1]):
        pass


def _layer(x, mem, w_in_bf16, w_kv, w_out, pre_g, mem_g, post_g, vng, vnb, w_spatial,
           b_spatial, buckets, sinks, rel_bias):
    batch, seq, _ = x.shape
    ts = SEQ_TILE
    n_tiles = seq // ts
    const2 = lambda b, t, *_: (0, 0)
    const3 = lambda b, t, *_: (0, 0, 0)
    grid_spec = pltpu.PrefetchScalarGridSpec(
        num_scalar_prefetch=1,
        grid=(batch, n_tiles),
        in_specs=[
            pl.BlockSpec((1, ts, D_MODEL), lambda b, t, *_: (b, t, 0)),
            pl.BlockSpec((1, MEM_LEN, D_MODEL), lambda b, t, *_: (b, 0, 0)),
            pl.BlockSpec((D_MODEL, IN_WIDTH), const2),
            pl.BlockSpec((D_MODEL, 2 * MEM_WIDTH), const2),
            pl.BlockSpec((MIX_WIDTH, D_MODEL), const2),
            pl.BlockSpec((1, D_MODEL), const2),
            pl.BlockSpec((1, D_MODEL), const2),
            pl.BlockSpec((1, D_MODEL), const2),
            pl.BlockSpec((1, A_WIDTH), const2),
            pl.BlockSpec((1, A_WIDTH), const2),
            pl.BlockSpec((A_GROUPS, CHUNK, CHUNK), const3),
            pl.BlockSpec((A_GROUPS, CHUNK), const2),
            pl.BlockSpec((N_BUCKETS, SWA_HEADS), const2),
            pl.BlockSpec((CHUNK, 2 * CHUNK), const2),
        ],
        out_specs=pl.BlockSpec((1, ts, D_MODEL), lambda b, t, *_: (b, t, 0)),
        scratch_shapes=[
            pltpu.VMEM((ts, D_MODEL), BF16),
            pltpu.VMEM((ts, A_WIDTH), F32),
            pltpu.VMEM((ts, A_WIDTH), BF16),
            pltpu.VMEM((ts, SWA_WIDTH + MEM_WIDTH), BF16),
            pltpu.VMEM((SWA_HEADS, LANES, CHUNK + ts), BF16),
            pltpu.VMEM((SWA_HEADS, CHUNK + ts, LANES), BF16),
            pltpu.VMEM((ts, MIX_WIDTH), F32),
            pltpu.VMEM((MEM_HEADS, LANES, MEM_LEN), BF16),
            pltpu.VMEM((MEM_HEADS, MEM_LEN, LANES), BF16),
            pltpu.VMEM((2, SWA_HEADS, CHUNK, 2 * CHUNK), F32),
            pltpu.VMEM((A_GROUPS, CHUNK, CHUNK), BF16),
            pltpu.VMEM((A_GROUPS, CHUNK, LANES), F32),
            pltpu.VMEM((D_MODEL, 2 * MEM_WIDTH), BF16),
            pltpu.VMEM((MIX_WIDTH, D_MODEL), BF16),
            pltpu.VMEM((ts, MIX_WIDTH), F32),
            pltpu.VMEM((ts, MIX_WIDTH), BF16),
            pltpu.VMEM((ts, D_MODEL), F32),
        ],
    )
    return pl.pallas_call(
        _layer_kernel,
        grid_spec=grid_spec,
        out_shape=jax.ShapeDtypeStruct(x.shape, x.dtype),
        compiler_params=pltpu.CompilerParams(
            dimension_semantics=("arbitrary", "arbitrary"),
            vmem_limit_bytes=VMEM_LIMIT_BYTES),
        name="layer",
    )(sinks, x, mem, w_in_bf16, w_kv, w_out, pre_g, mem_g, post_g, vng, vnb,
      w_spatial, b_spatial, rel_bias, buckets)


def kernel(x, mem, pre_norm_g, post_norm_g, mem_norm_g, w_in, w_mem_kv, v_norm_g, v_norm_b,
           w_spatial, b_spatial, attn_sinks, rel_bias, w_out):
    depth = w_in.shape[0]
    buckets = jnp.asarray(_band_buckets())
    for layer in range(depth):
        x = _layer(x, mem, w_in[layer].astype(BF16), w_mem_kv[layer], w_out[layer],
                   pre_norm_g[layer][None, :], mem_norm_g[layer][None, :], post_norm_g[layer][None, :],
                   v_norm_g[layer][None, :], v_norm_b[layer][None, :],
                   w_spatial[layer], b_spatial[layer], buckets, attn_sinks[layer], rel_bias)
    return x
```

```python
import numpy as np
import jax
import jax.numpy as jnp
from jax import lax
from jax.experimental import pallas as pl
from jax.experimental.pallas import tpu as pltpu

D_MODEL = 1024
MEM_LEN = 256
HEAD_DIM = 64
CHUNK = 128
A_GROUPS = 4
A_WIDTH = 512
SWA_HEADS = 4
SWA_WIDTH = 256
SWA_KV_WIDTH = 128
MEM_HEADS = 4
MEM_WIDTH = 256
MIX_WIDTH = 1024
IN_WIDTH = 2816
N_BUCKETS = 32
MAX_DISTANCE = 128
WINDOW = 128
EPS = 1e-6
NEG = -1e30
LOG2E = float(np.log2(np.e))
Q_SCALE = HEAD_DIM ** -0.5 * LOG2E

OFF_U = 0
OFF_V = A_WIDTH
OFF_SQ = 2 * A_WIDTH
OFF_SK = OFF_SQ + SWA_WIDTH
OFF_SV = OFF_SK + SWA_KV_WIDTH
OFF_MQ = OFF_SV + SWA_KV_WIDTH
OFF_Z = OFF_MQ + MEM_WIDTH

LANES = 128
SEQ_TILE = 1024
SUB_TILES = (256, 256, 256, 256)
PROJ_COLS = 512
ITEM_LAG = 2
PROJECT_COLUMN_PIECES = 6
PROJECT_OUT_DOTS = 1
HEAD_STAGES_PER_CHUNK = 2 * (SWA_HEADS // 2 + MEM_HEADS // 2)
VMEM_LIMIT_BYTES = 56 * 1024 * 1024

BF16 = jnp.bfloat16
F32 = jnp.float32


def _t5_causal_buckets(dist):
    n = np.maximum(dist, 0)
    max_exact = N_BUCKETS // 2
    large = max_exact + (np.log(np.maximum(n, 1) / max_exact) / np.log(MAX_DISTANCE / max_exact)
                         * (N_BUCKETS - max_exact)).astype(np.int32)
    large = np.minimum(large, N_BUCKETS - 1)
    return np.where(n < max_exact, n, large).astype(np.int32)


def _band_buckets():
    qi = np.arange(CHUNK)[:, None]
    kj = np.arange(2 * CHUNK)[None, :]
    dist = qi + CHUNK - kj
    valid = (dist >= 0) & (dist < WINDOW)
    return np.where(valid, _t5_causal_buckets(dist), -1).astype(np.int32)


def _gelu_tanh(x):
    c = np.float32(np.sqrt(2.0 / np.pi))
    ck = np.float32(np.sqrt(2.0 / np.pi) * 0.044715)
    hx = 0.5 * x
    return hx + hx * jnp.tanh(x * (c + ck * (x * x)))


def _silu(z):
    hz = 0.5 * z
    return hz + hz * jnp.tanh(hz)


def _dot(a, b):
    return jnp.dot(a, b, preferred_element_type=F32)


_DONE = object()


def _skewed(items, lag):
    waiting = []
    for item in items:
        if next(item, _DONE) is not _DONE:
            waiting.append(item)
        yield
        if len(waiting) > lag:
            next(waiting.pop(0), _DONE)
            yield
    for item in waiting:
        next(item, _DONE)
        yield


def _chain(*gens):
    for g in gens:
        yield from g


def _interleave(a, b, b_per_a):
    credit = 0.0
    a_live = b_live = True
    while a_live or b_live:
        if a_live:
            a_live = next(a, _DONE) is not _DONE
        credit += b_per_a
        while b_live and (credit >= 1.0 or not a_live):
            b_live = next(b, _DONE) is not _DONE
            credit -= 1.0


def _lane_half_mask(parity):
    lane = lax.broadcasted_iota(jnp.int32, (1, LANES), 1)
    return (lane >= HEAD_DIM) if parity else (lane < HEAD_DIM)


def _place_head_rows(kt_head, parity):
    zeros = jnp.zeros_like(kt_head)
    return jnp.concatenate([zeros, kt_head] if parity else [kt_head, zeros], axis=0)


def _layer_kernel(sinks_ref,
                  x_ref, mem_ref, win_ref, wkv_f32_ref, wout_f32_ref, preg_ref, memg_ref, postg_ref,
                  vng_ref, vnb_ref, ws_ref, bs_ref, relb_ref, buckets_ref,
                  out_ref,
                  h_scr, gu_scr, vn_scr, qs_scr, ktvar_scr, vvar_scr, zg_scr,
                  mk_scr, mv_scr, bias_scr, wsm_scr, bst_scr, wkv_ref, wout_ref,
                  y_scr, ybf_scr):
    ts = x_ref.shape[1]
    b = pl.program_id(0)
    t = pl.program_id(1)

    @pl.when((b == 0) & (t == 0))
    def _init():
        row = lax.broadcasted_iota(jnp.int32, (CHUNK, CHUNK), 0)
        col = lax.broadcasted_iota(jnp.int32, (CHUNK, CHUNK), 1)
        for g in range(A_GROUPS):
            wsm_scr[g] = jnp.where(row >= col, ws_ref[g], 0.0).astype(BF16)
            bs_col = jnp.sum(jnp.where(row == col, bs_ref[g:g + 1, :], 0.0), axis=1, keepdims=True)
            bst_scr[g] = jnp.broadcast_to(bs_col, (CHUNK, LANES))
        for r0 in range(0, D_MODEL, 2 * CHUNK):
            wrows = pl.ds(r0, 2 * CHUNK)
            wkv_ref[wrows, :] = wkv_f32_ref[wrows, :].astype(BF16)
            wout_ref[wrows, :] = wout_f32_ref[wrows, :].astype(BF16)
        buckets = buckets_ref[...]
        prev_key = lax.broadcasted_iota(jnp.int32, (CHUNK, 2 * CHUNK), 1) < CHUNK
        for hd in range(SWA_HEADS):
            tbl = jnp.full((CHUNK, 2 * CHUNK), NEG, F32)
            for bk in range(N_BUCKETS):
                tbl = jnp.where(buckets == bk, relb_ref[bk:bk + 1, hd:hd + 1] * LOG2E, tbl)
            bias_scr[0, hd] = tbl
            bias_scr[1, hd] = jnp.where(prev_key, NEG, tbl)

    @pl.when(t == 0)
    def _sequence_start():
        ktvar_scr[:, :, 0:CHUNK] = jnp.zeros((SWA_HEADS, LANES, CHUNK), BF16)
        vvar_scr[:, 0:CHUNK, :] = jnp.zeros((SWA_HEADS, CHUNK, LANES), BF16)
        m = mem_ref[0]
        ms = jnp.mean(m * m, axis=-1, keepdims=True)
        hm = (m * lax.rsqrt(ms + EPS) * memg_ref[...]).astype(BF16)
        kv = _dot(hm, wkv_ref[...])
        for pair in range(MEM_HEADS // 2):
            kt_pair = kv[:, pair * LANES:(pair + 1) * LANES].T
            v_pair = kv[:, MEM_WIDTH + pair * LANES:MEM_WIDTH + (pair + 1) * LANES]
            for parity in range(2):
                hd = 2 * pair + parity
                kt_head = kt_pair[parity * HEAD_DIM:(parity + 1) * HEAD_DIM]
                mk_scr[hd] = _place_head_rows(kt_head, parity).astype(BF16)
                mv_scr[hd] = jnp.where(_lane_half_mask(parity), v_pair, 0.0).astype(BF16)

    @pl.when(t > 0)
    def _copy_carry():
        ktvar_scr[:, :, 0:CHUNK] = ktvar_scr[:, :, ts:ts + CHUNK]
        vvar_scr[:, 0:CHUNK, :] = vvar_scr[:, ts:ts + CHUNK, :]

    lo = _lane_half_mask(0)
    hi = _lane_half_mask(1)
    first_tile = jnp.where(t == 0, 1, 0)

    def pre_norm(r0, nrows):
        for rc in range(nrows // CHUNK):
            crows = pl.ds(r0 + rc * CHUNK, CHUNK)
            x = x_ref[0, crows, :]
            ms = jnp.mean(x * x, axis=-1, keepdims=True)
            h_scr[crows, :] = (x * lax.rsqrt(ms + EPS) * preg_ref[...]).astype(BF16)

    def proj(rows, c0, width):
        return _dot(h_scr[rows, :], win_ref[:, c0:c0 + width])

    def project(r0, nrows):
        rows = pl.ds(r0, nrows)

        v_all = _gelu_tanh(proj(rows, OFF_V, A_WIDTH))
        for g in range(A_GROUPS):
            cols = pl.ds(g * LANES, LANES)
            v = v_all[:, g * LANES:(g + 1) * LANES]
            mu = jnp.mean(v, axis=-1, keepdims=True)
            d = v - mu
            var = jnp.mean(d * d, axis=-1, keepdims=True)
            vn = d * lax.rsqrt(var + EPS) * vng_ref[:, cols] + vnb_ref[:, cols]
            vn_scr[rows, cols] = vn.astype(BF16)
        yield

        gu_scr[rows, :] = _gelu_tanh(proj(rows, OFF_U, A_WIDTH))
        yield

        qkv = proj(rows, OFF_SQ, SWA_WIDTH + 2 * SWA_KV_WIDTH)
        qs_scr[rows, 0:SWA_WIDTH] = (qkv[:, 0:SWA_WIDTH] * Q_SCALE).astype(BF16)
        skt = qkv[:, SWA_WIDTH:SWA_WIDTH + LANES].T
        sv = qkv[:, SWA_WIDTH + LANES:]
        sv_rot = pltpu.roll(sv, HEAD_DIM, 1)
        new_rows = pl.ds(CHUNK + r0, nrows)
        for hd, (keep, vsrc) in enumerate([(lo, sv), (hi, sv_rot), (lo, sv_rot), (hi, sv)]):
            kv_head = hd // 2
            kt_head = skt[kv_head * HEAD_DIM:(kv_head + 1) * HEAD_DIM]
            ktvar_scr[hd, :, new_rows] = _place_head_rows(kt_head, hd % 2).astype(BF16)
            vvar_scr[hd, new_rows, :] = jnp.where(keep, vsrc, 0.0).astype(BF16)
        yield

        mqz = proj(rows, OFF_MQ, PROJ_COLS)
        qs_scr[rows, SWA_WIDTH:] = (mqz[:, 0:MEM_WIDTH] * Q_SCALE).astype(BF16)
        zg_scr[rows, 0:PROJ_COLS - MEM_WIDTH] = _silu(mqz[:, MEM_WIDTH:])
        yield

        for c in range(r0 // CHUNK, (r0 + nrows) // CHUNK):
            for g in range(A_GROUPS):
                spatial_gating(c, g)
            yield

        done = PROJ_COLS - MEM_WIDTH
        while done < MIX_WIDTH:
            width = min(PROJ_COLS, MIX_WIDTH - done)
            zg_scr[rows, done:done + width] = _silu(proj(rows, OFF_Z + done, width))
            done += width
            yield

    def spatial_gating(c, g):
        rows = pl.ds(c * CHUNK, CHUNK)
        cols = pl.ds(g * LANES, LANES)
        sv_g = _dot(wsm_scr[g], vn_scr[rows, cols]) + bst_scr[g]
        y_scr[rows, cols] = gu_scr[rows, cols] * sv_g

    def attn_item(c, pair, sliding):
        r0 = c * CHUNK
        rows = pl.ds(r0, CHUNK)
        heads = (2 * pair, 2 * pair + 1)
        if sliding:
            band = pl.ds(r0, 2 * CHUNK)
            qp = qs_scr[rows, pl.ds(pair * LANES, LANES)]
            keys = jnp.concatenate([ktvar_scr[hd, :, band] for hd in heads], axis=1)
            bias_sel = first_tile if c == 0 else 0
            bias = jnp.concatenate([bias_scr[bias_sel, hd] for hd in heads], axis=1)
            logits = _dot(qp, keys) + bias
        else:
            qp = qs_scr[rows, pl.ds(SWA_WIDTH + pair * LANES, LANES)]
            keys = jnp.concatenate([mk_scr[hd] for hd in heads], axis=1)
            logits = _dot(qp, keys)
        n_keys = logits.shape[1] // 2
        yield
        probs, recips = [], []
        for i, hd in enumerate(heads):
            s = logits[:, i * n_keys:(i + 1) * n_keys]
            m = jnp.max(s, axis=-1, keepdims=True)
            if sliding:
                sink = sinks_ref[hd] * LOG2E
                m = jnp.maximum(m, sink)
            p = jnp.exp2(s - m)
            denom = jnp.sum(p, axis=-1, keepdims=True)
            if sliding:
                denom = denom + jnp.exp2(sink - m)
            probs.append(p.astype(BF16))
            recips.append(1.0 / denom)
        if sliding:
            values = jnp.concatenate([vvar_scr[hd, band, :] for hd in heads], axis=0)
        else:
            values = jnp.concatenate([mv_scr[hd] for hd in heads], axis=0)
        o_pair = _dot(jnp.concatenate(probs, axis=1), values)
        o_pair = o_pair * jnp.where(lo, recips[0], recips[1])
        off = A_WIDTH if sliding else A_WIDTH + SWA_WIDTH
        y_scr[rows, pl.ds(off + pair * LANES, LANES)] = o_pair

    def head_items(r0, nrows):
        for c in range(r0 // CHUNK, (r0 + nrows) // CHUNK):
            for pair in range(SWA_HEADS // 2):
                yield attn_item(c, pair, True)
            for pair in range(MEM_HEADS // 2):
                yield attn_item(c, pair, False)

    def project_out(r0, nrows):
        rows = pl.ds(r0, nrows)
        ybf_scr[rows, :] = (y_scr[rows, :] * zg_scr[rows, :]).astype(BF16)
        o_all = _dot(ybf_scr[rows, :], wout_ref[...])
        yield
        for rc in range(nrows // CHUNK):
            crows = pl.ds(r0 + rc * CHUNK, CHUNK)
            o = o_all[rc * CHUNK:(rc + 1) * CHUNK]
            ms = jnp.mean(o * o, axis=-1, keepdims=True)
            out_ref[0, crows, :] = x_ref[0, crows, :] + o * lax.rsqrt(ms + EPS) * postg_ref[...]
            yield

    assert sum(SUB_TILES) == ts
    starts = [sum(SUB_TILES[:s]) for s in range(len(SUB_TILES))]
    subs = list(zip(starts, SUB_TILES))
    n_sub = len(subs)
    pre_norm(*subs[0])
    filler = project(*subs[0])
    for s in range(n_sub):
        if s + 1 < n_sub:
            pre_norm(*subs[s + 1])
        for _ in filler:
            pass
        heads = _skewed(head_items(*subs[s]), ITEM_LAG)
        n_filler = 0
        if s + 1 < n_sub:
            filler = project(*subs[s + 1])
            n_filler += PROJECT_COLUMN_PIECES + subs[s + 1][1] // CHUNK
        else:
            filler = iter(())
        if s >= 1:
            filler = _chain(project_out(*subs[s - 1]), filler)
            n_filler += PROJECT_OUT_DOTS + subs[s - 1][1] // CHUNK
        _interleave(heads, filler, n_filler / ((subs[s][1] // CHUNK) * HEAD_STAGES_PER_CHUNK))
        filler = iter(())
    for _ in project_out(*subs[-1]):
        pass


def _layer(x, mem, w_in_bf16, w_kv, w_out, pre_g, mem_g, post_g, vng, vnb, w_spatial,
           b_spatial, buckets, sinks, rel_bias):
    batch, seq, _ = x.shape
    ts = SEQ_TILE
    n_tiles = seq // ts
    const2 = lambda b, t, *_: (0, 0)
    const3 = lambda b, t, *_: (0, 0, 0)
    grid_spec = pltpu.PrefetchScalarGridSpec(
        num_scalar_prefetch=1,
        grid=(batch, n_tiles),
        in_specs=[
            pl.BlockSpec((1, ts, D_MODEL), lambda b, t, *_: (b, t, 0)),
            pl.BlockSpec((1, MEM_LEN, D_MODEL), lambda b, t, *_: (b, 0, 0)),
            pl.BlockSpec((D_MODEL, IN_WIDTH), const2),
            pl.BlockSpec((D_MODEL, 2 * MEM_WIDTH), const2),
            pl.BlockSpec((MIX_WIDTH, D_MODEL), const2),
            pl.BlockSpec((1, D_MODEL), const2),
            pl.BlockSpec((1, D_MODEL), const2),
            pl.BlockSpec((1, D_MODEL), const2),
            pl.BlockSpec((1, A_WIDTH), const2),
            pl.BlockSpec((1, A_WIDTH), const2),
            pl.BlockSpec((A_GROUPS, CHUNK, CHUNK), const3),
            pl.BlockSpec((A_GROUPS, CHUNK), const2),
            pl.BlockSpec((N_BUCKETS, SWA_HEADS), const2),
            pl.BlockSpec((CHUNK, 2 * CHUNK), const2),
        ],
        out_specs=pl.BlockSpec((1, ts, D_MODEL), lambda b, t, *_: (b, t, 0)),
        scratch_shapes=[
            pltpu.VMEM((ts, D_MODEL), BF16),
            pltpu.VMEM((ts, A_WIDTH), F32),
            pltpu.VMEM((ts, A_WIDTH), BF16),
            pltpu.VMEM((ts, SWA_WIDTH + MEM_WIDTH), BF16),
            pltpu.VMEM((SWA_HEADS, LANES, CHUNK + ts), BF16),
            pltpu.VMEM((SWA_HEADS, CHUNK + ts, LANES), BF16),
            pltpu.VMEM((ts, MIX_WIDTH), F32),
            pltpu.VMEM((MEM_HEADS, LANES, MEM_LEN), BF16),
            pltpu.VMEM((MEM_HEADS, MEM_LEN, LANES), BF16),
            pltpu.VMEM((2, SWA_HEADS, CHUNK, 2 * CHUNK), F32),
            pltpu.VMEM((A_GROUPS, CHUNK, CHUNK), BF16),
            pltpu.VMEM((A_GROUPS, CHUNK, LANES), F32),
            pltpu.VMEM((D_MODEL, 2 * MEM_WIDTH), BF16),
            pltpu.VMEM((MIX_WIDTH, D_MODEL), BF16),
            pltpu.VMEM((ts, MIX_WIDTH), F32),
            pltpu.VMEM((ts, MIX_WIDTH), BF16),
        ],
    )
    return pl.pallas_call(
        _layer_kernel,
        grid_spec=grid_spec,
        out_shape=jax.ShapeDtypeStruct(x.shape, x.dtype),
        compiler_params=pltpu.CompilerParams(
            dimension_semantics=("arbitrary", "arbitrary"),
            vmem_limit_bytes=VMEM_LIMIT_BYTES),
        name="layer",
    )(sinks, x, mem, w_in_bf16, w_kv, w_out, pre_g, mem_g, post_g, vng, vnb,
      w_spatial, b_spatial, rel_bias, buckets)


def kernel(x, mem, pre_norm_g, post_norm_g, mem_norm_g, w_in, w_mem_kv, v_norm_g, v_norm_b,
           w_spatial, b_spatial, attn_sinks, rel_bias, w_out):
    depth = w_in.shape[0]
    buckets = jnp.asarray(_band_buckets())
    for layer in range(depth):
        x = _layer(x, mem, w_in[layer].astype(BF16), w_mem_kv[layer], w_out[layer],
                   pre_norm_g[layer][None, :], mem_norm_g[layer][None, :], post_norm_g[layer][None, :],
                   v_norm_g[layer][None, :], v_norm_b[layer][None, :],
                   w_spatial[layer], b_spatial[layer], buckets, attn_sinks[layer], rel_bias)
    return x
```

```python
import numpy as np
import jax
import jax.numpy as jnp
from jax import lax
from jax.experimental import pallas as pl
from jax.experimental.pallas import tpu as pltpu

D_MODEL = 1024
MEM_LEN = 256
HEAD_DIM = 64
CHUNK = 128
A_GROUPS = 4
A_WIDTH = 512
SWA_HEADS = 4
SWA_WIDTH = 256
SWA_KV_WIDTH = 128
MEM_HEADS = 4
MEM_WIDTH = 256
MIX_WIDTH = 1024
IN_WIDTH = 2816
N_BUCKETS = 32
MAX_DISTANCE = 128
WINDOW = 128
EPS = 1e-6
NEG = -1e30
LOG2E = float(np.log2(np.e))
Q_SCALE = HEAD_DIM ** -0.5 * LOG2E

OFF_U = 0
OFF_V = A_WIDTH
OFF_SQ = 2 * A_WIDTH
OFF_SK = OFF_SQ + SWA_WIDTH
OFF_SV = OFF_SK + SWA_KV_WIDTH
OFF_MQ = OFF_SV + SWA_KV_WIDTH
OFF_Z = OFF_MQ + MEM_WIDTH

LANES = 128
SEQ_TILE = 1024
SUB_TILES = (256, 256, 256, 256)
PROJ_COLS = 512
ITEM_LAG = 2
PROJECT_COLUMN_PIECES = 6
PROJECT_OUT_DOTS = D_MODEL // PROJ_COLS
HEAD_STAGES_PER_CHUNK = 2 * (SWA_HEADS // 2 + MEM_HEADS // 2)
VMEM_LIMIT_BYTES = 62 * 1024 * 1024

BF16 = jnp.bfloat16
F32 = jnp.float32


def _t5_causal_buckets(dist):
    n = np.maximum(dist, 0)
    max_exact = N_BUCKETS // 2
    large = max_exact + (np.log(np.maximum(n, 1) / max_exact) / np.log(MAX_DISTANCE / max_exact)
                         * (N_BUCKETS - max_exact)).astype(np.int32)
    large = np.minimum(large, N_BUCKETS - 1)
    return np.where(n < max_exact, n, large).astype(np.int32)


def _band_buckets():
    qi = np.arange(CHUNK)[:, None]
    kj = np.arange(2 * CHUNK)[None, :]
    dist = qi + CHUNK - kj
    valid = (dist >= 0) & (dist < WINDOW)
    return np.where(valid, _t5_causal_buckets(dist), -1).astype(np.int32)


def _gelu_tanh(x):
    c = np.float32(np.sqrt(2.0 / np.pi))
    ck = np.float32(np.sqrt(2.0 / np.pi) * 0.044715)
    hx = 0.5 * x
    return hx + hx * jnp.tanh(x * (c + ck * (x * x)))


def _silu(z):
    hz = 0.5 * z
    return hz + hz * jnp.tanh(hz)


def _dot(a, b):
    return jnp.dot(a, b, preferred_element_type=F32)


_DONE = object()


def _skewed(items, lag):
    waiting = []
    for item in items:
        if next(item, _DONE) is not _DONE:
            waiting.append(item)
        yield
        if len(waiting) > lag:
            next(waiting.pop(0), _DONE)
            yield
    for item in waiting:
        next(item, _DONE)
        yield


def _chain(*gens):
    for g in gens:
        yield from g


def _interleave(a, b, b_per_a):
    credit = 0.0
    a_live = b_live = True
    while a_live or b_live:
        if a_live:
            a_live = next(a, _DONE) is not _DONE
        credit += b_per_a
        while b_live and (credit >= 1.0 or not a_live):
            b_live = next(b, _DONE) is not _DONE
            credit -= 1.0


def _lane_half_mask(parity):
    lane = lax.broadcasted_iota(jnp.int32, (1, LANES), 1)
    return (lane >= HEAD_DIM) if parity else (lane < HEAD_DIM)


def _place_head_rows(kt_head, parity):
    zeros = jnp.zeros_like(kt_head)
    return jnp.concatenate([zeros, kt_head] if parity else [kt_head, zeros], axis=0)


def _layer_kernel(sinks_ref,
                  x_ref, xnext_ref, mem_ref, win_ref, wkv_f32_ref, wout_f32_ref, preg_ref, memg_ref, postg_ref,
                  vng_ref, vnb_ref, ws_ref, bs_ref, relb_ref, buckets_ref,
                  out_ref,
                  h_scr, gu_scr, vn_scr, qs_scr, ktvar_scr, vvar_scr, zg_scr,
                  mk_scr, mv_scr, bias_scr, wsm_scr, bst_scr, wkv_ref, wout_ref,
                  y_scr, ybf_scr, o_scr):
    ts = x_ref.shape[1]
    b = pl.program_id(0)
    t = pl.program_id(1)

    @pl.when((b == 0) & (t == 0))
    def _init():
        row = lax.broadcasted_iota(jnp.int32, (CHUNK, CHUNK), 0)
        col = lax.broadcasted_iota(jnp.int32, (CHUNK, CHUNK), 1)
        for g in range(A_GROUPS):
            wsm_scr[g] = jnp.where(row >= col, ws_ref[g], 0.0).astype(BF16)
            bs_col = jnp.sum(jnp.where(row == col, bs_ref[g:g + 1, :], 0.0), axis=1, keepdims=True)
            bst_scr[g] = jnp.broadcast_to(bs_col, (CHUNK, LANES))
        for r0 in range(0, D_MODEL, 2 * CHUNK):
            wrows = pl.ds(r0, 2 * CHUNK)
            wkv_ref[wrows, :] = wkv_f32_ref[wrows, :].astype(BF16)
            wout_ref[wrows, :] = wout_f32_ref[wrows, :].astype(BF16)
        buckets = buckets_ref[...]
        prev_key = lax.broadcasted_iota(jnp.int32, (CHUNK, 2 * CHUNK), 1) < CHUNK
        for hd in range(SWA_HEADS):
            tbl = jnp.full((CHUNK, 2 * CHUNK), NEG, F32)
            for bk in range(N_BUCKETS):
                tbl = jnp.where(buckets == bk, relb_ref[bk:bk + 1, hd:hd + 1] * LOG2E, tbl)
            bias_scr[0, hd] = tbl
            bias_scr[1, hd] = jnp.where(prev_key, NEG, tbl)

    @pl.when(t == 0)
    def _sequence_start():
        ktvar_scr[:, :, 0:CHUNK] = jnp.zeros((SWA_HEADS, LANES, CHUNK), BF16)
        vvar_scr[:, 0:CHUNK, :] = jnp.zeros((SWA_HEADS, CHUNK, LANES), BF16)
        m = mem_ref[0]
        ms = jnp.mean(m * m, axis=-1, keepdims=True)
        hm = (m * lax.rsqrt(ms + EPS) * memg_ref[...]).astype(BF16)
        kv = _dot(hm, wkv_ref[...])
        for pair in range(MEM_HEADS // 2):
            kt_pair = kv[:, pair * LANES:(pair + 1) * LANES].T
            v_pair = kv[:, MEM_WIDTH + pair * LANES:MEM_WIDTH + (pair + 1) * LANES]
            for parity in range(2):
                hd = 2 * pair + parity
                kt_head = kt_pair[parity * HEAD_DIM:(parity + 1) * HEAD_DIM]
                mk_scr[hd] = _place_head_rows(kt_head, parity).astype(BF16)
                mv_scr[hd] = jnp.where(_lane_half_mask(parity), v_pair, 0.0).astype(BF16)

    @pl.when(t > 0)
    def _copy_carry():
        ktvar_scr[:, :, 0:CHUNK] = ktvar_scr[:, :, ts:ts + CHUNK]
        vvar_scr[:, 0:CHUNK, :] = vvar_scr[:, ts:ts + CHUNK, :]

    lo = _lane_half_mask(0)
    hi = _lane_half_mask(1)
    first_tile = jnp.where(t == 0, 1, 0)

    def pre_norm(r0, nrows, src_ref=x_ref):
        for rc in range(nrows // CHUNK):
            crows = pl.ds(r0 + rc * CHUNK, CHUNK)
            x = src_ref[0, crows, :]
            ms = jnp.mean(x * x, axis=-1, keepdims=True)
            h_scr[crows, :] = (x * lax.rsqrt(ms + EPS) * preg_ref[...]).astype(BF16)

    def proj(rows, c0, width):
        return _dot(h_scr[rows, :], win_ref[:, c0:c0 + width])

    def project(r0, nrows):
        rows = pl.ds(r0, nrows)

        v_all = _gelu_tanh(proj(rows, OFF_V, A_WIDTH))
        for g in range(A_GROUPS):
            cols = pl.ds(g * LANES, LANES)
            v = v_all[:, g * LANES:(g + 1) * LANES]
            mu = jnp.mean(v, axis=-1, keepdims=True)
            d = v - mu
            var = jnp.mean(d * d, axis=-1, keepdims=True)
            vn = d * lax.rsqrt(var + EPS) * vng_ref[:, cols] + vnb_ref[:, cols]
            vn_scr[rows, cols] = vn.astype(BF16)
        yield

        gu_scr[rows, :] = _gelu_tanh(proj(rows, OFF_U, A_WIDTH))
        yield

        qkv = proj(rows, OFF_SQ, SWA_WIDTH + 2 * SWA_KV_WIDTH)
        qs_scr[rows, 0:SWA_WIDTH] = (qkv[:, 0:SWA_WIDTH] * Q_SCALE).astype(BF16)
        skt = qkv[:, SWA_WIDTH:SWA_WIDTH + LANES].T
        sv = qkv[:, SWA_WIDTH + LANES:]
        sv_rot = pltpu.roll(sv, HEAD_DIM, 1)
        new_rows = pl.ds(CHUNK + r0, nrows)
        for hd, (keep, vsrc) in enumerate([(lo, sv), (hi, sv_rot), (lo, sv_rot), (hi, sv)]):
            kv_head = hd // 2
            kt_head = skt[kv_head * HEAD_DIM:(kv_head + 1) * HEAD_DIM]
            ktvar_scr[hd, :, new_rows] = _place_head_rows(kt_head, hd % 2).astype(BF16)
            vvar_scr[hd, new_rows, :] = jnp.where(keep, vsrc, 0.0).astype(BF16)
        yield

        mqz = proj(rows, OFF_MQ, PROJ_COLS)
        qs_scr[rows, SWA_WIDTH:] = (mqz[:, 0:MEM_WIDTH] * Q_SCALE).astype(BF16)
        zg_scr[rows, 0:PROJ_COLS - MEM_WIDTH] = _silu(mqz[:, MEM_WIDTH:])
        yield

        for c in range(r0 // CHUNK, (r0 + nrows) // CHUNK):
            for g in range(A_GROUPS):
                spatial_gating(c, g)
            yield

        done = PROJ_COLS - MEM_WIDTH
        while done < MIX_WIDTH:
            width = min(PROJ_COLS, MIX_WIDTH - done)
            zg_scr[rows, done:done + width] = _silu(proj(rows, OFF_Z + done, width))
            done += width
            yield

    def spatial_gating(c, g):
        rows = pl.ds(c * CHUNK, CHUNK)
        cols = pl.ds(g * LANES, LANES)
        sv_g = _dot(wsm_scr[g], vn_scr[rows, cols]) + bst_scr[g]
        y_scr[rows, cols] = gu_scr[rows, cols] * sv_g

    def attn_item(c, pair, sliding):
        r0 = c * CHUNK
        rows = pl.ds(r0, CHUNK)
        heads = (2 * pair, 2 * pair + 1)
        if sliding:
            band = pl.ds(r0, 2 * CHUNK)
            qp = qs_scr[rows, pl.ds(pair * LANES, LANES)]
            keys = jnp.concatenate([ktvar_scr[hd, :, band] for hd in heads], axis=1)
            bias_sel = first_tile if c == 0 else 0
            bias = jnp.concatenate([bias_scr[bias_sel, hd] for hd in heads], axis=1)
            logits = _dot(qp, keys) + bias
        else:
            qp = qs_scr[rows, pl.ds(SWA_WIDTH + pair * LANES, LANES)]
            keys = jnp.concatenate([mk_scr[hd] for hd in heads], axis=1)
            logits = _dot(qp, keys)
        n_keys = logits.shape[1] // 2
        yield
        probs, recips = [], []
        for i, hd in enumerate(heads):
            s = logits[:, i * n_keys:(i + 1) * n_keys]
            m = jnp.max(s, axis=-1, keepdims=True)
            if sliding:
                sink = sinks_ref[hd] * LOG2E
                m = jnp.maximum(m, sink)
            p = jnp.exp2(s - m)
            denom = jnp.sum(p, axis=-1, keepdims=True)
            if sliding:
                denom = denom + jnp.exp2(sink - m)
            probs.append(p.astype(BF16))
            recips.append(1.0 / denom)
        if sliding:
            values = jnp.concatenate([vvar_scr[hd, band, :] for hd in heads], axis=0)
        else:
            values = jnp.concatenate([mv_scr[hd] for hd in heads], axis=0)
        o_pair = _dot(jnp.concatenate(probs, axis=1), values)
        o_pair = o_pair * jnp.where(lo, recips[0], recips[1])
        off = A_WIDTH if sliding else A_WIDTH + SWA_WIDTH
        y_scr[rows, pl.ds(off + pair * LANES, LANES)] = o_pair

    def head_items(r0, nrows):
        for c in range(r0 // CHUNK, (r0 + nrows) // CHUNK):
            for pair in range(SWA_HEADS // 2):
                yield attn_item(c, pair, True)
            for pair in range(MEM_HEADS // 2):
                yield attn_item(c, pair, False)

    def project_out(r0, nrows):
        rows = pl.ds(r0, nrows)
        ybf_scr[rows, :] = (y_scr[rows, :] * zg_scr[rows, :]).astype(BF16)
        for c0 in range(0, D_MODEL, PROJ_COLS):
            o_scr[rows, c0:c0 + PROJ_COLS] = _dot(ybf_scr[rows, :], wout_ref[:, c0:c0 + PROJ_COLS])
            yield
        for rc in range(nrows // CHUNK):
            crows = pl.ds(r0 + rc * CHUNK, CHUNK)
            o = o_scr[crows, :]
            ms = jnp.mean(o * o, axis=-1, keepdims=True)
            out_ref[0, crows, :] = x_ref[0, crows, :] + o * lax.rsqrt(ms + EPS) * postg_ref[...]
            yield

    assert sum(SUB_TILES) == ts
    starts = [sum(SUB_TILES[:s]) for s in range(len(SUB_TILES))]
    subs = list(zip(starts, SUB_TILES))
    n_sub = len(subs)

    @pl.when((b == 0) & (t == 0))
    def _first_projection():
        pre_norm(*subs[0])
        for _ in project(*subs[0]):
            pass

    for s in range(n_sub):
        nxt = subs[(s + 1) % n_sub]
        pre_norm(*nxt, src_ref=x_ref if s + 1 < n_sub else xnext_ref)
        heads = _skewed(head_items(*subs[s]), ITEM_LAG)
        filler = project(*nxt)
        n_filler = PROJECT_COLUMN_PIECES + nxt[1] // CHUNK
        if s >= 1:
            filler = _chain(project_out(*subs[s - 1]), filler)
            n_filler += PROJECT_OUT_DOTS + subs[s - 1][1] // CHUNK
        _interleave(heads, filler, n_filler / ((subs[s][1] // CHUNK) * HEAD_STAGES_PER_CHUNK))
    for _ in project_out(*subs[-1]):
        pass


def _layer(x, mem, w_in_bf16, w_kv, w_out, pre_g, mem_g, post_g, vng, vnb, w_spatial,
           b_spatial, buckets, sinks, rel_bias):
    batch, seq, _ = x.shape
    ts = SEQ_TILE
    n_tiles = seq // ts
    const2 = lambda b, t, *_: (0, 0)
    const3 = lambda b, t, *_: (0, 0, 0)

    def next_first_sub_tile(b, t, *_):
        step = jnp.minimum(b * n_tiles + t + 1, batch * n_tiles - 1)
        return (step // n_tiles, (step % n_tiles) * (ts // SUB_TILES[0]), 0)

    grid_spec = pltpu.PrefetchScalarGridSpec(
        num_scalar_prefetch=1,
        grid=(batch, n_tiles),
        in_specs=[
            pl.BlockSpec((1, ts, D_MODEL), lambda b, t, *_: (b, t, 0)),
            pl.BlockSpec((1, SUB_TILES[0], D_MODEL), next_first_sub_tile),
            pl.BlockSpec((1, MEM_LEN, D_MODEL), lambda b, t, *_: (b, 0, 0)),
            pl.BlockSpec((D_MODEL, IN_WIDTH), const2),
            pl.BlockSpec((D_MODEL, 2 * MEM_WIDTH), const2),
            pl.BlockSpec((MIX_WIDTH, D_MODEL), const2),
            pl.BlockSpec((1, D_MODEL), const2),
            pl.BlockSpec((1, D_MODEL), const2),
            pl.BlockSpec((1, D_MODEL), const2),
            pl.BlockSpec((1, A_WIDTH), const2),
            pl.BlockSpec((1, A_WIDTH), const2),
            pl.BlockSpec((A_GROUPS, CHUNK, CHUNK), const3),
            pl.BlockSpec((A_GROUPS, CHUNK), const2),
            pl.BlockSpec((N_BUCKETS, SWA_HEADS), const2),
            pl.BlockSpec((CHUNK, 2 * CHUNK), const2),
        ],
        out_specs=pl.BlockSpec((1, ts, D_MODEL), lambda b, t, *_: (b, t, 0)),
        scratch_shapes=[
            pltpu.VMEM((ts, D_MODEL), BF16),
            pltpu.VMEM((ts, A_WIDTH), F32),
            pltpu.VMEM((ts, A_WIDTH), BF16),
            pltpu.VMEM((ts, SWA_WIDTH + MEM_WIDTH), BF16),
            pltpu.VMEM((SWA_HEADS, LANES, CHUNK + ts), BF16),
            pltpu.VMEM((SWA_HEADS, CHUNK + ts, LANES), BF16),
            pltpu.VMEM((ts, MIX_WIDTH), F32),
            pltpu.VMEM((MEM_HEADS, LANES, MEM_LEN), BF16),
            pltpu.VMEM((MEM_HEADS, MEM_LEN, LANES), BF16),
            pltpu.VMEM((2, SWA_HEADS, CHUNK, 2 * CHUNK), F32),
            pltpu.VMEM((A_GROUPS, CHUNK, CHUNK), BF16),
            pltpu.VMEM((A_GROUPS, CHUNK, LANES), F32),
            pltpu.VMEM((D_MODEL, 2 * MEM_WIDTH), BF16),
            pltpu.VMEM((MIX_WIDTH, D_MODEL), BF16),
            pltpu.VMEM((ts, MIX_WIDTH), F32),
            pltpu.VMEM((ts, MIX_WIDTH), BF16),
            pltpu.VMEM((ts, D_MODEL), F32),
        ],
    )
    return pl.pallas_call(
        _layer_kernel,
        grid_spec=grid_spec,
        out_shape=jax.ShapeDtypeStruct(x.shape, x.dtype),
        compiler_params=pltpu.CompilerParams(
            dimension_semantics=("arbitrary", "arbitrary"),
            vmem_limit_bytes=VMEM_LIMIT_BYTES),
        name="layer",
    )(sinks, x, x, mem, w_in_bf16, w_kv, w_out, pre_g, mem_g, post_g, vng, vnb,
      w_spatial, b_spatial, rel_bias, buckets)


def kernel(x, mem, pre_norm_g, post_norm_g, mem_norm_g, w_in, w_mem_kv, v_norm_g, v_norm_b,
           w_spatial, b_spatial, attn_sinks, rel_bias, w_out):
    depth = w_in.shape[0]
    buckets = jnp.asarray(_band_buckets())
    for layer in range(depth):
        x = _layer(x, mem, w_in[layer].astype(BF16), w_mem_kv[layer], w_out[layer],
                   pre_norm_g[layer][None, :], mem_norm_g[layer][None, :], post_norm_g[layer][None, :],
                   v_norm_g[layer][None, :], v_norm_b[layer][None, :],
                   w_spatial[layer], b_spatial[layer], buckets, attn_sinks[layer], rel_bias)
    return x
```

```python
import numpy as np
import jax
import jax.numpy as jnp
from jax import lax
from jax.experimental import pallas as pl
from jax.experimental.pallas import tpu as pltpu

D_MODEL = 1024
MEM_LEN = 256
HEAD_DIM = 64
CHUNK = 128
A_GROUPS = 4
A_WIDTH = 512
SWA_HEADS = 4
SWA_WIDTH = 256
SWA_KV_WIDTH = 128
MEM_HEADS = 4
MEM_WIDTH = 256
MIX_WIDTH = 1024
IN_WIDTH = 2816
N_BUCKETS = 32
MAX_DISTANCE = 128
WINDOW = 128
EPS = 1e-6
NEG = -1e30
LOG2E = float(np.log2(np.e))
Q_SCALE = HEAD_DIM ** -0.5 * LOG2E

OFF_U = 0
OFF_V = A_WIDTH
OFF_SQ = 2 * A_WIDTH
OFF_SK = OFF_SQ + SWA_WIDTH
OFF_SV = OFF_SK + SWA_KV_WIDTH
OFF_MQ = OFF_SV + SWA_KV_WIDTH
OFF_Z = OFF_MQ + MEM_WIDTH

LANES = 128
SEQ_TILE = 1024
SUB_TILES = (256, 256, 256, 256)
PROJ_COLS = 512
ITEM_LAG = 2
PROJECT_COLUMN_PIECES = 6
PROJECT_OUT_DOTS = D_MODEL // PROJ_COLS
HEAD_STAGES_PER_CHUNK = 2 * (SWA_HEADS // 2 + MEM_HEADS // 2)
VMEM_LIMIT_BYTES = 62 * 1024 * 1024

BF16 = jnp.bfloat16
F32 = jnp.float32


def _t5_causal_buckets(dist):
    n = np.maximum(dist, 0)
    max_exact = N_BUCKETS // 2
    large = max_exact + (np.log(np.maximum(n, 1) / max_exact) / np.log(MAX_DISTANCE / max_exact)
                         * (N_BUCKETS - max_exact)).astype(np.int32)
    large = np.minimum(large, N_BUCKETS - 1)
    return np.where(n < max_exact, n, large).astype(np.int32)


def _band_buckets():
    qi = np.arange(CHUNK)[:, None]
    kj = np.arange(2 * CHUNK)[None, :]
    dist = qi + CHUNK - kj
    valid = (dist >= 0) & (dist < WINDOW)
    return np.where(valid, _t5_causal_buckets(dist), -1).astype(np.int32)


def _gelu_tanh(x):
    c = np.float32(np.sqrt(2.0 / np.pi))
    ck = np.float32(np.sqrt(2.0 / np.pi) * 0.044715)
    hx = 0.5 * x
    return hx + hx * jnp.tanh(x * (c + ck * (x * x)))


def _silu(z):
    hz = 0.5 * z
    return hz + hz * jnp.tanh(hz)


def _dot(a, b):
    return jnp.dot(a, b, preferred_element_type=F32)


_DONE = object()


def _skewed(items, lag):
    waiting = []
    for item in items:
        if next(item, _DONE) is not _DONE:
            waiting.append(item)
        yield
        if len(waiting) > lag:
            next(waiting.pop(0), _DONE)
            yield
    for item in waiting:
        next(item, _DONE)
        yield


def _chain(*gens):
    for g in gens:
        yield from g


def _interleave(a, b, b_per_a):
    credit = 0.0
    a_live = b_live = True
    while a_live or b_live:
        if a_live:
            a_live = next(a, _DONE) is not _DONE
        credit += b_per_a
        while b_live and (credit >= 1.0 or not a_live):
            b_live = next(b, _DONE) is not _DONE
            credit -= 1.0


def _lane_half_mask(parity):
    lane = lax.broadcasted_iota(jnp.int32, (1, LANES), 1)
    return (lane >= HEAD_DIM) if parity else (lane < HEAD_DIM)


def _place_head_rows(kt_head, parity):
    zeros = jnp.zeros_like(kt_head)
    return jnp.concatenate([zeros, kt_head] if parity else [kt_head, zeros], axis=0)


def _layer_kernel(sinks_ref,
                  x_ref, xnext_ref, mem_ref, win_ref, wkv_f32_ref, wout_f32_ref, preg_ref, memg_ref, postg_ref,
                  vng_ref, vnb_ref, ws_ref, bs_ref, relb_ref, buckets_ref,
                  out_ref,
                  h_scr, gu_scr, vn_scr, qs_scr, ktvar_scr, vvar_scr, zg_scr,
                  mk_scr, mv_scr, bias_scr, wsm_scr, bst_scr, wkv_ref, wout_ref,
                  y_scr, ybf_scr, o_scr):
    ts = x_ref.shape[1]
    b = pl.program_id(0)
    t = pl.program_id(1)

    @pl.when((b == 0) & (t == 0))
    def _init():
        row = lax.broadcasted_iota(jnp.int32, (CHUNK, CHUNK), 0)
        col = lax.broadcasted_iota(jnp.int32, (CHUNK, CHUNK), 1)
        for g in range(A_GROUPS):
            wsm_scr[g] = jnp.where(row >= col, ws_ref[g], 0.0).astype(BF16)
            bs_col = jnp.sum(jnp.where(row == col, bs_ref[g:g + 1, :], 0.0), axis=1, keepdims=True)
            bst_scr[g] = jnp.broadcast_to(bs_col, (CHUNK, LANES))
        for r0 in range(0, D_MODEL, 2 * CHUNK):
            wrows = pl.ds(r0, 2 * CHUNK)
            wkv_ref[wrows, :] = wkv_f32_ref[wrows, :].astype(BF16)
            wout_ref[wrows, :] = wout_f32_ref[wrows, :].astype(BF16)
        buckets = buckets_ref[...]
        prev_key = lax.broadcasted_iota(jnp.int32, (CHUNK, 2 * CHUNK), 1) < CHUNK
        for hd in range(SWA_HEADS):
            tbl = jnp.full((CHUNK, 2 * CHUNK), NEG, F32)
            for bk in range(N_BUCKETS):
                tbl = jnp.where(buckets == bk, relb_ref[bk:bk + 1, hd:hd + 1] * LOG2E, tbl)
            bias_scr[0, hd] = tbl
            bias_scr[1, hd] = jnp.where(prev_key, NEG, tbl)

    @pl.when(t == 0)
    def _sequence_start():
        ktvar_scr[:, :, 0:CHUNK] = jnp.zeros((SWA_HEADS, LANES, CHUNK), BF16)
        vvar_scr[:, 0:CHUNK, :] = jnp.zeros((SWA_HEADS, CHUNK, LANES), BF16)
        m = mem_ref[0]
        ms = jnp.mean(m * m, axis=-1, keepdims=True)
        hm = (m * lax.rsqrt(ms + EPS) * memg_ref[...]).astype(BF16)
        kv = _dot(hm, wkv_ref[...])
        for pair in range(MEM_HEADS // 2):
            kt_pair = kv[:, pair * LANES:(pair + 1) * LANES].T
            v_pair = kv[:, MEM_WIDTH + pair * LANES:MEM_WIDTH + (pair + 1) * LANES]
            for parity in range(2):
                hd = 2 * pair + parity
                kt_head = kt_pair[parity * HEAD_DIM:(parity + 1) * HEAD_DIM]
                mk_scr[hd] = _place_head_rows(kt_head, parity).astype(BF16)
                mv_scr[hd] = jnp.where(_lane_half_mask(parity), v_pair, 0.0).astype(BF16)

    @pl.when(t > 0)
    def _copy_carry():
        ktvar_scr[:, :, 0:CHUNK] = ktvar_scr[:, :, ts:ts + CHUNK]
        vvar_scr[:, 0:CHUNK, :] = vvar_scr[:, ts:ts + CHUNK, :]

    lo = _lane_half_mask(0)
    hi = _lane_half_mask(1)
    first_tile = jnp.where(t == 0, 1, 0)

    def pre_norm(r0, nrows, src_ref=x_ref):
        for rc in range(nrows // CHUNK):
            crows = pl.ds(r0 + rc * CHUNK, CHUNK)
            x = src_ref[0, crows, :]
            ms = jnp.mean(x * x, axis=-1, keepdims=True)
            h_scr[crows, :] = (x * lax.rsqrt(ms + EPS) * preg_ref[...]).astype(BF16)

    def proj(rows, c0, width):
        return _dot(h_scr[rows, :], win_ref[:, c0:c0 + width])

    def project(r0, nrows):
        rows = pl.ds(r0, nrows)

        v_all = _gelu_tanh(proj(rows, OFF_V, A_WIDTH))
        for g in range(A_GROUPS):
            cols = pl.ds(g * LANES, LANES)
            v = v_all[:, g * LANES:(g + 1) * LANES]
            mu = jnp.mean(v, axis=-1, keepdims=True)
            d = v - mu
            var = jnp.mean(d * d, axis=-1, keepdims=True)
            vn = d * lax.rsqrt(var + EPS) * vng_ref[:, cols] + vnb_ref[:, cols]
            vn_scr[rows, cols] = vn.astype(BF16)
        yield

        gu_scr[rows, :] = _gelu_tanh(proj(rows, OFF_U, A_WIDTH))
        yield

        qkv = proj(rows, OFF_SQ, SWA_WIDTH + 2 * SWA_KV_WIDTH)
        qs_scr[rows, 0:SWA_WIDTH] = (qkv[:, 0:SWA_WIDTH] * Q_SCALE).astype(BF16)
        skt = qkv[:, SWA_WIDTH:SWA_WIDTH + LANES].T
        sv = qkv[:, SWA_WIDTH + LANES:]
        sv_rot = pltpu.roll(sv, HEAD_DIM, 1)
        new_rows = pl.ds(CHUNK + r0, nrows)
        for hd, (keep, vsrc) in enumerate([(lo, sv), (hi, sv_rot), (lo, sv_rot), (hi, sv)]):
            kv_head = hd // 2
            kt_head = skt[kv_head * HEAD_DIM:(kv_head + 1) * HEAD_DIM]
            ktvar_scr[hd, :, new_rows] = _place_head_rows(kt_head, hd % 2).astype(BF16)
            vvar_scr[hd, new_rows, :] = jnp.where(keep, vsrc, 0.0).astype(BF16)
        yield

        mqz = proj(rows, OFF_MQ, PROJ_COLS)
        qs_scr[rows, SWA_WIDTH:] = (mqz[:, 0:MEM_WIDTH] * Q_SCALE).astype(BF16)
        zg_scr[rows, 0:PROJ_COLS - MEM_WIDTH] = _silu(mqz[:, MEM_WIDTH:])
        yield

        for c in range(r0 // CHUNK, (r0 + nrows) // CHUNK):
            for g in range(A_GROUPS):
                spatial_gating(c, g)
            yield

        done = PROJ_COLS - MEM_WIDTH
        while done < MIX_WIDTH:
            width = min(PROJ_COLS, MIX_WIDTH - done)
            zg_scr[rows, done:done + width] = _silu(proj(rows, OFF_Z + done, width))
            done += width
            yield

    def spatial_gating(c, g):
        rows = pl.ds(c * CHUNK, CHUNK)
        cols = pl.ds(g * LANES, LANES)
        sv_g = _dot(wsm_scr[g], vn_scr[rows, cols]) + bst_scr[g]
        y_scr[rows, cols] = gu_scr[rows, cols] * sv_g

    def attn_item(c, pair, sliding):
        r0 = c * CHUNK
        rows = pl.ds(r0, CHUNK)
        heads = (2 * pair, 2 * pair + 1)
        if sliding:
            band = pl.ds(r0, 2 * CHUNK)
            qp = qs_scr[rows, pl.ds(pair * LANES, LANES)]
            keys = jnp.concatenate([ktvar_scr[hd, :, band] for hd in heads], axis=1)
            bias_sel = first_tile if c == 0 else 0
            bias = jnp.concatenate([bias_scr[bias_sel, hd] for hd in heads], axis=1)
            logits = _dot(qp, keys) + bias
        else:
            qp = qs_scr[rows, pl.ds(SWA_WIDTH + pair * LANES, LANES)]
            keys = jnp.concatenate([mk_scr[hd] for hd in heads], axis=1)
            logits = _dot(qp, keys)
        n_keys = logits.shape[1] // 2
        yield
        probs, recips = [], []
        for i, hd in enumerate(heads):
            s = logits[:, i * n_keys:(i + 1) * n_keys]
            m = jnp.max(s, axis=-1, keepdims=True)
            if sliding:
                sink = sinks_ref[hd] * LOG2E
                m = jnp.maximum(m, sink)
            p = jnp.exp2(s - m)
            denom = jnp.sum(p, axis=-1, keepdims=True)
            if sliding:
                denom = denom + jnp.exp2(sink - m)
            probs.append(p.astype(BF16))
            recips.append(1.0 / denom)
        if sliding:
            values = jnp.concatenate([vvar_scr[hd, band, :] for hd in heads], axis=0)
        else:
            values = jnp.concatenate([mv_scr[hd] for hd in heads], axis=0)
        o_pair = _dot(jnp.concatenate(probs, axis=1), values)
        o_pair = o_pair * jnp.where(lo, recips[0], recips[1])
        off = A_WIDTH if sliding else A_WIDTH + SWA_WIDTH
        y_scr[rows, pl.ds(off + pair * LANES, LANES)] = o_pair

    def head_items(r0, nrows):
        for c in range(r0 // CHUNK, (r0 + nrows) // CHUNK):
            for pair in range(SWA_HEADS // 2):
                yield attn_item(c, pair, True)
            for pair in range(MEM_HEADS // 2):
                yield attn_item(c, pair, False)

    def project_out(r0, nrows, k0=0, k1=MIX_WIDTH):
        rows = pl.ds(r0, nrows)
        ybf_scr[rows, k0:k1] = (y_scr[rows, k0:k1] * zg_scr[rows, k0:k1]).astype(BF16)
        for c0 in range(0, D_MODEL, PROJ_COLS):
            cols = pl.ds(c0, PROJ_COLS)
            part = _dot(ybf_scr[rows, k0:k1], wout_ref[k0:k1, cols])
            o_scr[rows, cols] = o_scr[rows, cols] + part if k0 else part
            yield
        if k1 < MIX_WIDTH:
            return
        for rc in range(nrows // CHUNK):
            crows = pl.ds(r0 + rc * CHUNK, CHUNK)
            o = o_scr[crows, :]
            ms = jnp.mean(o * o, axis=-1, keepdims=True)
            out_ref[0, crows, :] = x_ref[0, crows, :] + o * lax.rsqrt(ms + EPS) * postg_ref[...]
            yield

    assert sum(SUB_TILES) == ts
    starts = [sum(SUB_TILES[:s]) for s in range(len(SUB_TILES))]
    subs = list(zip(starts, SUB_TILES))
    n_sub = len(subs)

    @pl.when((b == 0) & (t == 0))
    def _first_projection():
        pre_norm(*subs[0])
        for _ in project(*subs[0]):
            pass

    for s in range(n_sub):
        nxt = subs[(s + 1) % n_sub]
        pre_norm(*nxt, src_ref=x_ref if s + 1 < n_sub else xnext_ref)
        heads = _skewed(head_items(*subs[s]), ITEM_LAG)
        filler = project(*nxt)
        n_filler = PROJECT_COLUMN_PIECES + nxt[1] // CHUNK
        if s + 1 == n_sub:
            filler = _chain(project_out(*subs[s], 0, A_WIDTH), filler)
            n_filler += PROJECT_OUT_DOTS
        if s >= 1:
            filler = _chain(project_out(*subs[s - 1]), filler)
            n_filler += PROJECT_OUT_DOTS + subs[s - 1][1] // CHUNK
        _interleave(heads, filler, n_filler / ((subs[s][1] // CHUNK) * HEAD_STAGES_PER_CHUNK))
    for _ in project_out(*subs[-1], A_WIDTH, MIX_WIDTH):
        pass


def _layer(x, mem, w_in_bf16, w_kv, w_out, pre_g, mem_g, post_g, vng, vnb, w_spatial,
           b_spatial, buckets, sinks, rel_bias):
    batch, seq, _ = x.shape
    ts = SEQ_TILE
    n_tiles = seq // ts
    const2 = lambda b, t, *_: (0, 0)
    const3 = lambda b, t, *_: (0, 0, 0)

    def next_first_sub_tile(b, t, *_):
        step = jnp.minimum(b * n_tiles + t + 1, batch * n_tiles - 1)
        return (step // n_tiles, (step % n_tiles) * (ts // SUB_TILES[0]), 0)

    grid_spec = pltpu.PrefetchScalarGridSpec(
        num_scalar_prefetch=1,
        grid=(batch, n_tiles),
        in_specs=[
            pl.BlockSpec((1, ts, D_MODEL), lambda b, t, *_: (b, t, 0)),
            pl.BlockSpec((1, SUB_TILES[0], D_MODEL), next_first_sub_tile),
            pl.BlockSpec((1, MEM_LEN, D_MODEL), lambda b, t, *_: (b, 0, 0)),
            pl.BlockSpec((D_MODEL, IN_WIDTH), const2),
            pl.BlockSpec((D_MODEL, 2 * MEM_WIDTH), const2),
            pl.BlockSpec((MIX_WIDTH, D_MODEL), const2),
            pl.BlockSpec((1, D_MODEL), const2),
            pl.BlockSpec((1, D_MODEL), const2),
            pl.BlockSpec((1, D_MODEL), const2),
            pl.BlockSpec((1, A_WIDTH), const2),
            pl.BlockSpec((1, A_WIDTH), const2),
            pl.BlockSpec((A_GROUPS, CHUNK, CHUNK), const3),
            pl.BlockSpec((A_GROUPS, CHUNK), const2),
            pl.BlockSpec((N_BUCKETS, SWA_HEADS), const2),
            pl.BlockSpec((CHUNK, 2 * CHUNK), const2),
        ],
        out_specs=pl.BlockSpec((1, ts, D_MODEL), lambda b, t, *_: (b, t, 0)),
        scratch_shapes=[
            pltpu.VMEM((ts, D_MODEL), BF16),
            pltpu.VMEM((ts, A_WIDTH), F32),
            pltpu.VMEM((ts, A_WIDTH), BF16),
            pltpu.VMEM((ts, SWA_WIDTH + MEM_WIDTH), BF16),
            pltpu.VMEM((SWA_HEADS, LANES, CHUNK + ts), BF16),
            pltpu.VMEM((SWA_HEADS, CHUNK + ts, LANES), BF16),
            pltpu.VMEM((ts, MIX_WIDTH), F32),
            pltpu.VMEM((MEM_HEADS, LANES, MEM_LEN), BF16),
            pltpu.VMEM((MEM_HEADS, MEM_LEN, LANES), BF16),
            pltpu.VMEM((2, SWA_HEADS, CHUNK, 2 * CHUNK), F32),
            pltpu.VMEM((A_GROUPS, CHUNK, CHUNK), BF16),
            pltpu.VMEM((A_GROUPS, CHUNK, LANES), F32),
            pltpu.VMEM((D_MODEL, 2 * MEM_WIDTH), BF16),
            pltpu.VMEM((MIX_WIDTH, D_MODEL), BF16),
            pltpu.VMEM((ts, MIX_WIDTH), F32),
            pltpu.VMEM((ts, MIX_WIDTH), BF16),
            pltpu.VMEM((ts, D_MODEL), F32),
        ],
    )
    return pl.pallas_call(
        _layer_kernel,
        grid_spec=grid_spec,
        out_shape=jax.ShapeDtypeStruct(x.shape, x.dtype),
        compiler_params=pltpu.CompilerParams(
            dimension_semantics=("arbitrary", "arbitrary"),
            vmem_limit_bytes=VMEM_LIMIT_BYTES),
        name="layer",
    )(sinks, x, x, mem, w_in_bf16, w_kv, w_out, pre_g, mem_g, post_g, vng, vnb,
      w_spatial, b_spatial, rel_bias, buckets)


def kernel(x, mem, pre_norm_g, post_norm_g, mem_norm_g, w_in, w_mem_kv, v_norm_g, v_norm_b,
           w_spatial, b_spatial, attn_sinks, rel_bias, w_out):
    depth = w_in.shape[0]
    buckets = jnp.asarray(_band_buckets())
    for layer in range(depth):
        x = _layer(x, mem, w_in[layer].astype(BF16), w_mem_kv[layer], w_out[layer],
                   pre_norm_g[layer][None, :], mem_norm_g[layer][None, :], post_norm_g[layer][None, :],
                   v_norm_g[layer][None, :], v_norm_b[layer][None, :],
                   w_spatial[layer], b_spatial[layer], buckets, attn_sinks[layer], rel_bias)
    return x
```

```python
import numpy as np
import jax
import jax.numpy as jnp
from jax import lax
from jax.experimental import pallas as pl
from jax.experimental.pallas import tpu as pltpu

D_MODEL = 1024
MEM_LEN = 256
HEAD_DIM = 64
CHUNK = 128
A_GROUPS = 4
A_WIDTH = 512
SWA_HEADS = 4
SWA_WIDTH = 256
SWA_KV_WIDTH = 128
MEM_HEADS = 4
MEM_WIDTH = 256
MIX_WIDTH = 1024
IN_WIDTH = 2816
N_BUCKETS = 32
MAX_DISTANCE = 128
WINDOW = 128
EPS = 1e-6
NEG = -1e30
LOG2E = float(np.log2(np.e))
Q_SCALE = HEAD_DIM ** -0.5 * LOG2E

OFF_U = 0
OFF_V = A_WIDTH
OFF_SQ = 2 * A_WIDTH
OFF_SK = OFF_SQ + SWA_WIDTH
OFF_SV = OFF_SK + SWA_KV_WIDTH
OFF_MQ = OFF_SV + SWA_KV_WIDTH
OFF_Z = OFF_MQ + MEM_WIDTH

LANES = 128
SEQ_TILE = 1024
SUB_TILES = (256, 256, 256, 256)
PROJ_COLS = 512
ITEM_LAG = 2
PROJECT_COLUMN_PIECES = 6
PROJECT_OUT_DOTS = D_MODEL // PROJ_COLS
HEAD_STAGES_PER_CHUNK = 2 * (SWA_HEADS // 2 + MEM_HEADS // 2)
VMEM_LIMIT_BYTES = 62 * 1024 * 1024

BF16 = jnp.bfloat16
F32 = jnp.float32


def _t5_causal_buckets(dist):
    n = np.maximum(dist, 0)
    max_exact = N_BUCKETS // 2
    large = max_exact + (np.log(np.maximum(n, 1) / max_exact) / np.log(MAX_DISTANCE / max_exact)
                         * (N_BUCKETS - max_exact)).astype(np.int32)
    large = np.minimum(large, N_BUCKETS - 1)
    return np.where(n < max_exact, n, large).astype(np.int32)


def _band_buckets():
    qi = np.arange(CHUNK)[:, None]
    kj = np.arange(2 * CHUNK)[None, :]
    dist = qi + CHUNK - kj
    valid = (dist >= 0) & (dist < WINDOW)
    return np.where(valid, _t5_causal_buckets(dist), -1).astype(np.int32)


def _gelu_tanh(x):
    c = np.float32(np.sqrt(2.0 / np.pi))
    ck = np.float32(np.sqrt(2.0 / np.pi) * 0.044715)
    hx = 0.5 * x
    return hx + hx * jnp.tanh(x * (c + ck * (x * x)))


def _silu(z):
    hz = 0.5 * z
    return hz + hz * jnp.tanh(hz)


def _dot(a, b):
    return jnp.dot(a, b, preferred_element_type=F32)


_DONE = object()


def _skewed(items, lag):
    waiting = []
    for item in items:
        if next(item, _DONE) is not _DONE:
            waiting.append(item)
        yield
        if len(waiting) > lag:
            next(waiting.pop(0), _DONE)
            yield
    for item in waiting:
        next(item, _DONE)
        yield


def _chain(*gens):
    for g in gens:
        yield from g


def _interleave(a, b, b_per_a):
    credit = 0.0
    a_live = b_live = True
    while a_live or b_live:
        if a_live:
            a_live = next(a, _DONE) is not _DONE
        credit += b_per_a
        while b_live and (credit >= 1.0 or not a_live):
            b_live = next(b, _DONE) is not _DONE
            credit -= 1.0


def _lane_half_mask(parity):
    lane = lax.broadcasted_iota(jnp.int32, (1, LANES), 1)
    return (lane >= HEAD_DIM) if parity else (lane < HEAD_DIM)


def _place_head_rows(kt_head, parity):
    zeros = jnp.zeros_like(kt_head)
    return jnp.concatenate([zeros, kt_head] if parity else [kt_head, zeros], axis=0)


def _layer_kernel(sinks_ref,
                  x_ref, xnext_ref, mem_ref, win_ref, wkv_f32_ref, wout_f32_ref, preg_ref, memg_ref, postg_ref,
                  vng_ref, vnb_ref, ws_ref, bs_ref, relb_ref, buckets_ref,
                  out_ref,
                  h_scr, gu_scr, vn_scr, qs_scr, ktvar_scr, vvar_scr, zg_scr,
                  mk_scr, mv_scr, bias_scr, wsm_scr, bst_scr, wkv_ref, wout_ref,
                  y_scr, ybf_scr, o_scr):
    ts = x_ref.shape[1]
    b = pl.program_id(0)
    t = pl.program_id(1)

    @pl.when((b == 0) & (t == 0))
    def _init():
        row = lax.broadcasted_iota(jnp.int32, (CHUNK, CHUNK), 0)
        col = lax.broadcasted_iota(jnp.int32, (CHUNK, CHUNK), 1)
        for g in range(A_GROUPS):
            wsm_scr[g] = jnp.where(row >= col, ws_ref[g], 0.0).astype(BF16)
            bs_col = jnp.sum(jnp.where(row == col, bs_ref[g:g + 1, :], 0.0), axis=1, keepdims=True)
            bst_scr[g] = jnp.broadcast_to(bs_col, (CHUNK, LANES))
        for r0 in range(0, D_MODEL, 2 * CHUNK):
            wrows = pl.ds(r0, 2 * CHUNK)
            wkv_ref[wrows, :] = wkv_f32_ref[wrows, :].astype(BF16)
            wout_ref[wrows, :] = wout_f32_ref[wrows, :].astype(BF16)
        buckets = buckets_ref[...]
        prev_key = lax.broadcasted_iota(jnp.int32, (CHUNK, 2 * CHUNK), 1) < CHUNK
        for hd in range(SWA_HEADS):
            tbl = jnp.full((CHUNK, 2 * CHUNK), NEG, F32)
            for bk in range(N_BUCKETS):
                tbl = jnp.where(buckets == bk, relb_ref[bk:bk + 1, hd:hd + 1] * LOG2E, tbl)
            bias_scr[0, hd] = tbl
            bias_scr[1, hd] = jnp.where(prev_key, NEG, tbl)

    @pl.when(t == 0)
    def _sequence_start():
        ktvar_scr[:, :, 0:CHUNK] = jnp.zeros((SWA_HEADS, LANES, CHUNK), BF16)
        vvar_scr[:, 0:CHUNK, :] = jnp.zeros((SWA_HEADS, CHUNK, LANES), BF16)
        m = mem_ref[0]
        ms = jnp.mean(m * m, axis=-1, keepdims=True)
        hm = (m * lax.rsqrt(ms + EPS) * memg_ref[...]).astype(BF16)
        kv = _dot(hm, wkv_ref[...])
        for pair in range(MEM_HEADS // 2):
            kt_pair = kv[:, pair * LANES:(pair + 1) * LANES].T
            v_pair = kv[:, MEM_WIDTH + pair * LANES:MEM_WIDTH + (pair + 1) * LANES]
            for parity in range(2):
                hd = 2 * pair + parity
                kt_head = kt_pair[parity * HEAD_DIM:(parity + 1) * HEAD_DIM]
                mk_scr[hd] = _place_head_rows(kt_head, parity).astype(BF16)
                mv_scr[hd] = jnp.where(_lane_half_mask(parity), v_pair, 0.0).astype(BF16)

    @pl.when(t > 0)
    def _copy_carry():
        ktvar_scr[:, :, 0:CHUNK] = ktvar_scr[:, :, ts:ts + CHUNK]
        vvar_scr[:, 0:CHUNK, :] = vvar_scr[:, ts:ts + CHUNK, :]

    lo = _lane_half_mask(0)
    hi = _lane_half_mask(1)
    first_tile = jnp.where(t == 0, 1, 0)

    def pre_norm(r0, nrows, src_ref=x_ref):
        for rc in range(nrows // CHUNK):
            crows = pl.ds(r0 + rc * CHUNK, CHUNK)
            x = src_ref[0, crows, :]
            ms = jnp.mean(x * x, axis=-1, keepdims=True)
            h_scr[crows, :] = (x * lax.rsqrt(ms + EPS) * preg_ref[...]).astype(BF16)

    def proj(rows, c0, width):
        return _dot(h_scr[rows, :], win_ref[:, c0:c0 + width])

    def project(r0, nrows):
        rows = pl.ds(r0, nrows)

        v_all = _gelu_tanh(proj(rows, OFF_V, A_WIDTH))
        for g in range(A_GROUPS):
            cols = pl.ds(g * LANES, LANES)
            v = v_all[:, g * LANES:(g + 1) * LANES]
            mu = jnp.mean(v, axis=-1, keepdims=True)
            d = v - mu
            var = jnp.mean(d * d, axis=-1, keepdims=True)
            vn = d * lax.rsqrt(var + EPS) * vng_ref[:, cols] + vnb_ref[:, cols]
            vn_scr[rows, cols] = vn.astype(BF16)
        yield

        gu_scr[rows, :] = _gelu_tanh(proj(rows, OFF_U, A_WIDTH))
        yield

        qkv = proj(rows, OFF_SQ, SWA_WIDTH + 2 * SWA_KV_WIDTH)
        qs_scr[rows, 0:SWA_WIDTH] = (qkv[:, 0:SWA_WIDTH] * Q_SCALE).astype(BF16)
        skt = qkv[:, SWA_WIDTH:SWA_WIDTH + LANES].T
        sv = qkv[:, SWA_WIDTH + LANES:]
        sv_rot = pltpu.roll(sv, HEAD_DIM, 1)
        new_rows = pl.ds(CHUNK + r0, nrows)
        for hd, (keep, vsrc) in enumerate([(lo, sv), (hi, sv_rot), (lo, sv_rot), (hi, sv)]):
            kv_head = hd // 2
            kt_head = skt[kv_head * HEAD_DIM:(kv_head + 1) * HEAD_DIM]
            ktvar_scr[hd, :, new_rows] = _place_head_rows(kt_head, hd % 2).astype(BF16)
            vvar_scr[hd, new_rows, :] = jnp.where(keep, vsrc, 0.0).astype(BF16)
        yield

        mqz = proj(rows, OFF_MQ, PROJ_COLS)
        qs_scr[rows, SWA_WIDTH:] = (mqz[:, 0:MEM_WIDTH] * Q_SCALE).astype(BF16)
        zg_scr[rows, 0:PROJ_COLS - MEM_WIDTH] = _silu(mqz[:, MEM_WIDTH:])
        yield

        for c in range(r0 // CHUNK, (r0 + nrows) // CHUNK):
            for g in range(A_GROUPS):
                spatial_gating(c, g)
            yield

        done = PROJ_COLS - MEM_WIDTH
        while done < MIX_WIDTH:
            width = min(PROJ_COLS, MIX_WIDTH - done)
            zg_scr[rows, done:done + width] = _silu(proj(rows, OFF_Z + done, width))
            done += width
            yield

    def spatial_gating(c, g):
        rows = pl.ds(c * CHUNK, CHUNK)
        cols = pl.ds(g * LANES, LANES)
        sv_g = _dot(wsm_scr[g], vn_scr[rows, cols]) + bst_scr[g]
        y_scr[rows, cols] = gu_scr[rows, cols] * sv_g

    def attn_item(c, pair, sliding):
        r0 = c * CHUNK
        rows = pl.ds(r0, CHUNK)
        heads = (2 * pair, 2 * pair + 1)
        if sliding:
            band = pl.ds(r0, 2 * CHUNK)
            qp = qs_scr[rows, pl.ds(pair * LANES, LANES)]
            keys = jnp.concatenate([ktvar_scr[hd, :, band] for hd in heads], axis=1)
            bias_sel = first_tile if c == 0 else 0
            bias = jnp.concatenate([bias_scr[bias_sel, hd] for hd in heads], axis=1)
            logits = _dot(qp, keys) + bias
        else:
            qp = qs_scr[rows, pl.ds(SWA_WIDTH + pair * LANES, LANES)]
            keys = jnp.concatenate([mk_scr[hd] for hd in heads], axis=1)
            logits = _dot(qp, keys)
        n_keys = logits.shape[1] // 2
        yield
        probs, recips = [], []
        for i, hd in enumerate(heads):
            s = logits[:, i * n_keys:(i + 1) * n_keys]
            m = jnp.max(s, axis=-1, keepdims=True)
            if sliding:
                sink = sinks_ref[hd] * LOG2E
                m = jnp.maximum(m, sink)
            p = jnp.exp2(s - m)
            denom = jnp.sum(p, axis=-1, keepdims=True)
            if sliding:
                denom = denom + jnp.exp2(sink - m)
            probs.append(p.astype(BF16))
            recips.append(1.0 / denom)
        if sliding:
            values = jnp.concatenate([vvar_scr[hd, band, :] for hd in heads], axis=0)
        else:
            values = jnp.concatenate([mv_scr[hd] for hd in heads], axis=0)
        o_pair = _dot(jnp.concatenate(probs, axis=1), values)
        o_pair = o_pair * jnp.where(lo, recips[0], recips[1])
        off = A_WIDTH if sliding else A_WIDTH + SWA_WIDTH
        y_scr[rows, pl.ds(off + pair * LANES, LANES)] = o_pair

    def head_items(r0, nrows):
        for c in range(r0 // CHUNK, (r0 + nrows) // CHUNK):
            for pair in range(SWA_HEADS // 2):
                yield attn_item(c, pair, True)
            for pair in range(MEM_HEADS // 2):
                yield attn_item(c, pair, False)

    def project_out(r0, nrows, k0=0, k1=MIX_WIDTH):
        rows = pl.ds(r0, nrows)
        ybf_scr[rows, k0:k1] = (y_scr[rows, k0:k1] * zg_scr[rows, k0:k1]).astype(BF16)
        for c0 in range(0, D_MODEL, PROJ_COLS):
            cols = pl.ds(c0, PROJ_COLS)
            part = _dot(ybf_scr[rows, k0:k1], wout_ref[k0:k1, cols])
            o_scr[rows, cols] = o_scr[rows, cols] + part if k0 else part
            yield
        if k1 < MIX_WIDTH:
            return
        for rc in range(nrows // CHUNK):
            crows = pl.ds(r0 + rc * CHUNK, CHUNK)
            o = o_scr[crows, :]
            ms = jnp.mean(o * o, axis=-1, keepdims=True)
            out_ref[0, crows, :] = x_ref[0, crows, :] + o * lax.rsqrt(ms + EPS) * postg_ref[...]
            yield

    assert sum(SUB_TILES) == ts
    starts = [sum(SUB_TILES[:s]) for s in range(len(SUB_TILES))]
    subs = list(zip(starts, SUB_TILES))
    n_sub = len(subs)

    @pl.when((b == 0) & (t == 0))
    def _first_projection():
        pre_norm(*subs[0])
        for _ in project(*subs[0]):
            pass

    for s in range(n_sub):
        heads = _skewed(head_items(*subs[s]), ITEM_LAG)
        if s + 1 < n_sub:
            pre_norm(*subs[s + 1])
            filler = project(*subs[s + 1])
            n_filler = PROJECT_COLUMN_PIECES + subs[s + 1][1] // CHUNK
        else:
            filler = project_out(*subs[s], 0, A_WIDTH)
            n_filler = PROJECT_OUT_DOTS
        if s >= 1:
            filler = _chain(project_out(*subs[s - 1]), filler)
            n_filler += PROJECT_OUT_DOTS + subs[s - 1][1] // CHUNK
        _interleave(heads, filler, n_filler / ((subs[s][1] // CHUNK) * HEAD_STAGES_PER_CHUNK))
    for _ in project_out(*subs[-1], A_WIDTH, MIX_WIDTH):
        pass
    pre_norm(*subs[0], src_ref=xnext_ref)
    for _ in project(*subs[0]):
        pass


def _layer(x, mem, w_in_bf16, w_kv, w_out, pre_g, mem_g, post_g, vng, vnb, w_spatial,
           b_spatial, buckets, sinks, rel_bias):
    batch, seq, _ = x.shape
    ts = SEQ_TILE
    n_tiles = seq // ts
    const2 = lambda b, t, *_: (0, 0)
    const3 = lambda b, t, *_: (0, 0, 0)

    def next_first_sub_tile(b, t, *_):
        step = jnp.minimum(b * n_tiles + t + 1, batch * n_tiles - 1)
        return (step // n_tiles, (step % n_tiles) * (ts // SUB_TILES[0]), 0)

    grid_spec = pltpu.PrefetchScalarGridSpec(
        num_scalar_prefetch=1,
        grid=(batch, n_tiles),
        in_specs=[
            pl.BlockSpec((1, ts, D_MODEL), lambda b, t, *_: (b, t, 0)),
            pl.BlockSpec((1, SUB_TILES[0], D_MODEL), next_first_sub_tile),
            pl.BlockSpec((1, MEM_LEN, D_MODEL), lambda b, t, *_: (b, 0, 0)),
            pl.BlockSpec((D_MODEL, IN_WIDTH), const2),
            pl.BlockSpec((D_MODEL, 2 * MEM_WIDTH), const2),
            pl.BlockSpec((MIX_WIDTH, D_MODEL), const2),
            pl.BlockSpec((1, D_MODEL), const2),
            pl.BlockSpec((1, D_MODEL), const2),
            pl.BlockSpec((1, D_MODEL), const2),
            pl.BlockSpec((1, A_WIDTH), const2),
            pl.BlockSpec((1, A_WIDTH), const2),
            pl.BlockSpec((A_GROUPS, CHUNK, CHUNK), const3),
            pl.BlockSpec((A_GROUPS, CHUNK), const2),
            pl.BlockSpec((N_BUCKETS, SWA_HEADS), const2),
            pl.BlockSpec((CHUNK, 2 * CHUNK), const2),
        ],
        out_specs=pl.BlockSpec((1, ts, D_MODEL), lambda b, t, *_: (b, t, 0)),
        scratch_shapes=[
            pltpu.VMEM((ts, D_MODEL), BF16),
            pltpu.VMEM((ts, A_WIDTH), F32),
            pltpu.VMEM((ts, A_WIDTH), BF16),
            pltpu.VMEM((ts, SWA_WIDTH + MEM_WIDTH), BF16),
            pltpu.VMEM((SWA_HEADS, LANES, CHUNK + ts), BF16),
            pltpu.VMEM((SWA_HEADS, CHUNK + ts, LANES), BF16),
            pltpu.VMEM((ts, MIX_WIDTH), F32),
            pltpu.VMEM((MEM_HEADS, LANES, MEM_LEN), BF16),
            pltpu.VMEM((MEM_HEADS, MEM_LEN, LANES), BF16),
            pltpu.VMEM((2, SWA_HEADS, CHUNK, 2 * CHUNK), F32),
            pltpu.VMEM((A_GROUPS, CHUNK, CHUNK), BF16),
            pltpu.VMEM((A_GROUPS, CHUNK, LANES), F32),
            pltpu.VMEM((D_MODEL, 2 * MEM_WIDTH), BF16),
            pltpu.VMEM((MIX_WIDTH, D_MODEL), BF16),
            pltpu.VMEM((ts, MIX_WIDTH), F32),
            pltpu.VMEM((ts, MIX_WIDTH), BF16),
            pltpu.VMEM((ts, D_MODEL), F32),
        ],
    )
    return pl.pallas_call(
        _layer_kernel,
        grid_spec=grid_spec,
        out_shape=jax.ShapeDtypeStruct(x.shape, x.dtype),
        compiler_params=pltpu.CompilerParams(
            dimension_semantics=("arbitrary", "arbitrary"),
            vmem_limit_bytes=VMEM_LIMIT_BYTES),
        name="layer",
    )(sinks, x, x, mem, w_in_bf16, w_kv, w_out, pre_g, mem_g, post_g, vng, vnb,
      w_spatial, b_spatial, rel_bias, buckets)


def kernel(x, mem, pre_norm_g, post_norm_g, mem_norm_g, w_in, w_mem_kv, v_norm_g, v_norm_b,
           w_spatial, b_spatial, attn_sinks, rel_bias, w_out):
    depth = w_in.shape[0]
    buckets = jnp.asarray(_band_buckets())
    for layer in range(depth):
        x = _layer(x, mem, w_in[layer].astype(BF16), w_mem_kv[layer], w_out[layer],
                   pre_norm_g[layer][None, :], mem_norm_g[layer][None, :], post_norm_g[layer][None, :],
                   v_norm_g[layer][None, :], v_norm_b[layer][None, :],
                   w_spatial[layer], b_spatial[layer], buckets, attn_sinks[layer], rel_bias)
    return x
```

```python
import numpy as np
import jax
import jax.numpy as jnp
from jax import lax
from jax.experimental import pallas as pl
from jax.experimental.pallas import tpu as pltpu

D_MODEL = 1024
MEM_LEN = 256
HEAD_DIM = 64
CHUNK = 128
A_GROUPS = 4
A_WIDTH = 512
SWA_HEADS = 4
SWA_WIDTH = 256
SWA_KV_WIDTH = 128
MEM_HEADS = 4
MEM_WIDTH = 256
MIX_WIDTH = 1024
IN_WIDTH = 2816
N_BUCKETS = 32
MAX_DISTANCE = 128
WINDOW = 128
EPS = 1e-6
NEG = -1e30
LOG2E = float(np.log2(np.e))
Q_SCALE = HEAD_DIM ** -0.5 * LOG2E

OFF_U = 0
OFF_V = A_WIDTH
OFF_SQ = 2 * A_WIDTH
OFF_SK = OFF_SQ + SWA_WIDTH
OFF_SV = OFF_SK + SWA_KV_WIDTH
OFF_MQ = OFF_SV + SWA_KV_WIDTH
OFF_Z = OFF_MQ + MEM_WIDTH

LANES = 128
SEQ_TILE = 1024
SUB_TILES = (256, 256, 256, 256)
PROJ_COLS = 512
ITEM_LAG = 2
PROJECT_COLUMN_PIECES = 6
PROJECT_OUT_DOTS = D_MODEL // PROJ_COLS
HEAD_STAGES_PER_CHUNK = 2 * (SWA_HEADS // 2 + MEM_HEADS // 2)
VMEM_LIMIT_BYTES = 62 * 1024 * 1024

BF16 = jnp.bfloat16
F32 = jnp.float32


def _t5_causal_buckets(dist):
    n = np.maximum(dist, 0)
    max_exact = N_BUCKETS // 2
    large = max_exact + (np.log(np.maximum(n, 1) / max_exact) / np.log(MAX_DISTANCE / max_exact)
                         * (N_BUCKETS - max_exact)).astype(np.int32)
    large = np.minimum(large, N_BUCKETS - 1)
    return np.where(n < max_exact, n, large).astype(np.int32)


def _band_buckets():
    qi = np.arange(CHUNK)[:, None]
    kj = np.arange(2 * CHUNK)[None, :]
    dist = qi + CHUNK - kj
    valid = (dist >= 0) & (dist < WINDOW)
    return np.where(valid, _t5_causal_buckets(dist), -1).astype(np.int32)


def _gelu_tanh(x):
    c = np.float32(np.sqrt(2.0 / np.pi))
    ck = np.float32(np.sqrt(2.0 / np.pi) * 0.044715)
    hx = 0.5 * x
    return hx + hx * jnp.tanh(x * (c + ck * (x * x)))


def _silu(z):
    hz = 0.5 * z
    return hz + hz * jnp.tanh(hz)


def _dot(a, b):
    return jnp.dot(a, b, preferred_element_type=F32)


_DONE = object()


def _skewed(items, lag):
    waiting = []
    for item in items:
        if next(item, _DONE) is not _DONE:
            waiting.append(item)
        yield
        if len(waiting) > lag:
            next(waiting.pop(0), _DONE)
            yield
    for item in waiting:
        next(item, _DONE)
        yield


def _chain(*gens):
    for g in gens:
        yield from g


def _interleave(a, b, b_per_a):
    credit = 0.0
    a_live = b_live = True
    while a_live or b_live:
        if a_live:
            a_live = next(a, _DONE) is not _DONE
        credit += b_per_a
        while b_live and (credit >= 1.0 or not a_live):
            b_live = next(b, _DONE) is not _DONE
            credit -= 1.0


def _lane_half_mask(parity):
    lane = lax.broadcasted_iota(jnp.int32, (1, LANES), 1)
    return (lane >= HEAD_DIM) if parity else (lane < HEAD_DIM)


def _place_head_rows(kt_head, parity):
    zeros = jnp.zeros_like(kt_head)
    return jnp.concatenate([zeros, kt_head] if parity else [kt_head, zeros], axis=0)


def _layer_kernel(sinks_ref,
                  x_ref, xnext_ref, mem_ref, win_ref, wkv_f32_ref, wout_f32_ref, preg_ref, memg_ref, postg_ref,
                  vng_ref, vnb_ref, ws_ref, bs_ref, relb_ref, buckets_ref,
                  out_ref,
                  h_scr, gu_scr, vn_scr, qs_scr, ktvar_scr, vvar_scr, zg_scr,
                  mk_scr, mv_scr, bias_scr, wsm_scr, bst_scr, wkv_ref, wout_ref,
                  y_scr, ybf_scr, o_scr):
    ts = x_ref.shape[1]
    b = pl.program_id(0)
    t = pl.program_id(1)

    @pl.when((b == 0) & (t == 0))
    def _init():
        row = lax.broadcasted_iota(jnp.int32, (CHUNK, CHUNK), 0)
        col = lax.broadcasted_iota(jnp.int32, (CHUNK, CHUNK), 1)
        for g in range(A_GROUPS):
            wsm_scr[g] = jnp.where(row >= col, ws_ref[g], 0.0).astype(BF16)
            bs_col = jnp.sum(jnp.where(row == col, bs_ref[g:g + 1, :], 0.0), axis=1, keepdims=True)
            bst_scr[g] = jnp.broadcast_to(bs_col, (CHUNK, LANES))
        for r0 in range(0, D_MODEL, 2 * CHUNK):
            wrows = pl.ds(r0, 2 * CHUNK)
            wkv_ref[wrows, :] = wkv_f32_ref[wrows, :].astype(BF16)
            wout_ref[wrows, :] = wout_f32_ref[wrows, :].astype(BF16)
        buckets = buckets_ref[...]
        prev_key = lax.broadcasted_iota(jnp.int32, (CHUNK, 2 * CHUNK), 1) < CHUNK
        for hd in range(SWA_HEADS):
            tbl = jnp.full((CHUNK, 2 * CHUNK), NEG, F32)
            for bk in range(N_BUCKETS):
                tbl = jnp.where(buckets == bk, relb_ref[bk:bk + 1, hd:hd + 1] * LOG2E, tbl)
            bias_scr[0, hd] = tbl
            bias_scr[1, hd] = jnp.where(prev_key, NEG, tbl)

    @pl.when(t == 0)
    def _sequence_start():
        ktvar_scr[:, :, 0:CHUNK] = jnp.zeros((SWA_HEADS, LANES, CHUNK), BF16)
        vvar_scr[:, 0:CHUNK, :] = jnp.zeros((SWA_HEADS, CHUNK, LANES), BF16)
        m = mem_ref[0]
        ms = jnp.mean(m * m, axis=-1, keepdims=True)
        hm = (m * lax.rsqrt(ms + EPS) * memg_ref[...]).astype(BF16)
        kv = _dot(hm, wkv_ref[...])
        for pair in range(MEM_HEADS // 2):
            kt_pair = kv[:, pair * LANES:(pair + 1) * LANES].T
            v_pair = kv[:, MEM_WIDTH + pair * LANES:MEM_WIDTH + (pair + 1) * LANES]
            for parity in range(2):
                hd = 2 * pair + parity
                kt_head = kt_pair[parity * HEAD_DIM:(parity + 1) * HEAD_DIM]
                mk_scr[hd] = _place_head_rows(kt_head, parity).astype(BF16)
                mv_scr[hd] = jnp.where(_lane_half_mask(parity), v_pair, 0.0).astype(BF16)

    @pl.when(t > 0)
    def _copy_carry():
        ktvar_scr[:, :, 0:CHUNK] = ktvar_scr[:, :, ts:ts + CHUNK]
        vvar_scr[:, 0:CHUNK, :] = vvar_scr[:, ts:ts + CHUNK, :]

    lo = _lane_half_mask(0)
    hi = _lane_half_mask(1)
    first_tile = jnp.where(t == 0, 1, 0)

    def pre_norm(r0, nrows, src_ref=x_ref):
        for rc in range(nrows // CHUNK):
            crows = pl.ds(r0 + rc * CHUNK, CHUNK)
            x = src_ref[0, crows, :]
            ms = jnp.mean(x * x, axis=-1, keepdims=True)
            h_scr[crows, :] = (x * lax.rsqrt(ms + EPS) * preg_ref[...]).astype(BF16)

    def proj(rows, c0, width):
        return _dot(h_scr[rows, :], win_ref[:, c0:c0 + width])

    def project(r0, nrows, gmlp=True, rest=True):
        rows = pl.ds(r0, nrows)

        if gmlp:
            v_all = _gelu_tanh(proj(rows, OFF_V, A_WIDTH))
            for g in range(A_GROUPS):
                cols = pl.ds(g * LANES, LANES)
                v = v_all[:, g * LANES:(g + 1) * LANES]
                mu = jnp.mean(v, axis=-1, keepdims=True)
                d = v - mu
                var = jnp.mean(d * d, axis=-1, keepdims=True)
                vn = d * lax.rsqrt(var + EPS) * vng_ref[:, cols] + vnb_ref[:, cols]
                vn_scr[rows, cols] = vn.astype(BF16)
            yield

            gu_scr[rows, :] = _gelu_tanh(proj(rows, OFF_U, A_WIDTH))
            yield

        if rest:
            qkv = proj(rows, OFF_SQ, SWA_WIDTH + 2 * SWA_KV_WIDTH)
            qs_scr[rows, 0:SWA_WIDTH] = (qkv[:, 0:SWA_WIDTH] * Q_SCALE).astype(BF16)
            skt = qkv[:, SWA_WIDTH:SWA_WIDTH + LANES].T
            sv = qkv[:, SWA_WIDTH + LANES:]
            sv_rot = pltpu.roll(sv, HEAD_DIM, 1)
            new_rows = pl.ds(CHUNK + r0, nrows)
            for hd, (keep, vsrc) in enumerate([(lo, sv), (hi, sv_rot), (lo, sv_rot), (hi, sv)]):
                kv_head = hd // 2
                kt_head = skt[kv_head * HEAD_DIM:(kv_head + 1) * HEAD_DIM]
                ktvar_scr[hd, :, new_rows] = _place_head_rows(kt_head, hd % 2).astype(BF16)
                vvar_scr[hd, new_rows, :] = jnp.where(keep, vsrc, 0.0).astype(BF16)
            yield

            mqz = proj(rows, OFF_MQ, PROJ_COLS)
            qs_scr[rows, SWA_WIDTH:] = (mqz[:, 0:MEM_WIDTH] * Q_SCALE).astype(BF16)
            zg_scr[rows, 0:PROJ_COLS - MEM_WIDTH] = _silu(mqz[:, MEM_WIDTH:])
            yield

        if gmlp:
            for c in range(r0 // CHUNK, (r0 + nrows) // CHUNK):
                for g in range(A_GROUPS):
                    spatial_gating(c, g)
                yield

        if rest:
            done = PROJ_COLS - MEM_WIDTH
            while done < MIX_WIDTH:
                width = min(PROJ_COLS, MIX_WIDTH - done)
                zg_scr[rows, done:done + width] = _silu(proj(rows, OFF_Z + done, width))
                done += width
                yield

    def spatial_gating(c, g):
        rows = pl.ds(c * CHUNK, CHUNK)
        cols = pl.ds(g * LANES, LANES)
        sv_g = _dot(wsm_scr[g], vn_scr[rows, cols]) + bst_scr[g]
        y_scr[rows, cols] = gu_scr[rows, cols] * sv_g

    def attn_item(c, pair, sliding):
        r0 = c * CHUNK
        rows = pl.ds(r0, CHUNK)
        heads = (2 * pair, 2 * pair + 1)
        if sliding:
            band = pl.ds(r0, 2 * CHUNK)
            qp = qs_scr[rows, pl.ds(pair * LANES, LANES)]
            keys = jnp.concatenate([ktvar_scr[hd, :, band] for hd in heads], axis=1)
            bias_sel = first_tile if c == 0 else 0
            bias = jnp.concatenate([bias_scr[bias_sel, hd] for hd in heads], axis=1)
            logits = _dot(qp, keys) + bias
        else:
            qp = qs_scr[rows, pl.ds(SWA_WIDTH + pair * LANES, LANES)]
            keys = jnp.concatenate([mk_scr[hd] for hd in heads], axis=1)
            logits = _dot(qp, keys)
        n_keys = logits.shape[1] // 2
        yield
        probs, recips = [], []
        for i, hd in enumerate(heads):
            s = logits[:, i * n_keys:(i + 1) * n_keys]
            m = jnp.max(s, axis=-1, keepdims=True)
            if sliding:
                sink = sinks_ref[hd] * LOG2E
                m = jnp.maximum(m, sink)
            p = jnp.exp2(s - m)
            denom = jnp.sum(p, axis=-1, keepdims=True)
            if sliding:
                denom = denom + jnp.exp2(sink - m)
            probs.append(p.astype(BF16))
            recips.append(1.0 / denom)
        if sliding:
            values = jnp.concatenate([vvar_scr[hd, band, :] for hd in heads], axis=0)
        else:
            values = jnp.concatenate([mv_scr[hd] for hd in heads], axis=0)
        o_pair = _dot(jnp.concatenate(probs, axis=1), values)
        o_pair = o_pair * jnp.where(lo, recips[0], recips[1])
        off = A_WIDTH if sliding else A_WIDTH + SWA_WIDTH
        y_scr[rows, pl.ds(off + pair * LANES, LANES)] = o_pair

    def head_items(r0, nrows):
        for c in range(r0 // CHUNK, (r0 + nrows) // CHUNK):
            for pair in range(SWA_HEADS // 2):
                yield attn_item(c, pair, True)
            for pair in range(MEM_HEADS // 2):
                yield attn_item(c, pair, False)

    def project_out(r0, nrows, k0=0, k1=MIX_WIDTH):
        rows = pl.ds(r0, nrows)
        ybf_scr[rows, k0:k1] = (y_scr[rows, k0:k1] * zg_scr[rows, k0:k1]).astype(BF16)
        for c0 in range(0, D_MODEL, PROJ_COLS):
            cols = pl.ds(c0, PROJ_COLS)
            part = _dot(ybf_scr[rows, k0:k1], wout_ref[k0:k1, cols])
            o_scr[rows, cols] = o_scr[rows, cols] + part if k0 else part
            yield
        if k1 < MIX_WIDTH:
            return
        for rc in range(nrows // CHUNK):
            crows = pl.ds(r0 + rc * CHUNK, CHUNK)
            o = o_scr[crows, :]
            ms = jnp.mean(o * o, axis=-1, keepdims=True)
            out_ref[0, crows, :] = x_ref[0, crows, :] + o * lax.rsqrt(ms + EPS) * postg_ref[...]
            yield

    assert sum(SUB_TILES) == ts
    starts = [sum(SUB_TILES[:s]) for s in range(len(SUB_TILES))]
    subs = list(zip(starts, SUB_TILES))
    n_sub = len(subs)

    @pl.when((b == 0) & (t == 0))
    def _first_projection():
        pre_norm(*subs[0])
        for _ in project(*subs[0]):
            pass

    for s in range(n_sub):
        heads = _skewed(head_items(*subs[s]), ITEM_LAG)
        if s + 1 < n_sub:
            pre_norm(*subs[s + 1])
            filler = project(*subs[s + 1])
            n_filler = PROJECT_COLUMN_PIECES + subs[s + 1][1] // CHUNK
        else:
            pre_norm(*subs[0], src_ref=xnext_ref)
            filler = _chain(project_out(*subs[s], 0, A_WIDTH), project(*subs[0], rest=False))
            n_filler = PROJECT_OUT_DOTS + 2 + subs[0][1] // CHUNK
        if s >= 1:
            filler = _chain(project_out(*subs[s - 1]), filler)
            n_filler += PROJECT_OUT_DOTS + subs[s - 1][1] // CHUNK
        _interleave(heads, filler, n_filler / ((subs[s][1] // CHUNK) * HEAD_STAGES_PER_CHUNK))
    for _ in project_out(*subs[-1], A_WIDTH, MIX_WIDTH):
        pass
    for _ in project(*subs[0], gmlp=False):
        pass


def _layer(x, mem, w_in_bf16, w_kv, w_out, pre_g, mem_g, post_g, vng, vnb, w_spatial,
           b_spatial, buckets, sinks, rel_bias):
    batch, seq, _ = x.shape
    ts = SEQ_TILE
    n_tiles = seq // ts
    const2 = lambda b, t, *_: (0, 0)
    const3 = lambda b, t, *_: (0, 0, 0)

    def next_first_sub_tile(b, t, *_):
        step = jnp.minimum(b * n_tiles + t + 1, batch * n_tiles - 1)
        return (step // n_tiles, (step % n_tiles) * (ts // SUB_TILES[0]), 0)

    grid_spec = pltpu.PrefetchScalarGridSpec(
        num_scalar_prefetch=1,
        grid=(batch, n_tiles),
        in_specs=[
            pl.BlockSpec((1, ts, D_MODEL), lambda b, t, *_: (b, t, 0)),
            pl.BlockSpec((1, SUB_TILES[0], D_MODEL), next_first_sub_tile),
            pl.BlockSpec((1, MEM_LEN, D_MODEL), lambda b, t, *_: (b, 0, 0)),
            pl.BlockSpec((D_MODEL, IN_WIDTH), const2),
            pl.BlockSpec((D_MODEL, 2 * MEM_WIDTH), const2),
            pl.BlockSpec((MIX_WIDTH, D_MODEL), const2),
            pl.BlockSpec((1, D_MODEL), const2),
            pl.BlockSpec((1, D_MODEL), const2),
            pl.BlockSpec((1, D_MODEL), const2),
            pl.BlockSpec((1, A_WIDTH), const2),
            pl.BlockSpec((1, A_WIDTH), const2),
            pl.BlockSpec((A_GROUPS, CHUNK, CHUNK), const3),
            pl.BlockSpec((A_GROUPS, CHUNK), const2),
            pl.BlockSpec((N_BUCKETS, SWA_HEADS), const2),
            pl.BlockSpec((CHUNK, 2 * CHUNK), const2),
        ],
        out_specs=pl.BlockSpec((1, ts, D_MODEL), lambda b, t, *_: (b, t, 0)),
        scratch_shapes=[
            pltpu.VMEM((ts, D_MODEL), BF16),
            pltpu.VMEM((ts, A_WIDTH), F32),
            pltpu.VMEM((ts, A_WIDTH), BF16),
            pltpu.VMEM((ts, SWA_WIDTH + MEM_WIDTH), BF16),
            pltpu.VMEM((SWA_HEADS, LANES, CHUNK + ts), BF16),
            pltpu.VMEM((SWA_HEADS, CHUNK + ts, LANES), BF16),
            pltpu.VMEM((ts, MIX_WIDTH), F32),
            pltpu.VMEM((MEM_HEADS, LANES, MEM_LEN), BF16),
            pltpu.VMEM((MEM_HEADS, MEM_LEN, LANES), BF16),
            pltpu.VMEM((2, SWA_HEADS, CHUNK, 2 * CHUNK), F32),
            pltpu.VMEM((A_GROUPS, CHUNK, CHUNK), BF16),
            pltpu.VMEM((A_GROUPS, CHUNK, LANES), F32),
            pltpu.VMEM((D_MODEL, 2 * MEM_WIDTH), BF16),
            pltpu.VMEM((MIX_WIDTH, D_MODEL), BF16),
            pltpu.VMEM((ts, MIX_WIDTH), F32),
            pltpu.VMEM((ts, MIX_WIDTH), BF16),
            pltpu.VMEM((ts, D_MODEL), F32),
        ],
    )
    return pl.pallas_call(
        _layer_kernel,
        grid_spec=grid_spec,
        out_shape=jax.ShapeDtypeStruct(x.shape, x.dtype),
        compiler_params=pltpu.CompilerParams(
            dimension_semantics=("arbitrary", "arbitrary"),
            vmem_limit_bytes=VMEM_LIMIT_BYTES),
        name="layer",
    )(sinks, x, x, mem, w_in_bf16, w_kv, w_out, pre_g, mem_g, post_g, vng, vnb,
      w_spatial, b_spatial, rel_bias, buckets)


def kernel(x, mem, pre_norm_g, post_norm_g, mem_norm_g, w_in, w_mem_kv, v_norm_g, v_norm_b,
           w_spatial, b_spatial, attn_sinks, rel_bias, w_out):
    depth = w_in.shape[0]
    buckets = jnp.asarray(_band_buckets())
    for layer in range(depth):
        x = _layer(x, mem, w_in[layer].astype(BF16), w_mem_kv[layer], w_out[layer],
                   pre_norm_g[layer][None, :], mem_norm_g[layer][None, :], post_norm_g[layer][None, :],
                   v_norm_g[layer][None, :], v_norm_b[layer][None, :],
                   w_spatial[layer], b_spatial[layer], buckets, attn_sinks[layer], rel_bias)
    return x
```

```python
import numpy as np
import jax
import jax.numpy as jnp
from jax import lax
from jax.experimental import pallas as pl
from jax.experimental.pallas import tpu as pltpu

D_MODEL = 1024
MEM_LEN = 256
HEAD_DIM = 64
CHUNK = 128
A_GROUPS = 4
A_WIDTH = 512
SWA_HEADS = 4
SWA_WIDTH = 256
SWA_KV_WIDTH = 128
MEM_HEADS = 4
MEM_WIDTH = 256
MIX_WIDTH = 1024
IN_WIDTH = 2816
N_BUCKETS = 32
MAX_DISTANCE = 128
WINDOW = 128
EPS = 1e-6
NEG = -1e30
LOG2E = float(np.log2(np.e))
Q_SCALE = HEAD_DIM ** -0.5 * LOG2E

OFF_U = 0
OFF_V = A_WIDTH
OFF_SQ = 2 * A_WIDTH
OFF_SK = OFF_SQ + SWA_WIDTH
OFF_SV = OFF_SK + SWA_KV_WIDTH
OFF_MQ = OFF_SV + SWA_KV_WIDTH
OFF_Z = OFF_MQ + MEM_WIDTH

LANES = 128
SEQ_TILE = 1024
SUB_TILES = (256, 256, 256, 256)
PROJ_COLS = 512
ITEM_LAG = 2
PROJECT_COLUMN_PIECES = 6
PROJECT_OUT_DOTS = D_MODEL // PROJ_COLS
HEAD_STAGES_PER_CHUNK = 2 * (SWA_HEADS // 2 + MEM_HEADS // 2)
VMEM_LIMIT_BYTES = 62 * 1024 * 1024

BF16 = jnp.bfloat16
F32 = jnp.float32


def _t5_causal_buckets(dist):
    n = np.maximum(dist, 0)
    max_exact = N_BUCKETS // 2
    large = max_exact + (np.log(np.maximum(n, 1) / max_exact) / np.log(MAX_DISTANCE / max_exact)
                         * (N_BUCKETS - max_exact)).astype(np.int32)
    large = np.minimum(large, N_BUCKETS - 1)
    return np.where(n < max_exact, n, large).astype(np.int32)


def _band_buckets():
    qi = np.arange(CHUNK)[:, None]
    kj = np.arange(2 * CHUNK)[None, :]
    dist = qi + CHUNK - kj
    valid = (dist >= 0) & (dist < WINDOW)
    return np.where(valid, _t5_causal_buckets(dist), -1).astype(np.int32)


def _gelu_tanh(x):
    c = np.float32(np.sqrt(2.0 / np.pi))
    ck = np.float32(np.sqrt(2.0 / np.pi) * 0.044715)
    hx = 0.5 * x
    return hx + hx * jnp.tanh(x * (c + ck * (x * x)))


def _silu(z):
    hz = 0.5 * z
    return hz + hz * jnp.tanh(hz)


def _dot(a, b):
    return jnp.dot(a, b, preferred_element_type=F32)


_DONE = object()


def _skewed(items, lag):
    waiting = []
    for item in items:
        if next(item, _DONE) is not _DONE:
            waiting.append(item)
        yield
        if len(waiting) > lag:
            next(waiting.pop(0), _DONE)
            yield
    for item in waiting:
        next(item, _DONE)
        yield


def _chain(*gens):
    for g in gens:
        yield from g


def _interleave(a, b, b_per_a):
    credit = 0.0
    a_live = b_live = True
    while a_live or b_live:
        if a_live:
            a_live = next(a, _DONE) is not _DONE
        credit += b_per_a
        while b_live and (credit >= 1.0 or not a_live):
            b_live = next(b, _DONE) is not _DONE
            credit -= 1.0


def _lane_half_mask(parity):
    lane = lax.broadcasted_iota(jnp.int32, (1, LANES), 1)
    return (lane >= HEAD_DIM) if parity else (lane < HEAD_DIM)


def _place_head_rows(kt_head, parity):
    zeros = jnp.zeros_like(kt_head)
    return jnp.concatenate([zeros, kt_head] if parity else [kt_head, zeros], axis=0)


def _layer_kernel(sinks_ref,
                  x_ref, xnext_ref, mem_ref, win_ref, wkv_f32_ref, wout_f32_ref, preg_ref, memg_ref, postg_ref,
                  vng_ref, vnb_ref, ws_ref, bs_ref, relb_ref, buckets_ref,
                  out_ref,
                  h_scr, gu_scr, vn_scr, qs_scr, ktvar_scr, vvar_scr, zg_scr,
                  mk_scr, mv_scr, bias_scr, wsm_scr, bst_scr, wkv_ref, wout_ref,
                  y_scr, ybf_scr, o_scr):
    ts = x_ref.shape[1]
    b = pl.program_id(0)
    t = pl.program_id(1)

    @pl.when((b == 0) & (t == 0))
    def _init():
        row = lax.broadcasted_iota(jnp.int32, (CHUNK, CHUNK), 0)
        col = lax.broadcasted_iota(jnp.int32, (CHUNK, CHUNK), 1)
        for g in range(A_GROUPS):
            wsm_scr[g] = jnp.where(row >= col, ws_ref[g], 0.0).astype(BF16)
            bs_col = jnp.sum(jnp.where(row == col, bs_ref[g:g + 1, :], 0.0), axis=1, keepdims=True)
            bst_scr[g] = jnp.broadcast_to(bs_col, (CHUNK, LANES))
        for r0 in range(0, D_MODEL, 2 * CHUNK):
            wrows = pl.ds(r0, 2 * CHUNK)
            wkv_ref[wrows, :] = wkv_f32_ref[wrows, :].astype(BF16)
            wout_ref[wrows, :] = wout_f32_ref[wrows, :].astype(BF16)
        buckets = buckets_ref[...]
        prev_key = lax.broadcasted_iota(jnp.int32, (CHUNK, 2 * CHUNK), 1) < CHUNK
        for hd in range(SWA_HEADS):
            tbl = jnp.full((CHUNK, 2 * CHUNK), NEG, F32)
            for bk in range(N_BUCKETS):
                tbl = jnp.where(buckets == bk, relb_ref[bk:bk + 1, hd:hd + 1] * LOG2E, tbl)
            bias_scr[0, hd] = tbl
            bias_scr[1, hd] = jnp.where(prev_key, NEG, tbl)

    @pl.when(t == 0)
    def _sequence_start():
        ktvar_scr[:, :, 0:CHUNK] = jnp.zeros((SWA_HEADS, LANES, CHUNK), BF16)
        vvar_scr[:, 0:CHUNK, :] = jnp.zeros((SWA_HEADS, CHUNK, LANES), BF16)
        m = mem_ref[0]
        ms = jnp.mean(m * m, axis=-1, keepdims=True)
        hm = (m * lax.rsqrt(ms + EPS) * memg_ref[...]).astype(BF16)
        kv = _dot(hm, wkv_ref[...])
        for pair in range(MEM_HEADS // 2):
            kt_pair = kv[:, pair * LANES:(pair + 1) * LANES].T
            v_pair = kv[:, MEM_WIDTH + pair * LANES:MEM_WIDTH + (pair + 1) * LANES]
            for parity in range(2):
                hd = 2 * pair + parity
                kt_head = kt_pair[parity * HEAD_DIM:(parity + 1) * HEAD_DIM]
                mk_scr[hd] = _place_head_rows(kt_head, parity).astype(BF16)
                mv_scr[hd] = jnp.where(_lane_half_mask(parity), v_pair, 0.0).astype(BF16)

    @pl.when(t > 0)
    def _copy_carry():
        ktvar_scr[:, :, 0:CHUNK] = ktvar_scr[:, :, ts:ts + CHUNK]
        vvar_scr[:, 0:CHUNK, :] = vvar_scr[:, ts:ts + CHUNK, :]

    lo = _lane_half_mask(0)
    hi = _lane_half_mask(1)
    first_tile = jnp.where(t == 0, 1, 0)

    def pre_norm(r0, nrows, src_ref=x_ref):
        for rc in range(nrows // CHUNK):
            crows = pl.ds(r0 + rc * CHUNK, CHUNK)
            x = src_ref[0, crows, :]
            ms = jnp.mean(x * x, axis=-1, keepdims=True)
            h_scr[crows, :] = (x * lax.rsqrt(ms + EPS) * preg_ref[...]).astype(BF16)

    def proj(rows, c0, width):
        return _dot(h_scr[rows, :], win_ref[:, c0:c0 + width])

    def project(r0, nrows, gmlp=True, rest=True):
        rows = pl.ds(r0, nrows)

        if gmlp:
            v_all = _gelu_tanh(proj(rows, OFF_V, A_WIDTH))
            for g in range(A_GROUPS):
                cols = pl.ds(g * LANES, LANES)
                v = v_all[:, g * LANES:(g + 1) * LANES]
                mu = jnp.mean(v, axis=-1, keepdims=True)
                d = v - mu
                var = jnp.mean(d * d, axis=-1, keepdims=True)
                vn = d * lax.rsqrt(var + EPS) * vng_ref[:, cols] + vnb_ref[:, cols]
                vn_scr[rows, cols] = vn.astype(BF16)
            yield

            gu_scr[rows, :] = _gelu_tanh(proj(rows, OFF_U, A_WIDTH))
            yield

        if rest:
            qkv = proj(rows, OFF_SQ, SWA_WIDTH + 2 * SWA_KV_WIDTH)
            qs_scr[rows, 0:SWA_WIDTH] = (qkv[:, 0:SWA_WIDTH] * Q_SCALE).astype(BF16)
            skt = qkv[:, SWA_WIDTH:SWA_WIDTH + LANES].T
            sv = qkv[:, SWA_WIDTH + LANES:]
            sv_rot = pltpu.roll(sv, HEAD_DIM, 1)
            new_rows = pl.ds(CHUNK + r0, nrows)
            for hd, (keep, vsrc) in enumerate([(lo, sv), (hi, sv_rot), (lo, sv_rot), (hi, sv)]):
                kv_head = hd // 2
                kt_head = skt[kv_head * HEAD_DIM:(kv_head + 1) * HEAD_DIM]
                ktvar_scr[hd, :, new_rows] = _place_head_rows(kt_head, hd % 2).astype(BF16)
                vvar_scr[hd, new_rows, :] = jnp.where(keep, vsrc, 0.0).astype(BF16)
            yield

            mqz = proj(rows, OFF_MQ, PROJ_COLS)
            qs_scr[rows, SWA_WIDTH:] = (mqz[:, 0:MEM_WIDTH] * Q_SCALE).astype(BF16)
            zg_scr[rows, 0:PROJ_COLS - MEM_WIDTH] = _silu(mqz[:, MEM_WIDTH:])
            yield

        if gmlp:
            for c in range(r0 // CHUNK, (r0 + nrows) // CHUNK):
                for g in range(A_GROUPS):
                    spatial_gating(c, g)
                yield

        if rest:
            done = PROJ_COLS - MEM_WIDTH
            while done < MIX_WIDTH:
                width = min(PROJ_COLS, MIX_WIDTH - done)
                zg_scr[rows, done:done + width] = _silu(proj(rows, OFF_Z + done, width))
                done += width
                yield

    def spatial_gating(c, g):
        rows = pl.ds(c * CHUNK, CHUNK)
        cols = pl.ds(g * LANES, LANES)
        sv_g = _dot(wsm_scr[g], vn_scr[rows, cols]) + bst_scr[g]
        y_scr[rows, cols] = gu_scr[rows, cols] * sv_g

    def attn_item(c, pair, sliding):
        r0 = c * CHUNK
        rows = pl.ds(r0, CHUNK)
        heads = (2 * pair, 2 * pair + 1)
        if sliding:
            band = pl.ds(r0, 2 * CHUNK)
            qp = qs_scr[rows, pl.ds(pair * LANES, LANES)]
            keys = jnp.concatenate([ktvar_scr[hd, :, band] for hd in heads], axis=1)
            bias_sel = first_tile if c == 0 else 0
            bias = jnp.concatenate([bias_scr[bias_sel, hd] for hd in heads], axis=1)
            logits = _dot(qp, keys) + bias
        else:
            qp = qs_scr[rows, pl.ds(SWA_WIDTH + pair * LANES, LANES)]
            keys = jnp.concatenate([mk_scr[hd] for hd in heads], axis=1)
            logits = _dot(qp, keys)
        n_keys = logits.shape[1] // 2
        yield
        probs, recips = [], []
        for i, hd in enumerate(heads):
            s = logits[:, i * n_keys:(i + 1) * n_keys]
            m = jnp.max(s, axis=-1, keepdims=True)
            if sliding:
                sink = sinks_ref[hd] * LOG2E
                m = jnp.maximum(m, sink)
            p = jnp.exp2(s - m)
            denom = jnp.sum(p, axis=-1, keepdims=True)
            if sliding:
                denom = denom + jnp.exp2(sink - m)
            probs.append(p.astype(BF16))
            recips.append(1.0 / denom)
        if sliding:
            values = jnp.concatenate([vvar_scr[hd, band, :] for hd in heads], axis=0)
        else:
            values = jnp.concatenate([mv_scr[hd] for hd in heads], axis=0)
        o_pair = _dot(jnp.concatenate(probs, axis=1), values)
        o_pair = o_pair * jnp.where(lo, recips[0], recips[1])
        off = A_WIDTH if sliding else A_WIDTH + SWA_WIDTH
        y_scr[rows, pl.ds(off + pair * LANES, LANES)] = o_pair

    def head_items(r0, nrows):
        for c in range(r0 // CHUNK, (r0 + nrows) // CHUNK):
            for pair in range(SWA_HEADS // 2):
                yield attn_item(c, pair, True)
            for pair in range(MEM_HEADS // 2):
                yield attn_item(c, pair, False)

    def project_out(r0, nrows, k0=0, k1=MIX_WIDTH):
        rows = pl.ds(r0, nrows)
        orows = pl.ds(0, nrows)
        ybf_scr[rows, k0:k1] = (y_scr[rows, k0:k1] * zg_scr[rows, k0:k1]).astype(BF16)
        for c0 in range(0, D_MODEL, PROJ_COLS):
            cols = pl.ds(c0, PROJ_COLS)
            part = _dot(ybf_scr[rows, k0:k1], wout_ref[k0:k1, cols])
            o_scr[orows, cols] = o_scr[orows, cols] + part if k0 else part
            yield
        if k1 < MIX_WIDTH:
            return
        for rc in range(nrows // CHUNK):
            crows = pl.ds(r0 + rc * CHUNK, CHUNK)
            o = o_scr[pl.ds(rc * CHUNK, CHUNK), :]
            ms = jnp.mean(o * o, axis=-1, keepdims=True)
            out_ref[0, crows, :] = x_ref[0, crows, :] + o * lax.rsqrt(ms + EPS) * postg_ref[...]
            yield

    assert sum(SUB_TILES) == ts
    starts = [sum(SUB_TILES[:s]) for s in range(len(SUB_TILES))]
    subs = list(zip(starts, SUB_TILES))
    n_sub = len(subs)

    @pl.when((b == 0) & (t == 0))
    def _first_projection():
        pre_norm(*subs[0])
        for _ in project(*subs[0]):
            pass

    for s in range(n_sub):
        heads = _skewed(head_items(*subs[s]), ITEM_LAG)
        if s + 1 < n_sub:
            pre_norm(*subs[s + 1])
            filler = project(*subs[s + 1])
            n_filler = PROJECT_COLUMN_PIECES + subs[s + 1][1] // CHUNK
        else:
            filler = project_out(*subs[s], 0, A_WIDTH)
            n_filler = PROJECT_OUT_DOTS
        if s >= 1:
            filler = _chain(project_out(*subs[s - 1]), filler)
            n_filler += PROJECT_OUT_DOTS + subs[s - 1][1] // CHUNK
        _interleave(heads, filler, n_filler / ((subs[s][1] // CHUNK) * HEAD_STAGES_PER_CHUNK))
    for _ in project_out(*subs[-1], A_WIDTH, MIX_WIDTH):
        pass
    pre_norm(*subs[0], src_ref=xnext_ref)
    for _ in project(*subs[0]):
        pass


def _layer(x, mem, w_in_bf16, w_kv, w_out, pre_g, mem_g, post_g, vng, vnb, w_spatial,
           b_spatial, buckets, sinks, rel_bias):
    batch, seq, _ = x.shape
    ts = SEQ_TILE
    n_tiles = seq // ts
    const2 = lambda b, t, *_: (0, 0)
    const3 = lambda b, t, *_: (0, 0, 0)

    def next_first_sub_tile(b, t, *_):
        step = jnp.minimum(b * n_tiles + t + 1, batch * n_tiles - 1)
        return (step // n_tiles, (step % n_tiles) * (ts // SUB_TILES[0]), 0)

    grid_spec = pltpu.PrefetchScalarGridSpec(
        num_scalar_prefetch=1,
        grid=(batch, n_tiles),
        in_specs=[
            pl.BlockSpec((1, ts, D_MODEL), lambda b, t, *_: (b, t, 0)),
            pl.BlockSpec((1, SUB_TILES[0], D_MODEL), next_first_sub_tile),
            pl.BlockSpec((1, MEM_LEN, D_MODEL), lambda b, t, *_: (b, 0, 0)),
            pl.BlockSpec((D_MODEL, IN_WIDTH), const2),
            pl.BlockSpec((D_MODEL, 2 * MEM_WIDTH), const2),
            pl.BlockSpec((MIX_WIDTH, D_MODEL), const2),
            pl.BlockSpec((1, D_MODEL), const2),
            pl.BlockSpec((1, D_MODEL), const2),
            pl.BlockSpec((1, D_MODEL), const2),
            pl.BlockSpec((1, A_WIDTH), const2),
            pl.BlockSpec((1, A_WIDTH), const2),
            pl.BlockSpec((A_GROUPS, CHUNK, CHUNK), const3),
            pl.BlockSpec((A_GROUPS, CHUNK), const2),
            pl.BlockSpec((N_BUCKETS, SWA_HEADS), const2),
            pl.BlockSpec((CHUNK, 2 * CHUNK), const2),
        ],
        out_specs=pl.BlockSpec((1, ts, D_MODEL), lambda b, t, *_: (b, t, 0)),
        scratch_shapes=[
            pltpu.VMEM((ts, D_MODEL), BF16),
            pltpu.VMEM((ts, A_WIDTH), F32),
            pltpu.VMEM((ts, A_WIDTH), BF16),
            pltpu.VMEM((ts, SWA_WIDTH + MEM_WIDTH), BF16),
            pltpu.VMEM((SWA_HEADS, LANES, CHUNK + ts), BF16),
            pltpu.VMEM((SWA_HEADS, CHUNK + ts, LANES), BF16),
            pltpu.VMEM((ts, MIX_WIDTH), F32),
            pltpu.VMEM((MEM_HEADS, LANES, MEM_LEN), BF16),
            pltpu.VMEM((MEM_HEADS, MEM_LEN, LANES), BF16),
            pltpu.VMEM((2, SWA_HEADS, CHUNK, 2 * CHUNK), F32),
            pltpu.VMEM((A_GROUPS, CHUNK, CHUNK), BF16),
            pltpu.VMEM((A_GROUPS, CHUNK, LANES), F32),
            pltpu.VMEM((D_MODEL, 2 * MEM_WIDTH), BF16),
            pltpu.VMEM((MIX_WIDTH, D_MODEL), BF16),
            pltpu.VMEM((ts, MIX_WIDTH), F32),
            pltpu.VMEM((ts, MIX_WIDTH), BF16),
            pltpu.VMEM((max(SUB_TILES), D_MODEL), F32),
        ],
    )
    return pl.pallas_call(
        _layer_kernel,
        grid_spec=grid_spec,
        out_shape=jax.ShapeDtypeStruct(x.shape, x.dtype),
        compiler_params=pltpu.CompilerParams(
            dimension_semantics=("arbitrary", "arbitrary"),
            vmem_limit_bytes=VMEM_LIMIT_BYTES),
        name="layer",
    )(sinks, x, x, mem, w_in_bf16, w_kv, w_out, pre_g, mem_g, post_g, vng, vnb,
      w_spatial, b_spatial, rel_bias, buckets)


def kernel(x, mem, pre_norm_g, post_norm_g, mem_norm_g, w_in, w_mem_kv, v_norm_g, v_norm_b,
           w_spatial, b_spatial, attn_sinks, rel_bias, w_out):
    depth = w_in.shape[0]
    buckets = jnp.asarray(_band_buckets())
    for layer in range(depth):
        x = _layer(x, mem, w_in[layer].astype(BF16), w_mem_kv[layer], w_out[layer],
                   pre_norm_g[layer][None, :], mem_norm_g[layer][None, :], post_norm_g[layer][None, :],
                   v_norm_g[layer][None, :], v_norm_b[layer][None, :],
                   w_spatial[layer], b_spatial[layer], buckets, attn_sinks[layer], rel_bias)
    return x
```

```python
import numpy as np
import jax
import jax.numpy as jnp
from jax import lax
from jax.experimental import pallas as pl
from jax.experimental.pallas import tpu as pltpu

D_MODEL = 1024
MEM_LEN = 256
HEAD_DIM = 64
CHUNK = 128
A_GROUPS = 4
A_WIDTH = 512
SWA_HEADS = 4
SWA_WIDTH = 256
SWA_KV_WIDTH = 128
MEM_HEADS = 4
MEM_WIDTH = 256
MIX_WIDTH = 1024
IN_WIDTH = 2816
N_BUCKETS = 32
MAX_DISTANCE = 128
WINDOW = 128
EPS = 1e-6
NEG = -1e30
LOG2E = float(np.log2(np.e))
Q_SCALE = HEAD_DIM ** -0.5 * LOG2E

OFF_U = 0
OFF_V = A_WIDTH
OFF_SQ = 2 * A_WIDTH
OFF_SK = OFF_SQ + SWA_WIDTH
OFF_SV = OFF_SK + SWA_KV_WIDTH
OFF_MQ = OFF_SV + SWA_KV_WIDTH
OFF_Z = OFF_MQ + MEM_WIDTH

LANES = 128
SEQ_TILE = 1024
SUB_TILES = (256, 256, 256, 256)
PROJ_COLS = 512
ITEM_LAG = 2
PROJECT_COLUMN_PIECES = 6
PROJECT_OUT_DOTS = D_MODEL // PROJ_COLS
HEAD_STAGES_PER_CHUNK = 2 * (SWA_HEADS // 2 + MEM_HEADS // 2)
VMEM_LIMIT_BYTES = 62 * 1024 * 1024

BF16 = jnp.bfloat16
F32 = jnp.float32


def _t5_causal_buckets(dist):
    n = np.maximum(dist, 0)
    max_exact = N_BUCKETS // 2
    large = max_exact + (np.log(np.maximum(n, 1) / max_exact) / np.log(MAX_DISTANCE / max_exact)
                         * (N_BUCKETS - max_exact)).astype(np.int32)
    large = np.minimum(large, N_BUCKETS - 1)
    return np.where(n < max_exact, n, large).astype(np.int32)


def _band_buckets():
    qi = np.arange(CHUNK)[:, None]
    kj = np.arange(2 * CHUNK)[None, :]
    dist = qi + CHUNK - kj
    valid = (dist >= 0) & (dist < WINDOW)
    return np.where(valid, _t5_causal_buckets(dist), -1).astype(np.int32)


def _gelu_tanh(x):
    c = np.float32(np.sqrt(2.0 / np.pi))
    ck = np.float32(np.sqrt(2.0 / np.pi) * 0.044715)
    hx = 0.5 * x
    return hx + hx * jnp.tanh(x * (c + ck * (x * x)))


def _silu(z):
    hz = 0.5 * z
    return hz + hz * jnp.tanh(hz)


def _dot(a, b):
    return jnp.dot(a, b, preferred_element_type=F32)


_DONE = object()


def _skewed(items, lag):
    waiting = []
    for item in items:
        if next(item, _DONE) is not _DONE:
            waiting.append(item)
        yield
        if len(waiting) > lag:
            next(waiting.pop(0), _DONE)
            yield
    for item in waiting:
        next(item, _DONE)
        yield


def _chain(*gens):
    for g in gens:
        yield from g


def _interleave(a, b, b_per_a):
    credit = 0.0
    a_live = b_live = True
    while a_live or b_live:
        if a_live:
            a_live = next(a, _DONE) is not _DONE
        credit += b_per_a
        while b_live and (credit >= 1.0 or not a_live):
            b_live = next(b, _DONE) is not _DONE
            credit -= 1.0


def _lane_half_mask(parity):
    lane = lax.broadcasted_iota(jnp.int32, (1, LANES), 1)
    return (lane >= HEAD_DIM) if parity else (lane < HEAD_DIM)


def _place_head_rows(kt_head, parity):
    zeros = jnp.zeros_like(kt_head)
    return jnp.concatenate([zeros, kt_head] if parity else [kt_head, zeros], axis=0)


def _layer_kernel(sinks_ref,
                  x_ref, xnext_ref, mem_ref, win_ref, wkv_f32_ref, wout_f32_ref, preg_ref, memg_ref, postg_ref,
                  vng_ref, vnb_ref, ws_ref, bs_ref, relb_ref, buckets_ref,
                  out_ref,
                  h_scr, gu_scr, vn_scr, qs_scr, ktvar_scr, vvar_scr, zg_scr,
                  mk_scr, mv_scr, bias_scr, wsm_scr, bst_scr, wkv_ref, wout_ref,
                  y_scr, ybf_scr, o_scr):
    ts = x_ref.shape[1]
    b = pl.program_id(0)
    t = pl.program_id(1)

    @pl.when((b == 0) & (t == 0))
    def _init():
        row = lax.broadcasted_iota(jnp.int32, (CHUNK, CHUNK), 0)
        col = lax.broadcasted_iota(jnp.int32, (CHUNK, CHUNK), 1)
        for g in range(A_GROUPS):
            wsm_scr[g] = jnp.where(row >= col, ws_ref[g], 0.0).astype(BF16)
            bs_col = jnp.sum(jnp.where(row == col, bs_ref[g:g + 1, :], 0.0), axis=1, keepdims=True)
            bst_scr[g] = jnp.broadcast_to(bs_col, (CHUNK, LANES))
        for r0 in range(0, D_MODEL, 2 * CHUNK):
            wrows = pl.ds(r0, 2 * CHUNK)
            wkv_ref[wrows, :] = wkv_f32_ref[wrows, :].astype(BF16)
            wout_ref[wrows, :] = wout_f32_ref[wrows, :].astype(BF16)
        buckets = buckets_ref[...]
        prev_key = lax.broadcasted_iota(jnp.int32, (CHUNK, 2 * CHUNK), 1) < CHUNK
        for hd in range(SWA_HEADS):
            tbl = jnp.full((CHUNK, 2 * CHUNK), NEG, F32)
            for bk in range(N_BUCKETS):
                tbl = jnp.where(buckets == bk, relb_ref[bk:bk + 1, hd:hd + 1] * LOG2E, tbl)
            bias_scr[0, hd] = tbl
            bias_scr[1, hd] = jnp.where(prev_key, NEG, tbl)

    @pl.when(t == 0)
    def _sequence_start():
        ktvar_scr[:, :, 0:CHUNK] = jnp.zeros((SWA_HEADS, LANES, CHUNK), BF16)
        vvar_scr[:, 0:CHUNK, :] = jnp.zeros((SWA_HEADS, CHUNK, LANES), BF16)
        m = mem_ref[0]
        ms = jnp.mean(m * m, axis=-1, keepdims=True)
        hm = (m * lax.rsqrt(ms + EPS) * memg_ref[...]).astype(BF16)
        kv = _dot(hm, wkv_ref[...])
        for pair in range(MEM_HEADS // 2):
            kt_pair = kv[:, pair * LANES:(pair + 1) * LANES].T
            v_pair = kv[:, MEM_WIDTH + pair * LANES:MEM_WIDTH + (pair + 1) * LANES]
            for parity in range(2):
                hd = 2 * pair + parity
                kt_head = kt_pair[parity * HEAD_DIM:(parity + 1) * HEAD_DIM]
                mk_scr[hd] = _place_head_rows(kt_head, parity).astype(BF16)
                mv_scr[hd] = jnp.where(_lane_half_mask(parity), v_pair, 0.0).astype(BF16)

    @pl.when(t > 0)
    def _copy_carry():
        ktvar_scr[:, :, 0:CHUNK] = ktvar_scr[:, :, ts:ts + CHUNK]
        vvar_scr[:, 0:CHUNK, :] = vvar_scr[:, ts:ts + CHUNK, :]

    lo = _lane_half_mask(0)
    hi = _lane_half_mask(1)
    first_tile = jnp.where(t == 0, 1, 0)

    def pre_norm(r0, nrows, src_ref=x_ref):
        for rc in range(nrows // CHUNK):
            crows = pl.ds(r0 + rc * CHUNK, CHUNK)
            x = src_ref[0, crows, :]
            ms = jnp.mean(x * x, axis=-1, keepdims=True)
            h_scr[crows, :] = (x * lax.rsqrt(ms + EPS) * preg_ref[...]).astype(BF16)

    def proj(rows, c0, width):
        return _dot(h_scr[rows, :], win_ref[:, c0:c0 + width])

    def gating_pieces(r0, nrows):
        for c in range(r0 // CHUNK, (r0 + nrows) // CHUNK):
            for g in range(A_GROUPS):
                spatial_gating(c, g)
            yield

    def project(r0, nrows, gmlp=True, rest=True, gating=True):
        rows = pl.ds(r0, nrows)

        if gmlp:
            v_all = _gelu_tanh(proj(rows, OFF_V, A_WIDTH))
            for g in range(A_GROUPS):
                cols = pl.ds(g * LANES, LANES)
                v = v_all[:, g * LANES:(g + 1) * LANES]
                mu = jnp.mean(v, axis=-1, keepdims=True)
                d = v - mu
                var = jnp.mean(d * d, axis=-1, keepdims=True)
                vn = d * lax.rsqrt(var + EPS) * vng_ref[:, cols] + vnb_ref[:, cols]
                vn_scr[rows, cols] = vn.astype(BF16)
            yield

            gu_scr[rows, :] = _gelu_tanh(proj(rows, OFF_U, A_WIDTH))
            yield

        if rest:
            qkv = proj(rows, OFF_SQ, SWA_WIDTH + 2 * SWA_KV_WIDTH)
            qs_scr[rows, 0:SWA_WIDTH] = (qkv[:, 0:SWA_WIDTH] * Q_SCALE).astype(BF16)
            skt = qkv[:, SWA_WIDTH:SWA_WIDTH + LANES].T
            sv = qkv[:, SWA_WIDTH + LANES:]
            sv_rot = pltpu.roll(sv, HEAD_DIM, 1)
            new_rows = pl.ds(CHUNK + r0, nrows)
            for hd, (keep, vsrc) in enumerate([(lo, sv), (hi, sv_rot), (lo, sv_rot), (hi, sv)]):
                kv_head = hd // 2
                kt_head = skt[kv_head * HEAD_DIM:(kv_head + 1) * HEAD_DIM]
                ktvar_scr[hd, :, new_rows] = _place_head_rows(kt_head, hd % 2).astype(BF16)
                vvar_scr[hd, new_rows, :] = jnp.where(keep, vsrc, 0.0).astype(BF16)
            yield

            mqz = proj(rows, OFF_MQ, PROJ_COLS)
            qs_scr[rows, SWA_WIDTH:] = (mqz[:, 0:MEM_WIDTH] * Q_SCALE).astype(BF16)
            zg_scr[rows, 0:PROJ_COLS - MEM_WIDTH] = _silu(mqz[:, MEM_WIDTH:])
            yield

        if gmlp and gating:
            yield from gating_pieces(r0, nrows)

        if rest:
            done = PROJ_COLS - MEM_WIDTH
            while done < MIX_WIDTH:
                width = min(PROJ_COLS, MIX_WIDTH - done)
                zg_scr[rows, done:done + width] = _silu(proj(rows, OFF_Z + done, width))
                done += width
                yield

    def spatial_gating(c, g):
        rows = pl.ds(c * CHUNK, CHUNK)
        cols = pl.ds(g * LANES, LANES)
        sv_g = _dot(wsm_scr[g], vn_scr[rows, cols]) + bst_scr[g]
        y_scr[rows, cols] = gu_scr[rows, cols] * sv_g

    def attn_item(c, pair, sliding):
        r0 = c * CHUNK
        rows = pl.ds(r0, CHUNK)
        heads = (2 * pair, 2 * pair + 1)
        if sliding:
            band = pl.ds(r0, 2 * CHUNK)
            qp = qs_scr[rows, pl.ds(pair * LANES, LANES)]
            keys = jnp.concatenate([ktvar_scr[hd, :, band] for hd in heads], axis=1)
            bias_sel = first_tile if c == 0 else 0
            bias = jnp.concatenate([bias_scr[bias_sel, hd] for hd in heads], axis=1)
            logits = _dot(qp, keys) + bias
        else:
            qp = qs_scr[rows, pl.ds(SWA_WIDTH + pair * LANES, LANES)]
            keys = jnp.concatenate([mk_scr[hd] for hd in heads], axis=1)
            logits = _dot(qp, keys)
        n_keys = logits.shape[1] // 2
        yield
        probs, recips = [], []
        for i, hd in enumerate(heads):
            s = logits[:, i * n_keys:(i + 1) * n_keys]
            m = jnp.max(s, axis=-1, keepdims=True)
            if sliding:
                sink = sinks_ref[hd] * LOG2E
                m = jnp.maximum(m, sink)
            p = jnp.exp2(s - m)
            denom = jnp.sum(p, axis=-1, keepdims=True)
            if sliding:
                denom = denom + jnp.exp2(sink - m)
            probs.append(p.astype(BF16))
            recips.append(1.0 / denom)
        if sliding:
            values = jnp.concatenate([vvar_scr[hd, band, :] for hd in heads], axis=0)
        else:
            values = jnp.concatenate([mv_scr[hd] for hd in heads], axis=0)
        o_pair = _dot(jnp.concatenate(probs, axis=1), values)
        o_pair = o_pair * jnp.where(lo, recips[0], recips[1])
        off = A_WIDTH if sliding else A_WIDTH + SWA_WIDTH
        y_scr[rows, pl.ds(off + pair * LANES, LANES)] = o_pair

    def head_items(r0, nrows):
        for c in range(r0 // CHUNK, (r0 + nrows) // CHUNK):
            for pair in range(SWA_HEADS // 2):
                yield attn_item(c, pair, True)
            for pair in range(MEM_HEADS // 2):
                yield attn_item(c, pair, False)

    def project_out(r0, nrows, k0=0, k1=MIX_WIDTH):
        rows = pl.ds(r0, nrows)
        ybf_scr[rows, k0:k1] = (y_scr[rows, k0:k1] * zg_scr[rows, k0:k1]).astype(BF16)
        for c0 in range(0, D_MODEL, PROJ_COLS):
            cols = pl.ds(c0, PROJ_COLS)
            part = _dot(ybf_scr[rows, k0:k1], wout_ref[k0:k1, cols])
            o_scr[rows, cols] = o_scr[rows, cols] + part if k0 else part
            yield
        if k1 < MIX_WIDTH:
            return
        for rc in range(nrows // CHUNK):
            crows = pl.ds(r0 + rc * CHUNK, CHUNK)
            o = o_scr[crows, :]
            ms = jnp.mean(o * o, axis=-1, keepdims=True)
            out_ref[0, crows, :] = x_ref[0, crows, :] + o * lax.rsqrt(ms + EPS) * postg_ref[...]
            yield

    assert sum(SUB_TILES) == ts
    starts = [sum(SUB_TILES[:s]) for s in range(len(SUB_TILES))]
    subs = list(zip(starts, SUB_TILES))
    n_sub = len(subs)

    @pl.when((b == 0) & (t == 0))
    def _first_projection():
        pre_norm(*subs[0])
        for _ in project(*subs[0], gating=False):
            pass

    for s in range(n_sub):
        heads = _skewed(head_items(*subs[s]), ITEM_LAG)
        if s + 1 < n_sub:
            pre_norm(*subs[s + 1])
            filler = project(*subs[s + 1])
            n_filler = PROJECT_COLUMN_PIECES + subs[s + 1][1] // CHUNK
            if s == 0:
                filler = _chain(gating_pieces(*subs[0]), filler)
                n_filler += subs[0][1] // CHUNK
        else:
            filler = project_out(*subs[s], 0, A_WIDTH)
            n_filler = PROJECT_OUT_DOTS
        if s >= 1:
            filler = _chain(project_out(*subs[s - 1]), filler)
            n_filler += PROJECT_OUT_DOTS + subs[s - 1][1] // CHUNK
        _interleave(heads, filler, n_filler / ((subs[s][1] // CHUNK) * HEAD_STAGES_PER_CHUNK))
    for _ in project_out(*subs[-1], A_WIDTH, MIX_WIDTH):
        pass
    pre_norm(*subs[0], src_ref=xnext_ref)
    for _ in project(*subs[0], gating=False):
        pass


def _layer(x, mem, w_in_bf16, w_kv, w_out, pre_g, mem_g, post_g, vng, vnb, w_spatial,
           b_spatial, buckets, sinks, rel_bias):
    batch, seq, _ = x.shape
    ts = SEQ_TILE
    n_tiles = seq // ts
    const2 = lambda b, t, *_: (0, 0)
    const3 = lambda b, t, *_: (0, 0, 0)

    def next_first_sub_tile(b, t, *_):
        step = jnp.minimum(b * n_tiles + t + 1, batch * n_tiles - 1)
        return (step // n_tiles, (step % n_tiles) * (ts // SUB_TILES[0]), 0)

    grid_spec = pltpu.PrefetchScalarGridSpec(
        num_scalar_prefetch=1,
        grid=(batch, n_tiles),
        in_specs=[
            pl.BlockSpec((1, ts, D_MODEL), lambda b, t, *_: (b, t, 0)),
            pl.BlockSpec((1, SUB_TILES[0], D_MODEL), next_first_sub_tile),
            pl.BlockSpec((1, MEM_LEN, D_MODEL), lambda b, t, *_: (b, 0, 0)),
            pl.BlockSpec((D_MODEL, IN_WIDTH), const2),
            pl.BlockSpec((D_MODEL, 2 * MEM_WIDTH), const2),
            pl.BlockSpec((MIX_WIDTH, D_MODEL), const2),
            pl.BlockSpec((1, D_MODEL), const2),
            pl.BlockSpec((1, D_MODEL), const2),
            pl.BlockSpec((1, D_MODEL), const2),
            pl.BlockSpec((1, A_WIDTH), const2),
            pl.BlockSpec((1, A_WIDTH), const2),
            pl.BlockSpec((A_GROUPS, CHUNK, CHUNK), const3),
            pl.BlockSpec((A_GROUPS, CHUNK), const2),
            pl.BlockSpec((N_BUCKETS, SWA_HEADS), const2),
            pl.BlockSpec((CHUNK, 2 * CHUNK), const2),
        ],
        out_specs=pl.BlockSpec((1, ts, D_MODEL), lambda b, t, *_: (b, t, 0)),
        scratch_shapes=[
            pltpu.VMEM((ts, D_MODEL), BF16),
            pltpu.VMEM((ts, A_WIDTH), F32),
            pltpu.VMEM((ts, A_WIDTH), BF16),
            pltpu.VMEM((ts, SWA_WIDTH + MEM_WIDTH), BF16),
            pltpu.VMEM((SWA_HEADS, LANES, CHUNK + ts), BF16),
            pltpu.VMEM((SWA_HEADS, CHUNK + ts, LANES), BF16),
            pltpu.VMEM((ts, MIX_WIDTH), F32),
            pltpu.VMEM((MEM_HEADS, LANES, MEM_LEN), BF16),
            pltpu.VMEM((MEM_HEADS, MEM_LEN, LANES), BF16),
            pltpu.VMEM((2, SWA_HEADS, CHUNK, 2 * CHUNK), F32),
            pltpu.VMEM((A_GROUPS, CHUNK, CHUNK), BF16),
            pltpu.VMEM((A_GROUPS, CHUNK, LANES), F32),
            pltpu.VMEM((D_MODEL, 2 * MEM_WIDTH), BF16),
            pltpu.VMEM((MIX_WIDTH, D_MODEL), BF16),
            pltpu.VMEM((ts, MIX_WIDTH), F32),
            pltpu.VMEM((ts, MIX_WIDTH), BF16),
            pltpu.VMEM((ts, D_MODEL), F32),
        ],
    )
    return pl.pallas_call(
        _layer_kernel,
        grid_spec=grid_spec,
        out_shape=jax.ShapeDtypeStruct(x.shape, x.dtype),
        compiler_params=pltpu.CompilerParams(
            dimension_semantics=("arbitrary", "arbitrary"),
            vmem_limit_bytes=VMEM_LIMIT_BYTES),
        name="layer",
    )(sinks, x, x, mem, w_in_bf16, w_kv, w_out, pre_g, mem_g, post_g, vng, vnb,
      w_spatial, b_spatial, rel_bias, buckets)


def kernel(x, mem, pre_norm_g, post_norm_g, mem_norm_g, w_in, w_mem_kv, v_norm_g, v_norm_b,
           w_spatial, b_spatial, attn_sinks, rel_bias, w_out):
    depth = w_in.shape[0]
    buckets = jnp.asarray(_band_buckets())
    for layer in range(depth):
        x = _layer(x, mem, w_in[layer].astype(BF16), w_mem_kv[layer], w_out[layer],
                   pre_norm_g[layer][None, :], mem_norm_g[layer][None, :], post_norm_g[layer][None, :],
                   v_norm_g[layer][None, :], v_norm_b[layer][None, :],
                   w_spatial[layer], b_spatial[layer], buckets, attn_sinks[layer], rel_bias)
    return x
```

```python
import numpy as np
import jax
import jax.numpy as jnp
from jax import lax
from jax.experimental import pallas as pl
from jax.experimental.pallas import tpu as pltpu

D_MODEL = 1024
MEM_LEN = 256
HEAD_DIM = 64
CHUNK = 128
A_GROUPS = 4
A_WIDTH = 512
SWA_HEADS = 4
SWA_WIDTH = 256
SWA_KV_WIDTH = 128
MEM_HEADS = 4
MEM_WIDTH = 256
MIX_WIDTH = 1024
IN_WIDTH = 2816
N_BUCKETS = 32
MAX_DISTANCE = 128
WINDOW = 128
EPS = 1e-6
NEG = -1e30
LOG2E = float(np.log2(np.e))
Q_SCALE = HEAD_DIM ** -0.5 * LOG2E

OFF_U = 0
OFF_V = A_WIDTH
OFF_SQ = 2 * A_WIDTH
OFF_SK = OFF_SQ + SWA_WIDTH
OFF_SV = OFF_SK + SWA_KV_WIDTH
OFF_MQ = OFF_SV + SWA_KV_WIDTH
OFF_Z = OFF_MQ + MEM_WIDTH

LANES = 128
SEQ_TILE = 1024
SUB_TILES = (256, 256, 256, 256)
PROJ_COLS = 512
ITEM_LAG = 2
PROJECT_COLUMN_PIECES = 6
PROJECT_OUT_DOTS = D_MODEL // PROJ_COLS
HEAD_STAGES_PER_CHUNK = 2 * (SWA_HEADS // 2 + MEM_HEADS // 2)
VMEM_LIMIT_BYTES = 62 * 1024 * 1024

BF16 = jnp.bfloat16
F32 = jnp.float32


def _t5_causal_buckets(dist):
    n = np.maximum(dist, 0)
    max_exact = N_BUCKETS // 2
    large = max_exact + (np.log(np.maximum(n, 1) / max_exact) / np.log(MAX_DISTANCE / max_exact)
                         * (N_BUCKETS - max_exact)).astype(np.int32)
    large = np.minimum(large, N_BUCKETS - 1)
    return np.where(n < max_exact, n, large).astype(np.int32)


def _band_buckets():
    qi = np.arange(CHUNK)[:, None]
    kj = np.arange(2 * CHUNK)[None, :]
    dist = qi + CHUNK - kj
    valid = (dist >= 0) & (dist < WINDOW)
    return np.where(valid, _t5_causal_buckets(dist), -1).astype(np.int32)


def _gelu_tanh(x):
    c = np.float32(np.sqrt(2.0 / np.pi))
    ck = np.float32(np.sqrt(2.0 / np.pi) * 0.044715)
    hx = 0.5 * x
    return hx + hx * jnp.tanh(x * (c + ck * (x * x)))


def _silu(z):
    hz = 0.5 * z
    return hz + hz * jnp.tanh(hz)


def _dot(a, b):
    return jnp.dot(a, b, preferred_element_type=F32)


_DONE = object()


def _skewed(items, lag):
    waiting = []
    for item in items:
        if next(item, _DONE) is not _DONE:
            waiting.append(item)
        yield
        if len(waiting) > lag:
            next(waiting.pop(0), _DONE)
            yield
    for item in waiting:
        next(item, _DONE)
        yield


def _chain(*gens):
    for g in gens:
        yield from g


def _interleave(a, b, b_per_a):
    credit = 0.0
    a_live = b_live = True
    while a_live or b_live:
        if a_live:
            a_live = next(a, _DONE) is not _DONE
        credit += b_per_a
        while b_live and (credit >= 1.0 or not a_live):
            b_live = next(b, _DONE) is not _DONE
            credit -= 1.0


def _lane_half_mask(parity):
    lane = lax.broadcasted_iota(jnp.int32, (1, LANES), 1)
    return (lane >= HEAD_DIM) if parity else (lane < HEAD_DIM)


def _place_head_rows(kt_head, parity):
    zeros = jnp.zeros_like(kt_head)
    return jnp.concatenate([zeros, kt_head] if parity else [kt_head, zeros], axis=0)


def _layer_kernel(sinks_ref,
                  x_ref, xnext_ref, mem_ref, win_ref, wkv_f32_ref, wout_f32_ref, preg_ref, memg_ref, postg_ref,
                  vng_ref, vnb_ref, ws_ref, bs_ref, relb_ref, buckets_ref,
                  out_ref,
                  h_scr, gu_scr, vn_scr, qs_scr, ktvar_scr, vvar_scr, zg_scr,
                  mk_scr, mv_scr, bias_scr, wsm_scr, bst_scr, wkv_ref, wout_ref,
                  y_scr, ybf_scr, o_scr):
    ts = x_ref.shape[1]
    b = pl.program_id(0)
    t = pl.program_id(1)

    @pl.when((b == 0) & (t == 0))
    def _init():
        row = lax.broadcasted_iota(jnp.int32, (CHUNK, CHUNK), 0)
        col = lax.broadcasted_iota(jnp.int32, (CHUNK, CHUNK), 1)
        for g in range(A_GROUPS):
            wsm_scr[g] = jnp.where(row >= col, ws_ref[g], 0.0).astype(BF16)
            bs_col = jnp.sum(jnp.where(row == col, bs_ref[g:g + 1, :], 0.0), axis=1, keepdims=True)
            bst_scr[g] = jnp.broadcast_to(bs_col, (CHUNK, LANES))
        for r0 in range(0, D_MODEL, 2 * CHUNK):
            wrows = pl.ds(r0, 2 * CHUNK)
            wkv_ref[wrows, :] = wkv_f32_ref[wrows, :].astype(BF16)
            wout_ref[wrows, :] = wout_f32_ref[wrows, :].astype(BF16)
        buckets = buckets_ref[...]
        prev_key = lax.broadcasted_iota(jnp.int32, (CHUNK, 2 * CHUNK), 1) < CHUNK
        for hd in range(SWA_HEADS):
            tbl = jnp.full((CHUNK, 2 * CHUNK), NEG, F32)
            for bk in range(N_BUCKETS):
                tbl = jnp.where(buckets == bk, relb_ref[bk:bk + 1, hd:hd + 1] * LOG2E, tbl)
            bias_scr[0, hd] = tbl
            bias_scr[1, hd] = jnp.where(prev_key, NEG, tbl)

    @pl.when(t == 0)
    def _sequence_start():
        ktvar_scr[:, :, 0:CHUNK] = jnp.zeros((SWA_HEADS, LANES, CHUNK), BF16)
        vvar_scr[:, 0:CHUNK, :] = jnp.zeros((SWA_HEADS, CHUNK, LANES), BF16)
        m = mem_ref[0]
        ms = jnp.mean(m * m, axis=-1, keepdims=True)
        hm = (m * lax.rsqrt(ms + EPS) * memg_ref[...]).astype(BF16)
        kv = _dot(hm, wkv_ref[...])
        for pair in range(MEM_HEADS // 2):
            kt_pair = kv[:, pair * LANES:(pair + 1) * LANES].T
            v_pair = kv[:, MEM_WIDTH + pair * LANES:MEM_WIDTH + (pair + 1) * LANES]
            for parity in range(2):
                hd = 2 * pair + parity
                kt_head = kt_pair[parity * HEAD_DIM:(parity + 1) * HEAD_DIM]
                mk_scr[hd] = _place_head_rows(kt_head, parity).astype(BF16)
                mv_scr[hd] = jnp.where(_lane_half_mask(parity), v_pair, 0.0).astype(BF16)

    @pl.when(t > 0)
    def _copy_carry():
        ktvar_scr[:, :, 0:CHUNK] = ktvar_scr[:, :, ts:ts + CHUNK]
        vvar_scr[:, 0:CHUNK, :] = vvar_scr[:, ts:ts + CHUNK, :]

    lo = _lane_half_mask(0)
    hi = _lane_half_mask(1)
    first_tile = jnp.where(t == 0, 1, 0)

    def pre_norm(r0, nrows, src_ref=x_ref):
        for rc in range(nrows // CHUNK):
            crows = pl.ds(r0 + rc * CHUNK, CHUNK)
            x = src_ref[0, crows, :]
            ms = jnp.mean(x * x, axis=-1, keepdims=True)
            h_scr[crows, :] = (x * lax.rsqrt(ms + EPS) * preg_ref[...]).astype(BF16)

    def proj(rows, c0, width):
        return _dot(h_scr[rows, :], win_ref[:, c0:c0 + width])

    def gating_pieces(r0, nrows):
        for c in range(r0 // CHUNK, (r0 + nrows) // CHUNK):
            for g in range(A_GROUPS):
                spatial_gating(c, g)
            yield

    def project(r0, nrows, gmlp=True, rest=True, gating=True, qkv_last=False):
        rows = pl.ds(r0, nrows)

        if gmlp:
            v_all = _gelu_tanh(proj(rows, OFF_V, A_WIDTH))
            for g in range(A_GROUPS):
                cols = pl.ds(g * LANES, LANES)
                v = v_all[:, g * LANES:(g + 1) * LANES]
                mu = jnp.mean(v, axis=-1, keepdims=True)
                d = v - mu
                var = jnp.mean(d * d, axis=-1, keepdims=True)
                vn = d * lax.rsqrt(var + EPS) * vng_ref[:, cols] + vnb_ref[:, cols]
                vn_scr[rows, cols] = vn.astype(BF16)
            yield

            gu_scr[rows, :] = _gelu_tanh(proj(rows, OFF_U, A_WIDTH))
            yield

        def qkv_piece():
            qkv = proj(rows, OFF_SQ, SWA_WIDTH + 2 * SWA_KV_WIDTH)
            qs_scr[rows, 0:SWA_WIDTH] = (qkv[:, 0:SWA_WIDTH] * Q_SCALE).astype(BF16)
            skt = qkv[:, SWA_WIDTH:SWA_WIDTH + LANES].T
            sv = qkv[:, SWA_WIDTH + LANES:]
            sv_rot = pltpu.roll(sv, HEAD_DIM, 1)
            new_rows = pl.ds(CHUNK + r0, nrows)
            for hd, (keep, vsrc) in enumerate([(lo, sv), (hi, sv_rot), (lo, sv_rot), (hi, sv)]):
                kv_head = hd // 2
                kt_head = skt[kv_head * HEAD_DIM:(kv_head + 1) * HEAD_DIM]
                ktvar_scr[hd, :, new_rows] = _place_head_rows(kt_head, hd % 2).astype(BF16)
                vvar_scr[hd, new_rows, :] = jnp.where(keep, vsrc, 0.0).astype(BF16)

        if rest:
            if not qkv_last:
                qkv_piece()
                yield

            mqz = proj(rows, OFF_MQ, PROJ_COLS)
            qs_scr[rows, SWA_WIDTH:] = (mqz[:, 0:MEM_WIDTH] * Q_SCALE).astype(BF16)
            zg_scr[rows, 0:PROJ_COLS - MEM_WIDTH] = _silu(mqz[:, MEM_WIDTH:])
            yield

        if gmlp and gating:
            yield from gating_pieces(r0, nrows)

        if rest:
            done = PROJ_COLS - MEM_WIDTH
            while done < MIX_WIDTH:
                width = min(PROJ_COLS, MIX_WIDTH - done)
                zg_scr[rows, done:done + width] = _silu(proj(rows, OFF_Z + done, width))
                done += width
                yield
            if qkv_last:
                qkv_piece()
                yield

    def spatial_gating(c, g):
        rows = pl.ds(c * CHUNK, CHUNK)
        cols = pl.ds(g * LANES, LANES)
        sv_g = _dot(wsm_scr[g], vn_scr[rows, cols]) + bst_scr[g]
        y_scr[rows, cols] = gu_scr[rows, cols] * sv_g

    def attn_item(c, pair, sliding):
        r0 = c * CHUNK
        rows = pl.ds(r0, CHUNK)
        heads = (2 * pair, 2 * pair + 1)
        if sliding:
            band = pl.ds(r0, 2 * CHUNK)
            qp = qs_scr[rows, pl.ds(pair * LANES, LANES)]
            keys = jnp.concatenate([ktvar_scr[hd, :, band] for hd in heads], axis=1)
            bias_sel = first_tile if c == 0 else 0
            bias = jnp.concatenate([bias_scr[bias_sel, hd] for hd in heads], axis=1)
            logits = _dot(qp, keys) + bias
        else:
            qp = qs_scr[rows, pl.ds(SWA_WIDTH + pair * LANES, LANES)]
            keys = jnp.concatenate([mk_scr[hd] for hd in heads], axis=1)
            logits = _dot(qp, keys)
        n_keys = logits.shape[1] // 2
        yield
        probs, recips = [], []
        for i, hd in enumerate(heads):
            s = logits[:, i * n_keys:(i + 1) * n_keys]
            m = jnp.max(s, axis=-1, keepdims=True)
            if sliding:
                sink = sinks_ref[hd] * LOG2E
                m = jnp.maximum(m, sink)
            p = jnp.exp2(s - m)
            denom = jnp.sum(p, axis=-1, keepdims=True)
            if sliding:
                denom = denom + jnp.exp2(sink - m)
            probs.append(p.astype(BF16))
            recips.append(1.0 / denom)
        if sliding:
            values = jnp.concatenate([vvar_scr[hd, band, :] for hd in heads], axis=0)
        else:
            values = jnp.concatenate([mv_scr[hd] for hd in heads], axis=0)
        o_pair = _dot(jnp.concatenate(probs, axis=1), values)
        o_pair = o_pair * jnp.where(lo, recips[0], recips[1])
        off = A_WIDTH if sliding else A_WIDTH + SWA_WIDTH
        y_scr[rows, pl.ds(off + pair * LANES, LANES)] = o_pair

    def head_items(r0, nrows):
        for c in range(r0 // CHUNK, (r0 + nrows) // CHUNK):
            for pair in range(SWA_HEADS // 2):
                yield attn_item(c, pair, True)
            for pair in range(MEM_HEADS // 2):
                yield attn_item(c, pair, False)

    def project_out(r0, nrows, k0=0, k1=MIX_WIDTH):
        rows = pl.ds(r0, nrows)
        ybf_scr[rows, k0:k1] = (y_scr[rows, k0:k1] * zg_scr[rows, k0:k1]).astype(BF16)
        for c0 in range(0, D_MODEL, PROJ_COLS):
            cols = pl.ds(c0, PROJ_COLS)
            part = _dot(ybf_scr[rows, k0:k1], wout_ref[k0:k1, cols])
            o_scr[rows, cols] = o_scr[rows, cols] + part if k0 else part
            yield
        if k1 < MIX_WIDTH:
            return
        for rc in range(nrows // CHUNK):
            crows = pl.ds(r0 + rc * CHUNK, CHUNK)
            o = o_scr[crows, :]
            ms = jnp.mean(o * o, axis=-1, keepdims=True)
            out_ref[0, crows, :] = x_ref[0, crows, :] + o * lax.rsqrt(ms + EPS) * postg_ref[...]
            yield

    assert sum(SUB_TILES) == ts
    starts = [sum(SUB_TILES[:s]) for s in range(len(SUB_TILES))]
    subs = list(zip(starts, SUB_TILES))
    n_sub = len(subs)

    @pl.when((b == 0) & (t == 0))
    def _first_projection():
        pre_norm(*subs[0])
        for _ in project(*subs[0]):
            pass

    for s in range(n_sub):
        heads = _skewed(head_items(*subs[s]), ITEM_LAG)
        if s + 1 < n_sub:
            pre_norm(*subs[s + 1])
            filler = project(*subs[s + 1])
            n_filler = PROJECT_COLUMN_PIECES + subs[s + 1][1] // CHUNK
        else:
            filler = project_out(*subs[s], 0, A_WIDTH)
            n_filler = PROJECT_OUT_DOTS
        if s >= 1:
            filler = _chain(project_out(*subs[s - 1]), filler)
            n_filler += PROJECT_OUT_DOTS + subs[s - 1][1] // CHUNK
        _interleave(heads, filler, n_filler / ((subs[s][1] // CHUNK) * HEAD_STAGES_PER_CHUNK))
    for _ in project_out(*subs[-1], A_WIDTH, MIX_WIDTH):
        pass
    pre_norm(*subs[0], src_ref=xnext_ref)
    for _ in project(*subs[0], qkv_last=True):
        pass


def _layer(x, mem, w_in_bf16, w_kv, w_out, pre_g, mem_g, post_g, vng, vnb, w_spatial,
           b_spatial, buckets, sinks, rel_bias):
    batch, seq, _ = x.shape
    ts = SEQ_TILE
    n_tiles = seq // ts
    const2 = lambda b, t, *_: (0, 0)
    const3 = lambda b, t, *_: (0, 0, 0)

    def next_first_sub_tile(b, t, *_):
        step = jnp.minimum(b * n_tiles + t + 1, batch * n_tiles - 1)
        return (step // n_tiles, (step % n_tiles) * (ts // SUB_TILES[0]), 0)

    grid_spec = pltpu.PrefetchScalarGridSpec(
        num_scalar_prefetch=1,
        grid=(batch, n_tiles),
        in_specs=[
            pl.BlockSpec((1, ts, D_MODEL), lambda b, t, *_: (b, t, 0)),
            pl.BlockSpec((1, SUB_TILES[0], D_MODEL), next_first_sub_tile),
            pl.BlockSpec((1, MEM_LEN, D_MODEL), lambda b, t, *_: (b, 0, 0)),
            pl.BlockSpec((D_MODEL, IN_WIDTH), const2),
            pl.BlockSpec((D_MODEL, 2 * MEM_WIDTH), const2),
            pl.BlockSpec((MIX_WIDTH, D_MODEL), const2),
            pl.BlockSpec((1, D_MODEL), const2),
            pl.BlockSpec((1, D_MODEL), const2),
            pl.BlockSpec((1, D_MODEL), const2),
            pl.BlockSpec((1, A_WIDTH), const2),
            pl.BlockSpec((1, A_WIDTH), const2),
            pl.BlockSpec((A_GROUPS, CHUNK, CHUNK), const3),
            pl.BlockSpec((A_GROUPS, CHUNK), const2),
            pl.BlockSpec((N_BUCKETS, SWA_HEADS), const2),
            pl.BlockSpec((CHUNK, 2 * CHUNK), const2),
        ],
        out_specs=pl.BlockSpec((1, ts, D_MODEL), lambda b, t, *_: (b, t, 0)),
        scratch_shapes=[
            pltpu.VMEM((ts, D_MODEL), BF16),
            pltpu.VMEM((ts, A_WIDTH), F32),
            pltpu.VMEM((ts, A_WIDTH), BF16),
            pltpu.VMEM((ts, SWA_WIDTH + MEM_WIDTH), BF16),
            pltpu.VMEM((SWA_HEADS, LANES, CHUNK + ts), BF16),
            pltpu.VMEM((SWA_HEADS, CHUNK + ts, LANES), BF16),
            pltpu.VMEM((ts, MIX_WIDTH), F32),
            pltpu.VMEM((MEM_HEADS, LANES, MEM_LEN), BF16),
            pltpu.VMEM((MEM_HEADS, MEM_LEN, LANES), BF16),
            pltpu.VMEM((2, SWA_HEADS, CHUNK, 2 * CHUNK), F32),
            pltpu.VMEM((A_GROUPS, CHUNK, CHUNK), BF16),
            pltpu.VMEM((A_GROUPS, CHUNK, LANES), F32),
            pltpu.VMEM((D_MODEL, 2 * MEM_WIDTH), BF16),
            pltpu.VMEM((MIX_WIDTH, D_MODEL), BF16),
            pltpu.VMEM((ts, MIX_WIDTH), F32),
            pltpu.VMEM((ts, MIX_WIDTH), BF16),
            pltpu.VMEM((ts, D_MODEL), F32),
        ],
    )
    return pl.pallas_call(
        _layer_kernel,
        grid_spec=grid_spec,
        out_shape=jax.ShapeDtypeStruct(x.shape, x.dtype),
        compiler_params=pltpu.CompilerParams(
            dimension_semantics=("arbitrary", "arbitrary"),
            vmem_limit_bytes=VMEM_LIMIT_BYTES),
        name="layer",
    )(sinks, x, x, mem, w_in_bf16, w_kv, w_out, pre_g, mem_g, post_g, vng, vnb,
      w_spatial, b_spatial, rel_bias, buckets)


def kernel(x, mem, pre_norm_g, post_norm_g, mem_norm_g, w_in, w_mem_kv, v_norm_g, v_norm_b,
           w_spatial, b_spatial, attn_sinks, rel_bias, w_out):
    depth = w_in.shape[0]
    buckets = jnp.asarray(_band_buckets())
    for layer in range(depth):
        x = _layer(x, mem, w_in[layer].astype(BF16), w_mem_kv[layer], w_out[layer],
                   pre_norm_g[layer][None, :], mem_norm_g[layer][None, :], post_norm_g[layer][None, :],
                   v_norm_g[layer][None, :], v_norm_b[layer][None, :],
                   w_spatial[layer], b_spatial[layer], buckets, attn_sinks[layer], rel_bias)
    return x
```

```python
import numpy as np
import jax
import jax.numpy as jnp
from jax import lax
from jax.experimental import pallas as pl
from jax.experimental.pallas import tpu as pltpu

D_MODEL = 1024
MEM_LEN = 256
HEAD_DIM = 64
CHUNK = 128
A_GROUPS = 4
A_WIDTH = 512
SWA_HEADS = 4
SWA_WIDTH = 256
SWA_KV_WIDTH = 128
MEM_HEADS = 4
MEM_WIDTH = 256
MIX_WIDTH = 1024
IN_WIDTH = 2816
N_BUCKETS = 32
MAX_DISTANCE = 128
WINDOW = 128
EPS = 1e-6
NEG = -1e30
LOG2E = float(np.log2(np.e))
Q_SCALE = HEAD_DIM ** -0.5 * LOG2E

OFF_U = 0
OFF_V = A_WIDTH
OFF_SQ = 2 * A_WIDTH
OFF_SK = OFF_SQ + SWA_WIDTH
OFF_SV = OFF_SK + SWA_KV_WIDTH
OFF_MQ = OFF_SV + SWA_KV_WIDTH
OFF_Z = OFF_MQ + MEM_WIDTH

LANES = 128
SEQ_TILE = 1024
SUB_TILES = (256, 256, 256, 256)
PROJ_COLS = 512
ITEM_LAG = 2
PROJECT_COLUMN_PIECES = 6
PROJECT_OUT_DOTS = D_MODEL // PROJ_COLS
HEAD_STAGES_PER_CHUNK = 2 * (SWA_HEADS // 2 + MEM_HEADS // 2)
VMEM_LIMIT_BYTES = 62 * 1024 * 1024

BF16 = jnp.bfloat16
F32 = jnp.float32


def _t5_causal_buckets(dist):
    n = np.maximum(dist, 0)
    max_exact = N_BUCKETS // 2
    large = max_exact + (np.log(np.maximum(n, 1) / max_exact) / np.log(MAX_DISTANCE / max_exact)
                         * (N_BUCKETS - max_exact)).astype(np.int32)
    large = np.minimum(large, N_BUCKETS - 1)
    return np.where(n < max_exact, n, large).astype(np.int32)


def _band_buckets():
    qi = np.arange(CHUNK)[:, None]
    kj = np.arange(2 * CHUNK)[None, :]
    dist = qi + CHUNK - kj
    valid = (dist >= 0) & (dist < WINDOW)
    return np.where(valid, _t5_causal_buckets(dist), -1).astype(np.int32)


def _gelu_tanh(x):
    c = np.float32(np.sqrt(2.0 / np.pi))
    ck = np.float32(np.sqrt(2.0 / np.pi) * 0.044715)
    hx = 0.5 * x
    return hx + hx * jnp.tanh(x * (c + ck * (x * x)))


def _silu(z):
    hz = 0.5 * z
    return hz + hz * jnp.tanh(hz)


def _dot(a, b):
    return jnp.dot(a, b, preferred_element_type=F32)


_DONE = object()


def _skewed(items, lag):
    waiting = []
    for item in items:
        if next(item, _DONE) is not _DONE:
            waiting.append(item)
        yield
        if len(waiting) > lag:
            next(waiting.pop(0), _DONE)
            yield
    for item in waiting:
        next(item, _DONE)
        yield


def _chain(*gens):
    for g in gens:
        yield from g


def _interleave(a, b, b_per_a):
    credit = 0.0
    a_live = b_live = True
    while a_live or b_live:
        if a_live:
            a_live = next(a, _DONE) is not _DONE
        credit += b_per_a
        while b_live and (credit >= 1.0 or not a_live):
            b_live = next(b, _DONE) is not _DONE
            credit -= 1.0


def _lane_half_mask(parity):
    lane = lax.broadcasted_iota(jnp.int32, (1, LANES), 1)
    return (lane >= HEAD_DIM) if parity else (lane < HEAD_DIM)


def _place_head_rows(kt_head, parity):
    zeros = jnp.zeros_like(kt_head)
    return jnp.concatenate([zeros, kt_head] if parity else [kt_head, zeros], axis=0)


def _layer_kernel(sinks_ref,
                  x_ref, xnext_ref, mem_ref, win_ref, wkv_f32_ref, wout_f32_ref, preg_ref, memg_ref, postg_ref,
                  vng_ref, vnb_ref, ws_ref, bs_ref, relb_ref, buckets_ref,
                  out_ref,
                  h_scr, gu_scr, vn_scr, qs_scr, ktvar_scr, vvar_scr, zg_scr,
                  mk_scr, mv_scr, bias_scr, wsm_scr, bst_scr, wkv_ref, wout_ref,
                  y_scr, ybf_scr, o_scr):
    ts = x_ref.shape[1]
    b = pl.program_id(0)
    t = pl.program_id(1)

    @pl.when((b == 0) & (t == 0))
    def _init():
        row = lax.broadcasted_iota(jnp.int32, (CHUNK, CHUNK), 0)
        col = lax.broadcasted_iota(jnp.int32, (CHUNK, CHUNK), 1)
        for g in range(A_GROUPS):
            wsm_scr[g] = jnp.where(row >= col, ws_ref[g], 0.0).astype(BF16)
            bs_col = jnp.sum(jnp.where(row == col, bs_ref[g:g + 1, :], 0.0), axis=1, keepdims=True)
            bst_scr[g] = jnp.broadcast_to(bs_col, (CHUNK, LANES))
        for r0 in range(0, D_MODEL, 2 * CHUNK):
            wrows = pl.ds(r0, 2 * CHUNK)
            wkv_ref[wrows, :] = wkv_f32_ref[wrows, :].astype(BF16)
            wout_ref[wrows, :] = wout_f32_ref[wrows, :].astype(BF16)
        buckets = buckets_ref[...]
        prev_key = lax.broadcasted_iota(jnp.int32, (CHUNK, 2 * CHUNK), 1) < CHUNK
        for hd in range(SWA_HEADS):
            tbl = jnp.full((CHUNK, 2 * CHUNK), NEG, F32)
            for bk in range(N_BUCKETS):
                tbl = jnp.where(buckets == bk, relb_ref[bk:bk + 1, hd:hd + 1] * LOG2E, tbl)
            bias_scr[0, hd] = tbl
            bias_scr[1, hd] = jnp.where(prev_key, NEG, tbl)

    @pl.when(t == 0)
    def _sequence_start():
        ktvar_scr[:, :, 0:CHUNK] = jnp.zeros((SWA_HEADS, LANES, CHUNK), BF16)
        vvar_scr[:, 0:CHUNK, :] = jnp.zeros((SWA_HEADS, CHUNK, LANES), BF16)
        m = mem_ref[0]
        ms = jnp.mean(m * m, axis=-1, keepdims=True)
        hm = (m * lax.rsqrt(ms + EPS) * memg_ref[...]).astype(BF16)
        kv = _dot(hm, wkv_ref[...])
        for pair in range(MEM_HEADS // 2):
            kt_pair = kv[:, pair * LANES:(pair + 1) * LANES].T
            v_pair = kv[:, MEM_WIDTH + pair * LANES:MEM_WIDTH + (pair + 1) * LANES]
            for parity in range(2):
                hd = 2 * pair + parity
                kt_head = kt_pair[parity * HEAD_DIM:(parity + 1) * HEAD_DIM]
                mk_scr[hd] = _place_head_rows(kt_head, parity).astype(BF16)
                mv_scr[hd] = jnp.where(_lane_half_mask(parity), v_pair, 0.0).astype(BF16)

    @pl.when(t > 0)
    def _copy_carry():
        ktvar_scr[:, :, 0:CHUNK] = ktvar_scr[:, :, ts:ts + CHUNK]
        vvar_scr[:, 0:CHUNK, :] = vvar_scr[:, ts:ts + CHUNK, :]

    lo = _lane_half_mask(0)
    hi = _lane_half_mask(1)
    first_tile = jnp.where(t == 0, 1, 0)

    def pre_norm(r0, nrows, src_ref=x_ref):
        for rc in range(nrows // CHUNK):
            crows = pl.ds(r0 + rc * CHUNK, CHUNK)
            x = src_ref[0, crows, :]
            ms = jnp.mean(x * x, axis=-1, keepdims=True)
            h_scr[crows, :] = (x * lax.rsqrt(ms + EPS) * preg_ref[...]).astype(BF16)

    def proj(rows, c0, width):
        return _dot(h_scr[rows, :], win_ref[:, c0:c0 + width])

    def gating_pieces(r0, nrows):
        for c in range(r0 // CHUNK, (r0 + nrows) // CHUNK):
            for g in range(A_GROUPS):
                spatial_gating(c, g)
            yield

    def project(r0, nrows, gmlp=True, rest=True, gating=True, qkv_last=False):
        rows = pl.ds(r0, nrows)

        if gmlp:
            v_all = _gelu_tanh(proj(rows, OFF_V, A_WIDTH))
            for g in range(A_GROUPS):
                cols = pl.ds(g * LANES, LANES)
                v = v_all[:, g * LANES:(g + 1) * LANES]
                mu = jnp.mean(v, axis=-1, keepdims=True)
                d = v - mu
                var = jnp.mean(d * d, axis=-1, keepdims=True)
                vn = d * lax.rsqrt(var + EPS) * vng_ref[:, cols] + vnb_ref[:, cols]
                vn_scr[rows, cols] = vn.astype(BF16)
            yield

            gu_scr[rows, :] = _gelu_tanh(proj(rows, OFF_U, A_WIDTH))
            yield

        def qkv_piece():
            qkv = proj(rows, OFF_SQ, SWA_WIDTH + 2 * SWA_KV_WIDTH)
            qs_scr[rows, 0:SWA_WIDTH] = (qkv[:, 0:SWA_WIDTH] * Q_SCALE).astype(BF16)
            skt = qkv[:, SWA_WIDTH:SWA_WIDTH + LANES].T
            sv = qkv[:, SWA_WIDTH + LANES:]
            sv_rot = pltpu.roll(sv, HEAD_DIM, 1)
            new_rows = pl.ds(CHUNK + r0, nrows)
            for hd, (keep, vsrc) in enumerate([(lo, sv), (hi, sv_rot), (lo, sv_rot), (hi, sv)]):
                kv_head = hd // 2
                kt_head = skt[kv_head * HEAD_DIM:(kv_head + 1) * HEAD_DIM]
                ktvar_scr[hd, :, new_rows] = _place_head_rows(kt_head, hd % 2).astype(BF16)
                vvar_scr[hd, new_rows, :] = jnp.where(keep, vsrc, 0.0).astype(BF16)

        if rest:
            if not qkv_last:
                qkv_piece()
                yield

            mqz = proj(rows, OFF_MQ, PROJ_COLS)
            qs_scr[rows, SWA_WIDTH:] = (mqz[:, 0:MEM_WIDTH] * Q_SCALE).astype(BF16)
            zg_scr[rows, 0:PROJ_COLS - MEM_WIDTH] = _silu(mqz[:, MEM_WIDTH:])
            yield

        if gmlp and gating:
            yield from gating_pieces(r0, nrows)

        if rest:
            done = PROJ_COLS - MEM_WIDTH
            while done < MIX_WIDTH:
                width = min(PROJ_COLS, MIX_WIDTH - done)
                zg_scr[rows, done:done + width] = _silu(proj(rows, OFF_Z + done, width))
                done += width
                yield
            if qkv_last:
                qkv_piece()
                yield

    def spatial_gating(c, g):
        rows = pl.ds(c * CHUNK, CHUNK)
        cols = pl.ds(g * LANES, LANES)
        sv_g = _dot(wsm_scr[g], vn_scr[rows, cols]) + bst_scr[g]
        y_scr[rows, cols] = gu_scr[rows, cols] * sv_g

    def attn_item(c, pair, sliding):
        r0 = c * CHUNK
        rows = pl.ds(r0, CHUNK)
        heads = (2 * pair, 2 * pair + 1)
        if sliding:
            band = pl.ds(r0, 2 * CHUNK)
            qp = qs_scr[rows, pl.ds(pair * LANES, LANES)]
            keys = jnp.concatenate([ktvar_scr[hd, :, band] for hd in heads], axis=1)
            bias_sel = first_tile if c == 0 else 0
            bias = jnp.concatenate([bias_scr[bias_sel, hd] for hd in heads], axis=1)
            logits = _dot(qp, keys) + bias
        else:
            qp = qs_scr[rows, pl.ds(SWA_WIDTH + pair * LANES, LANES)]
            keys = jnp.concatenate([mk_scr[hd] for hd in heads], axis=1)
            logits = _dot(qp, keys)
        n_keys = logits.shape[1] // 2
        yield
        probs, recips = [], []
        for i, hd in enumerate(heads):
            s = logits[:, i * n_keys:(i + 1) * n_keys]
            m = jnp.max(s, axis=-1, keepdims=True)
            if sliding:
                sink = sinks_ref[hd] * LOG2E
                m = jnp.maximum(m, sink)
            p = jnp.exp2(s - m)
            denom = jnp.sum(p, axis=-1, keepdims=True)
            if sliding:
                denom = denom + jnp.exp2(sink - m)
            probs.append(p.astype(BF16))
            recips.append(1.0 / denom)
        if sliding:
            values = jnp.concatenate([vvar_scr[hd, band, :] for hd in heads], axis=0)
        else:
            values = jnp.concatenate([mv_scr[hd] for hd in heads], axis=0)
        o_pair = _dot(jnp.concatenate(probs, axis=1), values)
        o_pair = o_pair * jnp.where(lo, recips[0], recips[1])
        off = A_WIDTH if sliding else A_WIDTH + SWA_WIDTH
        y_scr[rows, pl.ds(off + pair * LANES, LANES)] = o_pair

    def head_items(r0, nrows):
        for c in range(r0 // CHUNK, (r0 + nrows) // CHUNK):
            for pair in range(SWA_HEADS // 2):
                yield attn_item(c, pair, True)
            for pair in range(MEM_HEADS // 2):
                yield attn_item(c, pair, False)

    def project_out(r0, nrows, k0=0, k1=MIX_WIDTH):
        rows = pl.ds(r0, nrows)
        ybf_scr[rows, k0:k1] = (y_scr[rows, k0:k1] * zg_scr[rows, k0:k1]).astype(BF16)
        for c0 in range(0, D_MODEL, PROJ_COLS):
            cols = pl.ds(c0, PROJ_COLS)
            part = _dot(ybf_scr[rows, k0:k1], wout_ref[k0:k1, cols])
            o_scr[rows, cols] = o_scr[rows, cols] + part if k0 else part
            yield
        if k1 < MIX_WIDTH:
            return
        for rc in range(nrows // CHUNK):
            crows = pl.ds(r0 + rc * CHUNK, CHUNK)
            o = o_scr[crows, :]
            ms = jnp.mean(o * o, axis=-1, keepdims=True)
            out_ref[0, crows, :] = x_ref[0, crows, :] + o * lax.rsqrt(ms + EPS) * postg_ref[...]
            yield

    assert sum(SUB_TILES) == ts
    starts = [sum(SUB_TILES[:s]) for s in range(len(SUB_TILES))]
    subs = list(zip(starts, SUB_TILES))
    n_sub = len(subs)

    @pl.when((b == 0) & (t == 0))
    def _first_projection():
        pre_norm(*subs[0])
        for _ in project(*subs[0]):
            pass

    for s in range(n_sub):
        heads = _skewed(head_items(*subs[s]), ITEM_LAG)
        if s + 1 < n_sub:
            pre_norm(*subs[s + 1])
            filler = project(*subs[s + 1])
            n_filler = PROJECT_COLUMN_PIECES + subs[s + 1][1] // CHUNK
        else:
            filler = iter(())
            n_filler = 0
        if s >= 1:
            filler = _chain(project_out(*subs[s - 1]), filler)
            n_filler += PROJECT_OUT_DOTS + subs[s - 1][1] // CHUNK
        _interleave(heads, filler, n_filler / ((subs[s][1] // CHUNK) * HEAD_STAGES_PER_CHUNK))
    for _ in project_out(*subs[-1]):
        pass
    pre_norm(*subs[0], src_ref=xnext_ref)
    for _ in project(*subs[0]):
        pass


def _layer(x, mem, w_in_bf16, w_kv, w_out, pre_g, mem_g, post_g, vng, vnb, w_spatial,
           b_spatial, buckets, sinks, rel_bias):
    batch, seq, _ = x.shape
    ts = SEQ_TILE
    n_tiles = seq // ts
    const2 = lambda b, t, *_: (0, 0)
    const3 = lambda b, t, *_: (0, 0, 0)

    def next_first_sub_tile(b, t, *_):
        step = jnp.minimum(b * n_tiles + t + 1, batch * n_tiles - 1)
        return (step // n_tiles, (step % n_tiles) * (ts // SUB_TILES[0]), 0)

    grid_spec = pltpu.PrefetchScalarGridSpec(
        num_scalar_prefetch=1,
        grid=(batch, n_tiles),
        in_specs=[
            pl.BlockSpec((1, ts, D_MODEL), lambda b, t, *_: (b, t, 0)),
            pl.BlockSpec((1, SUB_TILES[0], D_MODEL), next_first_sub_tile),
            pl.BlockSpec((1, MEM_LEN, D_MODEL), lambda b, t, *_: (b, 0, 0)),
            pl.BlockSpec((D_MODEL, IN_WIDTH), const2),
            pl.BlockSpec((D_MODEL, 2 * MEM_WIDTH), const2),
            pl.BlockSpec((MIX_WIDTH, D_MODEL), const2),
            pl.BlockSpec((1, D_MODEL), const2),
            pl.BlockSpec((1, D_MODEL), const2),
            pl.BlockSpec((1, D_MODEL), const2),
            pl.BlockSpec((1, A_WIDTH), const2),
            pl.BlockSpec((1, A_WIDTH), const2),
            pl.BlockSpec((A_GROUPS, CHUNK, CHUNK), const3),
            pl.BlockSpec((A_GROUPS, CHUNK), const2),
            pl.BlockSpec((N_BUCKETS, SWA_HEADS), const2),
            pl.BlockSpec((CHUNK, 2 * CHUNK), const2),
        ],
        out_specs=pl.BlockSpec((1, ts, D_MODEL), lambda b, t, *_: (b, t, 0)),
        scratch_shapes=[
            pltpu.VMEM((ts, D_MODEL), BF16),
            pltpu.VMEM((ts, A_WIDTH), F32),
            pltpu.VMEM((ts, A_WIDTH), BF16),
            pltpu.VMEM((ts, SWA_WIDTH + MEM_WIDTH), BF16),
            pltpu.VMEM((SWA_HEADS, LANES, CHUNK + ts), BF16),
            pltpu.VMEM((SWA_HEADS, CHUNK + ts, LANES), BF16),
            pltpu.VMEM((ts, MIX_WIDTH), F32),
            pltpu.VMEM((MEM_HEADS, LANES, MEM_LEN), BF16),
            pltpu.VMEM((MEM_HEADS, MEM_LEN, LANES), BF16),
            pltpu.VMEM((2, SWA_HEADS, CHUNK, 2 * CHUNK), F32),
            pltpu.VMEM((A_GROUPS, CHUNK, CHUNK), BF16),
            pltpu.VMEM((A_GROUPS, CHUNK, LANES), F32),
            pltpu.VMEM((D_MODEL, 2 * MEM_WIDTH), BF16),
            pltpu.VMEM((MIX_WIDTH, D_MODEL), BF16),
            pltpu.VMEM((ts, MIX_WIDTH), F32),
            pltpu.VMEM((ts, MIX_WIDTH), BF16),
            pltpu.VMEM((ts, D_MODEL), F32),
        ],
    )
    return pl.pallas_call(
        _layer_kernel,
        grid_spec=grid_spec,
        out_shape=jax.ShapeDtypeStruct(x.shape, x.dtype),
        compiler_params=pltpu.CompilerParams(
            dimension_semantics=("arbitrary", "arbitrary"),
            vmem_limit_bytes=VMEM_LIMIT_BYTES),
        name="layer",
    )(sinks, x, x, mem, w_in_bf16, w_kv, w_out, pre_g, mem_g, post_g, vng, vnb,
      w_spatial, b_spatial, rel_bias, buckets)


def kernel(x, mem, pre_norm_g, post_norm_g, mem_norm_g, w_in, w_mem_kv, v_norm_g, v_norm_b,
           w_spatial, b_spatial, attn_sinks, rel_bias, w_out):
    depth = w_in.shape[0]
    buckets = jnp.asarray(_band_buckets())
    for layer in range(depth):
        x = _layer(x, mem, w_in[layer].astype(BF16), w_mem_kv[layer], w_out[layer],
                   pre_norm_g[layer][None, :], mem_norm_g[layer][None, :], post_norm_g[layer][None, :],
                   v_norm_g[layer][None, :], v_norm_b[layer][None, :],
                   w_spatial[layer], b_spatial[layer], buckets, attn_sinks[layer], rel_bias)
    return x
```

```python
import numpy as np
import jax
import jax.numpy as jnp
from jax import lax
from jax.experimental import pallas as pl
from jax.experimental.pallas import tpu as pltpu

D_MODEL = 1024
MEM_LEN = 256
HEAD_DIM = 64
CHUNK = 128
A_GROUPS = 4
A_WIDTH = 512
SWA_HEADS = 4
SWA_WIDTH = 256
SWA_KV_WIDTH = 128
MEM_HEADS = 4
MEM_WIDTH = 256
MIX_WIDTH = 1024
IN_WIDTH = 2816
N_BUCKETS = 32
MAX_DISTANCE = 128
WINDOW = 128
EPS = 1e-6
NEG = -1e30
LOG2E = float(np.log2(np.e))
Q_SCALE = HEAD_DIM ** -0.5 * LOG2E

OFF_U = 0
OFF_V = A_WIDTH
OFF_SQ = 2 * A_WIDTH
OFF_SK = OFF_SQ + SWA_WIDTH
OFF_SV = OFF_SK + SWA_KV_WIDTH
OFF_MQ = OFF_SV + SWA_KV_WIDTH
OFF_Z = OFF_MQ + MEM_WIDTH

LANES = 128
SEQ_TILE = 1024
SUB_TILES = (256, 256, 256, 256)
PROJ_COLS = 512
ITEM_LAG = 2
PROJECT_COLUMN_PIECES = 6
PROJECT_OUT_DOTS = D_MODEL // PROJ_COLS
HEAD_STAGES_PER_CHUNK = 2 * (SWA_HEADS // 2 + MEM_HEADS // 2)
VMEM_LIMIT_BYTES = 62 * 1024 * 1024

BF16 = jnp.bfloat16
F32 = jnp.float32


def _t5_causal_buckets(dist):
    n = np.maximum(dist, 0)
    max_exact = N_BUCKETS // 2
    large = max_exact + (np.log(np.maximum(n, 1) / max_exact) / np.log(MAX_DISTANCE / max_exact)
                         * (N_BUCKETS - max_exact)).astype(np.int32)
    large = np.minimum(large, N_BUCKETS - 1)
    return np.where(n < max_exact, n, large).astype(np.int32)


def _band_buckets():
    qi = np.arange(CHUNK)[:, None]
    kj = np.arange(2 * CHUNK)[None, :]
    dist = qi + CHUNK - kj
    valid = (dist >= 0) & (dist < WINDOW)
    return np.where(valid, _t5_causal_buckets(dist), -1).astype(np.int32)


def _gelu_tanh(x):
    c = np.float32(np.sqrt(2.0 / np.pi))
    ck = np.float32(np.sqrt(2.0 / np.pi) * 0.044715)
    hx = 0.5 * x
    return hx + hx * jnp.tanh(x * (c + ck * (x * x)))


def _silu(z):
    hz = 0.5 * z
    return hz + hz * jnp.tanh(hz)


def _dot(a, b):
    return jnp.dot(a, b, preferred_element_type=F32)


_DONE = object()


def _skewed(items, lag):
    waiting = []
    for item in items:
        if next(item, _DONE) is not _DONE:
            waiting.append(item)
        yield
        if len(waiting) > lag:
            next(waiting.pop(0), _DONE)
            yield
    for item in waiting:
        next(item, _DONE)
        yield


def _chain(*gens):
    for g in gens:
        yield from g


def _interleave(a, b, b_per_a):
    credit = 0.0
    a_live = b_live = True
    while a_live or b_live:
        if a_live:
            a_live = next(a, _DONE) is not _DONE
        credit += b_per_a
        while b_live and (credit >= 1.0 or not a_live):
            b_live = next(b, _DONE) is not _DONE
            credit -= 1.0


def _lane_half_mask(parity):
    lane = lax.broadcasted_iota(jnp.int32, (1, LANES), 1)
    return (lane >= HEAD_DIM) if parity else (lane < HEAD_DIM)


def _place_head_rows(kt_head, parity):
    zeros = jnp.zeros_like(kt_head)
    return jnp.concatenate([zeros, kt_head] if parity else [kt_head, zeros], axis=0)


def _layer_kernel(sinks_ref,
                  x_ref, xnext_ref, mem_ref, win_ref, wkv_f32_ref, wout_f32_ref, preg_ref, memg_ref, postg_ref,
                  vng_ref, vnb_ref, ws_ref, bs_ref, relb_ref, buckets_ref,
                  out_ref,
                  h_scr, gu_scr, vn_scr, qs_scr, ktvar_scr, vvar_scr, zg_scr,
                  mk_scr, mv_scr, bias_scr, wsm_scr, bst_scr, wkv_ref, wout_ref,
                  y_scr, ybf_scr, o_scr):
    ts = x_ref.shape[1]
    b = pl.program_id(0)
    t = pl.program_id(1)

    @pl.when((b == 0) & (t == 0))
    def _init():
        row = lax.broadcasted_iota(jnp.int32, (CHUNK, CHUNK), 0)
        col = lax.broadcasted_iota(jnp.int32, (CHUNK, CHUNK), 1)
        for g in range(A_GROUPS):
            wsm_scr[g] = jnp.where(row >= col, ws_ref[g], 0.0).astype(BF16)
            bs_col = jnp.sum(jnp.where(row == col, bs_ref[g:g + 1, :], 0.0), axis=1, keepdims=True)
            bst_scr[g] = jnp.broadcast_to(bs_col, (CHUNK, LANES))
        for r0 in range(0, D_MODEL, 2 * CHUNK):
            wrows = pl.ds(r0, 2 * CHUNK)
            wkv_ref[wrows, :] = wkv_f32_ref[wrows, :].astype(BF16)
            wout_ref[wrows, :] = wout_f32_ref[wrows, :].astype(BF16)
        buckets = buckets_ref[...]
        prev_key = lax.broadcasted_iota(jnp.int32, (CHUNK, 2 * CHUNK), 1) < CHUNK
        for hd in range(SWA_HEADS):
            tbl = jnp.full((CHUNK, 2 * CHUNK), NEG, F32)
            for bk in range(N_BUCKETS):
                tbl = jnp.where(buckets == bk, relb_ref[bk:bk + 1, hd:hd + 1] * LOG2E, tbl)
            bias_scr[0, hd] = tbl
            bias_scr[1, hd] = jnp.where(prev_key, NEG, tbl)

    @pl.when(t == 0)
    def _sequence_start():
        ktvar_scr[:, :, 0:CHUNK] = jnp.zeros((SWA_HEADS, LANES, CHUNK), BF16)
        vvar_scr[:, 0:CHUNK, :] = jnp.zeros((SWA_HEADS, CHUNK, LANES), BF16)
        m = mem_ref[0]
        ms = jnp.mean(m * m, axis=-1, keepdims=True)
        hm = (m * lax.rsqrt(ms + EPS) * memg_ref[...]).astype(BF16)
        kv = _dot(hm, wkv_ref[...])
        for pair in range(MEM_HEADS // 2):
            kt_pair = kv[:, pair * LANES:(pair + 1) * LANES].T
            v_pair = kv[:, MEM_WIDTH + pair * LANES:MEM_WIDTH + (pair + 1) * LANES]
            for parity in range(2):
                hd = 2 * pair + parity
                kt_head = kt_pair[parity * HEAD_DIM:(parity + 1) * HEAD_DIM]
                mk_scr[hd] = _place_head_rows(kt_head, parity).astype(BF16)
                mv_scr[hd] = jnp.where(_lane_half_mask(parity), v_pair, 0.0).astype(BF16)

    @pl.when(t > 0)
    def _copy_carry():
        ktvar_scr[:, :, 0:CHUNK] = ktvar_scr[:, :, ts:ts + CHUNK]
        vvar_scr[:, 0:CHUNK, :] = vvar_scr[:, ts:ts + CHUNK, :]

    lo = _lane_half_mask(0)
    hi = _lane_half_mask(1)
    first_tile = jnp.where(t == 0, 1, 0)

    def pre_norm(r0, nrows, src_ref=x_ref):
        for rc in range(nrows // CHUNK):
            crows = pl.ds(r0 + rc * CHUNK, CHUNK)
            x = src_ref[0, crows, :]
            ms = jnp.mean(x * x, axis=-1, keepdims=True)
            h_scr[crows, :] = (x * lax.rsqrt(ms + EPS) * preg_ref[...]).astype(BF16)

    def proj(rows, c0, width):
        return _dot(h_scr[rows, :], win_ref[:, c0:c0 + width])

    def gating_pieces(r0, nrows):
        for c in range(r0 // CHUNK, (r0 + nrows) // CHUNK):
            for g in range(A_GROUPS):
                spatial_gating(c, g)
            yield

    def project(r0, nrows, gmlp=True, rest=True, gating=True, qkv_last=False, attn=True,
                gate_rest=True):
        rows = pl.ds(r0, nrows)

        if gmlp:
            v_all = _gelu_tanh(proj(rows, OFF_V, A_WIDTH))
            for g in range(A_GROUPS):
                cols = pl.ds(g * LANES, LANES)
                v = v_all[:, g * LANES:(g + 1) * LANES]
                mu = jnp.mean(v, axis=-1, keepdims=True)
                d = v - mu
                var = jnp.mean(d * d, axis=-1, keepdims=True)
                vn = d * lax.rsqrt(var + EPS) * vng_ref[:, cols] + vnb_ref[:, cols]
                vn_scr[rows, cols] = vn.astype(BF16)
            yield

            gu_scr[rows, :] = _gelu_tanh(proj(rows, OFF_U, A_WIDTH))
            yield

        def qkv_piece():
            qkv = proj(rows, OFF_SQ, SWA_WIDTH + 2 * SWA_KV_WIDTH)
            qs_scr[rows, 0:SWA_WIDTH] = (qkv[:, 0:SWA_WIDTH] * Q_SCALE).astype(BF16)
            skt = qkv[:, SWA_WIDTH:SWA_WIDTH + LANES].T
            sv = qkv[:, SWA_WIDTH + LANES:]
            sv_rot = pltpu.roll(sv, HEAD_DIM, 1)
            new_rows = pl.ds(CHUNK + r0, nrows)
            for hd, (keep, vsrc) in enumerate([(lo, sv), (hi, sv_rot), (lo, sv_rot), (hi, sv)]):
                kv_head = hd // 2
                kt_head = skt[kv_head * HEAD_DIM:(kv_head + 1) * HEAD_DIM]
                ktvar_scr[hd, :, new_rows] = _place_head_rows(kt_head, hd % 2).astype(BF16)
                vvar_scr[hd, new_rows, :] = jnp.where(keep, vsrc, 0.0).astype(BF16)

        if rest and attn:
            if not qkv_last:
                qkv_piece()
                yield

            mqz = proj(rows, OFF_MQ, PROJ_COLS)
            qs_scr[rows, SWA_WIDTH:] = (mqz[:, 0:MEM_WIDTH] * Q_SCALE).astype(BF16)
            zg_scr[rows, 0:PROJ_COLS - MEM_WIDTH] = _silu(mqz[:, MEM_WIDTH:])
            yield

        if gmlp and gating:
            yield from gating_pieces(r0, nrows)

        if rest and gate_rest:
            done = PROJ_COLS - MEM_WIDTH
            while done < MIX_WIDTH:
                width = min(PROJ_COLS, MIX_WIDTH - done)
                zg_scr[rows, done:done + width] = _silu(proj(rows, OFF_Z + done, width))
                done += width
                yield
            if qkv_last:
                qkv_piece()
                yield

    def spatial_gating(c, g):
        rows = pl.ds(c * CHUNK, CHUNK)
        cols = pl.ds(g * LANES, LANES)
        sv_g = _dot(wsm_scr[g], vn_scr[rows, cols]) + bst_scr[g]
        y_scr[rows, cols] = gu_scr[rows, cols] * sv_g

    def attn_item(c, pair, sliding):
        r0 = c * CHUNK
        rows = pl.ds(r0, CHUNK)
        heads = (2 * pair, 2 * pair + 1)
        if sliding:
            band = pl.ds(r0, 2 * CHUNK)
            qp = qs_scr[rows, pl.ds(pair * LANES, LANES)]
            keys = jnp.concatenate([ktvar_scr[hd, :, band] for hd in heads], axis=1)
            bias_sel = first_tile if c == 0 else 0
            bias = jnp.concatenate([bias_scr[bias_sel, hd] for hd in heads], axis=1)
            logits = _dot(qp, keys) + bias
        else:
            qp = qs_scr[rows, pl.ds(SWA_WIDTH + pair * LANES, LANES)]
            keys = jnp.concatenate([mk_scr[hd] for hd in heads], axis=1)
            logits = _dot(qp, keys)
        n_keys = logits.shape[1] // 2
        yield
        probs, recips = [], []
        for i, hd in enumerate(heads):
            s = logits[:, i * n_keys:(i + 1) * n_keys]
            m = jnp.max(s, axis=-1, keepdims=True)
            if sliding:
                sink = sinks_ref[hd] * LOG2E
                m = jnp.maximum(m, sink)
            p = jnp.exp2(s - m)
            denom = jnp.sum(p, axis=-1, keepdims=True)
            if sliding:
                denom = denom + jnp.exp2(sink - m)
            probs.append(p.astype(BF16))
            recips.append(1.0 / denom)
        if sliding:
            values = jnp.concatenate([vvar_scr[hd, band, :] for hd in heads], axis=0)
        else:
            values = jnp.concatenate([mv_scr[hd] for hd in heads], axis=0)
        o_pair = _dot(jnp.concatenate(probs, axis=1), values)
        o_pair = o_pair * jnp.where(lo, recips[0], recips[1])
        off = A_WIDTH if sliding else A_WIDTH + SWA_WIDTH
        y_scr[rows, pl.ds(off + pair * LANES, LANES)] = o_pair

    def head_items(r0, nrows):
        for c in range(r0 // CHUNK, (r0 + nrows) // CHUNK):
            for pair in range(SWA_HEADS // 2):
                yield attn_item(c, pair, True)
            for pair in range(MEM_HEADS // 2):
                yield attn_item(c, pair, False)

    def project_out(r0, nrows, k0=0, k1=MIX_WIDTH):
        rows = pl.ds(r0, nrows)
        ybf_scr[rows, k0:k1] = (y_scr[rows, k0:k1] * zg_scr[rows, k0:k1]).astype(BF16)
        for c0 in range(0, D_MODEL, PROJ_COLS):
            cols = pl.ds(c0, PROJ_COLS)
            part = _dot(ybf_scr[rows, k0:k1], wout_ref[k0:k1, cols])
            o_scr[rows, cols] = o_scr[rows, cols] + part if k0 else part
            yield
        if k1 < MIX_WIDTH:
            return
        for rc in range(nrows // CHUNK):
            crows = pl.ds(r0 + rc * CHUNK, CHUNK)
            o = o_scr[crows, :]
            ms = jnp.mean(o * o, axis=-1, keepdims=True)
            out_ref[0, crows, :] = x_ref[0, crows, :] + o * lax.rsqrt(ms + EPS) * postg_ref[...]
            yield

    assert sum(SUB_TILES) == ts
    starts = [sum(SUB_TILES[:s]) for s in range(len(SUB_TILES))]
    subs = list(zip(starts, SUB_TILES))
    n_sub = len(subs)

    @pl.when((b == 0) & (t == 0))
    def _first_projection():
        pre_norm(*subs[0])
        for _ in project(*subs[0]):
            pass

    for s in range(n_sub):
        heads = _skewed(head_items(*subs[s]), ITEM_LAG)
        if s + 1 < n_sub:
            pre_norm(*subs[s + 1])
            filler = project(*subs[s + 1])
            n_filler = PROJECT_COLUMN_PIECES + subs[s + 1][1] // CHUNK
        else:
            pre_norm(*subs[0], src_ref=xnext_ref)
            filler = _chain(project_out(*subs[s], 0, A_WIDTH),
                            project(*subs[0], gmlp=False, attn=False))
            n_filler = PROJECT_OUT_DOTS + 2
        if s >= 1:
            filler = _chain(project_out(*subs[s - 1]), filler)
            n_filler += PROJECT_OUT_DOTS + subs[s - 1][1] // CHUNK
        _interleave(heads, filler, n_filler / ((subs[s][1] // CHUNK) * HEAD_STAGES_PER_CHUNK))
    for _ in project_out(*subs[-1], A_WIDTH, MIX_WIDTH):
        pass
    for _ in project(*subs[0], gate_rest=False):
        pass


def _layer(x, mem, w_in_bf16, w_kv, w_out, pre_g, mem_g, post_g, vng, vnb, w_spatial,
           b_spatial, buckets, sinks, rel_bias):
    batch, seq, _ = x.shape
    ts = SEQ_TILE
    n_tiles = seq // ts
    const2 = lambda b, t, *_: (0, 0)
    const3 = lambda b, t, *_: (0, 0, 0)

    def next_first_sub_tile(b, t, *_):
        step = jnp.minimum(b * n_tiles + t + 1, batch * n_tiles - 1)
        return (step // n_tiles, (step % n_tiles) * (ts // SUB_TILES[0]), 0)

    grid_spec = pltpu.PrefetchScalarGridSpec(
        num_scalar_prefetch=1,
        grid=(batch, n_tiles),
        in_specs=[
            pl.BlockSpec((1, ts, D_MODEL), lambda b, t, *_: (b, t, 0)),
            pl.BlockSpec((1, SUB_TILES[0], D_MODEL), next_first_sub_tile),
            pl.BlockSpec((1, MEM_LEN, D_MODEL), lambda b, t, *_: (b, 0, 0)),
            pl.BlockSpec((D_MODEL, IN_WIDTH), const2),
            pl.BlockSpec((D_MODEL, 2 * MEM_WIDTH), const2),
            pl.BlockSpec((MIX_WIDTH, D_MODEL), const2),
            pl.BlockSpec((1, D_MODEL), const2),
            pl.BlockSpec((1, D_MODEL), const2),
            pl.BlockSpec((1, D_MODEL), const2),
            pl.BlockSpec((1, A_WIDTH), const2),
            pl.BlockSpec((1, A_WIDTH), const2),
            pl.BlockSpec((A_GROUPS, CHUNK, CHUNK), const3),
            pl.BlockSpec((A_GROUPS, CHUNK), const2),
            pl.BlockSpec((N_BUCKETS, SWA_HEADS), const2),
            pl.BlockSpec((CHUNK, 2 * CHUNK), const2),
        ],
        out_specs=pl.BlockSpec((1, ts, D_MODEL), lambda b, t, *_: (b, t, 0)),
        scratch_shapes=[
            pltpu.VMEM((ts, D_MODEL), BF16),
            pltpu.VMEM((ts, A_WIDTH), F32),
            pltpu.VMEM((ts, A_WIDTH), BF16),
            pltpu.VMEM((ts, SWA_WIDTH + MEM_WIDTH), BF16),
            pltpu.VMEM((SWA_HEADS, LANES, CHUNK + ts), BF16),
            pltpu.VMEM((SWA_HEADS, CHUNK + ts, LANES), BF16),
            pltpu.VMEM((ts, MIX_WIDTH), F32),
            pltpu.VMEM((MEM_HEADS, LANES, MEM_LEN), BF16),
            pltpu.VMEM((MEM_HEADS, MEM_LEN, LANES), BF16),
            pltpu.VMEM((2, SWA_HEADS, CHUNK, 2 * CHUNK), F32),
            pltpu.VMEM((A_GROUPS, CHUNK, CHUNK), BF16),
            pltpu.VMEM((A_GROUPS, CHUNK, LANES), F32),
            pltpu.VMEM((D_MODEL, 2 * MEM_WIDTH), BF16),
            pltpu.VMEM((MIX_WIDTH, D_MODEL), BF16),
            pltpu.VMEM((ts, MIX_WIDTH), F32),
            pltpu.VMEM((ts, MIX_WIDTH), BF16),
            pltpu.VMEM((ts, D_MODEL), F32),
        ],
    )
    return pl.pallas_call(
        _layer_kernel,
        grid_spec=grid_spec,
        out_shape=jax.ShapeDtypeStruct(x.shape, x.dtype),
        compiler_params=pltpu.CompilerParams(
            dimension_semantics=("arbitrary", "arbitrary"),
            vmem_limit_bytes=VMEM_LIMIT_BYTES),
        name="layer",
    )(sinks, x, x, mem, w_in_bf16, w_kv, w_out, pre_g, mem_g, post_g, vng, vnb,
      w_spatial, b_spatial, rel_bias, buckets)


def kernel(x, mem, pre_norm_g, post_norm_g, mem_norm_g, w_in, w_mem_kv, v_norm_g, v_norm_b,
           w_spatial, b_spatial, attn_sinks, rel_bias, w_out):
    depth = w_in.shape[0]
    buckets = jnp.asarray(_band_buckets())
    for layer in range(depth):
        x = _layer(x, mem, w_in[layer].astype(BF16), w_mem_kv[layer], w_out[layer],
                   pre_norm_g[layer][None, :], mem_norm_g[layer][None, :], post_norm_g[layer][None, :],
                   v_norm_g[layer][None, :], v_norm_b[layer][None, :],
                   w_spatial[layer], b_spatial[layer], buckets, attn_sinks[layer], rel_bias)
    return x
```

```python
import numpy as np
import jax
import jax.numpy as jnp
from jax import lax
from jax.experimental import pallas as pl
from jax.experimental.pallas import tpu as pltpu

D_MODEL = 1024
MEM_LEN = 256
HEAD_DIM = 64
CHUNK = 128
A_GROUPS = 4
A_WIDTH = 512
SWA_HEADS = 4
SWA_WIDTH = 256
SWA_KV_WIDTH = 128
MEM_HEADS = 4
MEM_WIDTH = 256
MIX_WIDTH = 1024
IN_WIDTH = 2816
N_BUCKETS = 32
MAX_DISTANCE = 128
WINDOW = 128
EPS = 1e-6
NEG = -1e30
LOG2E = float(np.log2(np.e))
Q_SCALE = HEAD_DIM ** -0.5 * LOG2E

OFF_U = 0
OFF_V = A_WIDTH
OFF_SQ = 2 * A_WIDTH
OFF_SK = OFF_SQ + SWA_WIDTH
OFF_SV = OFF_SK + SWA_KV_WIDTH
OFF_MQ = OFF_SV + SWA_KV_WIDTH
OFF_Z = OFF_MQ + MEM_WIDTH

LANES = 128
SEQ_TILE = 1024
SUB_TILES = (256, 256, 256, 256)
PROJ_COLS = 512
ITEM_LAG = 2
PROJECT_COLUMN_PIECES = 6
PROJECT_OUT_DOTS = D_MODEL // PROJ_COLS
HEAD_STAGES_PER_CHUNK = 2 * (SWA_HEADS // 2 + MEM_HEADS // 2)
VMEM_LIMIT_BYTES = 62 * 1024 * 1024

BF16 = jnp.bfloat16
F32 = jnp.float32


def _t5_causal_buckets(dist):
    n = np.maximum(dist, 0)
    max_exact = N_BUCKETS // 2
    large = max_exact + (np.log(np.maximum(n, 1) / max_exact) / np.log(MAX_DISTANCE / max_exact)
                         * (N_BUCKETS - max_exact)).astype(np.int32)
    large = np.minimum(large, N_BUCKETS - 1)
    return np.where(n < max_exact, n, large).astype(np.int32)


def _band_buckets():
    qi = np.arange(CHUNK)[:, None]
    kj = np.arange(2 * CHUNK)[None, :]
    dist = qi + CHUNK - kj
    valid = (dist >= 0) & (dist < WINDOW)
    return np.where(valid, _t5_causal_buckets(dist), -1).astype(np.int32)


def _gelu_tanh(x):
    c = np.float32(np.sqrt(2.0 / np.pi))
    ck = np.float32(np.sqrt(2.0 / np.pi) * 0.044715)
    hx = 0.5 * x
    return hx + hx * jnp.tanh(x * (c + ck * (x * x)))


def _silu(z):
    hz = 0.5 * z
    return hz + hz * jnp.tanh(hz)


def _dot(a, b):
    return jnp.dot(a, b, preferred_element_type=F32)


_DONE = object()


def _skewed(items, lag):
    waiting = []
    for item in items:
        if next(item, _DONE) is not _DONE:
            waiting.append(item)
        yield
        if len(waiting) > lag:
            next(waiting.pop(0), _DONE)
            yield
    for item in waiting:
        next(item, _DONE)
        yield


def _chain(*gens):
    for g in gens:
        yield from g


def _interleave(a, b, b_per_a):
    credit = 0.0
    a_live = b_live = True
    while a_live or b_live:
        if a_live:
            a_live = next(a, _DONE) is not _DONE
        credit += b_per_a
        while b_live and (credit >= 1.0 or not a_live):
            b_live = next(b, _DONE) is not _DONE
            credit -= 1.0


def _lane_half_mask(parity):
    lane = lax.broadcasted_iota(jnp.int32, (1, LANES), 1)
    return (lane >= HEAD_DIM) if parity else (lane < HEAD_DIM)


def _place_head_rows(kt_head, parity):
    zeros = jnp.zeros_like(kt_head)
    return jnp.concatenate([zeros, kt_head] if parity else [kt_head, zeros], axis=0)


def _layer_kernel(sinks_ref,
                  x_ref, xnext_ref, mem_ref, win_ref, wkv_f32_ref, wout_f32_ref, preg_ref, memg_ref, postg_ref,
                  vng_ref, vnb_ref, ws_ref, bs_ref, relb_ref, buckets_ref,
                  out_ref,
                  h_scr, gu_scr, vn_scr, qs_scr, ktvar_scr, vvar_scr, zg_scr,
                  mk_scr, mv_scr, bias_scr, wsm_scr, bst_scr, wkv_ref, wout_ref,
                  y_scr, ybf_scr, o_scr):
    ts = x_ref.shape[1]
    b = pl.program_id(0)
    t = pl.program_id(1)

    @pl.when((b == 0) & (t == 0))
    def _init():
        row = lax.broadcasted_iota(jnp.int32, (CHUNK, CHUNK), 0)
        col = lax.broadcasted_iota(jnp.int32, (CHUNK, CHUNK), 1)
        for g in range(A_GROUPS):
            wsm_scr[g] = jnp.where(row >= col, ws_ref[g], 0.0).astype(BF16)
            bs_col = jnp.sum(jnp.where(row == col, bs_ref[g:g + 1, :], 0.0), axis=1, keepdims=True)
            bst_scr[g] = jnp.broadcast_to(bs_col, (CHUNK, LANES))
        for r0 in range(0, D_MODEL, 2 * CHUNK):
            wrows = pl.ds(r0, 2 * CHUNK)
            wkv_ref[wrows, :] = wkv_f32_ref[wrows, :].astype(BF16)
            wout_ref[wrows, :] = wout_f32_ref[wrows, :].astype(BF16)
        buckets = buckets_ref[...]
        prev_key = lax.broadcasted_iota(jnp.int32, (CHUNK, 2 * CHUNK), 1) < CHUNK
        for hd in range(SWA_HEADS):
            tbl = jnp.full((CHUNK, 2 * CHUNK), NEG, F32)
            for bk in range(N_BUCKETS):
                tbl = jnp.where(buckets == bk, relb_ref[bk:bk + 1, hd:hd + 1] * LOG2E, tbl)
            bias_scr[0, hd] = tbl
            bias_scr[1, hd] = jnp.where(prev_key, NEG, tbl)

    @pl.when(t == 0)
    def _sequence_start():
        ktvar_scr[:, :, 0:CHUNK] = jnp.zeros((SWA_HEADS, LANES, CHUNK), BF16)
        vvar_scr[:, 0:CHUNK, :] = jnp.zeros((SWA_HEADS, CHUNK, LANES), BF16)
        m = mem_ref[0]
        ms = jnp.mean(m * m, axis=-1, keepdims=True)
        hm = (m * lax.rsqrt(ms + EPS) * memg_ref[...]).astype(BF16)
        kv = _dot(hm, wkv_ref[...])
        for pair in range(MEM_HEADS // 2):
            kt_pair = kv[:, pair * LANES:(pair + 1) * LANES].T
            v_pair = kv[:, MEM_WIDTH + pair * LANES:MEM_WIDTH + (pair + 1) * LANES]
            for parity in range(2):
                hd = 2 * pair + parity
                kt_head = kt_pair[parity * HEAD_DIM:(parity + 1) * HEAD_DIM]
                mk_scr[hd] = _place_head_rows(kt_head, parity).astype(BF16)
                mv_scr[hd] = jnp.where(_lane_half_mask(parity), v_pair, 0.0).astype(BF16)

    @pl.when(t > 0)
    def _copy_carry():
        ktvar_scr[:, :, 0:CHUNK] = ktvar_scr[:, :, ts:ts + CHUNK]
        vvar_scr[:, 0:CHUNK, :] = vvar_scr[:, ts:ts + CHUNK, :]

    lo = _lane_half_mask(0)
    hi = _lane_half_mask(1)
    first_tile = jnp.where(t == 0, 1, 0)

    def pre_norm(r0, nrows, src_ref=x_ref):
        for rc in range(nrows // CHUNK):
            crows = pl.ds(r0 + rc * CHUNK, CHUNK)
            x = src_ref[0, crows, :]
            ms = jnp.mean(x * x, axis=-1, keepdims=True)
            h_scr[crows, :] = (x * lax.rsqrt(ms + EPS) * preg_ref[...]).astype(BF16)

    def proj(rows, c0, width):
        return _dot(h_scr[rows, :], win_ref[:, c0:c0 + width])

    def gating_pieces(r0, nrows):
        for c in range(r0 // CHUNK, (r0 + nrows) // CHUNK):
            for g in range(A_GROUPS):
                spatial_gating(c, g)
            yield

    def project(r0, nrows, gmlp=True, rest=True, gating=True, qkv_last=False, attn=True,
                gate_rest=True):
        rows = pl.ds(r0, nrows)

        if gmlp:
            v_all = _gelu_tanh(proj(rows, OFF_V, A_WIDTH))
            for g in range(A_GROUPS):
                cols = pl.ds(g * LANES, LANES)
                v = v_all[:, g * LANES:(g + 1) * LANES]
                mu = jnp.mean(v, axis=-1, keepdims=True)
                d = v - mu
                var = jnp.mean(d * d, axis=-1, keepdims=True)
                vn = d * lax.rsqrt(var + EPS) * vng_ref[:, cols] + vnb_ref[:, cols]
                vn_scr[rows, cols] = vn.astype(BF16)
            yield

            gu_scr[rows, :] = _gelu_tanh(proj(rows, OFF_U, A_WIDTH))
            yield

        def qkv_piece():
            qkv = proj(rows, OFF_SQ, SWA_WIDTH + 2 * SWA_KV_WIDTH)
            qs_scr[rows, 0:SWA_WIDTH] = (qkv[:, 0:SWA_WIDTH] * Q_SCALE).astype(BF16)
            skt = qkv[:, SWA_WIDTH:SWA_WIDTH + LANES].T
            sv = qkv[:, SWA_WIDTH + LANES:]
            sv_rot = pltpu.roll(sv, HEAD_DIM, 1)
            new_rows = pl.ds(CHUNK + r0, nrows)
            for hd, (keep, vsrc) in enumerate([(lo, sv), (hi, sv_rot), (lo, sv_rot), (hi, sv)]):
                kv_head = hd // 2
                kt_head = skt[kv_head * HEAD_DIM:(kv_head + 1) * HEAD_DIM]
                ktvar_scr[hd, :, new_rows] = _place_head_rows(kt_head, hd % 2).astype(BF16)
                vvar_scr[hd, new_rows, :] = jnp.where(keep, vsrc, 0.0).astype(BF16)

        if rest and attn:
            if not qkv_last:
                qkv_piece()
                yield

            mqz = proj(rows, OFF_MQ, PROJ_COLS)
            qs_scr[rows, SWA_WIDTH:] = (mqz[:, 0:MEM_WIDTH] * Q_SCALE).astype(BF16)
            zg_scr[rows, 0:PROJ_COLS - MEM_WIDTH] = _silu(mqz[:, MEM_WIDTH:])
            yield

        if gmlp and gating:
            yield from gating_pieces(r0, nrows)

        if rest and gate_rest:
            done = PROJ_COLS - MEM_WIDTH
            while done < MIX_WIDTH:
                width = min(PROJ_COLS, MIX_WIDTH - done)
                zg_scr[rows, done:done + width] = _silu(proj(rows, OFF_Z + done, width))
                done += width
                yield
            if qkv_last:
                qkv_piece()
                yield

    def spatial_gating(c, g):
        rows = pl.ds(c * CHUNK, CHUNK)
        cols = pl.ds(g * LANES, LANES)
        sv_g = _dot(wsm_scr[g], vn_scr[rows, cols]) + bst_scr[g]
        y_scr[rows, cols] = gu_scr[rows, cols] * sv_g

    def attn_item(c, pair, sliding):
        r0 = c * CHUNK
        rows = pl.ds(r0, CHUNK)
        heads = (2 * pair, 2 * pair + 1)
        if sliding:
            band = pl.ds(r0, 2 * CHUNK)
            qp = qs_scr[rows, pl.ds(pair * LANES, LANES)]
            keys = jnp.concatenate([ktvar_scr[hd, :, band] for hd in heads], axis=1)
            bias_sel = first_tile if c == 0 else 0
            bias = jnp.concatenate([bias_scr[bias_sel, hd] for hd in heads], axis=1)
            logits = _dot(qp, keys) + bias
        else:
            qp = qs_scr[rows, pl.ds(SWA_WIDTH + pair * LANES, LANES)]
            keys = jnp.concatenate([mk_scr[hd] for hd in heads], axis=1)
            logits = _dot(qp, keys)
        n_keys = logits.shape[1] // 2
        yield
        probs, recips = [], []
        for i, hd in enumerate(heads):
            s = logits[:, i * n_keys:(i + 1) * n_keys]
            m = jnp.max(s, axis=-1, keepdims=True)
            if sliding:
                sink = sinks_ref[hd] * LOG2E
                m = jnp.maximum(m, sink)
            p = jnp.exp2(s - m)
            denom = jnp.sum(p, axis=-1, keepdims=True)
            if sliding:
                denom = denom + jnp.exp2(sink - m)
            probs.append(p.astype(BF16))
            recips.append(1.0 / denom)
        if sliding:
            values = jnp.concatenate([vvar_scr[hd, band, :] for hd in heads], axis=0)
        else:
            values = jnp.concatenate([mv_scr[hd] for hd in heads], axis=0)
        o_pair = _dot(jnp.concatenate(probs, axis=1), values)
        o_pair = o_pair * jnp.where(lo, recips[0], recips[1])
        off = A_WIDTH if sliding else A_WIDTH + SWA_WIDTH
        y_scr[rows, pl.ds(off + pair * LANES, LANES)] = o_pair

    def head_items(r0, nrows):
        for c in range(r0 // CHUNK, (r0 + nrows) // CHUNK):
            for pair in range(SWA_HEADS // 2):
                yield attn_item(c, pair, True)
            for pair in range(MEM_HEADS // 2):
                yield attn_item(c, pair, False)

    def project_out(r0, nrows, k0=0, k1=MIX_WIDTH):
        rows = pl.ds(r0, nrows)
        ybf_scr[rows, k0:k1] = (y_scr[rows, k0:k1] * zg_scr[rows, k0:k1]).astype(BF16)
        for c0 in range(0, D_MODEL, PROJ_COLS):
            cols = pl.ds(c0, PROJ_COLS)
            part = _dot(ybf_scr[rows, k0:k1], wout_ref[k0:k1, cols])
            o_scr[rows, cols] = o_scr[rows, cols] + part if k0 else part
            yield
        if k1 < MIX_WIDTH:
            return
        for rc in range(nrows // CHUNK):
            crows = pl.ds(r0 + rc * CHUNK, CHUNK)
            o = o_scr[crows, :]
            ms = jnp.mean(o * o, axis=-1, keepdims=True)
            out_ref[0, crows, :] = x_ref[0, crows, :] + o * lax.rsqrt(ms + EPS) * postg_ref[...]
            yield

    assert sum(SUB_TILES) == ts
    starts = [sum(SUB_TILES[:s]) for s in range(len(SUB_TILES))]
    subs = list(zip(starts, SUB_TILES))
    n_sub = len(subs)

    @pl.when((b == 0) & (t == 0))
    def _first_projection():
        pre_norm(*subs[0])
        for _ in project(*subs[0]):
            pass

    for s in range(n_sub):
        heads = _skewed(head_items(*subs[s]), ITEM_LAG)
        if s + 1 < n_sub:
            pre_norm(*subs[s + 1])
            filler = project(*subs[s + 1])
            n_filler = PROJECT_COLUMN_PIECES + subs[s + 1][1] // CHUNK
        else:
            filler = project_out(*subs[s], 0, A_WIDTH)
            n_filler = PROJECT_OUT_DOTS
        if s >= 1:
            filler = _chain(project_out(*subs[s - 1]), filler)
            n_filler += PROJECT_OUT_DOTS + subs[s - 1][1] // CHUNK
        _interleave(heads, filler, n_filler / ((subs[s][1] // CHUNK) * HEAD_STAGES_PER_CHUNK))
    for _ in project_out(*subs[-1], A_WIDTH, MIX_WIDTH):
        pass
    pre_norm(*subs[0], src_ref=xnext_ref)
    for _ in _chain(project(*subs[0], gating=False), gating_pieces(*subs[0])):
        pass


def _layer(x, mem, w_in_bf16, w_kv, w_out, pre_g, mem_g, post_g, vng, vnb, w_spatial,
           b_spatial, buckets, sinks, rel_bias):
    batch, seq, _ = x.shape
    ts = SEQ_TILE
    n_tiles = seq // ts
    const2 = lambda b, t, *_: (0, 0)
    const3 = lambda b, t, *_: (0, 0, 0)

    def next_first_sub_tile(b, t, *_):
        step = jnp.minimum(b * n_tiles + t + 1, batch * n_tiles - 1)
        return (step // n_tiles, (step % n_tiles) * (ts // SUB_TILES[0]), 0)

    grid_spec = pltpu.PrefetchScalarGridSpec(
        num_scalar_prefetch=1,
        grid=(batch, n_tiles),
        in_specs=[
            pl.BlockSpec((1, ts, D_MODEL), lambda b, t, *_: (b, t, 0)),
            pl.BlockSpec((1, SUB_TILES[0], D_MODEL), next_first_sub_tile),
            pl.BlockSpec((1, MEM_LEN, D_MODEL), lambda b, t, *_: (b, 0, 0)),
            pl.BlockSpec((D_MODEL, IN_WIDTH), const2),
            pl.BlockSpec((D_MODEL, 2 * MEM_WIDTH), const2),
            pl.BlockSpec((MIX_WIDTH, D_MODEL), const2),
            pl.BlockSpec((1, D_MODEL), const2),
            pl.BlockSpec((1, D_MODEL), const2),
            pl.BlockSpec((1, D_MODEL), const2),
            pl.BlockSpec((1, A_WIDTH), const2),
            pl.BlockSpec((1, A_WIDTH), const2),
            pl.BlockSpec((A_GROUPS, CHUNK, CHUNK), const3),
            pl.BlockSpec((A_GROUPS, CHUNK), const2),
            pl.BlockSpec((N_BUCKETS, SWA_HEADS), const2),
            pl.BlockSpec((CHUNK, 2 * CHUNK), const2),
        ],
        out_specs=pl.BlockSpec((1, ts, D_MODEL), lambda b, t, *_: (b, t, 0)),
        scratch_shapes=[
            pltpu.VMEM((ts, D_MODEL), BF16),
            pltpu.VMEM((ts, A_WIDTH), F32),
            pltpu.VMEM((ts, A_WIDTH), BF16),
            pltpu.VMEM((ts, SWA_WIDTH + MEM_WIDTH), BF16),
            pltpu.VMEM((SWA_HEADS, LANES, CHUNK + ts), BF16),
            pltpu.VMEM((SWA_HEADS, CHUNK + ts, LANES), BF16),
            pltpu.VMEM((ts, MIX_WIDTH), F32),
            pltpu.VMEM((MEM_HEADS, LANES, MEM_LEN), BF16),
            pltpu.VMEM((MEM_HEADS, MEM_LEN, LANES), BF16),
            pltpu.VMEM((2, SWA_HEADS, CHUNK, 2 * CHUNK), F32),
            pltpu.VMEM((A_GROUPS, CHUNK, CHUNK), BF16),
            pltpu.VMEM((A_GROUPS, CHUNK, LANES), F32),
            pltpu.VMEM((D_MODEL, 2 * MEM_WIDTH), BF16),
            pltpu.VMEM((MIX_WIDTH, D_MODEL), BF16),
            pltpu.VMEM((ts, MIX_WIDTH), F32),
            pltpu.VMEM((ts, MIX_WIDTH), BF16),
            pltpu.VMEM((ts, D_MODEL), F32),
        ],
    )
    return pl.pallas_call(
        _layer_kernel,
        grid_spec=grid_spec,
        out_shape=jax.ShapeDtypeStruct(x.shape, x.dtype),
        compiler_params=pltpu.CompilerParams(
            dimension_semantics=("arbitrary", "arbitrary"),
            vmem_limit_bytes=VMEM_LIMIT_BYTES),
        name="layer",
    )(sinks, x, x, mem, w_in_bf16, w_kv, w_out, pre_g, mem_g, post_g, vng, vnb,
      w_spatial, b_spatial, rel_bias, buckets)


def kernel(x, mem, pre_norm_g, post_norm_g, mem_norm_g, w_in, w_mem_kv, v_norm_g, v_norm_b,
           w_spatial, b_spatial, attn_sinks, rel_bias, w_out):
    depth = w_in.shape[0]
    buckets = jnp.asarray(_band_buckets())
    for layer in range(depth):
        x = _layer(x, mem, w_in[layer].astype(BF16), w_mem_kv[layer], w_out[layer],
                   pre_norm_g[layer][None, :], mem_norm_g[layer][None, :], post_norm_g[layer][None, :],
                   v_norm_g[layer][None, :], v_norm_b[layer][None, :],
                   w_spatial[layer], b_spatial[layer], buckets, attn_sinks[layer], rel_bias)
    return x
```

```python
import numpy as np
import jax
import jax.numpy as jnp
from jax import lax
from jax.experimental import pallas as pl
from jax.experimental.pallas import tpu as pltpu

D_MODEL = 1024
MEM_LEN = 256
HEAD_DIM = 64
CHUNK = 128
A_GROUPS = 4
A_WIDTH = 512
SWA_HEADS = 4
SWA_WIDTH = 256
SWA_KV_WIDTH = 128
MEM_HEADS = 4
MEM_WIDTH = 256
MIX_WIDTH = 1024
IN_WIDTH = 2816
N_BUCKETS = 32
MAX_DISTANCE = 128
WINDOW = 128
EPS = 1e-6
NEG = -1e30
LOG2E = float(np.log2(np.e))
Q_SCALE = HEAD_DIM ** -0.5 * LOG2E

OFF_U = 0
OFF_V = A_WIDTH
OFF_SQ = 2 * A_WIDTH
OFF_SK = OFF_SQ + SWA_WIDTH
OFF_SV = OFF_SK + SWA_KV_WIDTH
OFF_MQ = OFF_SV + SWA_KV_WIDTH
OFF_Z = OFF_MQ + MEM_WIDTH

LANES = 128
SEQ_TILE = 1024
SUB_TILES = (256, 256, 256, 256)
PROJ_COLS = 512
ITEM_LAG = 2
PROJECT_COLUMN_PIECES = 6
PROJECT_OUT_DOTS = D_MODEL // PROJ_COLS
HEAD_STAGES_PER_CHUNK = 2 * (SWA_HEADS // 2 + MEM_HEADS // 2)
VMEM_LIMIT_BYTES = 62 * 1024 * 1024

BF16 = jnp.bfloat16
F32 = jnp.float32


def _t5_causal_buckets(dist):
    n = np.maximum(dist, 0)
    max_exact = N_BUCKETS // 2
    large = max_exact + (np.log(np.maximum(n, 1) / max_exact) / np.log(MAX_DISTANCE / max_exact)
                         * (N_BUCKETS - max_exact)).astype(np.int32)
    large = np.minimum(large, N_BUCKETS - 1)
    return np.where(n < max_exact, n, large).astype(np.int32)


def _band_buckets():
    qi = np.arange(CHUNK)[:, None]
    kj = np.arange(2 * CHUNK)[None, :]
    dist = qi + CHUNK - kj
    valid = (dist >= 0) & (dist < WINDOW)
    return np.where(valid, _t5_causal_buckets(dist), -1).astype(np.int32)


def _gelu_tanh(x):
    c = np.float32(np.sqrt(2.0 / np.pi))
    ck = np.float32(np.sqrt(2.0 / np.pi) * 0.044715)
    hx = 0.5 * x
    return hx + hx * jnp.tanh(x * (c + ck * (x * x)))


def _silu(z):
    hz = 0.5 * z
    return hz + hz * jnp.tanh(hz)


def _dot(a, b):
    return jnp.dot(a, b, preferred_element_type=F32)


_DONE = object()


def _skewed(items, lag):
    waiting = []
    for item in items:
        if next(item, _DONE) is not _DONE:
            waiting.append(item)
        yield
        if len(waiting) > lag:
            next(waiting.pop(0), _DONE)
            yield
    for item in waiting:
        next(item, _DONE)
        yield


def _chain(*gens):
    for g in gens:
        yield from g


def _interleave(a, b, b_per_a):
    credit = 0.0
    a_live = b_live = True
    while a_live or b_live:
        if a_live:
            a_live = next(a, _DONE) is not _DONE
        credit += b_per_a
        while b_live and (credit >= 1.0 or not a_live):
            b_live = next(b, _DONE) is not _DONE
            credit -= 1.0


def _lane_half_mask(parity):
    lane = lax.broadcasted_iota(jnp.int32, (1, LANES), 1)
    return (lane >= HEAD_DIM) if parity else (lane < HEAD_DIM)


def _place_head_rows(kt_head, parity):
    zeros = jnp.zeros_like(kt_head)
    return jnp.concatenate([zeros, kt_head] if parity else [kt_head, zeros], axis=0)


def _layer_kernel(sinks_ref,
                  x_ref, xnext_ref, mem_ref, win_ref, wkv_f32_ref, wout_f32_ref, preg_ref, memg_ref, postg_ref,
                  vng_ref, vnb_ref, ws_ref, bs_ref, relb_ref, buckets_ref,
                  out_ref,
                  h_scr, gu_scr, vn_scr, qs_scr, ktvar_scr, vvar_scr, zg_scr,
                  mk_scr, mv_scr, bias_scr, wsm_scr, bst_scr, wkv_ref, wout_ref,
                  y_scr, ybf_scr, o_scr):
    ts = x_ref.shape[1]
    b = pl.program_id(0)
    t = pl.program_id(1)

    @pl.when((b == 0) & (t == 0))
    def _init():
        row = lax.broadcasted_iota(jnp.int32, (CHUNK, CHUNK), 0)
        col = lax.broadcasted_iota(jnp.int32, (CHUNK, CHUNK), 1)
        for g in range(A_GROUPS):
            wsm_scr[g] = jnp.where(row >= col, ws_ref[g], 0.0).astype(BF16)
            bs_col = jnp.sum(jnp.where(row == col, bs_ref[g:g + 1, :], 0.0), axis=1, keepdims=True)
            bst_scr[g] = jnp.broadcast_to(bs_col, (CHUNK, LANES))
        for r0 in range(0, D_MODEL, 2 * CHUNK):
            wrows = pl.ds(r0, 2 * CHUNK)
            wkv_ref[wrows, :] = wkv_f32_ref[wrows, :].astype(BF16)
            wout_ref[wrows, :] = wout_f32_ref[wrows, :].astype(BF16)
        buckets = buckets_ref[...]
        prev_key = lax.broadcasted_iota(jnp.int32, (CHUNK, 2 * CHUNK), 1) < CHUNK
        for hd in range(SWA_HEADS):
            tbl = jnp.full((CHUNK, 2 * CHUNK), NEG, F32)
            for bk in range(N_BUCKETS):
                tbl = jnp.where(buckets == bk, relb_ref[bk:bk + 1, hd:hd + 1] * LOG2E, tbl)
            bias_scr[0, hd] = tbl
            bias_scr[1, hd] = jnp.where(prev_key, NEG, tbl)

    @pl.when(t == 0)
    def _sequence_start():
        ktvar_scr[:, :, 0:CHUNK] = jnp.zeros((SWA_HEADS, LANES, CHUNK), BF16)
        vvar_scr[:, 0:CHUNK, :] = jnp.zeros((SWA_HEADS, CHUNK, LANES), BF16)
        m = mem_ref[0]
        ms = jnp.mean(m * m, axis=-1, keepdims=True)
        hm = (m * lax.rsqrt(ms + EPS) * memg_ref[...]).astype(BF16)
        kv = _dot(hm, wkv_ref[...])
        for pair in range(MEM_HEADS // 2):
            kt_pair = kv[:, pair * LANES:(pair + 1) * LANES].T
            v_pair = kv[:, MEM_WIDTH + pair * LANES:MEM_WIDTH + (pair + 1) * LANES]
            for parity in range(2):
                hd = 2 * pair + parity
                kt_head = kt_pair[parity * HEAD_DIM:(parity + 1) * HEAD_DIM]
                mk_scr[hd] = _place_head_rows(kt_head, parity).astype(BF16)
                mv_scr[hd] = jnp.where(_lane_half_mask(parity), v_pair, 0.0).astype(BF16)

    @pl.when(t > 0)
    def _copy_carry():
        ktvar_scr[:, :, 0:CHUNK] = ktvar_scr[:, :, ts:ts + CHUNK]
        vvar_scr[:, 0:CHUNK, :] = vvar_scr[:, ts:ts + CHUNK, :]

    lo = _lane_half_mask(0)
    hi = _lane_half_mask(1)
    first_tile = jnp.where(t == 0, 1, 0)

    def pre_norm(r0, nrows, src_ref=x_ref):
        for rc in range(nrows // CHUNK):
            crows = pl.ds(r0 + rc * CHUNK, CHUNK)
            x = src_ref[0, crows, :]
            ms = jnp.mean(x * x, axis=-1, keepdims=True)
            h_scr[crows, :] = (x * lax.rsqrt(ms + EPS) * preg_ref[...]).astype(BF16)

    def proj(rows, c0, width):
        return _dot(h_scr[rows, :], win_ref[:, c0:c0 + width])

    def project(r0, nrows, gate_rest=True):
        rows = pl.ds(r0, nrows)

        v_all = _gelu_tanh(proj(rows, OFF_V, A_WIDTH))
        for g in range(A_GROUPS):
            cols = pl.ds(g * LANES, LANES)
            v = v_all[:, g * LANES:(g + 1) * LANES]
            mu = jnp.mean(v, axis=-1, keepdims=True)
            d = v - mu
            var = jnp.mean(d * d, axis=-1, keepdims=True)
            vn = d * lax.rsqrt(var + EPS) * vng_ref[:, cols] + vnb_ref[:, cols]
            vn_scr[rows, cols] = vn.astype(BF16)
        yield

        gu_scr[rows, :] = _gelu_tanh(proj(rows, OFF_U, A_WIDTH))
        yield

        qkv = proj(rows, OFF_SQ, SWA_WIDTH + 2 * SWA_KV_WIDTH)
        qs_scr[rows, 0:SWA_WIDTH] = (qkv[:, 0:SWA_WIDTH] * Q_SCALE).astype(BF16)
        skt = qkv[:, SWA_WIDTH:SWA_WIDTH + LANES].T
        sv = qkv[:, SWA_WIDTH + LANES:]
        sv_rot = pltpu.roll(sv, HEAD_DIM, 1)
        new_rows = pl.ds(CHUNK + r0, nrows)
        for hd, (keep, vsrc) in enumerate([(lo, sv), (hi, sv_rot), (lo, sv_rot), (hi, sv)]):
            kv_head = hd // 2
            kt_head = skt[kv_head * HEAD_DIM:(kv_head + 1) * HEAD_DIM]
            ktvar_scr[hd, :, new_rows] = _place_head_rows(kt_head, hd % 2).astype(BF16)
            vvar_scr[hd, new_rows, :] = jnp.where(keep, vsrc, 0.0).astype(BF16)
        yield

        mqz = proj(rows, OFF_MQ, PROJ_COLS)
        qs_scr[rows, SWA_WIDTH:] = (mqz[:, 0:MEM_WIDTH] * Q_SCALE).astype(BF16)
        zg_scr[rows, 0:PROJ_COLS - MEM_WIDTH] = _silu(mqz[:, MEM_WIDTH:])
        yield

        for c in range(r0 // CHUNK, (r0 + nrows) // CHUNK):
            for g in range(A_GROUPS):
                spatial_gating(c, g)
            yield

        if gate_rest:
            yield from project_gate_rest(r0, nrows)

    def project_gate_rest(r0, nrows):
        rows = pl.ds(r0, nrows)
        done = PROJ_COLS - MEM_WIDTH
        while done < MIX_WIDTH:
            width = min(PROJ_COLS, MIX_WIDTH - done)
            zg_scr[rows, done:done + width] = _silu(proj(rows, OFF_Z + done, width))
            done += width
            yield

    def spatial_gating(c, g):
        rows = pl.ds(c * CHUNK, CHUNK)
        cols = pl.ds(g * LANES, LANES)
        sv_g = _dot(wsm_scr[g], vn_scr[rows, cols]) + bst_scr[g]
        y_scr[rows, cols] = gu_scr[rows, cols] * sv_g

    def attn_item(c, pair, sliding):
        r0 = c * CHUNK
        rows = pl.ds(r0, CHUNK)
        heads = (2 * pair, 2 * pair + 1)
        if sliding:
            band = pl.ds(r0, 2 * CHUNK)
            qp = qs_scr[rows, pl.ds(pair * LANES, LANES)]
            keys = jnp.concatenate([ktvar_scr[hd, :, band] for hd in heads], axis=1)
            bias_sel = first_tile if c == 0 else 0
            bias = jnp.concatenate([bias_scr[bias_sel, hd] for hd in heads], axis=1)
            logits = _dot(qp, keys) + bias
        else:
            qp = qs_scr[rows, pl.ds(SWA_WIDTH + pair * LANES, LANES)]
            keys = jnp.concatenate([mk_scr[hd] for hd in heads], axis=1)
            logits = _dot(qp, keys)
        n_keys = logits.shape[1] // 2
        yield
        probs, recips = [], []
        for i, hd in enumerate(heads):
            s = logits[:, i * n_keys:(i + 1) * n_keys]
            m = jnp.max(s, axis=-1, keepdims=True)
            if sliding:
                sink = sinks_ref[hd] * LOG2E
                m = jnp.maximum(m, sink)
            p = jnp.exp2(s - m)
            denom = jnp.sum(p, axis=-1, keepdims=True)
            if sliding:
                denom = denom + jnp.exp2(sink - m)
            probs.append(p.astype(BF16))
            recips.append(1.0 / denom)
        if sliding:
            values = jnp.concatenate([vvar_scr[hd, band, :] for hd in heads], axis=0)
        else:
            values = jnp.concatenate([mv_scr[hd] for hd in heads], axis=0)
        o_pair = _dot(jnp.concatenate(probs, axis=1), values)
        o_pair = o_pair * jnp.where(lo, recips[0], recips[1])
        off = A_WIDTH if sliding else A_WIDTH + SWA_WIDTH
        y_scr[rows, pl.ds(off + pair * LANES, LANES)] = o_pair

    def head_items(r0, nrows):
        for c in range(r0 // CHUNK, (r0 + nrows) // CHUNK):
            for pair in range(SWA_HEADS // 2):
                yield attn_item(c, pair, True)
            for pair in range(MEM_HEADS // 2):
                yield attn_item(c, pair, False)

    def project_out(r0, nrows, k0=0, k1=MIX_WIDTH):
        rows = pl.ds(r0, nrows)
        ybf_scr[rows, k0:k1] = (y_scr[rows, k0:k1] * zg_scr[rows, k0:k1]).astype(BF16)
        for c0 in range(0, D_MODEL, PROJ_COLS):
            cols = pl.ds(c0, PROJ_COLS)
            part = _dot(ybf_scr[rows, k0:k1], wout_ref[k0:k1, cols])
            o_scr[rows, cols] = o_scr[rows, cols] + part if k0 else part
            yield
        if k1 < MIX_WIDTH:
            return
        for rc in range(nrows // CHUNK):
            crows = pl.ds(r0 + rc * CHUNK, CHUNK)
            o = o_scr[crows, :]
            ms = jnp.mean(o * o, axis=-1, keepdims=True)
            out_ref[0, crows, :] = x_ref[0, crows, :] + o * lax.rsqrt(ms + EPS) * postg_ref[...]
            yield

    assert sum(SUB_TILES) == ts
    starts = [sum(SUB_TILES[:s]) for s in range(len(SUB_TILES))]
    subs = list(zip(starts, SUB_TILES))
    n_sub = len(subs)

    @pl.when((b == 0) & (t == 0))
    def _first_projection():
        pre_norm(*subs[0])
        for _ in project(*subs[0]):
            pass

    for s in range(n_sub):
        heads = _skewed(head_items(*subs[s]), ITEM_LAG)
        if s + 1 < n_sub:
            last_projection = s + 2 == n_sub
            pre_norm(*subs[s + 1])
            filler = project(*subs[s + 1], gate_rest=not last_projection)
            n_filler = PROJECT_COLUMN_PIECES + subs[s + 1][1] // CHUNK - (2 if last_projection else 0)
        else:
            filler = _chain(project_gate_rest(*subs[s]), project_out(*subs[s], 0, A_WIDTH))
            n_filler = 2 + PROJECT_OUT_DOTS
        if s >= 1:
            filler = _chain(project_out(*subs[s - 1]), filler)
            n_filler += PROJECT_OUT_DOTS + subs[s - 1][1] // CHUNK
        _interleave(heads, filler, n_filler / ((subs[s][1] // CHUNK) * HEAD_STAGES_PER_CHUNK))
    for _ in project_out(*subs[-1], A_WIDTH, MIX_WIDTH):
        pass
    pre_norm(*subs[0], src_ref=xnext_ref)
    for _ in project(*subs[0]):
        pass


def _layer(x, mem, w_in_bf16, w_kv, w_out, pre_g, mem_g, post_g, vng, vnb, w_spatial,
           b_spatial, buckets, sinks, rel_bias):
    batch, seq, _ = x.shape
    ts = SEQ_TILE
    n_tiles = seq // ts
    const2 = lambda b, t, *_: (0, 0)
    const3 = lambda b, t, *_: (0, 0, 0)

    def next_first_sub_tile(b, t, *_):
        step = jnp.minimum(b * n_tiles + t + 1, batch * n_tiles - 1)
        return (step // n_tiles, (step % n_tiles) * (ts // SUB_TILES[0]), 0)

    grid_spec = pltpu.PrefetchScalarGridSpec(
        num_scalar_prefetch=1,
        grid=(batch, n_tiles),
        in_specs=[
            pl.BlockSpec((1, ts, D_MODEL), lambda b, t, *_: (b, t, 0)),
            pl.BlockSpec((1, SUB_TILES[0], D_MODEL), next_first_sub_tile),
            pl.BlockSpec((1, MEM_LEN, D_MODEL), lambda b, t, *_: (b, 0, 0)),
            pl.BlockSpec((D_MODEL, IN_WIDTH), const2),
            pl.BlockSpec((D_MODEL, 2 * MEM_WIDTH), const2),
            pl.BlockSpec((MIX_WIDTH, D_MODEL), const2),
            pl.BlockSpec((1, D_MODEL), const2),
            pl.BlockSpec((1, D_MODEL), const2),
            pl.BlockSpec((1, D_MODEL), const2),
            pl.BlockSpec((1, A_WIDTH), const2),
            pl.BlockSpec((1, A_WIDTH), const2),
            pl.BlockSpec((A_GROUPS, CHUNK, CHUNK), const3),
            pl.BlockSpec((A_GROUPS, CHUNK), const2),
            pl.BlockSpec((N_BUCKETS, SWA_HEADS), const2),
            pl.BlockSpec((CHUNK, 2 * CHUNK), const2),
        ],
        out_specs=pl.BlockSpec((1, ts, D_MODEL), lambda b, t, *_: (b, t, 0)),
        scratch_shapes=[
            pltpu.VMEM((ts, D_MODEL), BF16),
            pltpu.VMEM((ts, A_WIDTH), F32),
            pltpu.VMEM((ts, A_WIDTH), BF16),
            pltpu.VMEM((ts, SWA_WIDTH + MEM_WIDTH), BF16),
            pltpu.VMEM((SWA_HEADS, LANES, CHUNK + ts), BF16),
            pltpu.VMEM((SWA_HEADS, CHUNK + ts, LANES), BF16),
            pltpu.VMEM((ts, MIX_WIDTH), F32),
            pltpu.VMEM((MEM_HEADS, LANES, MEM_LEN), BF16),
            pltpu.VMEM((MEM_HEADS, MEM_LEN, LANES), BF16),
            pltpu.VMEM((2, SWA_HEADS, CHUNK, 2 * CHUNK), F32),
            pltpu.VMEM((A_GROUPS, CHUNK, CHUNK), BF16),
            pltpu.VMEM((A_GROUPS, CHUNK, LANES), F32),
            pltpu.VMEM((D_MODEL, 2 * MEM_WIDTH), BF16),
            pltpu.VMEM((MIX_WIDTH, D_MODEL), BF16),
            pltpu.VMEM((ts, MIX_WIDTH), F32),
            pltpu.VMEM((ts, MIX_WIDTH), BF16),
            pltpu.VMEM((ts, D_MODEL), F32),
        ],
    )
    return pl.pallas_call(
        _layer_kernel,
        grid_spec=grid_spec,
        out_shape=jax.ShapeDtypeStruct(x.shape, x.dtype),
        compiler_params=pltpu.CompilerParams(
            dimension_semantics=("arbitrary", "arbitrary"),
            vmem_limit_bytes=VMEM_LIMIT_BYTES),
        name="layer",
    )(sinks, x, x, mem, w_in_bf16, w_kv, w_out, pre_g, mem_g, post_g, vng, vnb,
      w_spatial, b_spatial, rel_bias, buckets)


def kernel(x, mem, pre_norm_g, post_norm_g, mem_norm_g, w_in, w_mem_kv, v_norm_g, v_norm_b,
           w_spatial, b_spatial, attn_sinks, rel_bias, w_out):
    depth = w_in.shape[0]
    buckets = jnp.asarray(_band_buckets())
    for layer in range(depth):
        x = _layer(x, mem, w_in[layer].astype(BF16), w_mem_kv[layer], w_out[layer],
                   pre_norm_g[layer][None, :], mem_norm_g[layer][None, :], post_norm_g[layer][None, :],
                   v_norm_g[layer][None, :], v_norm_b[layer][None, :],
                   w_spatial[layer], b_spatial[layer], buckets, attn_sinks[layer], rel_bias)
    return x
```

```python
import numpy as np
import jax
import jax.numpy as jnp
from jax import lax
from jax.experimental import pallas as pl
from jax.experimental.pallas import tpu as pltpu

D_MODEL = 1024
MEM_LEN = 256
HEAD_DIM = 64
CHUNK = 128
A_GROUPS = 4
A_WIDTH = 512
SWA_HEADS = 4
SWA_WIDTH = 256
SWA_KV_WIDTH = 128
MEM_HEADS = 4
MEM_WIDTH = 256
MIX_WIDTH = 1024
IN_WIDTH = 2816
N_BUCKETS = 32
MAX_DISTANCE = 128
WINDOW = 128
EPS = 1e-6
NEG = -1e30
LOG2E = float(np.log2(np.e))
Q_SCALE = HEAD_DIM ** -0.5 * LOG2E

OFF_U = 0
OFF_V = A_WIDTH
OFF_SQ = 2 * A_WIDTH
OFF_SK = OFF_SQ + SWA_WIDTH
OFF_SV = OFF_SK + SWA_KV_WIDTH
OFF_MQ = OFF_SV + SWA_KV_WIDTH
OFF_Z = OFF_MQ + MEM_WIDTH

LANES = 128
SEQ_TILE = 1024
SUB_TILES = (256, 256, 256, 256)
PROJ_COLS = 512
ITEM_LAG = 2
PROJECT_COLUMN_PIECES = 6
PROJECT_OUT_DOTS = D_MODEL // PROJ_COLS
HEAD_STAGES_PER_CHUNK = 2 * (SWA_HEADS // 2 + MEM_HEADS // 2)
VMEM_LIMIT_BYTES = 62 * 1024 * 1024

BF16 = jnp.bfloat16
F32 = jnp.float32


def _t5_causal_buckets(dist):
    n = np.maximum(dist, 0)
    max_exact = N_BUCKETS // 2
    large = max_exact + (np.log(np.maximum(n, 1) / max_exact) / np.log(MAX_DISTANCE / max_exact)
                         * (N_BUCKETS - max_exact)).astype(np.int32)
    large = np.minimum(large, N_BUCKETS - 1)
    return np.where(n < max_exact, n, large).astype(np.int32)


def _band_buckets():
    qi = np.arange(CHUNK)[:, None]
    kj = np.arange(2 * CHUNK)[None, :]
    dist = qi + CHUNK - kj
    valid = (dist >= 0) & (dist < WINDOW)
    return np.where(valid, _t5_causal_buckets(dist), -1).astype(np.int32)


def _gelu_tanh(x):
    c = np.float32(np.sqrt(2.0 / np.pi))
    ck = np.float32(np.sqrt(2.0 / np.pi) * 0.044715)
    hx = 0.5 * x
    return hx + hx * jnp.tanh(x * (c + ck * (x * x)))


def _silu(z):
    hz = 0.5 * z
    return hz + hz * jnp.tanh(hz)


def _dot(a, b):
    return jnp.dot(a, b, preferred_element_type=F32)


_DONE = object()


def _skewed(items, lag):
    waiting = []
    for item in items:
        if next(item, _DONE) is not _DONE:
            waiting.append(item)
        yield
        if len(waiting) > lag:
            next(waiting.pop(0), _DONE)
            yield
    for item in waiting:
        next(item, _DONE)
        yield


def _chain(*gens):
    for g in gens:
        yield from g


def _interleave(a, b, b_per_a):
    credit = 0.0
    a_live = b_live = True
    while a_live or b_live:
        if a_live:
            a_live = next(a, _DONE) is not _DONE
        credit += b_per_a
        while b_live and (credit >= 1.0 or not a_live):
            b_live = next(b, _DONE) is not _DONE
            credit -= 1.0


def _lane_half_mask(parity):
    lane = lax.broadcasted_iota(jnp.int32, (1, LANES), 1)
    return (lane >= HEAD_DIM) if parity else (lane < HEAD_DIM)


def _place_head_rows(kt_head, parity):
    zeros = jnp.zeros_like(kt_head)
    return jnp.concatenate([zeros, kt_head] if parity else [kt_head, zeros], axis=0)


def _layer_kernel(sinks_ref,
                  x_ref, xnext_ref, mem_ref, win_ref, wkv_f32_ref, wout_f32_ref, preg_ref, memg_ref, postg_ref,
                  vng_ref, vnb_ref, ws_ref, bs_ref, relb_ref, buckets_ref,
                  out_ref,
                  h_scr, gu_scr, vn_scr, qs_scr, ktvar_scr, vvar_scr, zg_scr,
                  mk_scr, mv_scr, bias_scr, wsm_scr, bst_scr, wkv_ref, wout_ref,
                  y_scr, ybf_scr, o_scr):
    ts = x_ref.shape[1]
    b = pl.program_id(0)
    t = pl.program_id(1)

    @pl.when((b == 0) & (t == 0))
    def _init():
        row = lax.broadcasted_iota(jnp.int32, (CHUNK, CHUNK), 0)
        col = lax.broadcasted_iota(jnp.int32, (CHUNK, CHUNK), 1)
        for g in range(A_GROUPS):
            wsm_scr[g] = jnp.where(row >= col, ws_ref[g], 0.0).astype(BF16)
            bs_col = jnp.sum(jnp.where(row == col, bs_ref[g:g + 1, :], 0.0), axis=1, keepdims=True)
            bst_scr[g] = jnp.broadcast_to(bs_col, (CHUNK, LANES))
        for r0 in range(0, D_MODEL, 2 * CHUNK):
            wrows = pl.ds(r0, 2 * CHUNK)
            wkv_ref[wrows, :] = wkv_f32_ref[wrows, :].astype(BF16)
            wout_ref[wrows, :] = wout_f32_ref[wrows, :].astype(BF16)
        buckets = buckets_ref[...]
        prev_key = lax.broadcasted_iota(jnp.int32, (CHUNK, 2 * CHUNK), 1) < CHUNK
        for hd in range(SWA_HEADS):
            tbl = jnp.full((CHUNK, 2 * CHUNK), NEG, F32)
            for bk in range(N_BUCKETS):
                tbl = jnp.where(buckets == bk, relb_ref[bk:bk + 1, hd:hd + 1] * LOG2E, tbl)
            bias_scr[0, hd] = tbl
            bias_scr[1, hd] = jnp.where(prev_key, NEG, tbl)

    @pl.when(t == 0)
    def _sequence_start():
        ktvar_scr[:, :, 0:CHUNK] = jnp.zeros((SWA_HEADS, LANES, CHUNK), BF16)
        vvar_scr[:, 0:CHUNK, :] = jnp.zeros((SWA_HEADS, CHUNK, LANES), BF16)
        m = mem_ref[0]
        ms = jnp.mean(m * m, axis=-1, keepdims=True)
        hm = (m * lax.rsqrt(ms + EPS) * memg_ref[...]).astype(BF16)
        kv = _dot(hm, wkv_ref[...])
        for pair in range(MEM_HEADS // 2):
            kt_pair = kv[:, pair * LANES:(pair + 1) * LANES].T
            v_pair = kv[:, MEM_WIDTH + pair * LANES:MEM_WIDTH + (pair + 1) * LANES]
            for parity in range(2):
                hd = 2 * pair + parity
                kt_head = kt_pair[parity * HEAD_DIM:(parity + 1) * HEAD_DIM]
                mk_scr[hd] = _place_head_rows(kt_head, parity).astype(BF16)
                mv_scr[hd] = jnp.where(_lane_half_mask(parity), v_pair, 0.0).astype(BF16)

    @pl.when(t > 0)
    def _copy_carry():
        ktvar_scr[:, :, 0:CHUNK] = ktvar_scr[:, :, ts:ts + CHUNK]
        vvar_scr[:, 0:CHUNK, :] = vvar_scr[:, ts:ts + CHUNK, :]

    lo = _lane_half_mask(0)
    hi = _lane_half_mask(1)
    first_tile = jnp.where(t == 0, 1, 0)

    def pre_norm(r0, nrows, src_ref=x_ref):
        for rc in range(nrows // CHUNK):
            crows = pl.ds(r0 + rc * CHUNK, CHUNK)
            x = src_ref[0, crows, :]
            ms = jnp.mean(x * x, axis=-1, keepdims=True)
            h_scr[crows, :] = (x * lax.rsqrt(ms + EPS) * preg_ref[...]).astype(BF16)

    def proj(rows, c0, width):
        return _dot(h_scr[rows, :], win_ref[:, c0:c0 + width])

    def project(r0, nrows):
        rows = pl.ds(r0, nrows)

        v_all = _gelu_tanh(proj(rows, OFF_V, A_WIDTH))
        for g in range(A_GROUPS):
            cols = pl.ds(g * LANES, LANES)
            v = v_all[:, g * LANES:(g + 1) * LANES]
            mu = jnp.mean(v, axis=-1, keepdims=True)
            d = v - mu
            var = jnp.mean(d * d, axis=-1, keepdims=True)
            vn = d * lax.rsqrt(var + EPS) * vng_ref[:, cols] + vnb_ref[:, cols]
            vn_scr[rows, cols] = vn.astype(BF16)
        yield

        gu_scr[rows, :] = _gelu_tanh(proj(rows, OFF_U, A_WIDTH))
        yield

        qkv = proj(rows, OFF_SQ, SWA_WIDTH + 2 * SWA_KV_WIDTH)
        qs_scr[rows, 0:SWA_WIDTH] = (qkv[:, 0:SWA_WIDTH] * Q_SCALE).astype(BF16)
        skt = qkv[:, SWA_WIDTH:SWA_WIDTH + LANES].T
        sv = qkv[:, SWA_WIDTH + LANES:]
        sv_rot = pltpu.roll(sv, HEAD_DIM, 1)
        new_rows = pl.ds(CHUNK + r0, nrows)
        for hd, (keep, vsrc) in enumerate([(lo, sv), (hi, sv_rot), (lo, sv_rot), (hi, sv)]):
            kv_head = hd // 2
            kt_head = skt[kv_head * HEAD_DIM:(kv_head + 1) * HEAD_DIM]
            ktvar_scr[hd, :, new_rows] = _place_head_rows(kt_head, hd % 2).astype(BF16)
            vvar_scr[hd, new_rows, :] = jnp.where(keep, vsrc, 0.0).astype(BF16)
        yield

        mqz = proj(rows, OFF_MQ, PROJ_COLS)
        qs_scr[rows, SWA_WIDTH:] = (mqz[:, 0:MEM_WIDTH] * Q_SCALE).astype(BF16)
        zg_scr[rows, 0:PROJ_COLS - MEM_WIDTH] = _silu(mqz[:, MEM_WIDTH:])
        yield

        for c in range(r0 // CHUNK, (r0 + nrows) // CHUNK):
            for g in range(A_GROUPS):
                spatial_gating(c, g)
            yield

        done = PROJ_COLS - MEM_WIDTH
        while done < MIX_WIDTH:
            width = (MIX_WIDTH - done) % PROJ_COLS or PROJ_COLS
            zg_scr[rows, done:done + width] = _silu(proj(rows, OFF_Z + done, width))
            done += width
            yield

    def spatial_gating(c, g):
        rows = pl.ds(c * CHUNK, CHUNK)
        cols = pl.ds(g * LANES, LANES)
        sv_g = _dot(wsm_scr[g], vn_scr[rows, cols]) + bst_scr[g]
        y_scr[rows, cols] = gu_scr[rows, cols] * sv_g

    def attn_item(c, pair, sliding):
        r0 = c * CHUNK
        rows = pl.ds(r0, CHUNK)
        heads = (2 * pair, 2 * pair + 1)
        if sliding:
            band = pl.ds(r0, 2 * CHUNK)
            qp = qs_scr[rows, pl.ds(pair * LANES, LANES)]
            keys = jnp.concatenate([ktvar_scr[hd, :, band] for hd in heads], axis=1)
            bias_sel = first_tile if c == 0 else 0
            bias = jnp.concatenate([bias_scr[bias_sel, hd] for hd in heads], axis=1)
            logits = _dot(qp, keys) + bias
        else:
            qp = qs_scr[rows, pl.ds(SWA_WIDTH + pair * LANES, LANES)]
            keys = jnp.concatenate([mk_scr[hd] for hd in heads], axis=1)
            logits = _dot(qp, keys)
        n_keys = logits.shape[1] // 2
        yield
        probs, recips = [], []
        for i, hd in enumerate(heads):
            s = logits[:, i * n_keys:(i + 1) * n_keys]
            m = jnp.max(s, axis=-1, keepdims=True)
            if sliding:
                sink = sinks_ref[hd] * LOG2E
                m = jnp.maximum(m, sink)
            p = jnp.exp2(s - m)
            denom = jnp.sum(p, axis=-1, keepdims=True)
            if sliding:
                denom = denom + jnp.exp2(sink - m)
            probs.append(p.astype(BF16))
            recips.append(1.0 / denom)
        if sliding:
            values = jnp.concatenate([vvar_scr[hd, band, :] for hd in heads], axis=0)
        else:
            values = jnp.concatenate([mv_scr[hd] for hd in heads], axis=0)
        o_pair = _dot(jnp.concatenate(probs, axis=1), values)
        o_pair = o_pair * jnp.where(lo, recips[0], recips[1])
        off = A_WIDTH if sliding else A_WIDTH + SWA_WIDTH
        y_scr[rows, pl.ds(off + pair * LANES, LANES)] = o_pair

    def head_items(r0, nrows):
        for c in range(r0 // CHUNK, (r0 + nrows) // CHUNK):
            for pair in range(SWA_HEADS // 2):
                yield attn_item(c, pair, True)
            for pair in range(MEM_HEADS // 2):
                yield attn_item(c, pair, False)

    def project_out(r0, nrows, k0=0, k1=MIX_WIDTH):
        rows = pl.ds(r0, nrows)
        ybf_scr[rows, k0:k1] = (y_scr[rows, k0:k1] * zg_scr[rows, k0:k1]).astype(BF16)
        for c0 in range(0, D_MODEL, PROJ_COLS):
            cols = pl.ds(c0, PROJ_COLS)
            part = _dot(ybf_scr[rows, k0:k1], wout_ref[k0:k1, cols])
            o_scr[rows, cols] = o_scr[rows, cols] + part if k0 else part
            yield
        if k1 < MIX_WIDTH:
            return
        for rc in range(nrows // CHUNK):
            crows = pl.ds(r0 + rc * CHUNK, CHUNK)
            o = o_scr[crows, :]
            ms = jnp.mean(o * o, axis=-1, keepdims=True)
            out_ref[0, crows, :] = x_ref[0, crows, :] + o * lax.rsqrt(ms + EPS) * postg_ref[...]
            yield

    assert sum(SUB_TILES) == ts
    starts = [sum(SUB_TILES[:s]) for s in range(len(SUB_TILES))]
    subs = list(zip(starts, SUB_TILES))
    n_sub = len(subs)

    @pl.when((b == 0) & (t == 0))
    def _first_projection():
        pre_norm(*subs[0])
        for _ in project(*subs[0]):
            pass

    for s in range(n_sub):
        heads = _skewed(head_items(*subs[s]), ITEM_LAG)
        if s + 1 < n_sub:
            pre_norm(*subs[s + 1])
            filler = project(*subs[s + 1])
            n_filler = PROJECT_COLUMN_PIECES + subs[s + 1][1] // CHUNK
        else:
            filler = project_out(*subs[s], 0, A_WIDTH)
            n_filler = PROJECT_OUT_DOTS
        if s >= 1:
            filler = _chain(project_out(*subs[s - 1]), filler)
            n_filler += PROJECT_OUT_DOTS + subs[s - 1][1] // CHUNK
        _interleave(heads, filler, n_filler / ((subs[s][1] // CHUNK) * HEAD_STAGES_PER_CHUNK))
    for _ in project_out(*subs[-1], A_WIDTH, MIX_WIDTH):
        pass
    pre_norm(*subs[0], src_ref=xnext_ref)
    for _ in project(*subs[0]):
        pass


def _layer(x, mem, w_in_bf16, w_kv, w_out, pre_g, mem_g, post_g, vng, vnb, w_spatial,
           b_spatial, buckets, sinks, rel_bias):
    batch, seq, _ = x.shape
    ts = SEQ_TILE
    n_tiles = seq // ts
    const2 = lambda b, t, *_: (0, 0)
    const3 = lambda b, t, *_: (0, 0, 0)

    def next_first_sub_tile(b, t, *_):
        step = jnp.minimum(b * n_tiles + t + 1, batch * n_tiles - 1)
        return (step // n_tiles, (step % n_tiles) * (ts // SUB_TILES[0]), 0)

    grid_spec = pltpu.PrefetchScalarGridSpec(
        num_scalar_prefetch=1,
        grid=(batch, n_tiles),
        in_specs=[
            pl.BlockSpec((1, ts, D_MODEL), lambda b, t, *_: (b, t, 0)),
            pl.BlockSpec((1, SUB_TILES[0], D_MODEL), next_first_sub_tile),
            pl.BlockSpec((1, MEM_LEN, D_MODEL), lambda b, t, *_: (b, 0, 0)),
            pl.BlockSpec((D_MODEL, IN_WIDTH), const2),
            pl.BlockSpec((D_MODEL, 2 * MEM_WIDTH), const2),
            pl.BlockSpec((MIX_WIDTH, D_MODEL), const2),
            pl.BlockSpec((1, D_MODEL), const2),
            pl.BlockSpec((1, D_MODEL), const2),
            pl.BlockSpec((1, D_MODEL), const2),
            pl.BlockSpec((1, A_WIDTH), const2),
            pl.BlockSpec((1, A_WIDTH), const2),
            pl.BlockSpec((A_GROUPS, CHUNK, CHUNK), const3),
            pl.BlockSpec((A_GROUPS, CHUNK), const2),
            pl.BlockSpec((N_BUCKETS, SWA_HEADS), const2),
            pl.BlockSpec((CHUNK, 2 * CHUNK), const2),
        ],
        out_specs=pl.BlockSpec((1, ts, D_MODEL), lambda b, t, *_: (b, t, 0)),
        scratch_shapes=[
            pltpu.VMEM((ts, D_MODEL), BF16),
            pltpu.VMEM((ts, A_WIDTH), F32),
            pltpu.VMEM((ts, A_WIDTH), BF16),
            pltpu.VMEM((ts, SWA_WIDTH + MEM_WIDTH), BF16),
            pltpu.VMEM((SWA_HEADS, LANES, CHUNK + ts), BF16),
            pltpu.VMEM((SWA_HEADS, CHUNK + ts, LANES), BF16),
            pltpu.VMEM((ts, MIX_WIDTH), F32),
            pltpu.VMEM((MEM_HEADS, LANES, MEM_LEN), BF16),
            pltpu.VMEM((MEM_HEADS, MEM_LEN, LANES), BF16),
            pltpu.VMEM((2, SWA_HEADS, CHUNK, 2 * CHUNK), F32),
            pltpu.VMEM((A_GROUPS, CHUNK, CHUNK), BF16),
            pltpu.VMEM((A_GROUPS, CHUNK, LANES), F32),
            pltpu.VMEM((D_MODEL, 2 * MEM_WIDTH), BF16),
            pltpu.VMEM((MIX_WIDTH, D_MODEL), BF16),
            pltpu.VMEM((ts, MIX_WIDTH), F32),
            pltpu.VMEM((ts, MIX_WIDTH), BF16),
            pltpu.VMEM((ts, D_MODEL), F32),
        ],
    )
    return pl.pallas_call(
        _layer_kernel,
        grid_spec=grid_spec,
        out_shape=jax.ShapeDtypeStruct(x.shape, x.dtype),
        compiler_params=pltpu.CompilerParams(
            dimension_semantics=("arbitrary", "arbitrary"),
            vmem_limit_bytes=VMEM_LIMIT_BYTES),
        name="layer",
    )(sinks, x, x, mem, w_in_bf16, w_kv, w_out, pre_g, mem_g, post_g, vng, vnb,
      w_spatial, b_spatial, rel_bias, buckets)


def kernel(x, mem, pre_norm_g, post_norm_g, mem_norm_g, w_in, w_mem_kv, v_norm_g, v_norm_b,
           w_spatial, b_spatial, attn_sinks, rel_bias, w_out):
    depth = w_in.shape[0]
    buckets = jnp.asarray(_band_buckets())
    for layer in range(depth):
        x = _layer(x, mem, w_in[layer].astype(BF16), w_mem_kv[layer], w_out[layer],
                   pre_norm_g[layer][None, :], mem_norm_g[layer][None, :], post_norm_g[layer][None, :],
                   v_norm_g[layer][None, :], v_norm_b[layer][None, :],
                   w_spatial[layer], b_spatial[layer], buckets, attn_sinks[layer], rel_bias)
    return x
```

```python
import numpy as np
import jax
import jax.numpy as jnp
from jax import lax
from jax.experimental import pallas as pl
from jax.experimental.pallas import tpu as pltpu

D_MODEL = 1024
MEM_LEN = 256
HEAD_DIM = 64
CHUNK = 128
A_GROUPS = 4
A_WIDTH = 512
SWA_HEADS = 4
SWA_WIDTH = 256
SWA_KV_WIDTH = 128
MEM_HEADS = 4
MEM_WIDTH = 256
MIX_WIDTH = 1024
IN_WIDTH = 2816
N_BUCKETS = 32
MAX_DISTANCE = 128
WINDOW = 128
EPS = 1e-6
NEG = -1e30
LOG2E = float(np.log2(np.e))
Q_SCALE = HEAD_DIM ** -0.5 * LOG2E

OFF_U = 0
OFF_V = A_WIDTH
OFF_SQ = 2 * A_WIDTH
OFF_SK = OFF_SQ + SWA_WIDTH
OFF_SV = OFF_SK + SWA_KV_WIDTH
OFF_MQ = OFF_SV + SWA_KV_WIDTH
OFF_Z = OFF_MQ + MEM_WIDTH

LANES = 128
SEQ_TILE = 1024
SUB_TILES = (256, 256, 256, 256)
PROJ_COLS = 512
ITEM_LAG = 2
PROJECT_COLUMN_PIECES = 6
PROJECT_OUT_DOTS = D_MODEL // PROJ_COLS
HEAD_STAGES_PER_CHUNK = 2 * (SWA_HEADS // 2 + MEM_HEADS // 2)
VMEM_LIMIT_BYTES = 62 * 1024 * 1024

BF16 = jnp.bfloat16
F32 = jnp.float32


def _t5_causal_buckets(dist):
    n = np.maximum(dist, 0)
    max_exact = N_BUCKETS // 2
    large = max_exact + (np.log(np.maximum(n, 1) / max_exact) / np.log(MAX_DISTANCE / max_exact)
                         * (N_BUCKETS - max_exact)).astype(np.int32)
    large = np.minimum(large, N_BUCKETS - 1)
    return np.where(n < max_exact, n, large).astype(np.int32)


def _band_buckets():
    qi = np.arange(CHUNK)[:, None]
    kj = np.arange(2 * CHUNK)[None, :]
    dist = qi + CHUNK - kj
    valid = (dist >= 0) & (dist < WINDOW)
    return np.where(valid, _t5_causal_buckets(dist), -1).astype(np.int32)


def _gelu_tanh(x):
    c = np.float32(np.sqrt(2.0 / np.pi))
    ck = np.float32(np.sqrt(2.0 / np.pi) * 0.044715)
    hx = 0.5 * x
    return hx + hx * jnp.tanh(x * (c + ck * (x * x)))


def _silu(z):
    hz = 0.5 * z
    return hz + hz * jnp.tanh(hz)


def _dot(a, b):
    return jnp.dot(a, b, preferred_element_type=F32)


_DONE = object()


def _skewed(items, lag):
    waiting = []
    for item in items:
        if next(item, _DONE) is not _DONE:
            waiting.append(item)
        yield
        if len(waiting) > lag:
            next(waiting.pop(0), _DONE)
            yield
    for item in waiting:
        next(item, _DONE)
        yield


def _chain(*gens):
    for g in gens:
        yield from g


def _interleave(a, b, b_per_a):
    credit = 0.0
    a_live = b_live = True
    while a_live or b_live:
        if a_live:
            a_live = next(a, _DONE) is not _DONE
        credit += b_per_a
        while b_live and (credit >= 1.0 or not a_live):
            b_live = next(b, _DONE) is not _DONE
            credit -= 1.0


def _lane_half_mask(parity):
    lane = lax.broadcasted_iota(jnp.int32, (1, LANES), 1)
    return (lane >= HEAD_DIM) if parity else (lane < HEAD_DIM)


def _place_head_rows(kt_head, parity):
    zeros = jnp.zeros_like(kt_head)
    return jnp.concatenate([zeros, kt_head] if parity else [kt_head, zeros], axis=0)


def _layer_kernel(sinks_ref,
                  x_ref, xnext_ref, mem_ref, win_ref, wkv_f32_ref, wout_f32_ref, preg_ref, memg_ref, postg_ref,
                  vng_ref, vnb_ref, ws_ref, bs_ref, relb_ref, buckets_ref,
                  out_ref,
                  h_scr, gu_scr, vn_scr, qs_scr, ktvar_scr, vvar_scr, zg_scr,
                  mk_scr, mv_scr, bias_scr, wsm_scr, bst_scr, wkv_ref, wout_ref,
                  y_scr, ybf_scr, o_scr):
    ts = x_ref.shape[1]
    b = pl.program_id(0)
    t = pl.program_id(1)

    @pl.when((b == 0) & (t == 0))
    def _init():
        row = lax.broadcasted_iota(jnp.int32, (CHUNK, CHUNK), 0)
        col = lax.broadcasted_iota(jnp.int32, (CHUNK, CHUNK), 1)
        for g in range(A_GROUPS):
            wsm_scr[g] = jnp.where(row >= col, ws_ref[g], 0.0).astype(BF16)
            bs_col = jnp.sum(jnp.where(row == col, bs_ref[g:g + 1, :], 0.0), axis=1, keepdims=True)
            bst_scr[g] = jnp.broadcast_to(bs_col, (CHUNK, LANES))
        for r0 in range(0, D_MODEL, 2 * CHUNK):
            wrows = pl.ds(r0, 2 * CHUNK)
            wkv_ref[wrows, :] = wkv_f32_ref[wrows, :].astype(BF16)
            wout_ref[wrows, :] = wout_f32_ref[wrows, :].astype(BF16)
        buckets = buckets_ref[...]
        prev_key = lax.broadcasted_iota(jnp.int32, (CHUNK, 2 * CHUNK), 1) < CHUNK
        for hd in range(SWA_HEADS):
            tbl = jnp.full((CHUNK, 2 * CHUNK), NEG, F32)
            for bk in range(N_BUCKETS):
                tbl = jnp.where(buckets == bk, relb_ref[bk:bk + 1, hd:hd + 1] * LOG2E, tbl)
            bias_scr[0, hd] = tbl
            bias_scr[1, hd] = jnp.where(prev_key, NEG, tbl)

    @pl.when(t == 0)
    def _sequence_start():
        ktvar_scr[:, :, 0:CHUNK] = jnp.zeros((SWA_HEADS, LANES, CHUNK), BF16)
        vvar_scr[:, 0:CHUNK, :] = jnp.zeros((SWA_HEADS, CHUNK, LANES), BF16)
        m = mem_ref[0]
        ms = jnp.mean(m * m, axis=-1, keepdims=True)
        hm = (m * lax.rsqrt(ms + EPS) * memg_ref[...]).astype(BF16)
        kv = _dot(hm, wkv_ref[...])
        for pair in range(MEM_HEADS // 2):
            kt_pair = kv[:, pair * LANES:(pair + 1) * LANES].T
            v_pair = kv[:, MEM_WIDTH + pair * LANES:MEM_WIDTH + (pair + 1) * LANES]
            for parity in range(2):
                hd = 2 * pair + parity
                kt_head = kt_pair[parity * HEAD_DIM:(parity + 1) * HEAD_DIM]
                mk_scr[hd] = _place_head_rows(kt_head, parity).astype(BF16)
                mv_scr[hd] = jnp.where(_lane_half_mask(parity), v_pair, 0.0).astype(BF16)

    @pl.when(t > 0)
    def _copy_carry():
        ktvar_scr[:, :, 0:CHUNK] = ktvar_scr[:, :, ts:ts + CHUNK]
        vvar_scr[:, 0:CHUNK, :] = vvar_scr[:, ts:ts + CHUNK, :]

    lo = _lane_half_mask(0)
    hi = _lane_half_mask(1)
    first_tile = jnp.where(t == 0, 1, 0)

    def pre_norm(r0, nrows, src_ref=x_ref):
        for rc in range(nrows // CHUNK):
            crows = pl.ds(r0 + rc * CHUNK, CHUNK)
            x = src_ref[0, crows, :]
            ms = jnp.mean(x * x, axis=-1, keepdims=True)
            h_scr[crows, :] = (x * lax.rsqrt(ms + EPS) * preg_ref[...]).astype(BF16)

    def proj(rows, c0, width):
        return _dot(h_scr[rows, :], win_ref[:, c0:c0 + width])

    def project(r0, nrows):
        rows = pl.ds(r0, nrows)

        v_all = _gelu_tanh(proj(rows, OFF_V, A_WIDTH))
        for g in range(A_GROUPS):
            cols = pl.ds(g * LANES, LANES)
            v = v_all[:, g * LANES:(g + 1) * LANES]
            mu = jnp.mean(v, axis=-1, keepdims=True)
            d = v - mu
            var = jnp.mean(d * d, axis=-1, keepdims=True)
            vn = d * lax.rsqrt(var + EPS) * vng_ref[:, cols] + vnb_ref[:, cols]
            vn_scr[rows, cols] = vn.astype(BF16)
        yield

        gu_scr[rows, :] = _gelu_tanh(proj(rows, OFF_U, A_WIDTH))
        yield

        qkv = proj(rows, OFF_SQ, SWA_WIDTH + 2 * SWA_KV_WIDTH)
        qs_scr[rows, 0:SWA_WIDTH] = (qkv[:, 0:SWA_WIDTH] * Q_SCALE).astype(BF16)
        skt = qkv[:, SWA_WIDTH:SWA_WIDTH + LANES].T
        sv = qkv[:, SWA_WIDTH + LANES:]
        sv_rot = pltpu.roll(sv, HEAD_DIM, 1)
        new_rows = pl.ds(CHUNK + r0, nrows)
        for hd, (keep, vsrc) in enumerate([(lo, sv), (hi, sv_rot), (lo, sv_rot), (hi, sv)]):
            kv_head = hd // 2
            kt_head = skt[kv_head * HEAD_DIM:(kv_head + 1) * HEAD_DIM]
            ktvar_scr[hd, :, new_rows] = _place_head_rows(kt_head, hd % 2).astype(BF16)
            vvar_scr[hd, new_rows, :] = jnp.where(keep, vsrc, 0.0).astype(BF16)
        yield

        mqz = proj(rows, OFF_MQ, PROJ_COLS)
        qs_scr[rows, SWA_WIDTH:] = (mqz[:, 0:MEM_WIDTH] * Q_SCALE).astype(BF16)
        zg_scr[rows, 0:PROJ_COLS - MEM_WIDTH] = _silu(mqz[:, MEM_WIDTH:])
        yield

        for c in range(r0 // CHUNK, (r0 + nrows) // CHUNK):
            for g in range(A_GROUPS):
                spatial_gating(c, g)
            yield

        done = PROJ_COLS - MEM_WIDTH
        while done < MIX_WIDTH:
            width = min(PROJ_COLS, MIX_WIDTH - done)
            zg_scr[rows, done:done + width] = _silu(proj(rows, OFF_Z + done, width))
            done += width
            yield

    def spatial_gating(c, g):
        rows = pl.ds(c * CHUNK, CHUNK)
        cols = pl.ds(g * LANES, LANES)
        sv_g = _dot(wsm_scr[g], vn_scr[rows, cols]) + bst_scr[g]
        y_scr[rows, cols] = gu_scr[rows, cols] * sv_g

    def attn_item(c, pair, sliding):
        r0 = c * CHUNK
        rows = pl.ds(r0, CHUNK)
        heads = (2 * pair, 2 * pair + 1)
        if sliding:
            band = pl.ds(r0, 2 * CHUNK)
            qp = qs_scr[rows, pl.ds(pair * LANES, LANES)]
            keys = jnp.concatenate([ktvar_scr[hd, :, band] for hd in heads], axis=1)
            bias_sel = first_tile if c == 0 else 0
            bias = jnp.concatenate([bias_scr[bias_sel, hd] for hd in heads], axis=1)
            logits = _dot(qp, keys) + bias
        else:
            qp = qs_scr[rows, pl.ds(SWA_WIDTH + pair * LANES, LANES)]
            keys = jnp.concatenate([mk_scr[hd] for hd in heads], axis=1)
            logits = _dot(qp, keys)
        n_keys = logits.shape[1] // 2
        yield
        probs, recips = [], []
        for i, hd in enumerate(heads):
            s = logits[:, i * n_keys:(i + 1) * n_keys]
            m = jnp.max(s, axis=-1, keepdims=True)
            if sliding:
                sink = sinks_ref[hd] * LOG2E
                m = jnp.maximum(m, sink)
            p = jnp.exp2(s - m)
            denom = jnp.sum(p, axis=-1, keepdims=True)
            if sliding:
                denom = denom + jnp.exp2(sink - m)
            probs.append(p.astype(BF16))
            recips.append(1.0 / denom)
        if sliding:
            values = jnp.concatenate([vvar_scr[hd, band, :] for hd in heads], axis=0)
        else:
            values = jnp.concatenate([mv_scr[hd] for hd in heads], axis=0)
        o_pair = _dot(jnp.concatenate(probs, axis=1), values)
        o_pair = o_pair * jnp.where(lo, recips[0], recips[1])
        off = A_WIDTH if sliding else A_WIDTH + SWA_WIDTH
        y_scr[rows, pl.ds(off + pair * LANES, LANES)] = o_pair

    def head_items(r0, nrows):
        for c in range(r0 // CHUNK, (r0 + nrows) // CHUNK):
            for pair in range(SWA_HEADS // 2):
                yield attn_item(c, pair, True)
            for pair in range(MEM_HEADS // 2):
                yield attn_item(c, pair, False)

    def project_out(r0, nrows, k0=0, k1=MIX_WIDTH):
        rows = pl.ds(r0, nrows)
        ybf_scr[rows, k0:k1] = (y_scr[rows, k0:k1] * zg_scr[rows, k0:k1]).astype(BF16)
        for c0 in range(0, D_MODEL, PROJ_COLS):
            cols = pl.ds(c0, PROJ_COLS)
            part = _dot(ybf_scr[rows, k0:k1], wout_ref[k0:k1, cols])
            o_scr[rows, cols] = o_scr[rows, cols] + part if k0 else part
            yield
        if k1 < MIX_WIDTH:
            return
        for rc in range(nrows // CHUNK):
            crows = pl.ds(r0 + rc * CHUNK, CHUNK)
            o = o_scr[crows, :]
            ms = jnp.mean(o * o, axis=-1, keepdims=True)
            out_ref[0, crows, :] = x_ref[0, crows, :] + o * lax.rsqrt(ms + EPS) * postg_ref[...]
            yield

    assert sum(SUB_TILES) == ts
    starts = [sum(SUB_TILES[:s]) for s in range(len(SUB_TILES))]
    subs = list(zip(starts, SUB_TILES))
    n_sub = len(subs)

    @pl.when((b == 0) & (t == 0))
    def _first_projection():
        pre_norm(*subs[0])
        for _ in project(*subs[0]):
            pass

    for s in range(n_sub):
        heads = _skewed(head_items(*subs[s]), ITEM_LAG)
        if s + 1 < n_sub:
            pre_norm(*subs[s + 1])
            filler = project(*subs[s + 1])
            n_filler = PROJECT_COLUMN_PIECES + subs[s + 1][1] // CHUNK
        else:
            filler = project_out(*subs[s], 0, A_WIDTH)
            n_filler = PROJECT_OUT_DOTS
        if s >= 1:
            filler = _chain(project_out(*subs[s - 1]), filler)
            n_filler += PROJECT_OUT_DOTS + subs[s - 1][1] // CHUNK
        _interleave(heads, filler, n_filler / ((subs[s][1] // CHUNK) * HEAD_STAGES_PER_CHUNK))
    pre_norm(*subs[0], src_ref=xnext_ref)
    look_ahead = project(*subs[0])
    next(look_ahead)
    next(look_ahead)
    for _ in project_out(*subs[-1], A_WIDTH, MIX_WIDTH):
        pass
    for _ in look_ahead:
        pass


def _layer(x, mem, w_in_bf16, w_kv, w_out, pre_g, mem_g, post_g, vng, vnb, w_spatial,
           b_spatial, buckets, sinks, rel_bias):
    batch, seq, _ = x.shape
    ts = SEQ_TILE
    n_tiles = seq // ts
    const2 = lambda b, t, *_: (0, 0)
    const3 = lambda b, t, *_: (0, 0, 0)

    def next_first_sub_tile(b, t, *_):
        step = jnp.minimum(b * n_tiles + t + 1, batch * n_tiles - 1)
        return (step // n_tiles, (step % n_tiles) * (ts // SUB_TILES[0]), 0)

    grid_spec = pltpu.PrefetchScalarGridSpec(
        num_scalar_prefetch=1,
        grid=(batch, n_tiles),
        in_specs=[
            pl.BlockSpec((1, ts, D_MODEL), lambda b, t, *_: (b, t, 0)),
            pl.BlockSpec((1, SUB_TILES[0], D_MODEL), next_first_sub_tile),
            pl.BlockSpec((1, MEM_LEN, D_MODEL), lambda b, t, *_: (b, 0, 0)),
            pl.BlockSpec((D_MODEL, IN_WIDTH), const2),
            pl.BlockSpec((D_MODEL, 2 * MEM_WIDTH), const2),
            pl.BlockSpec((MIX_WIDTH, D_MODEL), const2),
            pl.BlockSpec((1, D_MODEL), const2),
            pl.BlockSpec((1, D_MODEL), const2),
            pl.BlockSpec((1, D_MODEL), const2),
            pl.BlockSpec((1, A_WIDTH), const2),
            pl.BlockSpec((1, A_WIDTH), const2),
            pl.BlockSpec((A_GROUPS, CHUNK, CHUNK), const3),
            pl.BlockSpec((A_GROUPS, CHUNK), const2),
            pl.BlockSpec((N_BUCKETS, SWA_HEADS), const2),
            pl.BlockSpec((CHUNK, 2 * CHUNK), const2),
        ],
        out_specs=pl.BlockSpec((1, ts, D_MODEL), lambda b, t, *_: (b, t, 0)),
        scratch_shapes=[
            pltpu.VMEM((ts, D_MODEL), BF16),
            pltpu.VMEM((ts, A_WIDTH), F32),
            pltpu.VMEM((ts, A_WIDTH), BF16),
            pltpu.VMEM((ts, SWA_WIDTH + MEM_WIDTH), BF16),
            pltpu.VMEM((SWA_HEADS, LANES, CHUNK + ts), BF16),
            pltpu.VMEM((SWA_HEADS, CHUNK + ts, LANES), BF16),
            pltpu.VMEM((ts, MIX_WIDTH), F32),
            pltpu.VMEM((MEM_HEADS, LANES, MEM_LEN), BF16),
            pltpu.VMEM((MEM_HEADS, MEM_LEN, LANES), BF16),
            pltpu.VMEM((2, SWA_HEADS, CHUNK, 2 * CHUNK), F32),
            pltpu.VMEM((A_GROUPS, CHUNK, CHUNK), BF16),
            pltpu.VMEM((A_GROUPS, CHUNK, LANES), F32),
            pltpu.VMEM((D_MODEL, 2 * MEM_WIDTH), BF16),
            pltpu.VMEM((MIX_WIDTH, D_MODEL), BF16),
            pltpu.VMEM((ts, MIX_WIDTH), F32),
            pltpu.VMEM((ts, MIX_WIDTH), BF16),
            pltpu.VMEM((ts, D_MODEL), F32),
        ],
    )
    return pl.pallas_call(
        _layer_kernel,
        grid_spec=grid_spec,
        out_shape=jax.ShapeDtypeStruct(x.shape, x.dtype),
        compiler_params=pltpu.CompilerParams(
            dimension_semantics=("arbitrary", "arbitrary"),
            vmem_limit_bytes=VMEM_LIMIT_BYTES),
        name="layer",
    )(sinks, x, x, mem, w_in_bf16, w_kv, w_out, pre_g, mem_g, post_g, vng, vnb,
      w_spatial, b_spatial, rel_bias, buckets)


def kernel(x, mem, pre_norm_g, post_norm_g, mem_norm_g, w_in, w_mem_kv, v_norm_g, v_norm_b,
           w_spatial, b_spatial, attn_sinks, rel_bias, w_out):
    depth = w_in.shape[0]
    buckets = jnp.asarray(_band_buckets())
    for layer in range(depth):
        x = _layer(x, mem, w_in[layer].astype(BF16), w_mem_kv[layer], w_out[layer],
                   pre_norm_g[layer][None, :], mem_norm_g[layer][None, :], post_norm_g[layer][None, :],
                   v_norm_g[layer][None, :], v_norm_b[layer][None, :],
                   w_spatial[layer], b_spatial[layer], buckets, attn_sinks[layer], rel_bias)
    return x
```

```python
import numpy as np
import jax
import jax.numpy as jnp
from jax import lax
from jax.experimental import pallas as pl
from jax.experimental.pallas import tpu as pltpu

D_MODEL = 1024
MEM_LEN = 256
HEAD_DIM = 64
CHUNK = 128
A_GROUPS = 4
A_WIDTH = 512
SWA_HEADS = 4
SWA_WIDTH = 256
SWA_KV_WIDTH = 128
MEM_HEADS = 4
MEM_WIDTH = 256
MIX_WIDTH = 1024
IN_WIDTH = 2816
N_BUCKETS = 32
MAX_DISTANCE = 128
WINDOW = 128
EPS = 1e-6
NEG = -1e30
LOG2E = float(np.log2(np.e))
Q_SCALE = HEAD_DIM ** -0.5 * LOG2E

OFF_U = 0
OFF_V = A_WIDTH
OFF_SQ = 2 * A_WIDTH
OFF_SK = OFF_SQ + SWA_WIDTH
OFF_SV = OFF_SK + SWA_KV_WIDTH
OFF_MQ = OFF_SV + SWA_KV_WIDTH
OFF_Z = OFF_MQ + MEM_WIDTH

LANES = 128
SEQ_TILE = 1024
SUB_TILES = (256, 256, 256, 256)
PROJ_COLS = 512
ITEM_LAG = 2
PROJECT_COLUMN_PIECES = 6
PROJECT_OUT_DOTS = D_MODEL // PROJ_COLS
HEAD_STAGES_PER_CHUNK = 2 * (SWA_HEADS // 2 + MEM_HEADS // 2)
VMEM_LIMIT_BYTES = 62 * 1024 * 1024

BF16 = jnp.bfloat16
F32 = jnp.float32


def _t5_causal_buckets(dist):
    n = np.maximum(dist, 0)
    max_exact = N_BUCKETS // 2
    large = max_exact + (np.log(np.maximum(n, 1) / max_exact) / np.log(MAX_DISTANCE / max_exact)
                         * (N_BUCKETS - max_exact)).astype(np.int32)
    large = np.minimum(large, N_BUCKETS - 1)
    return np.where(n < max_exact, n, large).astype(np.int32)


def _band_buckets():
    qi = np.arange(CHUNK)[:, None]
    kj = np.arange(2 * CHUNK)[None, :]
    dist = qi + CHUNK - kj
    valid = (dist >= 0) & (dist < WINDOW)
    return np.where(valid, _t5_causal_buckets(dist), -1).astype(np.int32)


def _gelu_tanh(x):
    c = np.float32(np.sqrt(2.0 / np.pi))
    ck = np.float32(np.sqrt(2.0 / np.pi) * 0.044715)
    hx = 0.5 * x
    return hx + hx * jnp.tanh(x * (c + ck * (x * x)))


def _silu(z):
    hz = 0.5 * z
    return hz + hz * jnp.tanh(hz)


def _dot(a, b):
    return jnp.dot(a, b, preferred_element_type=F32)


_DONE = object()


def _skewed(items, lag):
    waiting = []
    for item in items:
        if next(item, _DONE) is not _DONE:
            waiting.append(item)
        yield
        if len(waiting) > lag:
            next(waiting.pop(0), _DONE)
            yield
    for item in waiting:
        next(item, _DONE)
        yield


def _chain(*gens):
    for g in gens:
        yield from g


def _interleave(a, b, b_per_a):
    credit = 0.0
    a_live = b_live = True
    while a_live or b_live:
        if a_live:
            a_live = next(a, _DONE) is not _DONE
        credit += b_per_a
        while b_live and (credit >= 1.0 or not a_live):
            b_live = next(b, _DONE) is not _DONE
            credit -= 1.0


def _lane_half_mask(parity):
    lane = lax.broadcasted_iota(jnp.int32, (1, LANES), 1)
    return (lane >= HEAD_DIM) if parity else (lane < HEAD_DIM)


def _place_head_rows(kt_head, parity):
    zeros = jnp.zeros_like(kt_head)
    return jnp.concatenate([zeros, kt_head] if parity else [kt_head, zeros], axis=0)


def _layer_kernel(sinks_ref,
                  x_ref, xnext_ref, mem_ref, win_ref, wkv_f32_ref, wout_f32_ref, preg_ref, memg_ref, postg_ref,
                  vng_ref, vnb_ref, ws_ref, bs_ref, relb_ref, buckets_ref,
                  out_ref,
                  h_scr, gu_scr, vn_scr, qs_scr, ktvar_scr, vvar_scr, zg_scr,
                  mk_scr, mv_scr, bias_scr, wsm_scr, bst_scr, wkv_ref, wout_ref,
                  y_scr, ybf_scr, o_scr):
    ts = x_ref.shape[1]
    b = pl.program_id(0)
    t = pl.program_id(1)

    @pl.when((b == 0) & (t == 0))
    def _init():
        row = lax.broadcasted_iota(jnp.int32, (CHUNK, CHUNK), 0)
        col = lax.broadcasted_iota(jnp.int32, (CHUNK, CHUNK), 1)
        for g in range(A_GROUPS):
            wsm_scr[g] = jnp.where(row >= col, ws_ref[g], 0.0).astype(BF16)
            bs_col = jnp.sum(jnp.where(row == col, bs_ref[g:g + 1, :], 0.0), axis=1, keepdims=True)
            bst_scr[g] = jnp.broadcast_to(bs_col, (CHUNK, LANES))
        for r0 in range(0, D_MODEL, 2 * CHUNK):
            wrows = pl.ds(r0, 2 * CHUNK)
            wkv_ref[wrows, :] = wkv_f32_ref[wrows, :].astype(BF16)
            wout_ref[wrows, :] = wout_f32_ref[wrows, :].astype(BF16)
        buckets = buckets_ref[...]
        prev_key = lax.broadcasted_iota(jnp.int32, (CHUNK, 2 * CHUNK), 1) < CHUNK
        for hd in range(SWA_HEADS):
            tbl = jnp.full((CHUNK, 2 * CHUNK), NEG, F32)
            for bk in range(N_BUCKETS):
                tbl = jnp.where(buckets == bk, relb_ref[bk:bk + 1, hd:hd + 1] * LOG2E, tbl)
            bias_scr[0, hd] = tbl
            bias_scr[1, hd] = jnp.where(prev_key, NEG, tbl)

    @pl.when(t == 0)
    def _sequence_start():
        ktvar_scr[:, :, 0:CHUNK] = jnp.zeros((SWA_HEADS, LANES, CHUNK), BF16)
        vvar_scr[:, 0:CHUNK, :] = jnp.zeros((SWA_HEADS, CHUNK, LANES), BF16)
        m = mem_ref[0]
        ms = jnp.mean(m * m, axis=-1, keepdims=True)
        hm = (m * lax.rsqrt(ms + EPS) * memg_ref[...]).astype(BF16)
        kv = _dot(hm, wkv_ref[...])
        for pair in range(MEM_HEADS // 2):
            kt_pair = kv[:, pair * LANES:(pair + 1) * LANES].T
            v_pair = kv[:, MEM_WIDTH + pair * LANES:MEM_WIDTH + (pair + 1) * LANES]
            for parity in range(2):
                hd = 2 * pair + parity
                kt_head = kt_pair[parity * HEAD_DIM:(parity + 1) * HEAD_DIM]
                mk_scr[hd] = _place_head_rows(kt_head, parity).astype(BF16)
                mv_scr[hd] = jnp.where(_lane_half_mask(parity), v_pair, 0.0).astype(BF16)

    @pl.when(t > 0)
    def _copy_carry():
        ktvar_scr[:, :, 0:CHUNK] = ktvar_scr[:, :, ts:ts + CHUNK]
        vvar_scr[:, 0:CHUNK, :] = vvar_scr[:, ts:ts + CHUNK, :]

    lo = _lane_half_mask(0)
    hi = _lane_half_mask(1)
    first_tile = jnp.where(t == 0, 1, 0)

    def pre_norm(r0, nrows, src_ref=x_ref):
        for rc in range(nrows // CHUNK):
            crows = pl.ds(r0 + rc * CHUNK, CHUNK)
            x = src_ref[0, crows, :]
            ms = jnp.mean(x * x, axis=-1, keepdims=True)
            h_scr[crows, :] = (x * lax.rsqrt(ms + EPS) * preg_ref[...]).astype(BF16)

    def proj(rows, c0, width):
        return _dot(h_scr[rows, :], win_ref[:, c0:c0 + width])

    def project(r0, nrows):
        rows = pl.ds(r0, nrows)

        v_all = _gelu_tanh(proj(rows, OFF_V, A_WIDTH))
        for g in range(A_GROUPS):
            cols = pl.ds(g * LANES, LANES)
            v = v_all[:, g * LANES:(g + 1) * LANES]
            mu = jnp.mean(v, axis=-1, keepdims=True)
            d = v - mu
            var = jnp.mean(d * d, axis=-1, keepdims=True)
            vn = d * lax.rsqrt(var + EPS) * vng_ref[:, cols] + vnb_ref[:, cols]
            vn_scr[rows, cols] = vn.astype(BF16)
        yield

        gu_scr[rows, :] = _gelu_tanh(proj(rows, OFF_U, A_WIDTH))
        yield

        qkv = proj(rows, OFF_SQ, SWA_WIDTH + 2 * SWA_KV_WIDTH)
        qs_scr[rows, 0:SWA_WIDTH] = (qkv[:, 0:SWA_WIDTH] * Q_SCALE).astype(BF16)
        skt = qkv[:, SWA_WIDTH:SWA_WIDTH + LANES].T
        sv = qkv[:, SWA_WIDTH + LANES:]
        sv_rot = pltpu.roll(sv, HEAD_DIM, 1)
        new_rows = pl.ds(CHUNK + r0, nrows)
        for hd, (keep, vsrc) in enumerate([(lo, sv), (hi, sv_rot), (lo, sv_rot), (hi, sv)]):
            kv_head = hd // 2
            kt_head = skt[kv_head * HEAD_DIM:(kv_head + 1) * HEAD_DIM]
            ktvar_scr[hd, :, new_rows] = _place_head_rows(kt_head, hd % 2).astype(BF16)
            vvar_scr[hd, new_rows, :] = jnp.where(keep, vsrc, 0.0).astype(BF16)
        yield

        mqz = proj(rows, OFF_MQ, PROJ_COLS)
        qs_scr[rows, SWA_WIDTH:] = (mqz[:, 0:MEM_WIDTH] * Q_SCALE).astype(BF16)
        zg_scr[rows, 0:PROJ_COLS - MEM_WIDTH] = _silu(mqz[:, MEM_WIDTH:])
        yield

        for c in range(r0 // CHUNK, (r0 + nrows) // CHUNK):
            for g in range(A_GROUPS):
                spatial_gating(c, g)
            yield

        done = PROJ_COLS - MEM_WIDTH
        while done < MIX_WIDTH:
            width = min(PROJ_COLS, MIX_WIDTH - done)
            zg_scr[rows, done:done + width] = _silu(proj(rows, OFF_Z + done, width))
            done += width
            yield

    def spatial_gating(c, g):
        rows = pl.ds(c * CHUNK, CHUNK)
        cols = pl.ds(g * LANES, LANES)
        sv_g = _dot(wsm_scr[g], vn_scr[rows, cols]) + bst_scr[g]
        y_scr[rows, cols] = gu_scr[rows, cols] * sv_g

    def attn_item(c, pair, sliding):
        r0 = c * CHUNK
        rows = pl.ds(r0, CHUNK)
        heads = (2 * pair, 2 * pair + 1)
        if sliding:
            band = pl.ds(r0, 2 * CHUNK)
            qp = qs_scr[rows, pl.ds(pair * LANES, LANES)]
            keys = jnp.concatenate([ktvar_scr[hd, :, band] for hd in heads], axis=1)
            bias_sel = first_tile if c == 0 else 0
            bias = jnp.concatenate([bias_scr[bias_sel, hd] for hd in heads], axis=1)
            logits = _dot(qp, keys) + bias
        else:
            qp = qs_scr[rows, pl.ds(SWA_WIDTH + pair * LANES, LANES)]
            keys = jnp.concatenate([mk_scr[hd] for hd in heads], axis=1)
            logits = _dot(qp, keys)
        n_keys = logits.shape[1] // 2
        yield
        probs, recips = [], []
        for i, hd in enumerate(heads):
            s = logits[:, i * n_keys:(i + 1) * n_keys]
            m = jnp.max(s, axis=-1, keepdims=True)
            if sliding:
                sink = sinks_ref[hd] * LOG2E
                m = jnp.maximum(m, sink)
            p = jnp.exp2(s - m)
            denom = jnp.sum(p, axis=-1, keepdims=True)
            if sliding:
                denom = denom + jnp.exp2(sink - m)
            probs.append(p.astype(BF16))
            recips.append(1.0 / denom)
        if sliding:
            values = jnp.concatenate([vvar_scr[hd, band, :] for hd in heads], axis=0)
        else:
            values = jnp.concatenate([mv_scr[hd] for hd in heads], axis=0)
        o_pair = _dot(jnp.concatenate(probs, axis=1), values)
        o_pair = o_pair * jnp.where(lo, recips[0], recips[1])
        off = A_WIDTH if sliding else A_WIDTH + SWA_WIDTH
        y_scr[rows, pl.ds(off + pair * LANES, LANES)] = o_pair

    def head_items(r0, nrows):
        for c in range(r0 // CHUNK, (r0 + nrows) // CHUNK):
            for pair in range(SWA_HEADS // 2):
                yield attn_item(c, pair, True)
            for pair in range(MEM_HEADS // 2):
                yield attn_item(c, pair, False)

    def project_out(r0, nrows, k0=0, k1=MIX_WIDTH):
        rows = pl.ds(r0, nrows)
        ybf_scr[rows, k0:k1] = (y_scr[rows, k0:k1] * zg_scr[rows, k0:k1]).astype(BF16)
        for c0 in range(0, D_MODEL, PROJ_COLS):
            cols = pl.ds(c0, PROJ_COLS)
            part = _dot(ybf_scr[rows, k0:k1], wout_ref[k0:k1, cols])
            o_scr[rows, cols] = o_scr[rows, cols] + part if k0 else part
            yield
        if k1 < MIX_WIDTH:
            return
        for rc in range(nrows // CHUNK):
            crows = pl.ds(r0 + rc * CHUNK, CHUNK)
            o = o_scr[crows, :]
            ms = jnp.mean(o * o, axis=-1, keepdims=True)
            out_ref[0, crows, :] = x_ref[0, crows, :] + o * lax.rsqrt(ms + EPS) * postg_ref[...]
            yield

    assert sum(SUB_TILES) == ts
    starts = [sum(SUB_TILES[:s]) for s in range(len(SUB_TILES))]
    subs = list(zip(starts, SUB_TILES))
    n_sub = len(subs)

    @pl.when((b == 0) & (t == 0))
    def _first_projection():
        pre_norm(*subs[0])
        for _ in project(*subs[0]):
            pass

    for s in range(n_sub):
        heads = _skewed(head_items(*subs[s]), ITEM_LAG)
        if s + 1 < n_sub:
            pre_norm(*subs[s + 1])
            filler = project(*subs[s + 1])
            n_filler = PROJECT_COLUMN_PIECES + subs[s + 1][1] // CHUNK
        else:
            filler = project_out(*subs[s], 0, A_WIDTH)
            n_filler = PROJECT_OUT_DOTS
        if s >= 1:
            prev_out = project_out(*subs[s - 1])
            filler = _chain(filler, prev_out) if s + 1 == n_sub else _chain(prev_out, filler)
            n_filler += PROJECT_OUT_DOTS + subs[s - 1][1] // CHUNK
        _interleave(heads, filler, n_filler / ((subs[s][1] // CHUNK) * HEAD_STAGES_PER_CHUNK))
    for _ in project_out(*subs[-1], A_WIDTH, MIX_WIDTH):
        pass
    pre_norm(*subs[0], src_ref=xnext_ref)
    for _ in project(*subs[0]):
        pass


def _layer(x, mem, w_in_bf16, w_kv, w_out, pre_g, mem_g, post_g, vng, vnb, w_spatial,
           b_spatial, buckets, sinks, rel_bias):
    batch, seq, _ = x.shape
    ts = SEQ_TILE
    n_tiles = seq // ts
    const2 = lambda b, t, *_: (0, 0)
    const3 = lambda b, t, *_: (0, 0, 0)

    def next_first_sub_tile(b, t, *_):
        step = jnp.minimum(b * n_tiles + t + 1, batch * n_tiles - 1)
        return (step // n_tiles, (step % n_tiles) * (ts // SUB_TILES[0]), 0)

    grid_spec = pltpu.PrefetchScalarGridSpec(
        num_scalar_prefetch=1,
        grid=(batch, n_tiles),
        in_specs=[
            pl.BlockSpec((1, ts, D_MODEL), lambda b, t, *_: (b, t, 0)),
            pl.BlockSpec((1, SUB_TILES[0], D_MODEL), next_first_sub_tile),
            pl.BlockSpec((1, MEM_LEN, D_MODEL), lambda b, t, *_: (b, 0, 0)),
            pl.BlockSpec((D_MODEL, IN_WIDTH), const2),
            pl.BlockSpec((D_MODEL, 2 * MEM_WIDTH), const2),
            pl.BlockSpec((MIX_WIDTH, D_MODEL), const2),
            pl.BlockSpec((1, D_MODEL), const2),
            pl.BlockSpec((1, D_MODEL), const2),
            pl.BlockSpec((1, D_MODEL), const2),
            pl.BlockSpec((1, A_WIDTH), const2),
            pl.BlockSpec((1, A_WIDTH), const2),
            pl.BlockSpec((A_GROUPS, CHUNK, CHUNK), const3),
            pl.BlockSpec((A_GROUPS, CHUNK), const2),
            pl.BlockSpec((N_BUCKETS, SWA_HEADS), const2),
            pl.BlockSpec((CHUNK, 2 * CHUNK), const2),
        ],
        out_specs=pl.BlockSpec((1, ts, D_MODEL), lambda b, t, *_: (b, t, 0)),
        scratch_shapes=[
            pltpu.VMEM((ts, D_MODEL), BF16),
            pltpu.VMEM((ts, A_WIDTH), F32),
            pltpu.VMEM((ts, A_WIDTH), BF16),
            pltpu.VMEM((ts, SWA_WIDTH + MEM_WIDTH), BF16),
            pltpu.VMEM((SWA_HEADS, LANES, CHUNK + ts), BF16),
            pltpu.VMEM((SWA_HEADS, CHUNK + ts, LANES), BF16),
            pltpu.VMEM((ts, MIX_WIDTH), F32),
            pltpu.VMEM((MEM_HEADS, LANES, MEM_LEN), BF16),
            pltpu.VMEM((MEM_HEADS, MEM_LEN, LANES), BF16),
            pltpu.VMEM((2, SWA_HEADS, CHUNK, 2 * CHUNK), F32),
            pltpu.VMEM((A_GROUPS, CHUNK, CHUNK), BF16),
            pltpu.VMEM((A_GROUPS, CHUNK, LANES), F32),
            pltpu.VMEM((D_MODEL, 2 * MEM_WIDTH), BF16),
            pltpu.VMEM((MIX_WIDTH, D_MODEL), BF16),
            pltpu.VMEM((ts, MIX_WIDTH), F32),
            pltpu.VMEM((ts, MIX_WIDTH), BF16),
            pltpu.VMEM((ts, D_MODEL), F32),
        ],
    )
    return pl.pallas_call(
        _layer_kernel,
        grid_spec=grid_spec,
        out_shape=jax.ShapeDtypeStruct(x.shape, x.dtype),
        compiler_params=pltpu.CompilerParams(
            dimension_semantics=("arbitrary", "arbitrary"),
            vmem_limit_bytes=VMEM_LIMIT_BYTES),
        name="layer",
    )(sinks, x, x, mem, w_in_bf16, w_kv, w_out, pre_g, mem_g, post_g, vng, vnb,
      w_spatial, b_spatial, rel_bias, buckets)


def kernel(x, mem, pre_norm_g, post_norm_g, mem_norm_g, w_in, w_mem_kv, v_norm_g, v_norm_b,
           w_spatial, b_spatial, attn_sinks, rel_bias, w_out):
    depth = w_in.shape[0]
    buckets = jnp.asarray(_band_buckets())
    for layer in range(depth):
        x = _layer(x, mem, w_in[layer].astype(BF16), w_mem_kv[layer], w_out[layer],
                   pre_norm_g[layer][None, :], mem_norm_g[layer][None, :], post_norm_g[layer][None, :],
                   v_norm_g[layer][None, :], v_norm_b[layer][None, :],
                   w_spatial[layer], b_spatial[layer], buckets, attn_sinks[layer], rel_bias)
    return x
```
